```python
import math
import jax, jax.numpy as jnp
from jax import lax
import numpy as np

D_MODEL = 1024
BATCH = 4
SEQ = 4096
DEPTH = 1

HG_HEADS = 8
HG_DK = 128
HG_DV = 128
HG_WIDTH = HG_HEADS * HG_DK
HG_CHUNK = 64
AT_HEADS = 8
AT_DH = 128
AT_WIDTH = AT_HEADS * AT_DH
MOBA_BLOCK = 256
MOBA_TOPK = 3
Q_CHUNK = 16
REL_BUCKETS = 32
REL_MAX_DIST = 128
D_FF = -(-8 * D_MODEL // (3 * 256)) * 256
N_BRANCH = 2
NORM_EPS = 1e-6
IN_SIZES = [HG_WIDTH, HG_WIDTH, HG_HEADS * HG_DV, HG_HEADS * HG_DV,
            AT_WIDTH, AT_WIDTH, AT_WIDTH, N_BRANCH * D_MODEL]
IN_COLS = sum(IN_SIZES)
IN_SPLITS = [int(s) for s in np.cumsum(IN_SIZES)[:-1]]

kernel_name = "hgrn2_moba_gated_hybrid"


def rmsnorm(x, g):
    xf = x.astype(jnp.float32)
    y = xf * lax.rsqrt(jnp.mean(xf * xf, axis=-1, keepdims=True) + NORM_EPS)
    return (y * g.astype(jnp.float32)).astype(x.dtype)


def t5_bucket(dist):
    n = jnp.maximum(dist, 0)
    max_exact = REL_BUCKETS // 2
    nf = jnp.maximum(n, 1).astype(jnp.float32)
    large = max_exact + (jnp.log(nf / max_exact) / math.log(REL_MAX_DIST / max_exact)
                         * (REL_BUCKETS - max_exact)).astype(jnp.int32)
    large = jnp.minimum(large, REL_BUCKETS - 1)
    return jnp.where(n < max_exact, n, large)


def hgrn2_mix(q, f_logit, i, g, lb, out_gain):
    B, S, _ = q.shape
    f32 = jnp.float32
    C = HG_CHUNK
    N = S // C
    z = f_logit.astype(f32)
    f = lb + (1.0 - lb) * jax.nn.sigmoid(z)
    log_f = jnp.log(f)
    k = (1.0 - lb) * jax.nn.sigmoid(-z)
    qs = q.astype(f32) * HG_DK ** -0.5

    def to_chunks(t, d):
        return t.reshape(B, N, C, HG_HEADS, d).transpose(1, 0, 3, 2, 4)

    qc = to_chunks(qs, HG_DK)
    kc = to_chunks(k, HG_DK)
    lfc = to_chunks(log_f, HG_DK)
    vc = to_chunks(i.astype(f32), HG_DV)
    causal = jnp.tril(jnp.ones((C, C), dtype=bool))

    def step(state, inp):
        q_, k_, v_, lf_ = inp
        b = jnp.cumsum(lf_, axis=2)
        o_inter = jnp.einsum('bhtk,bhkv->bhtv', q_ * jnp.exp(b), state)
        diff = b[:, :, :, None, :] - b[:, :, None, :, :]
        decay = jnp.exp(jnp.where(causal[:, :, None], diff, -jnp.inf))
        attn = jnp.einsum('bhtk,bhsk,bhtsk->bhts', q_, k_, decay)
        o = o_inter + jnp.einsum('bhts,bhsv->bhtv', attn, v_)
        b_last = b[:, :, -1:, :]
        k_dec = k_ * jnp.exp(b_last - b)
        new_state = (jnp.exp(b_last[:, :, 0, :])[..., None] * state
                     + jnp.einsum('bhsk,bhsv->bhkv', k_dec, v_))
        return new_state, o

    s0 = jnp.zeros((B, HG_HEADS, HG_DK, HG_DV), f32)
    _, o = lax.scan(step, s0, (qc, kc, vc, lfc))
    o = o.transpose(1, 0, 3, 2, 4).reshape(B, S, HG_HEADS, HG_DV)
    o = rmsnorm(o, out_gain) * jax.nn.silu(g.astype(f32).reshape(B, S, HG_HEADS, HG_DV))
    return o.reshape(B, S, HG_HEADS * HG_DV).astype(q.dtype)


def moba_attention(q, k, v, rel_table):
    B, H, S, Dh = q.shape
    f32 = jnp.float32
    BLK = MOBA_BLOCK
    NB = -(-S // BLK)
    pad = NB * BLK - S
    kp = jnp.pad(k, ((0, 0), (0, 0), (0, pad), (0, 0)))
    vp = jnp.pad(v, ((0, 0), (0, 0), (0, pad), (0, 0)))
    kb = kp.reshape(B, H, NB, BLK, Dh)
    vb = vp.reshape(B, H, NB, BLK, Dh)
    kmean = jnp.mean(kb.astype(f32), axis=3)
    topk = min(MOBA_TOPK, NB)
    scale = Dh ** -0.5
    table_h = rel_table.astype(f32).T
    b_idx = jnp.arange(B)[:, None, None, None]
    h_idx = jnp.arange(H)[None, :, None, None]
    h_idx5 = jnp.arange(H)[None, :, None, None, None]
    n_chunks = S // Q_CHUNK

    def one_chunk(c):
        t0 = c * Q_CHUNK
        qc = lax.dynamic_slice_in_dim(q, t0, Q_CHUNK, axis=2)
        tpos = t0 + jnp.arange(Q_CHUNK)
        cur = t0 // BLK
        gate = jnp.einsum('bhqd,bhnd->bhqn', qc.astype(f32), kmean)
        gate = jnp.where(jnp.arange(NB) < cur, gate, -jnp.inf)
        _, sel = lax.top_k(gate, topk)
        sel_valid = sel < cur
        ks = kb[b_idx, h_idx, sel]
        vs = vb[b_idx, h_idx, sel]
        lp = jnp.einsum('bhqd,bhqjsd->bhqjs', qc, ks).astype(f32) * scale
        kpos = sel[..., None] * BLK + jnp.arange(BLK)
        bias_p = table_h[h_idx5, t5_bucket(tpos[:, None, None] - kpos)]
        lp = jnp.where(sel_valid[..., None], lp + bias_p, -jnp.inf)
        ko = lax.dynamic_slice_in_dim(kp, cur * BLK, BLK, axis=2)
        vo = lax.dynamic_slice_in_dim(vp, cur * BLK, BLK, axis=2)
        lo = jnp.einsum('bhqd,bhsd->bhqs', qc, ko).astype(f32) * scale
        dist_o = tpos[:, None] - (cur * BLK + jnp.arange(BLK))[None, :]
        lo = jnp.where(dist_o >= 0, lo + table_h[:, t5_bucket(dist_o)], -jnp.inf)
        logits = jnp.concatenate([lo, lp.reshape(B, H, Q_CHUNK, topk * BLK)], axis=-1)
        p = jax.nn.softmax(logits, axis=-1)
        p_own = p[..., :BLK].astype(v.dtype)
        p_past = p[..., BLK:].reshape(B, H, Q_CHUNK, topk, BLK).astype(v.dtype)
        return (jnp.einsum('bhqs,bhsd->bhqd', p_own, vo)
                + jnp.einsum('bhqjs,bhqjsd->bhqd', p_past, vs))

    out = lax.map(one_chunk, jnp.arange(n_chunks))
    return out.transpose(1, 2, 0, 3, 4).reshape(B, H, S, Dh)


def setup_inputs(seed: int = 0) -> dict:
    key = jax.random.key(seed)
    ks = jax.random.split(key, 16)
    f32 = jnp.float32
    nrm = lambda k, shape, s: jax.random.normal(k, shape, f32) * s
    return {
        "x": jax.random.normal(ks[0], (BATCH, SEQ, D_MODEL), f32),
        "attn_norm_g": 1.0 + nrm(ks[1], (DEPTH, D_MODEL), 0.02),
        "w_in": nrm(ks[2], (DEPTH, D_MODEL, IN_COLS), D_MODEL ** -0.5),
        "hg_lb_gamma": nrm(ks[3], (DEPTH + 1, HG_WIDTH), 0.5),
        "hg_out_norm_g": 1.0 + nrm(ks[4], (DEPTH, HG_DV), 0.02),
        "q_norm_g": 1.0 + nrm(ks[5], (DEPTH, AT_DH), 0.02),
        "k_norm_g": 1.0 + nrm(ks[6], (DEPTH, AT_DH), 0.02),
        "rel_bias_table": nrm(ks[7], (REL_BUCKETS, AT_HEADS), 0.2),
        "w_branch_hg": nrm(ks[8], (DEPTH, HG_HEADS * HG_DV, D_MODEL), (HG_HEADS * HG_DV) ** -0.5),
        "w_branch_attn": nrm(ks[9], (DEPTH, AT_WIDTH, D_MODEL), AT_WIDTH ** -0.5),
        "w_out": nrm(ks[10], (DEPTH, D_MODEL, D_MODEL), D_MODEL ** -0.5),
        "ffn_norm_g": 1.0 + nrm(ks[11], (DEPTH, D_MODEL), 0.02),
        "w_ffn_gate": nrm(ks[12], (DEPTH, D_MODEL, D_FF), D_MODEL ** -0.5),
        "w_ffn_up": nrm(ks[13], (DEPTH, D_MODEL, D_FF), D_MODEL ** -0.5),
        "w_ffn_down": nrm(ks[14], (DEPTH, D_FF, D_MODEL), D_FF ** -0.5),
    }


def reference(x, attn_norm_g, w_in, hg_lb_gamma, hg_out_norm_g, q_norm_g, k_norm_g,
              rel_bias_table, w_branch_hg, w_branch_attn, w_out, ffn_norm_g,
              w_ffn_gate, w_ffn_up, w_ffn_down):
    B, S, D = x.shape
    lb_all = jnp.cumsum(jax.nn.softmax(hg_lb_gamma.astype(jnp.float32), axis=0), axis=0)[:DEPTH]
    for l in range(DEPTH):
        h = rmsnorm(x, attn_norm_g[l])
        proj = h @ w_in[l]
        hq, hf, hi, hg, aq, ak, av, gl = jnp.split(proj, IN_SPLITS, axis=-1)
        y_hg = hgrn2_mix(hq, hf, hi, hg, lb_all[l], hg_out_norm_g[l]) @ w_branch_hg[l]
        qh = rmsnorm(aq.reshape(B, S, AT_HEADS, AT_DH), q_norm_g[l]).transpose(0, 2, 1, 3)
        kh = rmsnorm(ak.reshape(B, S, AT_HEADS, AT_DH), k_norm_g[l]).transpose(0, 2, 1, 3)
        vh = av.reshape(B, S, AT_HEADS, AT_DH).transpose(0, 2, 1, 3)
        att = moba_attention(qh, kh, vh, rel_bias_table)
        y_at = att.transpose(0, 2, 1, 3).reshape(B, S, AT_WIDTH) @ w_branch_attn[l]
        gates = jax.nn.sigmoid(gl.reshape(B, S, N_BRANCH, D))
        merged = gates[:, :, 0, :] * y_hg + gates[:, :, 1, :] * y_at
        x = x + merged @ w_out[l]
        h2 = rmsnorm(x, ffn_norm_g[l])
        x = x + (jax.nn.silu(h2 @ w_ffn_gate[l]) * (h2 @ w_ffn_up[l])) @ w_ffn_down[l]
    return x
```

```python
import functools
import math

import numpy as np
import jax
import jax.numpy as jnp
from jax import lax
from jax.experimental import pallas as pl
from jax.experimental.pallas import tpu as pltpu

F32 = jnp.float32
BF16 = jnp.bfloat16

LANES = 128
NORM_EPS = 1e-6

HG_HEADS = 8
HG_DK = 128
HG_DV = 128
HG_CHUNK = 64
AT_HEADS = 8
AT_DH = 128
MOBA_BLOCK = 256
MOBA_TOPK = 3
REL_BUCKETS = 32
REL_MAX_DIST = 128
MASKED_LOGIT = -1e30

VMEM_LIMIT = 56 * 1024 * 1024


def _cparams(sem):
    return pltpu.CompilerParams(dimension_semantics=sem, vmem_limit_bytes=VMEM_LIMIT)


def _dot(a, b):
    return jnp.dot(a, b, preferred_element_type=F32)


def _dot_nt(a, b):
    return lax.dot_general(a, b, (((1,), (1,)), ((), ())), preferred_element_type=F32)


def _dot_tn(a, b):
    return lax.dot_general(a, b, (((0,), (0,)), ((), ())), preferred_element_type=F32)


def _rmsnorm_kernel(x_ref, g_ref, o_ref):
    x = x_ref[...]
    ms = jnp.mean(x * x, axis=-1, keepdims=True)
    o_ref[...] = (x * lax.rsqrt(ms + NORM_EPS) * g_ref[...]).astype(o_ref.dtype)


def _rmsnorm(x, g, tm=1024):
    T, D = x.shape
    return pl.pallas_call(
        _rmsnorm_kernel,
        grid=(T // tm,),
        in_specs=[pl.BlockSpec((tm, D), lambda i: (i, 0)),
                  pl.BlockSpec((1, D), lambda i: (0, 0))],
        out_specs=pl.BlockSpec((tm, D), lambda i: (i, 0)),
        out_shape=jax.ShapeDtypeStruct((T, D), BF16),
        compiler_params=_cparams(("parallel",)),
        name="rmsnorm",
    )(x, g.reshape(1, D).astype(F32))


def _proj_kernel(*refs, epilogue, n_extra, n_out):
    h_ref, w_ref = refs[0], refs[1]
    extras = refs[2:2 + n_extra]
    outs = refs[2 + n_extra:2 + n_extra + n_out]
    acc = _dot(h_ref[...], w_ref[...])
    res = epilogue(acc, *[e[...] for e in extras])
    for o_ref, r in zip(outs, res):
        o_ref[...] = r.astype(o_ref.dtype)


def _proj(h, w, col0, ncols, epilogue, extras, out_dtypes, name, tm=2048, tn=512):
    T, D = h.shape
    j0 = col0 // tn
    in_specs = [pl.BlockSpec((tm, D), lambda i, j: (i, 0)),
                pl.BlockSpec((D, tn), lambda i, j: (0, j0 + j))]
    in_specs += [pl.BlockSpec((e.shape[0], tn), lambda i, j: (0, j)) for e in extras]
    outs = pl.pallas_call(
        functools.partial(_proj_kernel, epilogue=epilogue, n_extra=len(extras),
                          n_out=len(out_dtypes)),
        grid=(T // tm, ncols // tn),
        in_specs=in_specs,
        out_specs=[pl.BlockSpec((tm, tn), lambda i, j: (i, j)) for _ in out_dtypes],
        out_shape=[jax.ShapeDtypeStruct((T, ncols), dt) for dt in out_dtypes],
        compiler_params=_cparams(("parallel", "arbitrary")),
        name=name,
    )(h, w, *extras)
    return outs


def _ep_identity(acc):
    return (acc,)


def _ep_scale(acc, *, scale):
    return (acc * scale,)


def _ep_silu(acc):
    return (acc * jax.nn.sigmoid(acc),)


def _ep_sigmoid(acc):
    return (jax.nn.sigmoid(acc),)


def _ep_forget(acc, gamma):
    gmax = jnp.max(gamma, axis=0, keepdims=True)
    eg = jnp.exp(gamma - gmax)
    lb = eg[0:1, :] / jnp.sum(eg, axis=0, keepdims=True)
    f = lb + (1.0 - lb) * jax.nn.sigmoid(acc)
    key = (1.0 - lb) * jax.nn.sigmoid(-acc)
    return jnp.log(f), key


def _ep_headnorm(acc, gain):
    cols = []
    for c in range(acc.shape[1] // LANES):
        blk = acc[:, c * LANES:(c + 1) * LANES]
        ms = jnp.mean(blk * blk, axis=-1, keepdims=True)
        cols.append(blk * lax.rsqrt(ms + NORM_EPS) * gain[:, c * LANES:(c + 1) * LANES])
    return (jnp.concatenate(cols, axis=1),)


def _hgrn_tables(C):
    nl = int(math.log2(C))
    assert 1 << nl == C
    G = np.zeros(((nl + 2) * C, C), np.float32)
    masks = np.zeros((nl + 1, C, C), np.float32)
    masks[0] = np.eye(C)
    t = np.arange(C)
    for L in range(1, nl + 1):
        blk, half = 1 << L, 1 << (L - 1)
        base = (t // blk) * blk
        r = base + half - 1
        upper = (t - base) >= half
        for row in range(C):
            if upper[row]:
                G[(L - 1) * C + row, r[row] + 1:row + 1] = 1.0
            else:
                G[(L - 1) * C + row, row + 1:r[row] + 1] = 1.0
        same = base[:, None] == base[None, :]
        masks[L] = (same & upper[:, None] & (~upper)[None, :]).astype(np.float32)
    G[nl * C:(nl + 1) * C] = np.tril(np.ones((C, C)))
    G[(nl + 1) * C:(nl + 2) * C] = np.triu(np.ones((C, C)), 1)
    return G, masks, nl


def _hgrn_kernel(q_ref, lf_ref, k_ref, v_ref, g_ref, gain_ref, G_ref, m_ref, o_ref,
                 e_sc, st_sc, *, C, nl, heads):
    @pl.when(pl.program_id(1) == 0)
    def _():
        st_sc[...] = jnp.zeros_like(st_sc)

    lf = lf_ref[...]
    hi = lf.astype(BF16)
    r1 = lf - hi.astype(F32)
    mid = r1.astype(BF16)
    lo = (r1 - mid.astype(F32)).astype(BF16)
    G = G_ref[...]
    e_sc[...] = jnp.exp(_dot(G, hi) + _dot(G, mid) + _dot(G, lo))

    gain = gain_ref[...]
    for h in range(heads):
        sl = slice(h * HG_DK, (h + 1) * HG_DK)
        qb = q_ref[:, sl]
        kb = k_ref[:, sl]
        vb = v_ref[:, sl]
        qf = qb.astype(F32)
        kf = kb.astype(F32)
        A = m_ref[0] * _dot_nt(qb, kb)
        for L in range(1, nl + 1):
            eL = e_sc[(L - 1) * C:L * C, sl]
            A = A + m_ref[L] * _dot_nt((qf * eL).astype(BF16), (kf * eL).astype(BF16))
        eb = e_sc[nl * C:(nl + 1) * C, sl]
        ek = e_sc[(nl + 1) * C:(nl + 2) * C, sl]
        st = st_sc[h]
        o = _dot(A.astype(BF16), vb) + _dot_nt((qf * eb).astype(BF16), st.astype(BF16))
        st_sc[h] = st * eb[C - 1:C, :] + _dot_tn(vb, (kf * ek).astype(BF16))
        ms = jnp.mean(o * o, axis=-1, keepdims=True)
        y = o * lax.rsqrt(ms + NORM_EPS) * gain * g_ref[:, sl].astype(F32)
        o_ref[:, sl] = y.astype(o_ref.dtype)


def _hgrn(q, lf, k, v, g, gain, B, S):
    T, W = q.shape
    C = HG_CHUNK
    N = S // C
    G, masks, nl = _hgrn_tables(C)
    row = lambda b, c: (b * N + c, 0)
    tok = lambda: pl.BlockSpec((C, W), row)
    return pl.pallas_call(
        functools.partial(_hgrn_kernel, C=C, nl=nl, heads=HG_HEADS),
        grid=(B, N),
        in_specs=[tok(), tok(), tok(), tok(), tok(),
                  pl.BlockSpec((1, HG_DV), lambda b, c: (0, 0)),
                  pl.BlockSpec(G.shape, lambda b, c: (0, 0)),
                  pl.BlockSpec(masks.shape, lambda b, c: (0, 0, 0))],
        out_specs=tok(),
        out_shape=jax.ShapeDtypeStruct((T, W), BF16),
        scratch_shapes=[pltpu.VMEM(((nl + 2) * C, W), F32),
                        pltpu.VMEM((HG_HEADS, HG_DV, HG_DK), F32)],
        compiler_params=_cparams(("parallel", "arbitrary")),
        name="hgrn2",
    )(q, lf, k, v, g, gain.reshape(1, HG_DV).astype(F32),
      jnp.asarray(G, BF16), jnp.asarray(masks, F32))


def _t5_bucket_np(dist):
    n = np.maximum(dist, 0)
    max_exact = REL_BUCKETS // 2
    nf = np.maximum(n, 1).astype(np.float32)
    large = max_exact + (np.log(nf / max_exact) / math.log(REL_MAX_DIST / max_exact)
                         * (REL_BUCKETS - max_exact)).astype(np.int32)
    large = np.minimum(large, REL_BUCKETS - 1)
    return np.where(n < max_exact, n, large).astype(np.int32)


def _bias_kernel(tab_ref, bucket_ref, o_ref):
    h = pl.program_id(0)
    bucket = bucket_ref[...]
    acc = jnp.zeros(bucket.shape, F32)
    for b in range(REL_BUCKETS):
        acc = jnp.where(bucket == b, tab_ref[b, h], acc)
    o_ref[0] = acc


def _bias_tiles(rel_table):
    BLK = MOBA_BLOCK
    t = np.arange(BLK)[:, None]
    s = np.arange(2 * BLK)[None, :]
    dist = np.where(s < BLK, t - s, t + BLK - (s - BLK))
    bucket = _t5_bucket_np(dist)
    H = rel_table.shape[1]
    return pl.pallas_call(
        _bias_kernel,
        grid=(H,),
        in_specs=[pl.BlockSpec(memory_space=pltpu.SMEM),
                  pl.BlockSpec(bucket.shape, lambda h: (0, 0))],
        out_specs=pl.BlockSpec((1, BLK, 2 * BLK), lambda h: (h, 0, 0)),
        out_shape=jax.ShapeDtypeStruct((H, BLK, 2 * BLK), F32),
        compiler_params=_cparams(("arbitrary",)),
        name="relbias",
    )(rel_table.astype(F32), jnp.asarray(bucket))


def _moba_kernel(q_ref, k_ref, v_ref, bias_ref, far_ref, o_ref, kmean_sc, *, NB, scale):
    BLK = MOBA_BLOCK
    h = pl.program_id(1)
    i = pl.program_id(2)

    @pl.when(i == 0)
    def _():
        for n in range(NB):
            kmean_sc[n:n + 1, :] = jnp.mean(k_ref[n * BLK:(n + 1) * BLK, :].astype(F32),
                                            axis=0, keepdims=True)

    q = q_ref[...]

    gate = lax.dot_general(q.astype(F32), kmean_sc[...], (((1,), (1,)), ((), ())),
                           precision=lax.Precision.HIGHEST, preferred_element_type=F32)
    col = lax.broadcasted_iota(jnp.int32, (BLK, NB), 1)
    valid = col < i
    gate = jnp.where(valid, gate, -jnp.inf)
    rank = jnp.zeros((BLK, NB), jnp.int32)
    for m in range(NB):
        gm = gate[:, m:m + 1]
        beats = (gm > gate) | ((gm == gate) & (col > m))
        rank = rank + beats.astype(jnp.int32)
    sel = valid & (rank < MOBA_TOPK)
    selmask = jnp.where(sel, 0.0, MASKED_LOGIT).astype(BF16)
    blk_row = lax.broadcasted_iota(jnp.int32, (NB, BLK), 0)

    def update(carry, s, vt):
        m, l, acc = carry
        m_new = jnp.maximum(m, jnp.max(s, axis=-1, keepdims=True))
        alpha = jnp.exp(m - m_new)
        p = jnp.exp(s - m_new)
        l = alpha * l + jnp.sum(p, axis=-1, keepdims=True)
        acc = alpha * acc + _dot(p.astype(BF16), vt)
        return m_new, l, acc

    def past_mask(j):
        onehot = (blk_row == j).astype(BF16)
        return _dot(selmask, onehot)

    r = lax.broadcasted_iota(jnp.int32, (BLK, BLK), 0)
    c = lax.broadcasted_iota(jnp.int32, (BLK, BLK), 1)
    kt = k_ref[pl.ds(pl.multiple_of(i * BLK, BLK), BLK), :]
    vt = v_ref[pl.ds(pl.multiple_of(i * BLK, BLK), BLK), :]
    s = _dot_nt(q, kt) * scale + bias_ref[0, :, 0:BLK]
    s = jnp.where(r >= c, s, MASKED_LOGIT)
    carry = (jnp.full((BLK, 1), MASKED_LOGIT, F32), jnp.zeros((BLK, 1), F32),
             jnp.zeros((BLK, AT_DH), F32))
    carry = update(carry, s, vt)

    def prev_block(carry):
        j = i - 1
        kt = k_ref[pl.ds(pl.multiple_of(j * BLK, BLK), BLK), :]
        vt = v_ref[pl.ds(pl.multiple_of(j * BLK, BLK), BLK), :]
        s = _dot_nt(q, kt) * scale + bias_ref[0, :, BLK:2 * BLK] + past_mask(j)
        return update(carry, s, vt)

    carry = lax.cond(i >= 1, prev_block, lambda cr: cr, carry)

    far = far_ref[pl.ds(h, 1), :][:, 0:1]

    def far_block(j, carry):
        kt = k_ref[pl.ds(pl.multiple_of(j * BLK, BLK), BLK), :]
        vt = v_ref[pl.ds(pl.multiple_of(j * BLK, BLK), BLK), :]
        s = _dot_nt(q, kt) * scale + far + past_mask(j)
        return update(carry, s, vt)

    m, l, acc = lax.fori_loop(0, jnp.maximum(i - 1, 0), far_block, carry)
    o_ref[...] = (acc / l).astype(o_ref.dtype)


def _moba(q, k, v, rel_table, B, S):
    T, W = q.shape
    H = AT_HEADS
    NB = S // MOBA_BLOCK
    BLK = MOBA_BLOCK
    assert int(_t5_bucket_np(np.array([BLK + 1]))[0]) == REL_BUCKETS - 1
    bias = _bias_tiles(rel_table)
    far = jnp.broadcast_to(rel_table[REL_BUCKETS - 1, :].astype(F32)[:, None], (H, LANES))
    return pl.pallas_call(
        functools.partial(_moba_kernel, NB=NB, scale=AT_DH ** -0.5),
        grid=(B, H, NB),
        in_specs=[pl.BlockSpec((BLK, AT_DH), lambda b, h, i: (b * NB + i, h)),
                  pl.BlockSpec((S, AT_DH), lambda b, h, i: (b, h)),
                  pl.BlockSpec((S, AT_DH), lambda b, h, i: (b, h)),
                  pl.BlockSpec((1, BLK, 2 * BLK), lambda b, h, i: (h, 0, 0)),
                  pl.BlockSpec((H, LANES), lambda b, h, i: (0, 0))],
        out_specs=pl.BlockSpec((BLK, AT_DH), lambda b, h, i: (b * NB + i, h)),
        out_shape=jax.ShapeDtypeStruct((T, W), BF16),
        scratch_shapes=[pltpu.VMEM((NB, AT_DH), F32)],
        compiler_params=_cparams(("parallel", "parallel", "arbitrary")),
        name="moba",
    )(q, k, v, bias, far)


def _merge_kernel(ohg_ref, oat_ref, gate_ref, x_ref, whg_ref, wat_ref, wout_ref, g2_ref,
                  x1_ref, h2_ref, *, D):
    y_hg = _dot(ohg_ref[...], whg_ref[...])
    y_at = _dot(oat_ref[...], wat_ref[...])
    merged = (gate_ref[:, 0:D].astype(F32) * y_hg + gate_ref[:, D:2 * D].astype(F32) * y_at)
    x1 = x_ref[...] + _dot(merged.astype(BF16), wout_ref[...])
    x1_ref[...] = x1
    ms = jnp.mean(x1 * x1, axis=-1, keepdims=True)
    h2_ref[...] = (x1 * lax.rsqrt(ms + NORM_EPS) * g2_ref[...]).astype(h2_ref.dtype)


def _merge(ohg, oat, gates, x, whg, wat, wout, g2, tm=512):
    T, D = x.shape
    tok = lambda w: pl.BlockSpec((tm, w), lambda i: (i, 0))
    full = lambda a: pl.BlockSpec(a.shape, lambda i: (0, 0))
    return pl.pallas_call(
        functools.partial(_merge_kernel, D=D),
        grid=(T // tm,),
        in_specs=[tok(ohg.shape[1]), tok(oat.shape[1]), tok(2 * D), tok(D),
                  full(whg), full(wat), full(wout), pl.BlockSpec((1, D), lambda i: (0, 0))],
        out_specs=[tok(D), tok(D)],
        out_shape=[jax.ShapeDtypeStruct((T, D), F32), jax.ShapeDtypeStruct((T, D), BF16)],
        compiler_params=_cparams(("parallel",)),
        name="merge",
    )(ohg, oat, gates, x, whg, wat, wout, g2.reshape(1, D).astype(F32))


def _ffn_kernel(h_ref, x_ref, wg_ref, wu_ref, wd_ref, o_ref, acc_sc):
    f = pl.program_id(1)

    @pl.when(f == 0)
    def _():
        acc_sc[...] = x_ref[...]

    h = h_ref[...]
    a = _dot(h, wg_ref[...])
    u = _dot(h, wu_ref[...])
    acc_sc[...] += _dot((a * jax.nn.sigmoid(a) * u).astype(BF16), wd_ref[...])

    @pl.when(f == pl.num_programs(1) - 1)
    def _():
        o_ref[...] = acc_sc[...]


def _ffn(h2, x1, wg, wu, wd, tm=1024, tf=256):
    T, D = x1.shape
    FF = wg.shape[1]
    return pl.pallas_call(
        _ffn_kernel,
        grid=(T // tm, FF // tf),
        in_specs=[pl.BlockSpec((tm, D), lambda i, f: (i, 0)),
                  pl.BlockSpec((tm, D), lambda i, f: (i, 0)),
                  pl.BlockSpec((D, tf), lambda i, f: (0, f)),
                  pl.BlockSpec((D, tf), lambda i, f: (0, f)),
                  pl.BlockSpec((tf, D), lambda i, f: (f, 0))],
        out_specs=pl.BlockSpec((tm, D), lambda i, f: (i, 0)),
        out_shape=jax.ShapeDtypeStruct((T, D), F32),
        scratch_shapes=[pltpu.VMEM((tm, D), F32)],
        compiler_params=_cparams(("parallel", "arbitrary")),
        name="ffn",
    )(h2, x1, wg, wu, wd)


def kernel(x, attn_norm_g, w_in, hg_lb_gamma, hg_out_norm_g, q_norm_g, k_norm_g, rel_bias_table,
           w_branch_hg, w_branch_attn, w_out, ffn_norm_g, w_ffn_gate, w_ffn_up, w_ffn_down):
    B, S, D = x.shape
    T = B * S
    depth = attn_norm_g.shape[0]
    assert depth == 1 and S % MOBA_BLOCK == 0 and S % HG_CHUNK == 0
    WH = HG_HEADS * HG_DK
    WV = HG_HEADS * HG_DV
    WA = AT_HEADS * AT_DH
    assert w_in.shape[2] == 2 * WH + 2 * WV + 3 * WA + 2 * D

    xt = x.reshape(T, D)
    for l in range(depth):
        w = w_in[l].astype(BF16)
        h = _rmsnorm(xt, attn_norm_g[l])
        c = 0
        (hq,) = _proj(h, w, c, WH, functools.partial(_ep_scale, scale=HG_DK ** -0.5), [],
                      [BF16], "proj_hq")
        c += WH
        lf, hk = _proj(h, w, c, WH, _ep_forget, [hg_lb_gamma.astype(F32)], [F32, BF16],
                       "proj_hf")
        c += WH
        (hi,) = _proj(h, w, c, WV, _ep_identity, [], [BF16], "proj_hi")
        c += WV
        (hg,) = _proj(h, w, c, WV, _ep_silu, [], [BF16], "proj_hg")
        c += WV
        qg = jnp.tile(q_norm_g[l].astype(F32), AT_HEADS).reshape(1, WA)
        kg = jnp.tile(k_norm_g[l].astype(F32), AT_HEADS).reshape(1, WA)
        (aq,) = _proj(h, w, c, WA, _ep_headnorm, [qg], [BF16], "proj_aq")
        c += WA
        (ak,) = _proj(h, w, c, WA, _ep_headnorm, [kg], [BF16], "proj_ak")
        c += WA
        (av,) = _proj(h, w, c, WA, _ep_identity, [], [BF16], "proj_av")
        c += WA
        (gates,) = _proj(h, w, c, 2 * D, _ep_sigmoid, [], [BF16], "proj_gate")

        o_hg = _hgrn(hq, lf, hk, hi, hg, hg_out_norm_g[l], B, S)
        o_at = _moba(aq, ak, av, rel_bias_table, B, S)
        x1, h2 = _merge(o_hg, o_at, gates, xt, w_branch_hg[l].astype(BF16),
                        w_branch_attn[l].astype(BF16), w_out[l].astype(BF16), ffn_norm_g[l])
        xt = _ffn(h2, x1, w_ffn_gate[l].astype(BF16), w_ffn_up[l].astype(BF16),
                  w_ffn_down[l].astype(BF16))
    return xt.reshape(B, S, D)
```

```python
import functools
import math

import numpy as np
import jax
import jax.numpy as jnp
from jax import lax
from jax.experimental import pallas as pl
from jax.experimental.pallas import tpu as pltpu

F32 = jnp.float32
BF16 = jnp.bfloat16

LANES = 128
NORM_EPS = 1e-6

HG_HEADS = 8
HG_DK = 128
HG_DV = 128
HG_CHUNK = 64
AT_HEADS = 8
AT_DH = 128
MOBA_BLOCK = 256
MOBA_TOPK = 3
REL_BUCKETS = 32
REL_MAX_DIST = 128
MASKED_LOGIT = -1e30

VMEM_LIMIT = 56 * 1024 * 1024


def _cparams(sem):
    return pltpu.CompilerParams(dimension_semantics=sem, vmem_limit_bytes=VMEM_LIMIT)


def _dot(a, b):
    return jnp.dot(a, b, preferred_element_type=F32)


def _dot_nt(a, b):
    return lax.dot_general(a, b, (((1,), (1,)), ((), ())), preferred_element_type=F32)


def _dot_tn(a, b):
    return lax.dot_general(a, b, (((0,), (0,)), ((), ())), preferred_element_type=F32)


def _rmsnorm_kernel(x_ref, g_ref, o_ref):
    x = x_ref[...]
    ms = jnp.mean(x * x, axis=-1, keepdims=True)
    o_ref[...] = (x * lax.rsqrt(ms + NORM_EPS) * g_ref[...]).astype(o_ref.dtype)


def _rmsnorm(x, g, tm=1024):
    T, D = x.shape
    return pl.pallas_call(
        _rmsnorm_kernel,
        grid=(T // tm,),
        in_specs=[pl.BlockSpec((tm, D), lambda i: (i, 0)),
                  pl.BlockSpec((1, D), lambda i: (0, 0))],
        out_specs=pl.BlockSpec((tm, D), lambda i: (i, 0)),
        out_shape=jax.ShapeDtypeStruct((T, D), BF16),
        compiler_params=_cparams(("parallel",)),
        name="rmsnorm",
    )(x, g.reshape(1, D).astype(F32))


def _proj_kernel(*refs, epilogue, n_extra, n_out):
    h_ref, w_ref = refs[0], refs[1]
    extras = refs[2:2 + n_extra]
    outs = refs[2 + n_extra:2 + n_extra + n_out]
    acc = _dot(h_ref[...], w_ref[...])
    res = epilogue(acc, *[e[...] for e in extras])
    for o_ref, r in zip(outs, res):
        o_ref[...] = r.astype(o_ref.dtype)


def _proj(h, w, col0, ncols, epilogue, extras, out_dtypes, name, tm=2048, tn=512):
    T, D = h.shape
    j0 = col0 // tn
    in_specs = [pl.BlockSpec((tm, D), lambda i, j: (i, 0)),
                pl.BlockSpec((D, tn), lambda i, j: (0, j0 + j))]
    in_specs += [pl.BlockSpec((e.shape[0], tn), lambda i, j: (0, j)) for e in extras]
    outs = pl.pallas_call(
        functools.partial(_proj_kernel, epilogue=epilogue, n_extra=len(extras),
                          n_out=len(out_dtypes)),
        grid=(T // tm, ncols // tn),
        in_specs=in_specs,
        out_specs=[pl.BlockSpec((tm, tn), lambda i, j: (i, j)) for _ in out_dtypes],
        out_shape=[jax.ShapeDtypeStruct((T, ncols), dt) for dt in out_dtypes],
        compiler_params=_cparams(("parallel", "arbitrary")),
        name=name,
    )(h, w, *extras)
    return outs


def _ep_identity(acc):
    return (acc,)


def _ep_scale(acc, *, scale):
    return (acc * scale,)


def _ep_silu(acc):
    return (acc * jax.nn.sigmoid(acc),)


def _ep_sigmoid(acc):
    return (jax.nn.sigmoid(acc),)


def _ep_forget(acc, gamma):
    gmax = jnp.max(gamma, axis=0, keepdims=True)
    eg = jnp.exp(gamma - gmax)
    lb = eg[0:1, :] / jnp.sum(eg, axis=0, keepdims=True)
    f = lb + (1.0 - lb) * jax.nn.sigmoid(acc)
    key = (1.0 - lb) * jax.nn.sigmoid(-acc)
    return jnp.log(f), key


def _ep_headnorm(acc, gain, *, scale):
    cols = []
    for c in range(acc.shape[1] // LANES):
        blk = acc[:, c * LANES:(c + 1) * LANES]
        ms = jnp.mean(blk * blk, axis=-1, keepdims=True)
        cols.append(blk * lax.rsqrt(ms + NORM_EPS) * gain[:, c * LANES:(c + 1) * LANES] * scale)
    return (jnp.concatenate(cols, axis=1),)


def _hgrn_tables(C):
    nl = int(math.log2(C))
    assert 1 << nl == C
    G = np.zeros(((nl + 2) * C, C), np.float32)
    masks = np.zeros((nl + 1, C, C), np.float32)
    masks[0] = np.eye(C)
    t = np.arange(C)
    for L in range(1, nl + 1):
        blk, half = 1 << L, 1 << (L - 1)
        base = (t // blk) * blk
        r = base + half - 1
        upper = (t - base) >= half
        for row in range(C):
            if upper[row]:
                G[(L - 1) * C + row, r[row] + 1:row + 1] = 1.0
            else:
                G[(L - 1) * C + row, row + 1:r[row] + 1] = 1.0
        same = base[:, None] == base[None, :]
        masks[L] = (same & upper[:, None] & (~upper)[None, :]).astype(np.float32)
    G[nl * C:(nl + 1) * C] = np.tril(np.ones((C, C)))
    G[(nl + 1) * C:(nl + 2) * C] = np.triu(np.ones((C, C)), 1)
    return G, masks, nl


def _hgrn_kernel(q_ref, lf_ref, k_ref, v_ref, g_ref, gain_ref, G_ref, m_ref, o_ref,
                 e_sc, st_sc, *, C, nl, heads):
    @pl.when(pl.program_id(1) == 0)
    def _():
        st_sc[...] = jnp.zeros_like(st_sc)

    lf = lf_ref[...]
    hi = lf.astype(BF16)
    r1 = lf - hi.astype(F32)
    mid = r1.astype(BF16)
    lo = (r1 - mid.astype(F32)).astype(BF16)
    G = G_ref[...]
    e_sc[...] = jnp.exp(_dot(G, hi) + _dot(G, mid) + _dot(G, lo))

    gain = gain_ref[...]
    for h in range(heads):
        sl = slice(h * HG_DK, (h + 1) * HG_DK)
        qb = q_ref[:, sl]
        kb = k_ref[:, sl]
        vb = v_ref[:, sl]
        qf = qb.astype(F32)
        kf = kb.astype(F32)
        A = m_ref[0] * _dot_nt(qb, kb)
        for L in range(1, nl + 1):
            eL = e_sc[(L - 1) * C:L * C, sl]
            A = A + m_ref[L] * _dot_nt((qf * eL).astype(BF16), (kf * eL).astype(BF16))
        eb = e_sc[nl * C:(nl + 1) * C, sl]
        ek = e_sc[(nl + 1) * C:(nl + 2) * C, sl]
        st = st_sc[h]
        o = _dot(A.astype(BF16), vb) + _dot_nt((qf * eb).astype(BF16), st.astype(BF16))
        st_sc[h] = st * eb[C - 1:C, :] + _dot_tn(vb, (kf * ek).astype(BF16))
        ms = jnp.mean(o * o, axis=-1, keepdims=True)
        y = o * lax.rsqrt(ms + NORM_EPS) * gain * g_ref[:, sl].astype(F32)
        o_ref[:, sl] = y.astype(o_ref.dtype)


def _hgrn(q, lf, k, v, g, gain, B, S):
    T, W = q.shape
    C = HG_CHUNK
    N = S // C
    G, masks, nl = _hgrn_tables(C)
    row = lambda b, c: (b * N + c, 0)
    tok = lambda: pl.BlockSpec((C, W), row)
    return pl.pallas_call(
        functools.partial(_hgrn_kernel, C=C, nl=nl, heads=HG_HEADS),
        grid=(B, N),
        in_specs=[tok(), tok(), tok(), tok(), tok(),
                  pl.BlockSpec((1, HG_DV), lambda b, c: (0, 0)),
                  pl.BlockSpec(G.shape, lambda b, c: (0, 0)),
                  pl.BlockSpec(masks.shape, lambda b, c: (0, 0, 0))],
        out_specs=tok(),
        out_shape=jax.ShapeDtypeStruct((T, W), BF16),
        scratch_shapes=[pltpu.VMEM(((nl + 2) * C, W), F32),
                        pltpu.VMEM((HG_HEADS, HG_DV, HG_DK), F32)],
        compiler_params=_cparams(("parallel", "arbitrary")),
        name="hgrn2",
    )(q, lf, k, v, g, gain.reshape(1, HG_DV).astype(F32),
      jnp.asarray(G, BF16), jnp.asarray(masks, F32))


def _t5_bucket_np(dist):
    n = np.maximum(dist, 0)
    max_exact = REL_BUCKETS // 2
    nf = np.maximum(n, 1).astype(np.float32)
    large = max_exact + (np.log(nf / max_exact) / math.log(REL_MAX_DIST / max_exact)
                         * (REL_BUCKETS - max_exact)).astype(np.int32)
    large = np.minimum(large, REL_BUCKETS - 1)
    return np.where(n < max_exact, n, large).astype(np.int32)


def _bias_kernel(tab_ref, bucket_ref, o_ref):
    h = pl.program_id(0)
    bucket = bucket_ref[...]
    acc = jnp.zeros(bucket.shape, F32)
    for b in range(REL_BUCKETS):
        acc = jnp.where(bucket == b, tab_ref[b, h], acc)
    delta = acc - tab_ref[REL_BUCKETS - 1, h]
    blk = bucket.shape[0]
    o_ref[0, 0] = delta
    o_ref[0, 1] = jnp.concatenate([delta[:, blk:], jnp.zeros((blk, blk), F32)], axis=1)


def _bias_tiles(rel_table):
    BLK = MOBA_BLOCK
    t = np.arange(BLK)[:, None]
    s = np.arange(2 * BLK)[None, :]
    bucket = _t5_bucket_np(t + BLK - s)
    H = rel_table.shape[1]
    return pl.pallas_call(
        _bias_kernel,
        grid=(H,),
        in_specs=[pl.BlockSpec(memory_space=pltpu.SMEM),
                  pl.BlockSpec(bucket.shape, lambda h: (0, 0))],
        out_specs=pl.BlockSpec((1, 2, BLK, 2 * BLK), lambda h: (h, 0, 0, 0)),
        out_shape=jax.ShapeDtypeStruct((H, 2, BLK, 2 * BLK), F32),
        compiler_params=_cparams(("arbitrary",)),
        name="relbias",
    )(rel_table.astype(F32), jnp.asarray(bucket))


def _moba_kernel(q_ref, k_ref, v_ref, dbias_ref, far_ref, o_ref,
                 kmean_sc, kaug_sc, vaug_sc, near_sc, far_sc, acc_sc, *, NB):
    BLK = MOBA_BLOCK
    W2 = 2 * BLK
    h = pl.program_id(1)
    i = pl.program_id(2)

    @pl.when(i == 0)
    def _():
        far = far_ref[pl.ds(h, 1), :]
        far_hi = far.astype(BF16).astype(F32)
        far_lo = far - far_hi
        lane = lax.broadcasted_iota(jnp.int32, (1, LANES), 1)
        tail = jnp.where(lane == NB, far_hi, jnp.where(lane == NB + 1, far_lo, 0.0))
        for n in range(NB):
            kb = k_ref[n * BLK:(n + 1) * BLK, :]
            kmean_sc[n:n + 1, :] = jnp.mean(kb.astype(F32), axis=0, keepdims=True)
            extra = jnp.where(lane == n, 1.0, tail)
            kaug_sc[n * BLK:(n + 1) * BLK, 0:AT_DH] = kb
            kaug_sc[n * BLK:(n + 1) * BLK, AT_DH:] = jnp.broadcast_to(
                extra, (BLK, LANES)).astype(BF16)
        vaug_sc[:, 0:AT_DH] = v_ref[...]
        vaug_sc[:, AT_DH:] = jnp.ones((vaug_sc.shape[0], LANES), BF16)

    q = q_ref[...]

    gate = lax.dot_general(kmean_sc[...], q.astype(F32), (((1,), (1,)), ((), ())),
                           precision=lax.Precision.HIGHEST, preferred_element_type=F32)
    blk = lax.broadcasted_iota(jnp.int32, (NB, BLK), 0)
    valid = blk < i
    gate = jnp.where(valid, gate, -jnp.inf)
    rank = jnp.zeros((NB, BLK), jnp.int32)
    for m in range(NB):
        gm = gate[m:m + 1, :]
        beats = (gm > gate) | ((gm == gate) & (blk > m))
        rank = rank + beats.astype(jnp.int32)
    sel = (valid & (rank < MOBA_TOPK)) | (blk == i)
    top = jnp.where(sel, 0.0, MASKED_LOGIT)
    rest = (lax.broadcasted_iota(jnp.int32, (LANES - NB, BLK), 0) < 2).astype(F32)
    aug = jnp.concatenate([top, rest], axis=0).T
    lane = lax.broadcasted_iota(jnp.int32, (BLK, LANES), 1)
    aug_far = jnp.where((lane == i) | (lane == i - 1), MASKED_LOGIT, aug)
    q_near = jnp.concatenate([q, aug.astype(BF16)], axis=1)
    q_far = jnp.concatenate([q, aug_far.astype(BF16)], axis=1)

    def lane_max(s):
        out = s[:, 0:LANES]
        for c in range(1, s.shape[1] // LANES):
            out = jnp.maximum(out, s[:, c * LANES:(c + 1) * LANES])
        return out

    def probs(s, m_b):
        return jnp.concatenate(
            [jnp.exp(s[:, c * LANES:(c + 1) * LANES] - m_b) for c in range(s.shape[1] // LANES)],
            axis=1).astype(BF16)

    a = jnp.maximum(i - 1, 0)
    near0 = pl.multiple_of(a * BLK, BLK)
    s = _dot_nt(q_near, kaug_sc[pl.ds(near0, W2), :])
    delta = dbias_ref[0, jnp.where(i == 0, 1, 0)]
    r = lax.broadcasted_iota(jnp.int32, (BLK, W2), 0)
    c = lax.broadcasted_iota(jnp.int32, (BLK, W2), 1)
    s = jnp.where(c - r <= (i - a) * BLK, s + delta, MASKED_LOGIT)
    near_sc[...] = s

    npairs = jnp.maximum(i, 1) // 2

    def scores(jj, mx):
        s = _dot_nt(q_far, kaug_sc[pl.ds(pl.multiple_of(jj * W2, W2), W2), :])
        far_sc[jj] = s
        return jnp.maximum(mx, lane_max(s))

    mx = lax.fori_loop(0, npairs, scores, lane_max(s))
    m_b = jnp.broadcast_to(jnp.max(mx, axis=-1, keepdims=True), (BLK, LANES))

    acc_sc[...] = _dot(probs(near_sc[...], m_b), vaug_sc[pl.ds(near0, W2), :])

    def accumulate(jj, carry):
        acc_sc[...] += _dot(probs(far_sc[jj], m_b),
                            vaug_sc[pl.ds(pl.multiple_of(jj * W2, W2), W2), :])
        return carry

    lax.fori_loop(0, npairs, accumulate, 0)
    o_ref[...] = (acc_sc[:, 0:AT_DH] / acc_sc[:, AT_DH:]).astype(o_ref.dtype)


def _moba(q, k, v, rel_table, B, S):
    T, W = q.shape
    H = AT_HEADS
    NB = S // MOBA_BLOCK
    BLK = MOBA_BLOCK
    assert NB + 2 <= LANES and NB % 2 == 0
    assert int(_t5_bucket_np(np.array([BLK + 1]))[0]) == REL_BUCKETS - 1
    dbias = _bias_tiles(rel_table)
    far = jnp.broadcast_to(rel_table[REL_BUCKETS - 1, :].astype(F32)[:, None], (H, LANES))
    return pl.pallas_call(
        functools.partial(_moba_kernel, NB=NB),
        grid=(B, H, NB),
        in_specs=[pl.BlockSpec((BLK, AT_DH), lambda b, h, i: (b * NB + i, h)),
                  pl.BlockSpec((S, AT_DH), lambda b, h, i: (b, h)),
                  pl.BlockSpec((S, AT_DH), lambda b, h, i: (b, h)),
                  pl.BlockSpec((1, 2, BLK, 2 * BLK), lambda b, h, i: (h, 0, 0, 0)),
                  pl.BlockSpec((H, LANES), lambda b, h, i: (0, 0))],
        out_specs=pl.BlockSpec((BLK, AT_DH), lambda b, h, i: (b * NB + i, h)),
        out_shape=jax.ShapeDtypeStruct((T, W), BF16),
        scratch_shapes=[pltpu.VMEM((NB, AT_DH), F32),
                        pltpu.VMEM((S, AT_DH + LANES), BF16),
                        pltpu.VMEM((S, AT_DH + LANES), BF16),
                        pltpu.VMEM((BLK, 2 * BLK), F32),
                        pltpu.VMEM((NB // 2, BLK, 2 * BLK), F32),
                        pltpu.VMEM((BLK, AT_DH + LANES), F32)],
        compiler_params=_cparams(("parallel", "parallel", "arbitrary")),
        name="moba",
    )(q, k, v, dbias, far)


def _merge_kernel(ohg_ref, oat_ref, gate_ref, x_ref, whg_ref, wat_ref, wout_ref, g2_ref,
                  x1_ref, h2_ref, *, D):
    y_hg = _dot(ohg_ref[...], whg_ref[...])
    y_at = _dot(oat_ref[...], wat_ref[...])
    merged = (gate_ref[:, 0:D].astype(F32) * y_hg + gate_ref[:, D:2 * D].astype(F32) * y_at)
    x1 = x_ref[...] + _dot(merged.astype(BF16), wout_ref[...])
    x1_ref[...] = x1
    ms = jnp.mean(x1 * x1, axis=-1, keepdims=True)
    h2_ref[...] = (x1 * lax.rsqrt(ms + NORM_EPS) * g2_ref[...]).astype(h2_ref.dtype)


def _merge(ohg, oat, gates, x, whg, wat, wout, g2, tm=512):
    T, D = x.shape
    tok = lambda w: pl.BlockSpec((tm, w), lambda i: (i, 0))
    full = lambda a: pl.BlockSpec(a.shape, lambda i: (0, 0))
    return pl.pallas_call(
        functools.partial(_merge_kernel, D=D),
        grid=(T // tm,),
        in_specs=[tok(ohg.shape[1]), tok(oat.shape[1]), tok(2 * D), tok(D),
                  full(whg), full(wat), full(wout), pl.BlockSpec((1, D), lambda i: (0, 0))],
        out_specs=[tok(D), tok(D)],
        out_shape=[jax.ShapeDtypeStruct((T, D), F32), jax.ShapeDtypeStruct((T, D), BF16)],
        compiler_params=_cparams(("parallel",)),
        name="merge",
    )(ohg, oat, gates, x, whg, wat, wout, g2.reshape(1, D).astype(F32))


def _ffn_kernel(h_ref, x_ref, wg_ref, wu_ref, wd_ref, o_ref, acc_sc):
    f = pl.program_id(1)

    @pl.when(f == 0)
    def _():
        acc_sc[...] = x_ref[...]

    h = h_ref[...]
    a = _dot(h, wg_ref[...])
    u = _dot(h, wu_ref[...])
    acc_sc[...] += _dot((a * jax.nn.sigmoid(a) * u).astype(BF16), wd_ref[...])

    @pl.when(f == pl.num_programs(1) - 1)
    def _():
        o_ref[...] = acc_sc[...]


def _ffn(h2, x1, wg, wu, wd, tm=1024, tf=256):
    T, D = x1.shape
    FF = wg.shape[1]
    return pl.pallas_call(
        _ffn_kernel,
        grid=(T // tm, FF // tf),
        in_specs=[pl.BlockSpec((tm, D), lambda i, f: (i, 0)),
                  pl.BlockSpec((tm, D), lambda i, f: (i, 0)),
                  pl.BlockSpec((D, tf), lambda i, f: (0, f)),
                  pl.BlockSpec((D, tf), lambda i, f: (0, f)),
                  pl.BlockSpec((tf, D), lambda i, f: (f, 0))],
        out_specs=pl.BlockSpec((tm, D), lambda i, f: (i, 0)),
        out_shape=jax.ShapeDtypeStruct((T, D), F32),
        scratch_shapes=[pltpu.VMEM((tm, D), F32)],
        compiler_params=_cparams(("parallel", "arbitrary")),
        name="ffn",
    )(h2, x1, wg, wu, wd)


def kernel(x, attn_norm_g, w_in, hg_lb_gamma, hg_out_norm_g, q_norm_g, k_norm_g, rel_bias_table,
           w_branch_hg, w_branch_attn, w_out, ffn_norm_g, w_ffn_gate, w_ffn_up, w_ffn_down):
    B, S, D = x.shape
    T = B * S
    depth = attn_norm_g.shape[0]
    assert depth == 1 and S % MOBA_BLOCK == 0 and S % HG_CHUNK == 0
    WH = HG_HEADS * HG_DK
    WV = HG_HEADS * HG_DV
    WA = AT_HEADS * AT_DH
    assert w_in.shape[2] == 2 * WH + 2 * WV + 3 * WA + 2 * D

    xt = x.reshape(T, D)
    for l in range(depth):
        w = w_in[l].astype(BF16)
        h = _rmsnorm(xt, attn_norm_g[l])
        c = 0
        (hq,) = _proj(h, w, c, WH, functools.partial(_ep_scale, scale=HG_DK ** -0.5), [],
                      [BF16], "proj_hq")
        c += WH
        lf, hk = _proj(h, w, c, WH, _ep_forget, [hg_lb_gamma.astype(F32)], [F32, BF16],
                       "proj_hf")
        c += WH
        (hi,) = _proj(h, w, c, WV, _ep_identity, [], [BF16], "proj_hi")
        c += WV
        (hg,) = _proj(h, w, c, WV, _ep_silu, [], [BF16], "proj_hg")
        c += WV
        qg = jnp.tile(q_norm_g[l].astype(F32), AT_HEADS).reshape(1, WA)
        kg = jnp.tile(k_norm_g[l].astype(F32), AT_HEADS).reshape(1, WA)
        (aq,) = _proj(h, w, c, WA, functools.partial(_ep_headnorm, scale=AT_DH ** -0.5), [qg],
                      [BF16], "proj_aq")
        c += WA
        (ak,) = _proj(h, w, c, WA, functools.partial(_ep_headnorm, scale=1.0), [kg], [BF16],
                      "proj_ak")
        c += WA
        (av,) = _proj(h, w, c, WA, _ep_identity, [], [BF16], "proj_av")
        c += WA
        (gates,) = _proj(h, w, c, 2 * D, _ep_sigmoid, [], [BF16], "proj_gate")

        o_hg = _hgrn(hq, lf, hk, hi, hg, hg_out_norm_g[l], B, S)
        o_at = _moba(aq, ak, av, rel_bias_table, B, S)
        x1, h2 = _merge(o_hg, o_at, gates, xt, w_branch_hg[l].astype(BF16),
                        w_branch_attn[l].astype(BF16), w_out[l].astype(BF16), ffn_norm_g[l])
        xt = _ffn(h2, x1, w_ffn_gate[l].astype(BF16), w_ffn_up[l].astype(BF16),
                  w_ffn_down[l].astype(BF16))
    return xt.reshape(B, S, D)
```

```python
import functools
import math

import numpy as np
import jax
import jax.numpy as jnp
from jax import lax
from jax.experimental import pallas as pl
from jax.experimental.pallas import tpu as pltpu

F32 = jnp.float32
BF16 = jnp.bfloat16

LANES = 128
NORM_EPS = 1e-6

HG_HEADS = 8
HG_DK = 128
HG_DV = 128
HG_CHUNK = 64
AT_HEADS = 8
AT_DH = 128
MOBA_BLOCK = 256
MOBA_TOPK = 3
REL_BUCKETS = 32
REL_MAX_DIST = 128
MASKED_LOGIT = -1e30

VMEM_LIMIT = 56 * 1024 * 1024


def _cparams(sem):
    return pltpu.CompilerParams(dimension_semantics=sem, vmem_limit_bytes=VMEM_LIMIT)


def _dot(a, b):
    return jnp.dot(a, b, preferred_element_type=F32)


def _dot_nt(a, b):
    return lax.dot_general(a, b, (((1,), (1,)), ((), ())), preferred_element_type=F32)


def _dot_tn(a, b):
    return lax.dot_general(a, b, (((0,), (0,)), ((), ())), preferred_element_type=F32)


def _rmsnorm_kernel(x_ref, g_ref, o_ref):
    x = x_ref[...]
    ms = jnp.mean(x * x, axis=-1, keepdims=True)
    o_ref[...] = (x * lax.rsqrt(ms + NORM_EPS) * g_ref[...]).astype(o_ref.dtype)


def _rmsnorm(x, g, tm=1024):
    T, D = x.shape
    return pl.pallas_call(
        _rmsnorm_kernel,
        grid=(T // tm,),
        in_specs=[pl.BlockSpec((tm, D), lambda i: (i, 0)),
                  pl.BlockSpec((1, D), lambda i: (0, 0))],
        out_specs=pl.BlockSpec((tm, D), lambda i: (i, 0)),
        out_shape=jax.ShapeDtypeStruct((T, D), BF16),
        compiler_params=_cparams(("parallel",)),
        name="rmsnorm",
    )(x, g.reshape(1, D).astype(F32))


def _proj_kernel(*refs, epilogue, n_extra, n_out):
    h_ref, w_ref = refs[0], refs[1]
    extras = refs[2:2 + n_extra]
    outs = refs[2 + n_extra:2 + n_extra + n_out]
    acc = _dot(h_ref[...], w_ref[...])
    res = epilogue(acc, *[e[...] for e in extras])
    for o_ref, r in zip(outs, res):
        o_ref[...] = r.astype(o_ref.dtype)


def _proj(h, w, col0, ncols, epilogue, extras, out_dtypes, name, tm=2048, tn=512):
    T, D = h.shape
    j0 = col0 // tn
    in_specs = [pl.BlockSpec((tm, D), lambda i, j: (i, 0)),
                pl.BlockSpec((D, tn), lambda i, j: (0, j0 + j))]
    in_specs += [pl.BlockSpec((e.shape[0], tn), lambda i, j: (0, j)) for e in extras]
    outs = pl.pallas_call(
        functools.partial(_proj_kernel, epilogue=epilogue, n_extra=len(extras),
                          n_out=len(out_dtypes)),
        grid=(T // tm, ncols // tn),
        in_specs=in_specs,
        out_specs=[pl.BlockSpec((tm, tn), lambda i, j: (i, j)) for _ in out_dtypes],
        out_shape=[jax.ShapeDtypeStruct((T, ncols), dt) for dt in out_dtypes],
        compiler_params=_cparams(("parallel", "arbitrary")),
        name=name,
    )(h, w, *extras)
    return outs


def _ep_identity(acc):
    return (acc,)


def _ep_scale(acc, *, scale):
    return (acc * scale,)


def _ep_silu(acc):
    return (acc * jax.nn.sigmoid(acc),)


def _ep_sigmoid(acc):
    return (jax.nn.sigmoid(acc),)


def _ep_forget(acc, gamma):
    gmax = jnp.max(gamma, axis=0, keepdims=True)
    eg = jnp.exp(gamma - gmax)
    lb = eg[0:1, :] / jnp.sum(eg, axis=0, keepdims=True)
    f = lb + (1.0 - lb) * jax.nn.sigmoid(acc)
    key = (1.0 - lb) * jax.nn.sigmoid(-acc)
    return jnp.log(f), key


def _ep_headnorm(acc, gain, *, scale):
    cols = []
    for c in range(acc.shape[1] // LANES):
        blk = acc[:, c * LANES:(c + 1) * LANES]
        ms = jnp.mean(blk * blk, axis=-1, keepdims=True)
        cols.append(blk * lax.rsqrt(ms + NORM_EPS) * gain[:, c * LANES:(c + 1) * LANES] * scale)
    return (jnp.concatenate(cols, axis=1),)


def _hgrn_tables(C):
    nl = int(math.log2(C))
    assert 1 << nl == C
    G = np.zeros(((nl + 2) * C, C), np.float32)
    masks = np.zeros((nl + 1, C, C), np.float32)
    masks[0] = np.eye(C)
    t = np.arange(C)
    for L in range(1, nl + 1):
        blk, half = 1 << L, 1 << (L - 1)
        base = (t // blk) * blk
        r = base + half - 1
        upper = (t - base) >= half
        for row in range(C):
            if upper[row]:
                G[(L - 1) * C + row, r[row] + 1:row + 1] = 1.0
            else:
                G[(L - 1) * C + row, row + 1:r[row] + 1] = 1.0
        same = base[:, None] == base[None, :]
        masks[L] = (same & upper[:, None] & (~upper)[None, :]).astype(np.float32)
    G[nl * C:(nl + 1) * C] = np.tril(np.ones((C, C)))
    G[(nl + 1) * C:(nl + 2) * C] = np.triu(np.ones((C, C)), 1)
    return G, masks, nl


def _hgrn_kernel(q_ref, lf_ref, k_ref, v_ref, g_ref, gain_ref, G_ref, m_ref, o_ref,
                 e_sc, st_sc, *, C, nl, heads):
    @pl.when(pl.program_id(1) == 0)
    def _():
        st_sc[...] = jnp.zeros_like(st_sc)

    lf = lf_ref[...]
    hi = lf.astype(BF16)
    r1 = lf - hi.astype(F32)
    mid = r1.astype(BF16)
    lo = (r1 - mid.astype(F32)).astype(BF16)
    G = G_ref[...]
    e_sc[...] = jnp.exp(_dot(G, hi) + _dot(G, mid) + _dot(G, lo))

    gain = gain_ref[...]
    for h in range(heads):
        sl = slice(h * HG_DK, (h + 1) * HG_DK)
        qb = q_ref[:, sl]
        kb = k_ref[:, sl]
        vb = v_ref[:, sl]
        qf = qb.astype(F32)
        kf = kb.astype(F32)
        A = m_ref[0] * _dot_nt(qb, kb)
        for L in range(1, nl + 1):
            eL = e_sc[(L - 1) * C:L * C, sl]
            A = A + m_ref[L] * _dot_nt((qf * eL).astype(BF16), (kf * eL).astype(BF16))
        eb = e_sc[nl * C:(nl + 1) * C, sl]
        ek = e_sc[(nl + 1) * C:(nl + 2) * C, sl]
        st = st_sc[h]
        o = _dot(A.astype(BF16), vb) + _dot_nt((qf * eb).astype(BF16), st.astype(BF16))
        st_sc[h] = st * eb[C - 1:C, :] + _dot_tn(vb, (kf * ek).astype(BF16))
        ms = jnp.mean(o * o, axis=-1, keepdims=True)
        y = o * lax.rsqrt(ms + NORM_EPS) * gain * g_ref[:, sl].astype(F32)
        o_ref[:, sl] = y.astype(o_ref.dtype)


def _hgrn(q, lf, k, v, g, gain, B, S):
    T, W = q.shape
    C = HG_CHUNK
    N = S // C
    G, masks, nl = _hgrn_tables(C)
    row = lambda b, c: (b * N + c, 0)
    tok = lambda: pl.BlockSpec((C, W), row)
    return pl.pallas_call(
        functools.partial(_hgrn_kernel, C=C, nl=nl, heads=HG_HEADS),
        grid=(B, N),
        in_specs=[tok(), tok(), tok(), tok(), tok(),
                  pl.BlockSpec((1, HG_DV), lambda b, c: (0, 0)),
                  pl.BlockSpec(G.shape, lambda b, c: (0, 0)),
                  pl.BlockSpec(masks.shape, lambda b, c: (0, 0, 0))],
        out_specs=tok(),
        out_shape=jax.ShapeDtypeStruct((T, W), BF16),
        scratch_shapes=[pltpu.VMEM(((nl + 2) * C, W), F32),
                        pltpu.VMEM((HG_HEADS, HG_DV, HG_DK), F32)],
        compiler_params=_cparams(("parallel", "arbitrary")),
        name="hgrn2",
    )(q, lf, k, v, g, gain.reshape(1, HG_DV).astype(F32),
      jnp.asarray(G, BF16), jnp.asarray(masks, F32))


def _t5_bucket_np(dist):
    n = np.maximum(dist, 0)
    max_exact = REL_BUCKETS // 2
    nf = np.maximum(n, 1).astype(np.float32)
    large = max_exact + (np.log(nf / max_exact) / math.log(REL_MAX_DIST / max_exact)
                         * (REL_BUCKETS - max_exact)).astype(np.int32)
    large = np.minimum(large, REL_BUCKETS - 1)
    return np.where(n < max_exact, n, large).astype(np.int32)


def _bias_kernel(tab_ref, bucket_ref, o_ref):
    h = pl.program_id(0)
    bucket = bucket_ref[...]
    acc = jnp.zeros(bucket.shape, F32)
    for b in range(REL_BUCKETS):
        acc = jnp.where(bucket == b, tab_ref[b, h], acc)
    delta = acc - tab_ref[REL_BUCKETS - 1, h]
    blk = bucket.shape[0]
    r = lax.broadcasted_iota(jnp.int32, bucket.shape, 0)
    c = lax.broadcasted_iota(jnp.int32, bucket.shape, 1)
    o_ref[0] = jnp.where(c - blk <= r, delta, MASKED_LOGIT)


def _bias_tiles(rel_table):
    BLK = MOBA_BLOCK
    t = np.arange(BLK)[:, None]
    s = np.arange(2 * BLK)[None, :]
    bucket = _t5_bucket_np(t + BLK - s)
    H = rel_table.shape[1]
    return pl.pallas_call(
        _bias_kernel,
        grid=(H,),
        in_specs=[pl.BlockSpec(memory_space=pltpu.SMEM),
                  pl.BlockSpec(bucket.shape, lambda h: (0, 0))],
        out_specs=pl.BlockSpec((1, BLK, 2 * BLK), lambda h: (h, 0, 0)),
        out_shape=jax.ShapeDtypeStruct((H, BLK, 2 * BLK), F32),
        compiler_params=_cparams(("arbitrary",)),
        name="relbias",
    )(rel_table.astype(F32), jnp.asarray(bucket))


def _moba_kernel(q_ref, k_ref, v_ref, dbias_ref, far_ref, o_ref,
                 qnear_sc, qfar_sc, kaug_sc, vaug_sc, near_sc, far_sc, acc_sc, *, NB):
    BLK = MOBA_BLOCK
    W2 = 2 * BLK
    PAD = NB + 2
    NS = NB // 2 - 1
    h = pl.program_id(1)
    ip = pl.program_id(2)

    @pl.when(ip == 0)
    def _():
        far = far_ref[pl.ds(h, 1), :]
        far_hi = far.astype(BF16).astype(F32)
        far_lo = far - far_hi
        lane = lax.broadcasted_iota(jnp.int32, (1, LANES), 1)
        tail = jnp.where(lane == NB, far_hi, jnp.where(lane == NB + 1, far_lo, 0.0))
        kaug_sc[0:BLK, 0:AT_DH] = jnp.zeros((BLK, AT_DH), BF16)
        kaug_sc[0:BLK, AT_DH:] = jnp.broadcast_to(
            jnp.where(lane == PAD, 1.0, 0.0), (BLK, LANES)).astype(BF16)
        vaug_sc[0:BLK, :] = jnp.zeros((BLK, AT_DH + LANES), BF16)
        kms = []
        for n in range(NB):
            kb = k_ref[n * BLK:(n + 1) * BLK, :]
            kms.append(jnp.mean(kb.astype(F32), axis=0, keepdims=True))
            kaug_sc[(n + 1) * BLK:(n + 2) * BLK, 0:AT_DH] = kb
            kaug_sc[(n + 1) * BLK:(n + 2) * BLK, AT_DH:] = jnp.broadcast_to(
                jnp.where(lane == n, 1.0, tail), (BLK, LANES)).astype(BF16)
        vaug_sc[BLK:, 0:AT_DH] = v_ref[...]
        vaug_sc[BLK:, AT_DH:] = jnp.ones((NB * BLK, LANES), BF16)

        kmean = jnp.concatenate(kms, axis=0)
        km_hi = kmean.astype(BF16)
        km_lo = (kmean - km_hi.astype(F32)).astype(BF16)
        blk = lax.broadcasted_iota(jnp.int32, (NB, BLK), 0)
        rowid = lax.broadcasted_iota(jnp.int32, (LANES - NB, BLK), 0) + NB
        rest = jnp.where(rowid < NB + 2, 1.0, jnp.where(rowid == PAD, MASKED_LOGIT, 0.0))
        for t in range(NB):
            qt = q_ref[t * BLK:(t + 1) * BLK, :]
            gate = jnp.where(blk < t, _dot_nt(km_hi, qt) + _dot_nt(km_lo, qt), -jnp.inf)
            rank = jnp.zeros((NB, BLK), jnp.int32)
            for m in range(t):
                gm = gate[m:m + 1, :]
                rank = rank + ((gm > gate) | ((gm == gate) & (blk > m))).astype(jnp.int32)
            sel = ((blk < t) & (rank < MOBA_TOPK)) | (blk == t)
            near = jnp.where(sel, 0.0, MASKED_LOGIT)
            older = jnp.where(blk >= t - 1, MASKED_LOGIT, near)
            rows = slice(t * BLK, (t + 1) * BLK)
            qnear_sc[rows, 0:AT_DH] = qt
            qfar_sc[rows, 0:AT_DH] = qt
            qnear_sc[rows, AT_DH:] = jnp.concatenate([near, rest], axis=0).T.astype(BF16)
            qfar_sc[rows, AT_DH:] = jnp.concatenate([older, rest], axis=0).T.astype(BF16)

    def lane_max(s):
        out = s[:, 0:LANES]
        for c in range(1, s.shape[1] // LANES):
            out = jnp.maximum(out, s[:, c * LANES:(c + 1) * LANES])
        return out

    def probs(s, m_b):
        return jnp.concatenate(
            [jnp.exp(s[:, c * LANES:(c + 1) * LANES] - m_b) for c in range(s.shape[1] // LANES)],
            axis=1).astype(BF16)

    blocks = (ip, NB - 1 - ip)
    rows = [pl.multiple_of(b * BLK, BLK) for b in blocks]
    n_first = jnp.maximum(ip, 1) // 2
    visits = []
    for st in range(NS):
        second = st >= n_first
        jj = jnp.where(second, st - n_first, st)
        visits.append((second, jnp.where(second, rows[1], rows[0]),
                       pl.multiple_of(jj * W2 + BLK, BLK)))

    mx = []
    for slot in range(2):
        s = (_dot_nt(qnear_sc[pl.ds(rows[slot], BLK), :], kaug_sc[pl.ds(rows[slot], W2), :])
             + dbias_ref[0])
        near_sc[slot] = s
        mx.append(lane_max(s))
    for st, (second, qrow, krow) in enumerate(visits):
        s = _dot_nt(qfar_sc[pl.ds(qrow, BLK), :], kaug_sc[pl.ds(krow, W2), :])
        far_sc[st] = s
        lm = lane_max(s)
        mx[0] = jnp.maximum(mx[0], jnp.where(second, MASKED_LOGIT, lm))
        mx[1] = jnp.maximum(mx[1], jnp.where(second, lm, MASKED_LOGIT))

    m_b = [jnp.broadcast_to(jnp.max(m, axis=-1, keepdims=True), (BLK, LANES)) for m in mx]
    for slot in range(2):
        acc_sc[slot] = _dot(probs(near_sc[slot], m_b[slot]), vaug_sc[pl.ds(rows[slot], W2), :])
    for st, (second, qrow, krow) in enumerate(visits):
        p = probs(far_sc[st], jnp.where(second, m_b[1], m_b[0]))
        acc_sc[second.astype(jnp.int32)] += _dot(p, vaug_sc[pl.ds(krow, W2), :])
    for slot in range(2):
        o_ref[pl.ds(rows[slot], BLK), :] = (
            acc_sc[slot, :, 0:AT_DH] / acc_sc[slot, :, AT_DH:]).astype(o_ref.dtype)


def _moba(q, k, v, rel_table, B, S):
    T, W = q.shape
    H = AT_HEADS
    NB = S // MOBA_BLOCK
    BLK = MOBA_BLOCK
    assert NB + 3 <= LANES and NB % 2 == 0
    assert int(_t5_bucket_np(np.array([BLK + 1]))[0]) == REL_BUCKETS - 1
    dbias = _bias_tiles(rel_table)
    far = jnp.broadcast_to(rel_table[REL_BUCKETS - 1, :].astype(F32)[:, None], (H, LANES))
    head = lambda: pl.BlockSpec((S, AT_DH), lambda b, h, i: (b, h))
    return pl.pallas_call(
        functools.partial(_moba_kernel, NB=NB),
        grid=(B, H, NB // 2),
        in_specs=[head(), head(), head(),
                  pl.BlockSpec((1, BLK, 2 * BLK), lambda b, h, i: (h, 0, 0)),
                  pl.BlockSpec((H, LANES), lambda b, h, i: (0, 0))],
        out_specs=head(),
        out_shape=jax.ShapeDtypeStruct((T, W), BF16),
        scratch_shapes=[pltpu.VMEM((S, AT_DH + LANES), BF16),
                        pltpu.VMEM((S, AT_DH + LANES), BF16),
                        pltpu.VMEM((S + BLK, AT_DH + LANES), BF16),
                        pltpu.VMEM((S + BLK, AT_DH + LANES), BF16),
                        pltpu.VMEM((2, BLK, 2 * BLK), F32),
                        pltpu.VMEM((max(NB // 2 - 1, 1), BLK, 2 * BLK), F32),
                        pltpu.VMEM((2, BLK, AT_DH + LANES), F32)],
        compiler_params=_cparams(("parallel", "parallel", "arbitrary")),
        name="moba",
    )(q, k, v, dbias, far)


def _merge_kernel(ohg_ref, oat_ref, gate_ref, x_ref, whg_ref, wat_ref, wout_ref, g2_ref,
                  x1_ref, h2_ref, *, D):
    y_hg = _dot(ohg_ref[...], whg_ref[...])
    y_at = _dot(oat_ref[...], wat_ref[...])
    merged = (gate_ref[:, 0:D].astype(F32) * y_hg + gate_ref[:, D:2 * D].astype(F32) * y_at)
    x1 = x_ref[...] + _dot(merged.astype(BF16), wout_ref[...])
    x1_ref[...] = x1
    ms = jnp.mean(x1 * x1, axis=-1, keepdims=True)
    h2_ref[...] = (x1 * lax.rsqrt(ms + NORM_EPS) * g2_ref[...]).astype(h2_ref.dtype)


def _merge(ohg, oat, gates, x, whg, wat, wout, g2, tm=512):
    T, D = x.shape
    tok = lambda w: pl.BlockSpec((tm, w), lambda i: (i, 0))
    full = lambda a: pl.BlockSpec(a.shape, lambda i: (0, 0))
    return pl.pallas_call(
        functools.partial(_merge_kernel, D=D),
        grid=(T // tm,),
        in_specs=[tok(ohg.shape[1]), tok(oat.shape[1]), tok(2 * D), tok(D),
                  full(whg), full(wat), full(wout), pl.BlockSpec((1, D), lambda i: (0, 0))],
        out_specs=[tok(D), tok(D)],
        out_shape=[jax.ShapeDtypeStruct((T, D), F32), jax.ShapeDtypeStruct((T, D), BF16)],
        compiler_params=_cparams(("parallel",)),
        name="merge",
    )(ohg, oat, gates, x, whg, wat, wout, g2.reshape(1, D).astype(F32))


def _ffn_kernel(h_ref, x_ref, wg_ref, wu_ref, wd_ref, o_ref, acc_sc):
    f = pl.program_id(1)

    @pl.when(f == 0)
    def _():
        acc_sc[...] = x_ref[...]

    h = h_ref[...]
    a = _dot(h, wg_ref[...])
    u = _dot(h, wu_ref[...])
    acc_sc[...] += _dot((a * jax.nn.sigmoid(a) * u).astype(BF16), wd_ref[...])

    @pl.when(f == pl.num_programs(1) - 1)
    def _():
        o_ref[...] = acc_sc[...]


def _ffn(h2, x1, wg, wu, wd, tm=1024, tf=256):
    T, D = x1.shape
    FF = wg.shape[1]
    return pl.pallas_call(
        _ffn_kernel,
        grid=(T // tm, FF // tf),
        in_specs=[pl.BlockSpec((tm, D), lambda i, f: (i, 0)),
                  pl.BlockSpec((tm, D), lambda i, f: (i, 0)),
                  pl.BlockSpec((D, tf), lambda i, f: (0, f)),
                  pl.BlockSpec((D, tf), lambda i, f: (0, f)),
                  pl.BlockSpec((tf, D), lambda i, f: (f, 0))],
        out_specs=pl.BlockSpec((tm, D), lambda i, f: (i, 0)),
        out_shape=jax.ShapeDtypeStruct((T, D), F32),
        scratch_shapes=[pltpu.VMEM((tm, D), F32)],
        compiler_params=_cparams(("parallel", "arbitrary")),
        name="ffn",
    )(h2, x1, wg, wu, wd)


def kernel(x, attn_norm_g, w_in, hg_lb_gamma, hg_out_norm_g, q_norm_g, k_norm_g, rel_bias_table,
           w_branch_hg, w_branch_attn, w_out, ffn_norm_g, w_ffn_gate, w_ffn_up, w_ffn_down):
    B, S, D = x.shape
    T = B * S
    depth = attn_norm_g.shape[0]
    assert depth == 1 and S % MOBA_BLOCK == 0 and S % HG_CHUNK == 0
    WH = HG_HEADS * HG_DK
    WV = HG_HEADS * HG_DV
    WA = AT_HEADS * AT_DH
    assert w_in.shape[2] == 2 * WH + 2 * WV + 3 * WA + 2 * D

    xt = x.reshape(T, D)
    for l in range(depth):
        w = w_in[l].astype(BF16)
        h = _rmsnorm(xt, attn_norm_g[l])
        c = 0
        (hq,) = _proj(h, w, c, WH, functools.partial(_ep_scale, scale=HG_DK ** -0.5), [],
                      [BF16], "proj_hq")
        c += WH
        lf, hk = _proj(h, w, c, WH, _ep_forget, [hg_lb_gamma.astype(F32)], [F32, BF16],
                       "proj_hf")
        c += WH
        (hi,) = _proj(h, w, c, WV, _ep_identity, [], [BF16], "proj_hi")
        c += WV
        (hg,) = _proj(h, w, c, WV, _ep_silu, [], [BF16], "proj_hg")
        c += WV
        qg = jnp.tile(q_norm_g[l].astype(F32), AT_HEADS).reshape(1, WA)
        kg = jnp.tile(k_norm_g[l].astype(F32), AT_HEADS).reshape(1, WA)
        (aq,) = _proj(h, w, c, WA, functools.partial(_ep_headnorm, scale=AT_DH ** -0.5), [qg],
                      [BF16], "proj_aq")
        c += WA
        (ak,) = _proj(h, w, c, WA, functools.partial(_ep_headnorm, scale=1.0), [kg], [BF16],
                      "proj_ak")
        c += WA
        (av,) = _proj(h, w, c, WA, _ep_identity, [], [BF16], "proj_av")
        c += WA
        (gates,) = _proj(h, w, c, 2 * D, _ep_sigmoid, [], [BF16], "proj_gate")

        o_hg = _hgrn(hq, lf, hk, hi, hg, hg_out_norm_g[l], B, S)
        o_at = _moba(aq, ak, av, rel_bias_table, B, S)
        x1, h2 = _merge(o_hg, o_at, gates, xt, w_branch_hg[l].astype(BF16),
                        w_branch_attn[l].astype(BF16), w_out[l].astype(BF16), ffn_norm_g[l])
        xt = _ffn(h2, x1, w_ffn_gate[l].astype(BF16), w_ffn_up[l].astype(BF16),
                  w_ffn_down[l].astype(BF16))
    return xt.reshape(B, S, D)
```

```python
import functools
import math

import numpy as np
import jax
import jax.numpy as jnp
from jax import lax
from jax.experimental import pallas as pl
from jax.experimental.pallas import tpu as pltpu

F32 = jnp.float32
BF16 = jnp.bfloat16

LANES = 128
NORM_EPS = 1e-6

HG_HEADS = 8
HG_DK = 128
HG_DV = 128
HG_CHUNK = 64
AT_HEADS = 8
AT_DH = 128
MOBA_BLOCK = 256
MOBA_TOPK = 3
REL_BUCKETS = 32
REL_MAX_DIST = 128
MASKED_LOGIT = -1e30

VMEM_LIMIT = 56 * 1024 * 1024


def _cparams(sem):
    return pltpu.CompilerParams(dimension_semantics=sem, vmem_limit_bytes=VMEM_LIMIT)


def _dot(a, b):
    return jnp.dot(a, b, preferred_element_type=F32)


def _dot_nt(a, b):
    return lax.dot_general(a, b, (((1,), (1,)), ((), ())), preferred_element_type=F32)


def _dot_tn(a, b):
    return lax.dot_general(a, b, (((0,), (0,)), ((), ())), preferred_element_type=F32)


def _rmsnorm_kernel(x_ref, g_ref, o_ref):
    x = x_ref[...]
    ms = jnp.mean(x * x, axis=-1, keepdims=True)
    o_ref[...] = (x * lax.rsqrt(ms + NORM_EPS) * g_ref[...]).astype(o_ref.dtype)


def _rmsnorm(x, g, tm=1024):
    T, D = x.shape
    return pl.pallas_call(
        _rmsnorm_kernel,
        grid=(T // tm,),
        in_specs=[pl.BlockSpec((tm, D), lambda i: (i, 0)),
                  pl.BlockSpec((1, D), lambda i: (0, 0))],
        out_specs=pl.BlockSpec((tm, D), lambda i: (i, 0)),
        out_shape=jax.ShapeDtypeStruct((T, D), BF16),
        compiler_params=_cparams(("parallel",)),
        name="rmsnorm",
    )(x, g.reshape(1, D).astype(F32))


def _proj_kernel(*refs, epilogue, n_extra, n_out):
    h_ref, w_ref = refs[0], refs[1]
    extras = refs[2:2 + n_extra]
    outs = refs[2 + n_extra:2 + n_extra + n_out]
    acc = _dot(h_ref[...], w_ref[...])
    res = epilogue(acc, *[e[...] for e in extras])
    for o_ref, r in zip(outs, res):
        o_ref[...] = r.astype(o_ref.dtype)


def _proj(h, w, col0, ncols, epilogue, extras, out_dtypes, name, tm=2048, tn=512):
    T, D = h.shape
    j0 = col0 // tn
    in_specs = [pl.BlockSpec((tm, D), lambda i, j: (i, 0)),
                pl.BlockSpec((D, tn), lambda i, j: (0, j0 + j))]
    in_specs += [pl.BlockSpec((e.shape[0], tn), lambda i, j: (0, j)) for e in extras]
    outs = pl.pallas_call(
        functools.partial(_proj_kernel, epilogue=epilogue, n_extra=len(extras),
                          n_out=len(out_dtypes)),
        grid=(T // tm, ncols // tn),
        in_specs=in_specs,
        out_specs=[pl.BlockSpec((tm, tn), lambda i, j: (i, j)) for _ in out_dtypes],
        out_shape=[jax.ShapeDtypeStruct((T, ncols), dt) for dt in out_dtypes],
        compiler_params=_cparams(("parallel", "arbitrary")),
        name=name,
    )(h, w, *extras)
    return outs


def _ep_identity(acc):
    return (acc,)


def _ep_scale(acc, *, scale):
    return (acc * scale,)


def _ep_silu(acc):
    return (acc * jax.nn.sigmoid(acc),)


def _ep_sigmoid(acc):
    return (jax.nn.sigmoid(acc),)


def _ep_forget(acc, gamma):
    gmax = jnp.max(gamma, axis=0, keepdims=True)
    eg = jnp.exp(gamma - gmax)
    lb = eg[0:1, :] / jnp.sum(eg, axis=0, keepdims=True)
    f = lb + (1.0 - lb) * jax.nn.sigmoid(acc)
    key = (1.0 - lb) * jax.nn.sigmoid(-acc)
    return jnp.log(f), key


def _ep_headnorm(acc, gain, *, scale):
    cols = []
    for c in range(acc.shape[1] // LANES):
        blk = acc[:, c * LANES:(c + 1) * LANES]
        ms = jnp.mean(blk * blk, axis=-1, keepdims=True)
        cols.append(blk * lax.rsqrt(ms + NORM_EPS) * gain[:, c * LANES:(c + 1) * LANES] * scale)
    return (jnp.concatenate(cols, axis=1),)


def _hgrn_tables(C):
    nl = int(math.log2(C))
    assert 1 << nl == C and nl >= 3
    masks = np.zeros((nl + 1, C, C), np.float32)
    masks[0] = np.eye(C)
    t = np.arange(C)
    for L in range(1, nl + 1):
        blk, half = 1 << L, 1 << (L - 1)
        base = (t // blk) * blk
        upper = (t - base) >= half
        same = base[:, None] == base[None, :]
        masks[L] = (same & upper[:, None] & (~upper)[None, :]).astype(np.float32)
    return np.tril(np.ones((C, C), np.float32)), masks, nl


def _hgrn_kernel(q_ref, lf_ref, k_ref, v_ref, g_ref, gain_ref, tri_ref, m_ref, o_ref,
                 b_sc, e_sc, st_sc, *, C, nl, heads):
    @pl.when(pl.program_id(1) == 0)
    def _():
        st_sc[...] = jnp.zeros_like(st_sc)

    lf = lf_ref[...]
    hi = lf.astype(BF16)
    r1 = lf - hi.astype(F32)
    mid = r1.astype(BF16)
    lo = (r1 - mid.astype(F32)).astype(BF16)
    tri = tri_ref[...]
    b = _dot(tri, hi) + _dot(tri, mid) + _dot(tri, lo)
    b_sc[...] = b

    row = lax.broadcasted_iota(jnp.int32, lf.shape, 0)
    e_sc[0:C, :] = jnp.exp(jnp.where((row & 1) == 1, lf, 0.0)).astype(BF16)
    prev = pltpu.roll(lf, 1, axis=0)
    nxt = pltpu.roll(lf, C - 1, axis=0)
    ph = row & 3
    x2 = jnp.where(ph == 0, nxt, jnp.where(ph == 1, 0.0, jnp.where(ph == 2, lf, lf + prev)))
    e_sc[C:2 * C, :] = jnp.exp(x2).astype(BF16)
    for L in range(3, nl + 1):
        blk, half = 1 << L, 1 << (L - 1)
        pieces = [b_sc[base:base + blk, :] - b_sc[base + half - 1:base + half, :]
                  for base in range(0, C, blk)]
        d = pieces[0] if len(pieces) == 1 else jnp.concatenate(pieces, axis=0)
        e_sc[(L - 1) * C:L * C, :] = jnp.exp(-jnp.abs(d)).astype(BF16)
    b_last = b_sc[C - 1:C, :]
    e_sc[nl * C:(nl + 1) * C, :] = jnp.exp(b).astype(BF16)
    e_sc[(nl + 1) * C:(nl + 2) * C, :] = jnp.exp(b_last - b).astype(BF16)
    d_last = jnp.exp(b_last)

    gain = gain_ref[...]
    for h in range(heads):
        sl = slice(h * HG_DK, (h + 1) * HG_DK)
        qb = q_ref[:, sl]
        kb = k_ref[:, sl]
        vb = v_ref[:, sl]
        A = m_ref[0] * _dot_nt(qb, kb)
        for L in range(1, nl + 1):
            eL = e_sc[(L - 1) * C:L * C, sl]
            A = A + m_ref[L] * _dot_nt(qb * eL, kb * eL)
        eb = e_sc[nl * C:(nl + 1) * C, sl]
        ek = e_sc[(nl + 1) * C:(nl + 2) * C, sl]
        st = st_sc[h]
        o = _dot(A.astype(BF16), vb) + _dot_nt(qb * eb, st.astype(BF16))
        st_sc[h] = st * d_last[:, sl] + _dot_tn(vb, kb * ek)
        ms = jnp.mean(o * o, axis=-1, keepdims=True)
        y = o * lax.rsqrt(ms + NORM_EPS) * gain * g_ref[:, sl].astype(F32)
        o_ref[:, sl] = y.astype(o_ref.dtype)


def _hgrn(q, lf, k, v, g, gain, B, S):
    T, W = q.shape
    C = HG_CHUNK
    N = S // C
    tri, masks, nl = _hgrn_tables(C)
    row = lambda b, c: (b * N + c, 0)
    tok = lambda: pl.BlockSpec((C, W), row)
    return pl.pallas_call(
        functools.partial(_hgrn_kernel, C=C, nl=nl, heads=HG_HEADS),
        grid=(B, N),
        in_specs=[tok(), tok(), tok(), tok(), tok(),
                  pl.BlockSpec((1, HG_DV), lambda b, c: (0, 0)),
                  pl.BlockSpec(tri.shape, lambda b, c: (0, 0)),
                  pl.BlockSpec(masks.shape, lambda b, c: (0, 0, 0))],
        out_specs=tok(),
        out_shape=jax.ShapeDtypeStruct((T, W), BF16),
        scratch_shapes=[pltpu.VMEM((C, W), F32),
                        pltpu.VMEM(((nl + 2) * C, W), BF16),
                        pltpu.VMEM((HG_HEADS, HG_DV, HG_DK), F32)],
        compiler_params=_cparams(("parallel", "arbitrary")),
        name="hgrn2",
    )(q, lf, k, v, g, gain.reshape(1, HG_DV).astype(F32),
      jnp.asarray(tri, BF16), jnp.asarray(masks, F32))


def _t5_bucket_np(dist):
    n = np.maximum(dist, 0)
    max_exact = REL_BUCKETS // 2
    nf = np.maximum(n, 1).astype(np.float32)
    large = max_exact + (np.log(nf / max_exact) / math.log(REL_MAX_DIST / max_exact)
                         * (REL_BUCKETS - max_exact)).astype(np.int32)
    large = np.minimum(large, REL_BUCKETS - 1)
    return np.where(n < max_exact, n, large).astype(np.int32)


def _bias_kernel(tab_ref, bucket_ref, o_ref):
    h = pl.program_id(0)
    bucket = bucket_ref[...]
    acc = jnp.zeros(bucket.shape, F32)
    for b in range(REL_BUCKETS):
        acc = jnp.where(bucket == b, tab_ref[b, h], acc)
    delta = acc - tab_ref[REL_BUCKETS - 1, h]
    blk = bucket.shape[0]
    r = lax.broadcasted_iota(jnp.int32, bucket.shape, 0)
    c = lax.broadcasted_iota(jnp.int32, bucket.shape, 1)
    o_ref[0] = jnp.where(c - blk <= r, delta, MASKED_LOGIT)


def _bias_tiles(rel_table):
    BLK = MOBA_BLOCK
    t = np.arange(BLK)[:, None]
    s = np.arange(2 * BLK)[None, :]
    bucket = _t5_bucket_np(t + BLK - s)
    H = rel_table.shape[1]
    return pl.pallas_call(
        _bias_kernel,
        grid=(H,),
        in_specs=[pl.BlockSpec(memory_space=pltpu.SMEM),
                  pl.BlockSpec(bucket.shape, lambda h: (0, 0))],
        out_specs=pl.BlockSpec((1, BLK, 2 * BLK), lambda h: (h, 0, 0)),
        out_shape=jax.ShapeDtypeStruct((H, BLK, 2 * BLK), F32),
        compiler_params=_cparams(("arbitrary",)),
        name="relbias",
    )(rel_table.astype(F32), jnp.asarray(bucket))


def _moba_kernel(q_ref, k_ref, v_ref, dbias_ref, far_ref, o_ref,
                 qnear_sc, qfar_sc, kaug_sc, vaug_sc, near_sc, far_sc, acc_sc, *, NB):
    BLK = MOBA_BLOCK
    W2 = 2 * BLK
    PAD = NB + 2
    NS = NB // 2 - 1
    h = pl.program_id(1)
    ip = pl.program_id(2)

    @pl.when(ip == 0)
    def _():
        far = far_ref[pl.ds(h, 1), :]
        far_hi = far.astype(BF16).astype(F32)
        far_lo = far - far_hi
        lane = lax.broadcasted_iota(jnp.int32, (1, LANES), 1)
        tail = jnp.where(lane == NB, far_hi, jnp.where(lane == NB + 1, far_lo, 0.0))
        kaug_sc[0:BLK, 0:AT_DH] = jnp.zeros((BLK, AT_DH), BF16)
        kaug_sc[0:BLK, AT_DH:] = jnp.broadcast_to(
            jnp.where(lane == PAD, 1.0, 0.0), (BLK, LANES)).astype(BF16)
        vaug_sc[0:BLK, :] = jnp.zeros((BLK, AT_DH + LANES), BF16)
        kms = []
        for n in range(NB):
            kb = k_ref[n * BLK:(n + 1) * BLK, :]
            kms.append(jnp.mean(kb.astype(F32), axis=0, keepdims=True))
            kaug_sc[(n + 1) * BLK:(n + 2) * BLK, 0:AT_DH] = kb
            kaug_sc[(n + 1) * BLK:(n + 2) * BLK, AT_DH:] = jnp.broadcast_to(
                jnp.where(lane == n, 1.0, tail), (BLK, LANES)).astype(BF16)
        vaug_sc[BLK:, 0:AT_DH] = v_ref[...]
        vaug_sc[BLK:, AT_DH:] = jnp.ones((NB * BLK, LANES), BF16)

        kmean = jnp.concatenate(kms, axis=0)
        km_hi = kmean.astype(BF16)
        km_lo = (kmean - km_hi.astype(F32)).astype(BF16)
        blk = lax.broadcasted_iota(jnp.int32, (NB, BLK), 0)
        rowid = lax.broadcasted_iota(jnp.int32, (LANES - NB, BLK), 0) + NB
        rest = jnp.where(rowid < NB + 2, 1.0, jnp.where(rowid == PAD, MASKED_LOGIT, 0.0))
        for t in range(NB):
            qt = q_ref[t * BLK:(t + 1) * BLK, :]
            gate = jnp.where(blk < t, _dot_nt(km_hi, qt) + _dot_nt(km_lo, qt), -jnp.inf)
            rank = jnp.zeros((NB, BLK), jnp.int32)
            for m in range(t):
                gm = gate[m:m + 1, :]
                rank = rank + ((gm > gate) | ((gm == gate) & (blk > m))).astype(jnp.int32)
            sel = ((blk < t) & (rank < MOBA_TOPK)) | (blk == t)
            near = jnp.where(sel, 0.0, MASKED_LOGIT)
            older = jnp.where(blk >= t - 1, MASKED_LOGIT, near)
            rows = slice(t * BLK, (t + 1) * BLK)
            qnear_sc[rows, 0:AT_DH] = qt
            qfar_sc[rows, 0:AT_DH] = qt
            qnear_sc[rows, AT_DH:] = jnp.concatenate([near, rest], axis=0).T.astype(BF16)
            qfar_sc[rows, AT_DH:] = jnp.concatenate([older, rest], axis=0).T.astype(BF16)

    def lane_max(s):
        out = s[:, 0:LANES]
        for c in range(1, s.shape[1] // LANES):
            out = jnp.maximum(out, s[:, c * LANES:(c + 1) * LANES])
        return out

    def probs(s, m_b):
        return jnp.concatenate(
            [jnp.exp(s[:, c * LANES:(c + 1) * LANES] - m_b) for c in range(s.shape[1] // LANES)],
            axis=1).astype(BF16)

    blocks = (ip, NB - 1 - ip)
    rows = [pl.multiple_of(b * BLK, BLK) for b in blocks]
    n_first = jnp.maximum(ip, 1) // 2
    visits = []
    for st in range(NS):
        second = st >= n_first
        jj = jnp.where(second, st - n_first, st)
        visits.append((second, jnp.where(second, rows[1], rows[0]),
                       pl.multiple_of(jj * W2 + BLK, BLK)))

    mx = []
    for slot in range(2):
        s = (_dot_nt(qnear_sc[pl.ds(rows[slot], BLK), :], kaug_sc[pl.ds(rows[slot], W2), :])
             + dbias_ref[0])
        near_sc[slot] = s
        mx.append(lane_max(s))
    for st, (second, qrow, krow) in enumerate(visits):
        s = _dot_nt(qfar_sc[pl.ds(qrow, BLK), :], kaug_sc[pl.ds(krow, W2), :])
        far_sc[st] = s
        lm = lane_max(s)
        mx[0] = jnp.maximum(mx[0], jnp.where(second, MASKED_LOGIT, lm))
        mx[1] = jnp.maximum(mx[1], jnp.where(second, lm, MASKED_LOGIT))

    m_b = [jnp.broadcast_to(jnp.max(m, axis=-1, keepdims=True), (BLK, LANES)) for m in mx]
    for slot in range(2):
        acc_sc[slot] = _dot(probs(near_sc[slot], m_b[slot]), vaug_sc[pl.ds(rows[slot], W2), :])
    for st, (second, qrow, krow) in enumerate(visits):
        p = probs(far_sc[st], jnp.where(second, m_b[1], m_b[0]))
        acc_sc[second.astype(jnp.int32)] += _dot(p, vaug_sc[pl.ds(krow, W2), :])
    for slot in range(2):
        o_ref[pl.ds(rows[slot], BLK), :] = (
            acc_sc[slot, :, 0:AT_DH] / acc_sc[slot, :, AT_DH:]).astype(o_ref.dtype)


def _moba(q, k, v, rel_table, B, S):
    T, W = q.shape
    H = AT_HEADS
    NB = S // MOBA_BLOCK
    BLK = MOBA_BLOCK
    assert NB + 3 <= LANES and NB % 2 == 0
    assert int(_t5_bucket_np(np.array([BLK + 1]))[0]) == REL_BUCKETS - 1
    dbias = _bias_tiles(rel_table)
    far = jnp.broadcast_to(rel_table[REL_BUCKETS - 1, :].astype(F32)[:, None], (H, LANES))
    head = lambda: pl.BlockSpec((S, AT_DH), lambda b, h, i: (b, h))
    return pl.pallas_call(
        functools.partial(_moba_kernel, NB=NB),
        grid=(B, H, NB // 2),
        in_specs=[head(), head(), head(),
                  pl.BlockSpec((1, BLK, 2 * BLK), lambda b, h, i: (h, 0, 0)),
                  pl.BlockSpec((H, LANES), lambda b, h, i: (0, 0))],
        out_specs=head(),
        out_shape=jax.ShapeDtypeStruct((T, W), BF16),
        scratch_shapes=[pltpu.VMEM((S, AT_DH + LANES), BF16),
                        pltpu.VMEM((S, AT_DH + LANES), BF16),
                        pltpu.VMEM((S + BLK, AT_DH + LANES), BF16),
                        pltpu.VMEM((S + BLK, AT_DH + LANES), BF16),
                        pltpu.VMEM((2, BLK, 2 * BLK), F32),
                        pltpu.VMEM((max(NB // 2 - 1, 1), BLK, 2 * BLK), F32),
                        pltpu.VMEM((2, BLK, AT_DH + LANES), F32)],
        compiler_params=_cparams(("parallel", "parallel", "arbitrary")),
        name="moba",
    )(q, k, v, dbias, far)


def _merge_kernel(ohg_ref, oat_ref, gate_ref, x_ref, whg_ref, wat_ref, wout_ref, g2_ref,
                  x1_ref, h2_ref, *, D):
    y_hg = _dot(ohg_ref[...], whg_ref[...])
    y_at = _dot(oat_ref[...], wat_ref[...])
    merged = (gate_ref[:, 0:D].astype(F32) * y_hg + gate_ref[:, D:2 * D].astype(F32) * y_at)
    x1 = x_ref[...] + _dot(merged.astype(BF16), wout_ref[...])
    x1_ref[...] = x1
    ms = jnp.mean(x1 * x1, axis=-1, keepdims=True)
    h2_ref[...] = (x1 * lax.rsqrt(ms + NORM_EPS) * g2_ref[...]).astype(h2_ref.dtype)


def _merge(ohg, oat, gates, x, whg, wat, wout, g2, tm=512):
    T, D = x.shape
    tok = lambda w: pl.BlockSpec((tm, w), lambda i: (i, 0))
    full = lambda a: pl.BlockSpec(a.shape, lambda i: (0, 0))
    return pl.pallas_call(
        functools.partial(_merge_kernel, D=D),
        grid=(T // tm,),
        in_specs=[tok(ohg.shape[1]), tok(oat.shape[1]), tok(2 * D), tok(D),
                  full(whg), full(wat), full(wout), pl.BlockSpec((1, D), lambda i: (0, 0))],
        out_specs=[tok(D), tok(D)],
        out_shape=[jax.ShapeDtypeStruct((T, D), F32), jax.ShapeDtypeStruct((T, D), BF16)],
        compiler_params=_cparams(("parallel",)),
        name="merge",
    )(ohg, oat, gates, x, whg, wat, wout, g2.reshape(1, D).astype(F32))


def _ffn_kernel(h_ref, x_ref, wg_ref, wu_ref, wd_ref, o_ref, acc_sc):
    f = pl.program_id(1)

    @pl.when(f == 0)
    def _():
        acc_sc[...] = x_ref[...]

    h = h_ref[...]
    a = _dot(h, wg_ref[...])
    u = _dot(h, wu_ref[...])
    acc_sc[...] += _dot((a * jax.nn.sigmoid(a) * u).astype(BF16), wd_ref[...])

    @pl.when(f == pl.num_programs(1) - 1)
    def _():
        o_ref[...] = acc_sc[...]


def _ffn(h2, x1, wg, wu, wd, tm=1024, tf=256):
    T, D = x1.shape
    FF = wg.shape[1]
    return pl.pallas_call(
        _ffn_kernel,
        grid=(T // tm, FF // tf),
        in_specs=[pl.BlockSpec((tm, D), lambda i, f: (i, 0)),
                  pl.BlockSpec((tm, D), lambda i, f: (i, 0)),
                  pl.BlockSpec((D, tf), lambda i, f: (0, f)),
                  pl.BlockSpec((D, tf), lambda i, f: (0, f)),
                  pl.BlockSpec((tf, D), lambda i, f: (f, 0))],
        out_specs=pl.BlockSpec((tm, D), lambda i, f: (i, 0)),
        out_shape=jax.ShapeDtypeStruct((T, D), F32),
        scratch_shapes=[pltpu.VMEM((tm, D), F32)],
        compiler_params=_cparams(("parallel", "arbitrary")),
        name="ffn",
    )(h2, x1, wg, wu, wd)


def kernel(x, attn_norm_g, w_in, hg_lb_gamma, hg_out_norm_g, q_norm_g, k_norm_g, rel_bias_table,
           w_branch_hg, w_branch_attn, w_out, ffn_norm_g, w_ffn_gate, w_ffn_up, w_ffn_down):
    B, S, D = x.shape
    T = B * S
    depth = attn_norm_g.shape[0]
    assert depth == 1 and S % MOBA_BLOCK == 0 and S % HG_CHUNK == 0
    WH = HG_HEADS * HG_DK
    WV = HG_HEADS * HG_DV
    WA = AT_HEADS * AT_DH
    assert w_in.shape[2] == 2 * WH + 2 * WV + 3 * WA + 2 * D

    xt = x.reshape(T, D)
    for l in range(depth):
        w = w_in[l].astype(BF16)
        h = _rmsnorm(xt, attn_norm_g[l])
        c = 0
        (hq,) = _proj(h, w, c, WH, functools.partial(_ep_scale, scale=HG_DK ** -0.5), [],
                      [BF16], "proj_hq")
        c += WH
        lf, hk = _proj(h, w, c, WH, _ep_forget, [hg_lb_gamma.astype(F32)], [F32, BF16],
                       "proj_hf")
        c += WH
        (hi,) = _proj(h, w, c, WV, _ep_identity, [], [BF16], "proj_hi")
        c += WV
        (hg,) = _proj(h, w, c, WV, _ep_silu, [], [BF16], "proj_hg")
        c += WV
        qg = jnp.tile(q_norm_g[l].astype(F32), AT_HEADS).reshape(1, WA)
        kg = jnp.tile(k_norm_g[l].astype(F32), AT_HEADS).reshape(1, WA)
        (aq,) = _proj(h, w, c, WA, functools.partial(_ep_headnorm, scale=AT_DH ** -0.5), [qg],
                      [BF16], "proj_aq")
        c += WA
        (ak,) = _proj(h, w, c, WA, functools.partial(_ep_headnorm, scale=1.0), [kg], [BF16],
                      "proj_ak")
        c += WA
        (av,) = _proj(h, w, c, WA, _ep_identity, [], [BF16], "proj_av")
        c += WA
        (gates,) = _proj(h, w, c, 2 * D, _ep_sigmoid, [], [BF16], "proj_gate")

        o_hg = _hgrn(hq, lf, hk, hi, hg, hg_out_norm_g[l], B, S)
        o_at = _moba(aq, ak, av, rel_bias_table, B, S)
        x1, h2 = _merge(o_hg, o_at, gates, xt, w_branch_hg[l].astype(BF16),
                        w_branch_attn[l].astype(BF16), w_out[l].astype(BF16), ffn_norm_g[l])
        xt = _ffn(h2, x1, w_ffn_gate[l].astype(BF16), w_ffn_up[l].astype(BF16),
                  w_ffn_down[l].astype(BF16))
    return xt.reshape(B, S, D)
```

```python
import functools
import math

import numpy as np
import jax
import jax.numpy as jnp
from jax import lax
from jax.experimental import pallas as pl
from jax.experimental.pallas import tpu as pltpu

F32 = jnp.float32
BF16 = jnp.bfloat16

LANES = 128
NORM_EPS = 1e-6

HG_HEADS = 8
HG_DK = 128
HG_DV = 128
HG_CHUNK = 64
HG_STEP_TOKENS = 512
HG_HEAD_GROUP = 8
AT_HEADS = 8
AT_DH = 128
MOBA_BLOCK = 256
MOBA_TOPK = 3
REL_BUCKETS = 32
REL_MAX_DIST = 128
MASKED_LOGIT = -1e30

VMEM_LIMIT = 56 * 1024 * 1024


def _cparams(sem):
    return pltpu.CompilerParams(dimension_semantics=sem, vmem_limit_bytes=VMEM_LIMIT)


def _dot(a, b):
    return jnp.dot(a, b, preferred_element_type=F32)


def _dot_nt(a, b):
    return lax.dot_general(a, b, (((1,), (1,)), ((), ())), preferred_element_type=F32)


def _dot_tn(a, b):
    return lax.dot_general(a, b, (((0,), (0,)), ((), ())), preferred_element_type=F32)


def _rmsnorm_kernel(x_ref, g_ref, o_ref):
    x = x_ref[...]
    ms = jnp.mean(x * x, axis=-1, keepdims=True)
    o_ref[...] = (x * lax.rsqrt(ms + NORM_EPS) * g_ref[...]).astype(o_ref.dtype)


def _rmsnorm(x, g, tm=1024):
    T, D = x.shape
    return pl.pallas_call(
        _rmsnorm_kernel,
        grid=(T // tm,),
        in_specs=[pl.BlockSpec((tm, D), lambda i: (i, 0)),
                  pl.BlockSpec((1, D), lambda i: (0, 0))],
        out_specs=pl.BlockSpec((tm, D), lambda i: (i, 0)),
        out_shape=jax.ShapeDtypeStruct((T, D), BF16),
        compiler_params=_cparams(("parallel",)),
        name="rmsnorm",
    )(x, g.reshape(1, D).astype(F32))


def _proj_kernel(*refs, epilogue, n_extra, n_out):
    h_ref, w_ref = refs[0], refs[1]
    extras = refs[2:2 + n_extra]
    outs = refs[2 + n_extra:2 + n_extra + n_out]
    acc = _dot(h_ref[...], w_ref[...])
    res = epilogue(acc, *[e[...] for e in extras])
    for o_ref, r in zip(outs, res):
        o_ref[...] = r.astype(o_ref.dtype)


def _proj(h, w, col0, ncols, epilogue, extras, out_dtypes, name, tm=2048, tn=512):
    T, D = h.shape
    j0 = col0 // tn
    in_specs = [pl.BlockSpec((tm, D), lambda i, j: (i, 0)),
                pl.BlockSpec((D, tn), lambda i, j: (0, j0 + j))]
    in_specs += [pl.BlockSpec((e.shape[0], tn), lambda i, j: (0, j)) for e in extras]
    outs = pl.pallas_call(
        functools.partial(_proj_kernel, epilogue=epilogue, n_extra=len(extras),
                          n_out=len(out_dtypes)),
        grid=(T // tm, ncols // tn),
        in_specs=in_specs,
        out_specs=[pl.BlockSpec((tm, tn), lambda i, j: (i, j)) for _ in out_dtypes],
        out_shape=[jax.ShapeDtypeStruct((T, ncols), dt) for dt in out_dtypes],
        compiler_params=_cparams(("parallel", "arbitrary")),
        name=name,
    )(h, w, *extras)
    return outs


def _ep_identity(acc):
    return (acc,)


def _ep_scale(acc, *, scale):
    return (acc * scale,)


def _ep_silu(acc):
    return (acc * jax.nn.sigmoid(acc),)


def _ep_sigmoid(acc):
    return (jax.nn.sigmoid(acc),)


def _ep_forget(acc, gamma):
    gmax = jnp.max(gamma, axis=0, keepdims=True)
    eg = jnp.exp(gamma - gmax)
    lb = eg[0:1, :] / jnp.sum(eg, axis=0, keepdims=True)
    f = lb + (1.0 - lb) * jax.nn.sigmoid(acc)
    key = (1.0 - lb) * jax.nn.sigmoid(-acc)
    return jnp.log2(f), key


def _ep_headnorm(acc, gain, *, scale):
    cols = []
    for c in range(acc.shape[1] // LANES):
        blk = acc[:, c * LANES:(c + 1) * LANES]
        ms = jnp.mean(blk * blk, axis=-1, keepdims=True)
        cols.append(blk * lax.rsqrt(ms + NORM_EPS) * gain[:, c * LANES:(c + 1) * LANES] * scale)
    return (jnp.concatenate(cols, axis=1),)


def _hgrn_tables(C):
    nl = int(math.log2(C))
    assert 1 << nl == C and nl >= 3
    masks = np.zeros((nl + 1, C, C), np.float32)
    masks[0] = np.eye(C)
    t = np.arange(C)
    for L in range(1, nl + 1):
        blk, half = 1 << L, 1 << (L - 1)
        base = (t // blk) * blk
        upper = (t - base) >= half
        same = base[:, None] == base[None, :]
        masks[L] = (same & upper[:, None] & (~upper)[None, :]).astype(np.float32)
    return np.tril(np.ones((C, C), np.float32)), masks, nl


def _hgrn_kernel(q_ref, lf_ref, k_ref, v_ref, g_ref, gain_ref, tri_ref, m_ref, o_ref,
                 e_sc, d_sc, st_sc, *, C, nl, heads, chunks):
    @pl.when(pl.program_id(1) == 0)
    def _():
        st_sc[...] = jnp.zeros_like(st_sc)

    tri = tri_ref[...]
    gain = gain_ref[...]
    row = lax.broadcasted_iota(jnp.int32, (C, lf_ref.shape[1]), 0)
    odd = (row & 1) == 1
    ph = row & 3

    def chunk_rows(c):
        return pl.ds(pl.multiple_of(c * C, C), C)

    def prepare(c, buf):
        lf = lf_ref[chunk_rows(c), :]
        hi = lf.astype(BF16)
        r1 = lf - hi.astype(F32)
        mid = r1.astype(BF16)
        lo = (r1 - mid.astype(F32)).astype(BF16)
        b = _dot(tri, hi) + _dot(tri, mid) + _dot(tri, lo)
        e_sc[buf, 0:C, :] = jnp.exp2(jnp.where(odd, lf, 0.0)).astype(BF16)
        prev = pltpu.roll(lf, 1, axis=0)
        nxt = pltpu.roll(lf, C - 1, axis=0)
        x2 = jnp.where(ph == 0, nxt, jnp.where(ph == 1, 0.0, jnp.where(ph == 2, lf, lf + prev)))
        e_sc[buf, C:2 * C, :] = jnp.exp2(x2).astype(BF16)
        for L in range(3, nl + 1):
            blk, half = 1 << L, 1 << (L - 1)
            pieces = [b[base:base + blk, :] - b[base + half - 1:base + half, :]
                      for base in range(0, C, blk)]
            d = pieces[0] if len(pieces) == 1 else jnp.concatenate(pieces, axis=0)
            e_sc[buf, (L - 1) * C:L * C, :] = jnp.exp2(-jnp.abs(d)).astype(BF16)
        b_last = b[C - 1:C, :]
        e_sc[buf, nl * C:(nl + 1) * C, :] = jnp.exp2(b).astype(BF16)
        e_sc[buf, (nl + 1) * C:(nl + 2) * C, :] = jnp.exp2(b_last - b).astype(BF16)
        d_sc[buf] = jnp.exp2(b_last)

    def heads_of(c, buf):
        rows = chunk_rows(c)
        for h0 in range(0, heads, HG_HEAD_GROUP):
            hs = range(h0, h0 + HG_HEAD_GROUP)
            sls = [slice(h * HG_DK, (h + 1) * HG_DK) for h in hs]
            qb = [q_ref[rows, sl] for sl in sls]
            kb = [k_ref[rows, sl] for sl in sls]
            vb = [v_ref[rows, sl] for sl in sls]
            A = [m_ref[0] * _dot_nt(q, k) for q, k in zip(qb, kb)]
            for L in range(1, nl + 1):
                for j, sl in enumerate(sls):
                    eL = e_sc[buf, (L - 1) * C:L * C, sl]
                    A[j] = A[j] + m_ref[L] * _dot_nt(qb[j] * eL, kb[j] * eL)
            for j, (h, sl) in enumerate(zip(hs, sls)):
                eb = e_sc[buf, nl * C:(nl + 1) * C, sl]
                ek = e_sc[buf, (nl + 1) * C:(nl + 2) * C, sl]
                st = st_sc[h]
                o = _dot(A[j].astype(BF16), vb[j]) + _dot_nt(qb[j] * eb, st.astype(BF16))
                st_sc[h] = st * d_sc[buf, :, sl] + _dot_tn(vb[j], kb[j] * ek)
                ms = jnp.mean(o * o, axis=-1, keepdims=True)
                y = o * lax.rsqrt(ms + NORM_EPS) * gain * g_ref[rows, sl].astype(F32)
                o_ref[rows, sl] = y.astype(o_ref.dtype)

    prepare(0, 0)

    def step(c, carry):
        prepare(jnp.minimum(c + 1, chunks - 1), (c + 1) & 1)
        heads_of(c, c & 1)
        return carry

    lax.fori_loop(0, chunks, step, 0)


def _hgrn(q, lf, k, v, g, gain, B, S):
    T, W = q.shape
    C = HG_CHUNK
    TS = HG_STEP_TOKENS
    assert S % TS == 0 and TS % C == 0
    N = S // TS
    tri, masks, nl = _hgrn_tables(C)
    row = lambda b, c: (b * N + c, 0)
    tok = lambda: pl.BlockSpec((TS, W), row)
    return pl.pallas_call(
        functools.partial(_hgrn_kernel, C=C, nl=nl, heads=HG_HEADS, chunks=TS // C),
        grid=(B, N),
        in_specs=[tok(), tok(), tok(), tok(), tok(),
                  pl.BlockSpec((1, HG_DV), lambda b, c: (0, 0)),
                  pl.BlockSpec(tri.shape, lambda b, c: (0, 0)),
                  pl.BlockSpec(masks.shape, lambda b, c: (0, 0, 0))],
        out_specs=tok(),
        out_shape=jax.ShapeDtypeStruct((T, W), BF16),
        scratch_shapes=[pltpu.VMEM((2, (nl + 2) * C, W), BF16),
                        pltpu.VMEM((2, 1, W), F32),
                        pltpu.VMEM((HG_HEADS, HG_DV, HG_DK), F32)],
        compiler_params=_cparams(("parallel", "arbitrary")),
        name="hgrn2",
    )(q, lf, k, v, g, gain.reshape(1, HG_DV).astype(F32),
      jnp.asarray(tri, BF16), jnp.asarray(masks, F32))


def _t5_bucket_np(dist):
    n = np.maximum(dist, 0)
    max_exact = REL_BUCKETS // 2
    nf = np.maximum(n, 1).astype(np.float32)
    large = max_exact + (np.log(nf / max_exact) / math.log(REL_MAX_DIST / max_exact)
                         * (REL_BUCKETS - max_exact)).astype(np.int32)
    large = np.minimum(large, REL_BUCKETS - 1)
    return np.where(n < max_exact, n, large).astype(np.int32)


def _bias_kernel(tab_ref, bucket_ref, o_ref):
    h = pl.program_id(0)
    bucket = bucket_ref[...]
    acc = jnp.zeros(bucket.shape, F32)
    for b in range(REL_BUCKETS):
        acc = jnp.where(bucket == b, tab_ref[b, h], acc)
    delta = acc - tab_ref[REL_BUCKETS - 1, h]
    blk = bucket.shape[0]
    r = lax.broadcasted_iota(jnp.int32, bucket.shape, 0)
    c = lax.broadcasted_iota(jnp.int32, bucket.shape, 1)
    o_ref[0] = jnp.where(c - blk <= r, delta, MASKED_LOGIT)


def _bias_tiles(rel_table):
    BLK = MOBA_BLOCK
    t = np.arange(BLK)[:, None]
    s = np.arange(2 * BLK)[None, :]
    bucket = _t5_bucket_np(t + BLK - s)
    H = rel_table.shape[1]
    return pl.pallas_call(
        _bias_kernel,
        grid=(H,),
        in_specs=[pl.BlockSpec(memory_space=pltpu.SMEM),
                  pl.BlockSpec(bucket.shape, lambda h: (0, 0))],
        out_specs=pl.BlockSpec((1, BLK, 2 * BLK), lambda h: (h, 0, 0)),
        out_shape=jax.ShapeDtypeStruct((H, BLK, 2 * BLK), F32),
        compiler_params=_cparams(("arbitrary",)),
        name="relbias",
    )(rel_table.astype(F32), jnp.asarray(bucket))


def _moba_kernel(q_ref, k_ref, v_ref, dbias_ref, far_ref, o_ref,
                 qnear_sc, qfar_sc, kaug_sc, vaug_sc, near_sc, far_sc, acc_sc, *, NB):
    BLK = MOBA_BLOCK
    W2 = 2 * BLK
    PAD = NB + 2
    NS = NB // 2 - 1
    h = pl.program_id(1)
    ip = pl.program_id(2)

    @pl.when(ip == 0)
    def _():
        far = far_ref[pl.ds(h, 1), :]
        far_hi = far.astype(BF16).astype(F32)
        far_lo = far - far_hi
        lane = lax.broadcasted_iota(jnp.int32, (1, LANES), 1)
        tail = jnp.where(lane == NB, far_hi, jnp.where(lane == NB + 1, far_lo, 0.0))
        kaug_sc[0:BLK, 0:AT_DH] = jnp.zeros((BLK, AT_DH), BF16)
        kaug_sc[0:BLK, AT_DH:] = jnp.broadcast_to(
            jnp.where(lane == PAD, 1.0, 0.0), (BLK, LANES)).astype(BF16)
        vaug_sc[0:BLK, :] = jnp.zeros((BLK, AT_DH + LANES), BF16)
        kms = []
        for n in range(NB):
            kb = k_ref[n * BLK:(n + 1) * BLK, :]
            kms.append(jnp.mean(kb.astype(F32), axis=0, keepdims=True))
            kaug_sc[(n + 1) * BLK:(n + 2) * BLK, 0:AT_DH] = kb
            kaug_sc[(n + 1) * BLK:(n + 2) * BLK, AT_DH:] = jnp.broadcast_to(
                jnp.where(lane == n, 1.0, tail), (BLK, LANES)).astype(BF16)
        vaug_sc[BLK:, 0:AT_DH] = v_ref[...]
        vaug_sc[BLK:, AT_DH:] = jnp.ones((NB * BLK, LANES), BF16)

        kmean = jnp.concatenate(kms, axis=0)
        km_hi = kmean.astype(BF16)
        km_lo = (kmean - km_hi.astype(F32)).astype(BF16)
        blk = lax.broadcasted_iota(jnp.int32, (NB, BLK), 0)
        rowid = lax.broadcasted_iota(jnp.int32, (LANES - NB, BLK), 0) + NB
        rest = jnp.where(rowid < NB + 2, 1.0, jnp.where(rowid == PAD, MASKED_LOGIT, 0.0))
        for t in range(NB):
            qt = q_ref[t * BLK:(t + 1) * BLK, :]
            gate = jnp.where(blk < t, _dot_nt(km_hi, qt) + _dot_nt(km_lo, qt), -jnp.inf)
            rank = jnp.zeros((NB, BLK), jnp.int32)
            for m in range(t):
                gm = gate[m:m + 1, :]
                rank = rank + ((gm > gate) | ((gm == gate) & (blk > m))).astype(jnp.int32)
            sel = ((blk < t) & (rank < MOBA_TOPK)) | (blk == t)
            near = jnp.where(sel, 0.0, MASKED_LOGIT)
            older = jnp.where(blk >= t - 1, MASKED_LOGIT, near)
            rows = slice(t * BLK, (t + 1) * BLK)
            qnear_sc[rows, 0:AT_DH] = qt
            qfar_sc[rows, 0:AT_DH] = qt
            qnear_sc[rows, AT_DH:] = jnp.concatenate([near, rest], axis=0).T.astype(BF16)
            qfar_sc[rows, AT_DH:] = jnp.concatenate([older, rest], axis=0).T.astype(BF16)

    def lane_max(s):
        out = s[:, 0:LANES]
        for c in range(1, s.shape[1] // LANES):
            out = jnp.maximum(out, s[:, c * LANES:(c + 1) * LANES])
        return out

    def probs(s, m_b):
        return jnp.concatenate(
            [jnp.exp(s[:, c * LANES:(c + 1) * LANES] - m_b) for c in range(s.shape[1] // LANES)],
            axis=1).astype(BF16)

    blocks = (ip, NB - 1 - ip)
    rows = [pl.multiple_of(b * BLK, BLK) for b in blocks]
    n_first = jnp.maximum(ip, 1) // 2
    visits = []
    for st in range(NS):
        second = st >= n_first
        jj = jnp.where(second, st - n_first, st)
        visits.append((second, jnp.where(second, rows[1], rows[0]),
                       pl.multiple_of(jj * W2 + BLK, BLK)))

    mx = []
    for slot in range(2):
        s = (_dot_nt(qnear_sc[pl.ds(rows[slot], BLK), :], kaug_sc[pl.ds(rows[slot], W2), :])
             + dbias_ref[0])
        near_sc[slot] = s
        mx.append(lane_max(s))
    for st, (second, qrow, krow) in enumerate(visits):
        s = _dot_nt(qfar_sc[pl.ds(qrow, BLK), :], kaug_sc[pl.ds(krow, W2), :])
        far_sc[st] = s
        lm = lane_max(s)
        mx[0] = jnp.maximum(mx[0], jnp.where(second, MASKED_LOGIT, lm))
        mx[1] = jnp.maximum(mx[1], jnp.where(second, lm, MASKED_LOGIT))

    m_b = [jnp.broadcast_to(jnp.max(m, axis=-1, keepdims=True), (BLK, LANES)) for m in mx]
    for slot in range(2):
        acc_sc[slot] = _dot(probs(near_sc[slot], m_b[slot]), vaug_sc[pl.ds(rows[slot], W2), :])
    for st, (second, qrow, krow) in enumerate(visits):
        p = probs(far_sc[st], jnp.where(second, m_b[1], m_b[0]))
        acc_sc[second.astype(jnp.int32)] += _dot(p, vaug_sc[pl.ds(krow, W2), :])
    for slot in range(2):
        o_ref[pl.ds(rows[slot], BLK), :] = (
            acc_sc[slot, :, 0:AT_DH] / acc_sc[slot, :, AT_DH:]).astype(o_ref.dtype)


def _moba(q, k, v, rel_table, B, S):
    T, W = q.shape
    H = AT_HEADS
    NB = S // MOBA_BLOCK
    BLK = MOBA_BLOCK
    assert NB + 3 <= LANES and NB % 2 == 0
    assert int(_t5_bucket_np(np.array([BLK + 1]))[0]) == REL_BUCKETS - 1
    dbias = _bias_tiles(rel_table)
    far = jnp.broadcast_to(rel_table[REL_BUCKETS - 1, :].astype(F32)[:, None], (H, LANES))
    head = lambda: pl.BlockSpec((S, AT_DH), lambda b, h, i: (b, h))
    return pl.pallas_call(
        functools.partial(_moba_kernel, NB=NB),
        grid=(B, H, NB // 2),
        in_specs=[head(), head(), head(),
                  pl.BlockSpec((1, BLK, 2 * BLK), lambda b, h, i: (h, 0, 0)),
                  pl.BlockSpec((H, LANES), lambda b, h, i: (0, 0))],
        out_specs=head(),
        out_shape=jax.ShapeDtypeStruct((T, W), BF16),
        scratch_shapes=[pltpu.VMEM((S, AT_DH + LANES), BF16),
                        pltpu.VMEM((S, AT_DH + LANES), BF16),
                        pltpu.VMEM((S + BLK, AT_DH + LANES), BF16),
                        pltpu.VMEM((S + BLK, AT_DH + LANES), BF16),
                        pltpu.VMEM((2, BLK, 2 * BLK), F32),
                        pltpu.VMEM((max(NB // 2 - 1, 1), BLK, 2 * BLK), F32),
                        pltpu.VMEM((2, BLK, AT_DH + LANES), F32)],
        compiler_params=_cparams(("parallel", "parallel", "arbitrary")),
        name="moba",
    )(q, k, v, dbias, far)


def _merge_kernel(ohg_ref, oat_ref, gate_ref, x_ref, whg_ref, wat_ref, wout_ref, g2_ref,
                  x1_ref, h2_ref, *, D):
    y_hg = _dot(ohg_ref[...], whg_ref[...])
    y_at = _dot(oat_ref[...], wat_ref[...])
    merged = (gate_ref[:, 0:D].astype(F32) * y_hg + gate_ref[:, D:2 * D].astype(F32) * y_at)
    x1 = x_ref[...] + _dot(merged.astype(BF16), wout_ref[...])
    x1_ref[...] = x1
    ms = jnp.mean(x1 * x1, axis=-1, keepdims=True)
    h2_ref[...] = (x1 * lax.rsqrt(ms + NORM_EPS) * g2_ref[...]).astype(h2_ref.dtype)


def _merge(ohg, oat, gates, x, whg, wat, wout, g2, tm=512):
    T, D = x.shape
    tok = lambda w: pl.BlockSpec((tm, w), lambda i: (i, 0))
    full = lambda a: pl.BlockSpec(a.shape, lambda i: (0, 0))
    return pl.pallas_call(
        functools.partial(_merge_kernel, D=D),
        grid=(T // tm,),
        in_specs=[tok(ohg.shape[1]), tok(oat.shape[1]), tok(2 * D), tok(D),
                  full(whg), full(wat), full(wout), pl.BlockSpec((1, D), lambda i: (0, 0))],
        out_specs=[tok(D), tok(D)],
        out_shape=[jax.ShapeDtypeStruct((T, D), F32), jax.ShapeDtypeStruct((T, D), BF16)],
        compiler_params=_cparams(("parallel",)),
        name="merge",
    )(ohg, oat, gates, x, whg, wat, wout, g2.reshape(1, D).astype(F32))


def _ffn_kernel(h_ref, x_ref, wg_ref, wu_ref, wd_ref, o_ref, acc_sc):
    f = pl.program_id(1)

    @pl.when(f == 0)
    def _():
        acc_sc[...] = x_ref[...]

    h = h_ref[...]
    a = _dot(h, wg_ref[...])
    u = _dot(h, wu_ref[...])
    acc_sc[...] += _dot((a * jax.nn.sigmoid(a) * u).astype(BF16), wd_ref[...])

    @pl.when(f == pl.num_programs(1) - 1)
    def _():
        o_ref[...] = acc_sc[...]


def _ffn(h2, x1, wg, wu, wd, tm=1024, tf=256):
    T, D = x1.shape
    FF = wg.shape[1]
    return pl.pallas_call(
        _ffn_kernel,
        grid=(T // tm, FF // tf),
        in_specs=[pl.BlockSpec((tm, D), lambda i, f: (i, 0)),
                  pl.BlockSpec((tm, D), lambda i, f: (i, 0)),
                  pl.BlockSpec((D, tf), lambda i, f: (0, f)),
                  pl.BlockSpec((D, tf), lambda i, f: (0, f)),
                  pl.BlockSpec((tf, D), lambda i, f: (f, 0))],
        out_specs=pl.BlockSpec((tm, D), lambda i, f: (i, 0)),
        out_shape=jax.ShapeDtypeStruct((T, D), F32),
        scratch_shapes=[pltpu.VMEM((tm, D), F32)],
        compiler_params=_cparams(("parallel", "arbitrary")),
        name="ffn",
    )(h2, x1, wg, wu, wd)


def kernel(x, attn_norm_g, w_in, hg_lb_gamma, hg_out_norm_g, q_norm_g, k_norm_g, rel_bias_table,
           w_branch_hg, w_branch_attn, w_out, ffn_norm_g, w_ffn_gate, w_ffn_up, w_ffn_down):
    B, S, D = x.shape
    T = B * S
    depth = attn_norm_g.shape[0]
    assert depth == 1 and S % MOBA_BLOCK == 0 and S % HG_CHUNK == 0
    WH = HG_HEADS * HG_DK
    WV = HG_HEADS * HG_DV
    WA = AT_HEADS * AT_DH
    assert w_in.shape[2] == 2 * WH + 2 * WV + 3 * WA + 2 * D

    xt = x.reshape(T, D)
    for l in range(depth):
        w = w_in[l].astype(BF16)
        h = _rmsnorm(xt, attn_norm_g[l])
        c = 0
        (hq,) = _proj(h, w, c, WH, functools.partial(_ep_scale, scale=HG_DK ** -0.5), [],
                      [BF16], "proj_hq")
        c += WH
        lf, hk = _proj(h, w, c, WH, _ep_forget, [hg_lb_gamma.astype(F32)], [F32, BF16],
                       "proj_hf")
        c += WH
        (hi,) = _proj(h, w, c, WV, _ep_identity, [], [BF16], "proj_hi")
        c += WV
        (hg,) = _proj(h, w, c, WV, _ep_silu, [], [BF16], "proj_hg")
        c += WV
        qg = jnp.tile(q_norm_g[l].astype(F32), AT_HEADS).reshape(1, WA)
        kg = jnp.tile(k_norm_g[l].astype(F32), AT_HEADS).reshape(1, WA)
        (aq,) = _proj(h, w, c, WA, functools.partial(_ep_headnorm, scale=AT_DH ** -0.5), [qg],
                      [BF16], "proj_aq")
        c += WA
        (ak,) = _proj(h, w, c, WA, functools.partial(_ep_headnorm, scale=1.0), [kg], [BF16],
                      "proj_ak")
        c += WA
        (av,) = _proj(h, w, c, WA, _ep_identity, [], [BF16], "proj_av")
        c += WA
        (gates,) = _proj(h, w, c, 2 * D, _ep_sigmoid, [], [BF16], "proj_gate")

        o_hg = _hgrn(hq, lf, hk, hi, hg, hg_out_norm_g[l], B, S)
        o_at = _moba(aq, ak, av, rel_bias_table, B, S)
        x1, h2 = _merge(o_hg, o_at, gates, xt, w_branch_hg[l].astype(BF16),
                        w_branch_attn[l].astype(BF16), w_out[l].astype(BF16), ffn_norm_g[l])
        xt = _ffn(h2, x1, w_ffn_gate[l].astype(BF16), w_ffn_up[l].astype(BF16),
                  w_ffn_down[l].astype(BF16))
    return xt.reshape(B, S, D)
```

```python
import functools
import math

import numpy as np
import jax
import jax.numpy as jnp
from jax import lax
from jax.experimental import pallas as pl
from jax.experimental.pallas import tpu as pltpu

F32 = jnp.float32
BF16 = jnp.bfloat16

LANES = 128
NORM_EPS = 1e-6

HG_HEADS = 8
HG_DK = 128
HG_DV = 128
HG_CHUNK = 64
HG_STEP_TOKENS = 512
HG_HEAD_GROUP = 8
AT_HEADS = 8
AT_DH = 128
MOBA_BLOCK = 256
MOBA_TOPK = 3
REL_BUCKETS = 32
REL_MAX_DIST = 128
MASKED_LOGIT = -1e30

VMEM_LIMIT = 56 * 1024 * 1024


def _cparams(sem):
    return pltpu.CompilerParams(dimension_semantics=sem, vmem_limit_bytes=VMEM_LIMIT)


def _dot(a, b):
    return jnp.dot(a, b, preferred_element_type=F32)


def _dot_nt(a, b):
    return lax.dot_general(a, b, (((1,), (1,)), ((), ())), preferred_element_type=F32)


def _dot_tn(a, b):
    return lax.dot_general(a, b, (((0,), (0,)), ((), ())), preferred_element_type=F32)


def _resident(a):
    return pl.BlockSpec(a.shape, lambda *_: (0,) * a.ndim, pipeline_mode=pl.Buffered(1))


def _head_rmsnorm(acc, gain, scale):
    cols = []
    for c in range(acc.shape[1] // LANES):
        blk = acc[:, c * LANES:(c + 1) * LANES]
        ms = jnp.mean(blk * blk, axis=-1, keepdims=True)
        cols.append(blk * lax.rsqrt(ms + NORM_EPS) * (gain * scale))
    return jnp.concatenate(cols, axis=1)


def _inproj_kernel(x_ref, g_ref, w_ref, gamma_ref, qg_ref, kg_ref,
                   hq_ref, lf_ref, hk_ref, hi_ref, hg_ref, aq_ref, ak_ref, av_ref, gate_ref,
                   *, WH, WV, WA):
    x = x_ref[...]
    ms = jnp.mean(x * x, axis=-1, keepdims=True)
    h = (x * lax.rsqrt(ms + NORM_EPS) * g_ref[...]).astype(BF16)
    col = [0]

    def segment(width):
        acc = _dot(h, w_ref[:, col[0]:col[0] + width])
        col[0] += width
        return acc

    hq_ref[...] = (segment(WH) * HG_DK ** -0.5).astype(hq_ref.dtype)

    z = segment(WH)
    gamma = gamma_ref[...]
    eg = jnp.exp(gamma - jnp.max(gamma, axis=0, keepdims=True))
    lb = eg[0:1, :] / jnp.sum(eg, axis=0, keepdims=True)
    lf_ref[...] = jnp.log2(lb + (1.0 - lb) * jax.nn.sigmoid(z))
    hk_ref[...] = ((1.0 - lb) * jax.nn.sigmoid(-z)).astype(hk_ref.dtype)

    hi_ref[...] = segment(WV).astype(hi_ref.dtype)
    g = segment(WV)
    hg_ref[...] = (g * jax.nn.sigmoid(g)).astype(hg_ref.dtype)
    aq_ref[...] = _head_rmsnorm(segment(WA), qg_ref[...], AT_DH ** -0.5).astype(aq_ref.dtype)
    ak_ref[...] = _head_rmsnorm(segment(WA), kg_ref[...], 1.0).astype(ak_ref.dtype)
    av_ref[...] = segment(WA).astype(av_ref.dtype)
    half = gate_ref.shape[1] // 2
    gate_ref[:, 0:half] = jax.nn.sigmoid(segment(half)).astype(gate_ref.dtype)
    gate_ref[:, half:] = jax.nn.sigmoid(segment(half)).astype(gate_ref.dtype)


def _inproj(x, g, w, gamma, qg, kg, WH, WV, WA, tm=512):
    T, D = x.shape
    tok = lambda width: pl.BlockSpec((tm, width), lambda i: (i, 0))
    small = lambda a: pl.BlockSpec(a.shape, lambda i: (0, 0))
    widths = [WH, WH, WH, WV, WV, WA, WA, WA, 2 * D]
    dtypes = [BF16, F32, BF16, BF16, BF16, BF16, BF16, BF16, BF16]
    g = g.reshape(1, D).astype(F32)
    gamma = gamma.astype(F32)
    qg = qg.reshape(1, AT_DH).astype(F32)
    kg = kg.reshape(1, AT_DH).astype(F32)
    return pl.pallas_call(
        functools.partial(_inproj_kernel, WH=WH, WV=WV, WA=WA),
        grid=(T // tm,),
        in_specs=[tok(D), small(g), _resident(w), small(gamma), small(qg), small(kg)],
        out_specs=[tok(wd) for wd in widths],
        out_shape=[jax.ShapeDtypeStruct((T, wd), dt) for wd, dt in zip(widths, dtypes)],
        compiler_params=_cparams(("parallel",)),
        name="inproj",
    )(x, g, w, gamma, qg, kg)


def _hgrn_tables(C):
    nl = int(math.log2(C))
    assert 1 << nl == C and nl >= 3
    masks = np.zeros((nl + 1, C, C), np.float32)
    masks[0] = np.eye(C)
    t = np.arange(C)
    for L in range(1, nl + 1):
        blk, half = 1 << L, 1 << (L - 1)
        base = (t // blk) * blk
        upper = (t - base) >= half
        same = base[:, None] == base[None, :]
        masks[L] = (same & upper[:, None] & (~upper)[None, :]).astype(np.float32)
    return np.tril(np.ones((C, C), np.float32)), masks, nl


def _hgrn_kernel(q_ref, lf_ref, k_ref, v_ref, g_ref, gain_ref, tri_ref, m_ref, o_ref,
                 e_sc, d_sc, st_sc, *, C, nl, heads, chunks):
    @pl.when(pl.program_id(1) == 0)
    def _():
        st_sc[...] = jnp.zeros_like(st_sc)

    tri = tri_ref[...]
    gain = gain_ref[...]
    row = lax.broadcasted_iota(jnp.int32, (C, lf_ref.shape[1]), 0)
    odd = (row & 1) == 1
    ph = row & 3

    def chunk_rows(c):
        return pl.ds(pl.multiple_of(c * C, C), C)

    def prepare(c, buf):
        lf = lf_ref[chunk_rows(c), :]
        hi = lf.astype(BF16)
        r1 = lf - hi.astype(F32)
        mid = r1.astype(BF16)
        lo = (r1 - mid.astype(F32)).astype(BF16)
        b = _dot(tri, hi) + _dot(tri, mid) + _dot(tri, lo)
        e_sc[buf, 0:C, :] = jnp.exp2(jnp.where(odd, lf, 0.0)).astype(BF16)
        prev = pltpu.roll(lf, 1, axis=0)
        nxt = pltpu.roll(lf, C - 1, axis=0)
        x2 = jnp.where(ph == 0, nxt, jnp.where(ph == 1, 0.0, jnp.where(ph == 2, lf, lf + prev)))
        e_sc[buf, C:2 * C, :] = jnp.exp2(x2).astype(BF16)
        for L in range(3, nl + 1):
            blk, half = 1 << L, 1 << (L - 1)
            pieces = [b[base:base + blk, :] - b[base + half - 1:base + half, :]
                      for base in range(0, C, blk)]
            d = pieces[0] if len(pieces) == 1 else jnp.concatenate(pieces, axis=0)
            e_sc[buf, (L - 1) * C:L * C, :] = jnp.exp2(-jnp.abs(d)).astype(BF16)
        b_last = b[C - 1:C, :]
        e_sc[buf, nl * C:(nl + 1) * C, :] = jnp.exp2(b).astype(BF16)
        e_sc[buf, (nl + 1) * C:(nl + 2) * C, :] = jnp.exp2(b_last - b).astype(BF16)
        d_sc[buf] = jnp.exp2(b_last)

    def heads_of(c, buf):
        rows = chunk_rows(c)
        for h0 in range(0, heads, HG_HEAD_GROUP):
            hs = range(h0, h0 + HG_HEAD_GROUP)
            sls = [slice(h * HG_DK, (h + 1) * HG_DK) for h in hs]
            qb = [q_ref[rows, sl] for sl in sls]
            kb = [k_ref[rows, sl] for sl in sls]
            vb = [v_ref[rows, sl] for sl in sls]
            A = [m_ref[0] * _dot_nt(q, k) for q, k in zip(qb, kb)]
            for L in range(1, nl + 1):
                for j, sl in enumerate(sls):
                    eL = e_sc[buf, (L - 1) * C:L * C, sl]
                    A[j] = A[j] + m_ref[L] * _dot_nt(qb[j] * eL, kb[j] * eL)
            for j, (h, sl) in enumerate(zip(hs, sls)):
                eb = e_sc[buf, nl * C:(nl + 1) * C, sl]
                ek = e_sc[buf, (nl + 1) * C:(nl + 2) * C, sl]
                st = st_sc[h]
                o = _dot(A[j].astype(BF16), vb[j]) + _dot_nt(qb[j] * eb, st.astype(BF16))
                st_sc[h] = st * d_sc[buf, :, sl] + _dot_tn(vb[j], kb[j] * ek)
                ms = jnp.mean(o * o, axis=-1, keepdims=True)
                y = o * lax.rsqrt(ms + NORM_EPS) * gain * g_ref[rows, sl].astype(F32)
                o_ref[rows, sl] = y.astype(o_ref.dtype)

    prepare(0, 0)

    def step(c, carry):
        prepare(jnp.minimum(c + 1, chunks - 1), (c + 1) & 1)
        heads_of(c, c & 1)
        return carry

    lax.fori_loop(0, chunks, step, 0)


def _hgrn(q, lf, k, v, g, gain, B, S):
    T, W = q.shape
    C = HG_CHUNK
    TS = HG_STEP_TOKENS
    assert S % TS == 0 and TS % C == 0
    N = S // TS
    tri, masks, nl = _hgrn_tables(C)
    row = lambda b, c: (b * N + c, 0)
    tok = lambda: pl.BlockSpec((TS, W), row)
    return pl.pallas_call(
        functools.partial(_hgrn_kernel, C=C, nl=nl, heads=HG_HEADS, chunks=TS // C),
        grid=(B, N),
        in_specs=[tok(), tok(), tok(), tok(), tok(),
                  pl.BlockSpec((1, HG_DV), lambda b, c: (0, 0)),
                  pl.BlockSpec(tri.shape, lambda b, c: (0, 0)),
                  pl.BlockSpec(masks.shape, lambda b, c: (0, 0, 0))],
        out_specs=tok(),
        out_shape=jax.ShapeDtypeStruct((T, W), BF16),
        scratch_shapes=[pltpu.VMEM((2, (nl + 2) * C, W), BF16),
                        pltpu.VMEM((2, 1, W), F32),
                        pltpu.VMEM((HG_HEADS, HG_DV, HG_DK), F32)],
        compiler_params=_cparams(("parallel", "arbitrary")),
        name="hgrn2",
    )(q, lf, k, v, g, gain.reshape(1, HG_DV).astype(F32),
      jnp.asarray(tri, BF16), jnp.asarray(masks, F32))


def _t5_bucket_np(dist):
    n = np.maximum(dist, 0)
    max_exact = REL_BUCKETS // 2
    nf = np.maximum(n, 1).astype(np.float32)
    large = max_exact + (np.log(nf / max_exact) / math.log(REL_MAX_DIST / max_exact)
                         * (REL_BUCKETS - max_exact)).astype(np.int32)
    large = np.minimum(large, REL_BUCKETS - 1)
    return np.where(n < max_exact, n, large).astype(np.int32)


def _bias_kernel(tab_ref, bucket_ref, o_ref):
    h = pl.program_id(0)
    bucket = bucket_ref[...]
    acc = jnp.zeros(bucket.shape, F32)
    for b in range(REL_BUCKETS):
        acc = jnp.where(bucket == b, tab_ref[b, h], acc)
    delta = acc - tab_ref[REL_BUCKETS - 1, h]
    blk = bucket.shape[0]
    r = lax.broadcasted_iota(jnp.int32, bucket.shape, 0)
    c = lax.broadcasted_iota(jnp.int32, bucket.shape, 1)
    o_ref[0] = jnp.where(c - blk <= r, delta, MASKED_LOGIT)


def _bias_tiles(rel_table):
    BLK = MOBA_BLOCK
    t = np.arange(BLK)[:, None]
    s = np.arange(2 * BLK)[None, :]
    bucket = _t5_bucket_np(t + BLK - s)
    H = rel_table.shape[1]
    return pl.pallas_call(
        _bias_kernel,
        grid=(H,),
        in_specs=[pl.BlockSpec(memory_space=pltpu.SMEM),
                  pl.BlockSpec(bucket.shape, lambda h: (0, 0))],
        out_specs=pl.BlockSpec((1, BLK, 2 * BLK), lambda h: (h, 0, 0)),
        out_shape=jax.ShapeDtypeStruct((H, BLK, 2 * BLK), F32),
        compiler_params=_cparams(("arbitrary",)),
        name="relbias",
    )(rel_table.astype(F32), jnp.asarray(bucket))


def _moba_kernel(q_ref, k_ref, v_ref, dbias_ref, far_ref, o_ref,
                 qnear_sc, qfar_sc, kaug_sc, vaug_sc, near_sc, far_sc, acc_sc, *, NB):
    BLK = MOBA_BLOCK
    W2 = 2 * BLK
    PAD = NB + 2
    NS = NB // 2 - 1
    h = pl.program_id(1)
    ip = pl.program_id(2)

    @pl.when(ip == 0)
    def _():
        far = far_ref[pl.ds(h, 1), :]
        far_hi = far.astype(BF16).astype(F32)
        far_lo = far - far_hi
        lane = lax.broadcasted_iota(jnp.int32, (1, LANES), 1)
        tail = jnp.where(lane == NB, far_hi, jnp.where(lane == NB + 1, far_lo, 0.0))
        kaug_sc[0:BLK, 0:AT_DH] = jnp.zeros((BLK, AT_DH), BF16)
        kaug_sc[0:BLK, AT_DH:] = jnp.broadcast_to(
            jnp.where(lane == PAD, 1.0, 0.0), (BLK, LANES)).astype(BF16)
        vaug_sc[0:BLK, :] = jnp.zeros((BLK, AT_DH + LANES), BF16)
        kms = []
        for n in range(NB):
            kb = k_ref[n * BLK:(n + 1) * BLK, :]
            kms.append(jnp.mean(kb.astype(F32), axis=0, keepdims=True))
            kaug_sc[(n + 1) * BLK:(n + 2) * BLK, 0:AT_DH] = kb
            kaug_sc[(n + 1) * BLK:(n + 2) * BLK, AT_DH:] = jnp.broadcast_to(
                jnp.where(lane == n, 1.0, tail), (BLK, LANES)).astype(BF16)
        vaug_sc[BLK:, 0:AT_DH] = v_ref[...]
        vaug_sc[BLK:, AT_DH:] = jnp.ones((NB * BLK, LANES), BF16)

        kmean = jnp.concatenate(kms, axis=0)
        km_hi = kmean.astype(BF16)
        km_lo = (kmean - km_hi.astype(F32)).astype(BF16)
        blk = lax.broadcasted_iota(jnp.int32, (NB, BLK), 0)
        rowid = lax.broadcasted_iota(jnp.int32, (LANES - NB, BLK), 0) + NB
        rest = jnp.where(rowid < NB + 2, 1.0, jnp.where(rowid == PAD, MASKED_LOGIT, 0.0))
        for t in range(NB):
            qt = q_ref[t * BLK:(t + 1) * BLK, :]
            gate = jnp.where(blk < t, _dot_nt(km_hi, qt) + _dot_nt(km_lo, qt), -jnp.inf)
            rank = jnp.zeros((NB, BLK), jnp.int32)
            for m in range(t):
                gm = gate[m:m + 1, :]
                rank = rank + ((gm > gate) | ((gm == gate) & (blk > m))).astype(jnp.int32)
            sel = ((blk < t) & (rank < MOBA_TOPK)) | (blk == t)
            near = jnp.where(sel, 0.0, MASKED_LOGIT)
            older = jnp.where(blk >= t - 1, MASKED_LOGIT, near)
            rows = slice(t * BLK, (t + 1) * BLK)
            qnear_sc[rows, 0:AT_DH] = qt
            qfar_sc[rows, 0:AT_DH] = qt
            qnear_sc[rows, AT_DH:] = jnp.concatenate([near, rest], axis=0).T.astype(BF16)
            qfar_sc[rows, AT_DH:] = jnp.concatenate([older, rest], axis=0).T.astype(BF16)

    def lane_max(s):
        out = s[:, 0:LANES]
        for c in range(1, s.shape[1] // LANES):
            out = jnp.maximum(out, s[:, c * LANES:(c + 1) * LANES])
        return out

    def probs(s, m_b):
        return jnp.concatenate(
            [jnp.exp(s[:, c * LANES:(c + 1) * LANES] - m_b) for c in range(s.shape[1] // LANES)],
            axis=1).astype(BF16)

    blocks = (ip, NB - 1 - ip)
    rows = [pl.multiple_of(b * BLK, BLK) for b in blocks]
    n_first = jnp.maximum(ip, 1) // 2
    visits = []
    for st in range(NS):
        second = st >= n_first
        jj = jnp.where(second, st - n_first, st)
        visits.append((second, jnp.where(second, rows[1], rows[0]),
                       pl.multiple_of(jj * W2 + BLK, BLK)))

    mx = []
    for slot in range(2):
        s = (_dot_nt(qnear_sc[pl.ds(rows[slot], BLK), :], kaug_sc[pl.ds(rows[slot], W2), :])
             + dbias_ref[0])
        near_sc[slot] = s
        mx.append(lane_max(s))
    for st, (second, qrow, krow) in enumerate(visits):
        s = _dot_nt(qfar_sc[pl.ds(qrow, BLK), :], kaug_sc[pl.ds(krow, W2), :])
        far_sc[st] = s
        lm = lane_max(s)
        mx[0] = jnp.maximum(mx[0], jnp.where(second, MASKED_LOGIT, lm))
        mx[1] = jnp.maximum(mx[1], jnp.where(second, lm, MASKED_LOGIT))

    m_b = [jnp.broadcast_to(jnp.max(m, axis=-1, keepdims=True), (BLK, LANES)) for m in mx]
    for slot in range(2):
        acc_sc[slot] = _dot(probs(near_sc[slot], m_b[slot]), vaug_sc[pl.ds(rows[slot], W2), :])
    for st, (second, qrow, krow) in enumerate(visits):
        p = probs(far_sc[st], jnp.where(second, m_b[1], m_b[0]))
        acc_sc[second.astype(jnp.int32)] += _dot(p, vaug_sc[pl.ds(krow, W2), :])
    for slot in range(2):
        o_ref[pl.ds(rows[slot], BLK), :] = (
            acc_sc[slot, :, 0:AT_DH] / acc_sc[slot, :, AT_DH:]).astype(o_ref.dtype)


def _moba(q, k, v, rel_table, B, S):
    T, W = q.shape
    H = AT_HEADS
    NB = S // MOBA_BLOCK
    BLK = MOBA_BLOCK
    assert NB + 3 <= LANES and NB % 2 == 0
    assert int(_t5_bucket_np(np.array([BLK + 1]))[0]) == REL_BUCKETS - 1
    dbias = _bias_tiles(rel_table)
    far = jnp.broadcast_to(rel_table[REL_BUCKETS - 1, :].astype(F32)[:, None], (H, LANES))
    head = lambda: pl.BlockSpec((S, AT_DH), lambda b, h, i: (b, h))
    return pl.pallas_call(
        functools.partial(_moba_kernel, NB=NB),
        grid=(B, H, NB // 2),
        in_specs=[head(), head(), head(),
                  pl.BlockSpec((1, BLK, 2 * BLK), lambda b, h, i: (h, 0, 0)),
                  pl.BlockSpec((H, LANES), lambda b, h, i: (0, 0))],
        out_specs=head(),
        out_shape=jax.ShapeDtypeStruct((T, W), BF16),
        scratch_shapes=[pltpu.VMEM((S, AT_DH + LANES), BF16),
                        pltpu.VMEM((S, AT_DH + LANES), BF16),
                        pltpu.VMEM((S + BLK, AT_DH + LANES), BF16),
                        pltpu.VMEM((S + BLK, AT_DH + LANES), BF16),
                        pltpu.VMEM((2, BLK, 2 * BLK), F32),
                        pltpu.VMEM((max(NB // 2 - 1, 1), BLK, 2 * BLK), F32),
                        pltpu.VMEM((2, BLK, AT_DH + LANES), F32)],
        compiler_params=_cparams(("parallel", "parallel", "arbitrary")),
        name="moba",
    )(q, k, v, dbias, far)


def _merge_kernel(ohg_ref, oat_ref, gate_ref, x_ref, whg_ref, wat_ref, wout_ref, g2_ref,
                  x1_ref, h2_ref, *, D):
    y_hg = _dot(ohg_ref[...], whg_ref[...])
    y_at = _dot(oat_ref[...], wat_ref[...])
    merged = (gate_ref[:, 0:D].astype(F32) * y_hg + gate_ref[:, D:2 * D].astype(F32) * y_at)
    x1 = x_ref[...] + _dot(merged.astype(BF16), wout_ref[...])
    x1_ref[...] = x1
    ms = jnp.mean(x1 * x1, axis=-1, keepdims=True)
    h2_ref[...] = (x1 * lax.rsqrt(ms + NORM_EPS) * g2_ref[...]).astype(h2_ref.dtype)


def _merge(ohg, oat, gates, x, whg, wat, wout, g2, tm=512):
    T, D = x.shape
    tok = lambda w: pl.BlockSpec((tm, w), lambda i: (i, 0))
    full = lambda a: pl.BlockSpec(a.shape, lambda i: (0, 0))
    return pl.pallas_call(
        functools.partial(_merge_kernel, D=D),
        grid=(T // tm,),
        in_specs=[tok(ohg.shape[1]), tok(oat.shape[1]), tok(2 * D), tok(D),
                  full(whg), full(wat), full(wout), pl.BlockSpec((1, D), lambda i: (0, 0))],
        out_specs=[tok(D), tok(D)],
        out_shape=[jax.ShapeDtypeStruct((T, D), F32), jax.ShapeDtypeStruct((T, D), BF16)],
        compiler_params=_cparams(("parallel",)),
        name="merge",
    )(ohg, oat, gates, x, whg, wat, wout, g2.reshape(1, D).astype(F32))


def _ffn_kernel(h_ref, x_ref, wg_ref, wu_ref, wd_ref, o_ref, *, tf):
    h = h_ref[...]
    acc = x_ref[...]
    for f0 in range(0, wg_ref.shape[1], tf):
        a = _dot(h, wg_ref[:, f0:f0 + tf])
        u = _dot(h, wu_ref[:, f0:f0 + tf])
        acc = acc + _dot((a * jax.nn.sigmoid(a) * u).astype(BF16), wd_ref[f0:f0 + tf, :])
    o_ref[...] = acc


def _ffn(h2, x1, wg, wu, wd, tm=512, tf=2816):
    T, D = x1.shape
    FF = wg.shape[1]
    assert FF % tf == 0
    tok = lambda: pl.BlockSpec((tm, D), lambda i: (i, 0))
    return pl.pallas_call(
        functools.partial(_ffn_kernel, tf=tf),
        grid=(T // tm,),
        in_specs=[tok(), tok(), _resident(wg), _resident(wu), _resident(wd)],
        out_specs=tok(),
        out_shape=jax.ShapeDtypeStruct((T, D), F32),
        compiler_params=_cparams(("parallel",)),
        name="ffn",
    )(h2, x1, wg, wu, wd)


def kernel(x, attn_norm_g, w_in, hg_lb_gamma, hg_out_norm_g, q_norm_g, k_norm_g, rel_bias_table,
           w_branch_hg, w_branch_attn, w_out, ffn_norm_g, w_ffn_gate, w_ffn_up, w_ffn_down):
    B, S, D = x.shape
    T = B * S
    depth = attn_norm_g.shape[0]
    assert depth == 1 and S % MOBA_BLOCK == 0 and S % HG_CHUNK == 0
    WH = HG_HEADS * HG_DK
    WV = HG_HEADS * HG_DV
    WA = AT_HEADS * AT_DH
    assert w_in.shape[2] == 2 * WH + 2 * WV + 3 * WA + 2 * D

    xt = x.reshape(T, D)
    for l in range(depth):
        hq, lf, hk, hi, hg, aq, ak, av, gates = _inproj(
            xt, attn_norm_g[l], w_in[l].astype(BF16), hg_lb_gamma, q_norm_g[l], k_norm_g[l],
            WH, WV, WA)
        o_hg = _hgrn(hq, lf, hk, hi, hg, hg_out_norm_g[l], B, S)
        o_at = _moba(aq, ak, av, rel_bias_table, B, S)
        x1, h2 = _merge(o_hg, o_at, gates, xt, w_branch_hg[l].astype(BF16),
                        w_branch_attn[l].astype(BF16), w_out[l].astype(BF16), ffn_norm_g[l])
        xt = _ffn(h2, x1, w_ffn_gate[l].astype(BF16), w_ffn_up[l].astype(BF16),
                  w_ffn_down[l].astype(BF16))
    return xt.reshape(B, S, D)
```

```python
import functools
import math

import numpy as np
import jax
import jax.numpy as jnp
from jax import lax
from jax.experimental import pallas as pl
from jax.experimental.pallas import tpu as pltpu

F32 = jnp.float32
BF16 = jnp.bfloat16

LANES = 128
NORM_EPS = 1e-6

HG_HEADS = 8
HG_DK = 128
HG_DV = 128
HG_CHUNK = 64
HG_STEP_TOKENS = 512
HG_HEAD_GROUP = 8
AT_HEADS = 8
AT_DH = 128
MOBA_BLOCK = 256
MOBA_TOPK = 3
REL_BUCKETS = 32
REL_MAX_DIST = 128
MASKED_LOGIT = -1e30
LOG2E = math.log2(math.e)

VMEM_LIMIT = 56 * 1024 * 1024


def _cparams(sem):
    return pltpu.CompilerParams(dimension_semantics=sem, vmem_limit_bytes=VMEM_LIMIT)


def _dot(a, b):
    return jnp.dot(a, b, preferred_element_type=F32)


def _dot_nt(a, b):
    return lax.dot_general(a, b, (((1,), (1,)), ((), ())), preferred_element_type=F32)


def _dot_tn(a, b):
    return lax.dot_general(a, b, (((0,), (0,)), ((), ())), preferred_element_type=F32)


def _resident(a):
    return pl.BlockSpec(a.shape, lambda *_: (0,) * a.ndim, pipeline_mode=pl.Buffered(1))


def _head_rmsnorm(acc, gain, scale):
    cols = []
    for c in range(acc.shape[1] // LANES):
        blk = acc[:, c * LANES:(c + 1) * LANES]
        ms = jnp.mean(blk * blk, axis=-1, keepdims=True)
        cols.append(blk * lax.rsqrt(ms + NORM_EPS) * (gain * scale))
    return jnp.concatenate(cols, axis=1)


def _inproj_kernel(x_ref, g_ref, w_ref, gamma_ref, qg_ref, kg_ref,
                   hq_ref, lf_ref, hk_ref, hi_ref, hg_ref, aq_ref, ak_ref, av_ref, gate_ref,
                   *, WH, WV, WA):
    x = x_ref[...]
    ms = jnp.mean(x * x, axis=-1, keepdims=True)
    h = (x * lax.rsqrt(ms + NORM_EPS) * g_ref[...]).astype(BF16)
    col = [0]

    def segment(width):
        acc = _dot(h, w_ref[:, col[0]:col[0] + width])
        col[0] += width
        return acc

    hq_ref[...] = (segment(WH) * HG_DK ** -0.5).astype(hq_ref.dtype)

    z = segment(WH)
    gamma = gamma_ref[...]
    eg = jnp.exp(gamma - jnp.max(gamma, axis=0, keepdims=True))
    lb = eg[0:1, :] / jnp.sum(eg, axis=0, keepdims=True)
    lf_ref[...] = jnp.log2(lb + (1.0 - lb) * jax.nn.sigmoid(z))
    hk_ref[...] = ((1.0 - lb) * jax.nn.sigmoid(-z)).astype(hk_ref.dtype)

    hi_ref[...] = segment(WV).astype(hi_ref.dtype)
    g = segment(WV)
    hg_ref[...] = (g * jax.nn.sigmoid(g)).astype(hg_ref.dtype)
    aq_ref[...] = _head_rmsnorm(segment(WA), qg_ref[...],
                                AT_DH ** -0.5 * LOG2E).astype(aq_ref.dtype)
    ak_ref[...] = _head_rmsnorm(segment(WA), kg_ref[...], 1.0).astype(ak_ref.dtype)
    av_ref[...] = segment(WA).astype(av_ref.dtype)
    half = gate_ref.shape[1] // 2
    gate_ref[:, 0:half] = jax.nn.sigmoid(segment(half)).astype(gate_ref.dtype)
    gate_ref[:, half:] = jax.nn.sigmoid(segment(half)).astype(gate_ref.dtype)


def _inproj(x, g, w, gamma, qg, kg, WH, WV, WA, tm=512):
    T, D = x.shape
    tok = lambda width: pl.BlockSpec((tm, width), lambda i: (i, 0))
    small = lambda a: pl.BlockSpec(a.shape, lambda i: (0, 0))
    widths = [WH, WH, WH, WV, WV, WA, WA, WA, 2 * D]
    dtypes = [BF16, F32, BF16, BF16, BF16, BF16, BF16, BF16, BF16]
    g = g.reshape(1, D).astype(F32)
    gamma = gamma.astype(F32)
    qg = qg.reshape(1, AT_DH).astype(F32)
    kg = kg.reshape(1, AT_DH).astype(F32)
    return pl.pallas_call(
        functools.partial(_inproj_kernel, WH=WH, WV=WV, WA=WA),
        grid=(T // tm,),
        in_specs=[tok(D), small(g), _resident(w), small(gamma), small(qg), small(kg)],
        out_specs=[tok(wd) for wd in widths],
        out_shape=[jax.ShapeDtypeStruct((T, wd), dt) for wd, dt in zip(widths, dtypes)],
        compiler_params=_cparams(("parallel",)),
        name="inproj",
    )(x, g, w, gamma, qg, kg)


def _hgrn_tables(C):
    nl = int(math.log2(C))
    assert 1 << nl == C and nl >= 3
    masks = np.zeros((nl + 1, C, C), np.float32)
    masks[0] = np.eye(C)
    t = np.arange(C)
    for L in range(1, nl + 1):
        blk, half = 1 << L, 1 << (L - 1)
        base = (t // blk) * blk
        upper = (t - base) >= half
        same = base[:, None] == base[None, :]
        masks[L] = (same & upper[:, None] & (~upper)[None, :]).astype(np.float32)
    return np.tril(np.ones((C, C), np.float32)), masks, nl


def _hgrn_kernel(q_ref, lf_ref, k_ref, v_ref, g_ref, gain_ref, tri_ref, m_ref, o_ref,
                 e_sc, d_sc, st_sc, *, C, nl, heads, chunks):
    @pl.when(pl.program_id(1) == 0)
    def _():
        st_sc[...] = jnp.zeros_like(st_sc)

    tri = tri_ref[...]
    gain = gain_ref[...]
    row = lax.broadcasted_iota(jnp.int32, (C, lf_ref.shape[1]), 0)
    odd = (row & 1) == 1
    ph = row & 3

    def chunk_rows(c):
        return pl.ds(pl.multiple_of(c * C, C), C)

    def prepare(c, buf):
        lf = lf_ref[chunk_rows(c), :]
        hi = lf.astype(BF16)
        r1 = lf - hi.astype(F32)
        mid = r1.astype(BF16)
        lo = (r1 - mid.astype(F32)).astype(BF16)
        b = _dot(tri, hi) + _dot(tri, mid) + _dot(tri, lo)
        e_sc[buf, 0:C, :] = jnp.exp2(jnp.where(odd, lf, 0.0)).astype(BF16)
        prev = pltpu.roll(lf, 1, axis=0)
        nxt = pltpu.roll(lf, C - 1, axis=0)
        x2 = jnp.where(ph == 0, nxt, jnp.where(ph == 1, 0.0, jnp.where(ph == 2, lf, lf + prev)))
        e_sc[buf, C:2 * C, :] = jnp.exp2(x2).astype(BF16)
        for L in range(3, nl + 1):
            blk, half = 1 << L, 1 << (L - 1)
            pieces = [b[base:base + blk, :] - b[base + half - 1:base + half, :]
                      for base in range(0, C, blk)]
            d = pieces[0] if len(pieces) == 1 else jnp.concatenate(pieces, axis=0)
            e_sc[buf, (L - 1) * C:L * C, :] = jnp.exp2(-jnp.abs(d)).astype(BF16)
        b_last = b[C - 1:C, :]
        e_sc[buf, nl * C:(nl + 1) * C, :] = jnp.exp2(b).astype(BF16)
        e_sc[buf, (nl + 1) * C:(nl + 2) * C, :] = jnp.exp2(b_last - b).astype(BF16)
        d_sc[buf] = jnp.exp2(b_last)

    def heads_of(c, buf):
        rows = chunk_rows(c)
        for h0 in range(0, heads, HG_HEAD_GROUP):
            hs = range(h0, h0 + HG_HEAD_GROUP)
            sls = [slice(h * HG_DK, (h + 1) * HG_DK) for h in hs]
            qb = [q_ref[rows, sl] for sl in sls]
            kb = [k_ref[rows, sl] for sl in sls]
            vb = [v_ref[rows, sl] for sl in sls]
            A = [m_ref[0] * _dot_nt(q, k) for q, k in zip(qb, kb)]
            for L in range(1, nl + 1):
                for j, sl in enumerate(sls):
                    eL = e_sc[buf, (L - 1) * C:L * C, sl]
                    A[j] = A[j] + m_ref[L] * _dot_nt(qb[j] * eL, kb[j] * eL)
            for j, (h, sl) in enumerate(zip(hs, sls)):
                eb = e_sc[buf, nl * C:(nl + 1) * C, sl]
                ek = e_sc[buf, (nl + 1) * C:(nl + 2) * C, sl]
                st = st_sc[h]
                o = _dot(A[j].astype(BF16), vb[j]) + _dot_nt(qb[j] * eb, st.astype(BF16))
                st_sc[h] = st * d_sc[buf, :, sl] + _dot_tn(vb[j], kb[j] * ek)
                ms = jnp.mean(o * o, axis=-1, keepdims=True)
                y = o * lax.rsqrt(ms + NORM_EPS) * gain * g_ref[rows, sl].astype(F32)
                o_ref[rows, sl] = y.astype(o_ref.dtype)

    prepare(0, 0)

    def step(c, carry):
        prepare(jnp.minimum(c + 1, chunks - 1), (c + 1) & 1)
        heads_of(c, c & 1)
        return carry

    lax.fori_loop(0, chunks, step, 0)


def _hgrn(q, lf, k, v, g, gain, B, S):
    T, W = q.shape
    C = HG_CHUNK
    TS = HG_STEP_TOKENS
    assert S % TS == 0 and TS % C == 0
    N = S // TS
    tri, masks, nl = _hgrn_tables(C)
    row = lambda b, c: (b * N + c, 0)
    tok = lambda: pl.BlockSpec((TS, W), row)
    return pl.pallas_call(
        functools.partial(_hgrn_kernel, C=C, nl=nl, heads=HG_HEADS, chunks=TS // C),
        grid=(B, N),
        in_specs=[tok(), tok(), tok(), tok(), tok(),
                  pl.BlockSpec((1, HG_DV), lambda b, c: (0, 0)),
                  pl.BlockSpec(tri.shape, lambda b, c: (0, 0)),
                  pl.BlockSpec(masks.shape, lambda b, c: (0, 0, 0))],
        out_specs=tok(),
        out_shape=jax.ShapeDtypeStruct((T, W), BF16),
        scratch_shapes=[pltpu.VMEM((2, (nl + 2) * C, W), BF16),
                        pltpu.VMEM((2, 1, W), F32),
                        pltpu.VMEM((HG_HEADS, HG_DV, HG_DK), F32)],
        compiler_params=_cparams(("parallel", "arbitrary")),
        name="hgrn2",
    )(q, lf, k, v, g, gain.reshape(1, HG_DV).astype(F32),
      jnp.asarray(tri, BF16), jnp.asarray(masks, F32))


def _t5_bucket_np(dist):
    n = np.maximum(dist, 0)
    max_exact = REL_BUCKETS // 2
    nf = np.maximum(n, 1).astype(np.float32)
    large = max_exact + (np.log(nf / max_exact) / math.log(REL_MAX_DIST / max_exact)
                         * (REL_BUCKETS - max_exact)).astype(np.int32)
    large = np.minimum(large, REL_BUCKETS - 1)
    return np.where(n < max_exact, n, large).astype(np.int32)


def _bias_kernel(tab_ref, bucket_ref, o_ref):
    h = pl.program_id(0)
    bucket = bucket_ref[...]
    acc = jnp.zeros(bucket.shape, F32)
    for b in range(REL_BUCKETS):
        acc = jnp.where(bucket == b, tab_ref[b, h], acc)
    delta = (acc - tab_ref[REL_BUCKETS - 1, h]) * LOG2E
    blk = bucket.shape[0]
    r = lax.broadcasted_iota(jnp.int32, bucket.shape, 0)
    c = lax.broadcasted_iota(jnp.int32, bucket.shape, 1)
    o_ref[0] = jnp.where(c - blk <= r, delta, MASKED_LOGIT)


def _bias_tiles(rel_table):
    BLK = MOBA_BLOCK
    t = np.arange(BLK)[:, None]
    s = np.arange(2 * BLK)[None, :]
    bucket = _t5_bucket_np(t + BLK - s)
    H = rel_table.shape[1]
    return pl.pallas_call(
        _bias_kernel,
        grid=(H,),
        in_specs=[pl.BlockSpec(memory_space=pltpu.SMEM),
                  pl.BlockSpec(bucket.shape, lambda h: (0, 0))],
        out_specs=pl.BlockSpec((1, BLK, 2 * BLK), lambda h: (h, 0, 0)),
        out_shape=jax.ShapeDtypeStruct((H, BLK, 2 * BLK), F32),
        compiler_params=_cparams(("arbitrary",)),
        name="relbias",
    )(rel_table.astype(F32), jnp.asarray(bucket))


MOBA_HEADS_PER_STEP = 1
MOBA_BOUND_SLACK = 1.0 + 2.0 ** -8
MOBA_SAFE_GAP = 100.0


def _moba_kernel(q_ref, k_ref, v_ref, dbias_ref, far_ref, o_ref,
                 qnear_sc, qfar_sc, kaug_sc, vaug_sc, near_sc, far_sc, mx_sc, mb_sc, acc_sc,
                 safe_sc, *, NB, HPS):
    BLK = MOBA_BLOCK
    W2 = 2 * BLK
    PAD = NB + 2
    SHIFT = NB + 3
    NS = NB // 2 - 1
    hg = pl.program_id(1)
    ip = pl.program_id(2)

    def prepare(hh):
        cols = slice(hh * AT_DH, (hh + 1) * AT_DH)
        kaug, vaug, qnear, qfar = kaug_sc.at[hh], vaug_sc.at[hh], qnear_sc.at[hh], qfar_sc.at[hh]
        far = far_ref[pl.ds(hg * HPS + hh, 1), :]
        far_hi = far.astype(BF16).astype(F32)
        far_lo = far - far_hi
        lane = lax.broadcasted_iota(jnp.int32, (1, LANES), 1)
        tail = jnp.where(lane == NB, far_hi, jnp.where(lane == NB + 1, far_lo, 0.0))
        tail = jnp.where((lane == SHIFT) | (lane == SHIFT + 1), 1.0, tail)
        kaug[0:BLK, 0:AT_DH] = jnp.zeros((BLK, AT_DH), BF16)
        kaug[0:BLK, AT_DH:] = jnp.broadcast_to(
            jnp.where(lane == PAD, 1.0, 0.0), (BLK, LANES)).astype(BF16)
        vaug[0:BLK, :] = jnp.zeros((BLK, AT_DH + LANES), BF16)
        kms = []
        ones8 = jnp.ones((8, AT_DH), BF16)
        k_sq = jnp.zeros((1, BLK), F32)
        for n in range(NB):
            kb = k_ref[n * BLK:(n + 1) * BLK, cols]
            kf = kb.astype(F32)
            kms.append(jnp.mean(kf, axis=0, keepdims=True))
            k_sq = jnp.maximum(k_sq, _dot_nt(ones8, (kf * kf).astype(BF16))[0:1, :])
            kaug[(n + 1) * BLK:(n + 2) * BLK, 0:AT_DH] = kb
            kaug[(n + 1) * BLK:(n + 2) * BLK, AT_DH:] = jnp.broadcast_to(
                jnp.where(lane == n, 1.0, tail), (BLK, LANES)).astype(BF16)
        vaug[BLK:, 0:AT_DH] = v_ref[:, cols]
        vaug[BLK:, AT_DH:] = jnp.ones((NB * BLK, LANES), BF16)
        k_norm = jnp.sqrt(jnp.max(k_sq, axis=-1, keepdims=True))
        bias_max = jnp.max(dbias_ref[hh], keepdims=True)[0:1, 0:1] + far[:, 0:1]
        bias_self = dbias_ref[hh][0:1, BLK:BLK + 1] + far[:, 0:1]

        kmean = jnp.concatenate(kms, axis=0)
        km_hi = kmean.astype(BF16)
        km_lo = (kmean - km_hi.astype(F32)).astype(BF16)
        blk = lax.broadcasted_iota(jnp.int32, (NB, BLK), 0)
        rowid = lax.broadcasted_iota(jnp.int32, (LANES - NB, BLK), 0) + NB
        rest0 = jnp.where(rowid < NB + 2, 1.0, jnp.where(rowid == PAD, MASKED_LOGIT, 0.0))
        gap = jnp.zeros((1, BLK), F32)
        for t in range(NB):
            qt = q_ref[t * BLK:(t + 1) * BLK, cols]
            qf = qt.astype(F32)
            q_norm = jnp.sqrt(_dot_nt(ones8, (qf * qf).astype(BF16))[0:1, :])
            shift = q_norm * k_norm * MOBA_BOUND_SLACK + bias_max
            kt = k_ref[t * BLK:(t + 1) * BLK, cols].astype(F32)
            self_logit = _dot_nt(ones8, (qf * kt).astype(BF16))[0:1, :] + bias_self
            gap = jnp.maximum(gap, shift - self_logit)
            shift_hi = shift.astype(BF16).astype(F32)
            rest = jnp.where(rowid == SHIFT, -shift_hi,
                             jnp.where(rowid == SHIFT + 1, shift_hi - shift, rest0))
            gate = jnp.where(blk < t, _dot_nt(km_hi, qt) + _dot_nt(km_lo, qt), -jnp.inf)
            rank = jnp.zeros((NB, BLK), jnp.int32)
            for m in range(t):
                gm = gate[m:m + 1, :]
                rank = rank + ((gm > gate) | ((gm == gate) & (blk > m))).astype(jnp.int32)
            sel = ((blk < t) & (rank < MOBA_TOPK)) | (blk == t)
            near = jnp.where(sel, 0.0, MASKED_LOGIT)
            older = jnp.where(blk >= t - 1, MASKED_LOGIT, near)
            rows = slice(t * BLK, (t + 1) * BLK)
            qnear[rows, 0:AT_DH] = qt
            qfar[rows, 0:AT_DH] = qt
            qnear[rows, AT_DH:] = jnp.concatenate([near, rest], axis=0).T.astype(BF16)
            qfar[rows, AT_DH:] = jnp.concatenate([older, rest], axis=0).T.astype(BF16)
        return jnp.max(gap)

    @pl.when(ip == 0)
    def _():
        worst = prepare(0)
        for hh in range(1, HPS):
            worst = jnp.maximum(worst, prepare(hh))
        safe_sc[0] = (worst <= MOBA_SAFE_GAP).astype(jnp.int32)

    def lane_max(s):
        out = s[:, 0:LANES]
        for c in range(1, s.shape[1] // LANES):
            out = jnp.maximum(out, s[:, c * LANES:(c + 1) * LANES])
        return out

    def probs(s, m_b):
        return jnp.exp2(s - jnp.concatenate([m_b] * (W2 // LANES), axis=1)).astype(BF16)

    blocks = (ip, NB - 1 - ip)
    rows = [pl.multiple_of(b * BLK, BLK) for b in blocks]
    n_first = jnp.maximum(ip, 1) // 2
    visits = []
    for st in range(NS):
        second = st >= n_first
        jj = jnp.where(second, st - n_first, st)
        visits.append((second.astype(jnp.int32), jnp.where(second, rows[1], rows[0]),
                       pl.multiple_of(jj * W2 + BLK, BLK)))

    def one_pass():
        for hh in range(HPS):
            for slot in range(2):
                s = (_dot_nt(qnear_sc[hh, pl.ds(rows[slot], BLK), :],
                             kaug_sc[hh, pl.ds(rows[slot], W2), :]) + dbias_ref[hh])
                acc_sc[hh, slot] = _dot(jnp.exp2(s).astype(BF16),
                                        vaug_sc[hh, pl.ds(rows[slot], W2), :])
            for st, (slot, qrow, krow) in enumerate(visits):
                s = _dot_nt(qfar_sc[hh, pl.ds(qrow, BLK), :], kaug_sc[hh, pl.ds(krow, W2), :])
                acc_sc[hh, slot] += _dot(jnp.exp2(s).astype(BF16),
                                         vaug_sc[hh, pl.ds(krow, W2), :])

    def two_pass():
        for hh in range(HPS):
            for slot in range(2):
                near_sc[hh, slot] = (_dot_nt(qnear_sc[hh, pl.ds(rows[slot], BLK), :],
                                             kaug_sc[hh, pl.ds(rows[slot], W2), :])
                                     + dbias_ref[hh])
            for st, (slot, qrow, krow) in enumerate(visits):
                far_sc[hh, st] = _dot_nt(qfar_sc[hh, pl.ds(qrow, BLK), :],
                                         kaug_sc[hh, pl.ds(krow, W2), :])
        mx_sc[...] = jnp.full(mx_sc.shape, MASKED_LOGIT, F32)
        for hh in range(HPS):
            for st, (slot, qrow, krow) in enumerate(visits):
                mx_sc[hh, slot * NS + st] = lane_max(far_sc[hh, st])
            for slot in range(2):
                m = lane_max(near_sc[hh, slot])
                for st in range(NS):
                    m = jnp.maximum(m, mx_sc[hh, slot * NS + st])
                mb_sc[hh, slot] = jnp.broadcast_to(jnp.max(m, axis=-1, keepdims=True),
                                                   (BLK, LANES))
        for hh in range(HPS):
            for slot in range(2):
                acc_sc[hh, slot] = _dot(probs(near_sc[hh, slot], mb_sc[hh, slot]),
                                        vaug_sc[hh, pl.ds(rows[slot], W2), :])
            for st, (slot, qrow, krow) in enumerate(visits):
                acc_sc[hh, slot] += _dot(probs(far_sc[hh, st], mb_sc[hh, slot]),
                                         vaug_sc[hh, pl.ds(krow, W2), :])

    lax.cond(safe_sc[0] != 0, one_pass, two_pass)

    for hh in range(HPS):
        for slot in range(2):
            o_ref[pl.ds(rows[slot], BLK), hh * AT_DH:(hh + 1) * AT_DH] = (
                acc_sc[hh, slot, :, 0:AT_DH] / acc_sc[hh, slot, :, AT_DH:]).astype(o_ref.dtype)


def _moba(q, k, v, rel_table, B, S):
    T, W = q.shape
    H = AT_HEADS
    HPS = MOBA_HEADS_PER_STEP
    NB = S // MOBA_BLOCK
    BLK = MOBA_BLOCK
    NS = max(NB // 2 - 1, 1)
    assert NB + 5 <= LANES and NB % 2 == 0 and H % HPS == 0
    assert int(_t5_bucket_np(np.array([BLK + 1]))[0]) == REL_BUCKETS - 1
    dbias = _bias_tiles(rel_table)
    far = jnp.broadcast_to((rel_table[REL_BUCKETS - 1, :].astype(F32) * LOG2E)[:, None],
                           (H, LANES))
    heads = lambda: pl.BlockSpec((S, HPS * AT_DH), lambda b, g, i: (b, g))
    aug = AT_DH + LANES
    return pl.pallas_call(
        functools.partial(_moba_kernel, NB=NB, HPS=HPS),
        grid=(B, H // HPS, NB // 2),
        in_specs=[heads(), heads(), heads(),
                  pl.BlockSpec((HPS, BLK, 2 * BLK), lambda b, g, i: (g, 0, 0)),
                  pl.BlockSpec((H, LANES), lambda b, g, i: (0, 0))],
        out_specs=heads(),
        out_shape=jax.ShapeDtypeStruct((T, W), BF16),
        scratch_shapes=[pltpu.VMEM((HPS, S, aug), BF16),
                        pltpu.VMEM((HPS, S, aug), BF16),
                        pltpu.VMEM((HPS, S + BLK, aug), BF16),
                        pltpu.VMEM((HPS, S + BLK, aug), BF16),
                        pltpu.VMEM((HPS, 2, BLK, 2 * BLK), F32),
                        pltpu.VMEM((HPS, NS, BLK, 2 * BLK), F32),
                        pltpu.VMEM((HPS, 2 * NS, BLK, LANES), F32),
                        pltpu.VMEM((HPS, 2, BLK, LANES), F32),
                        pltpu.VMEM((HPS, 2, BLK, aug), F32),
                        pltpu.SMEM((1,), jnp.int32)],
        compiler_params=_cparams(("parallel", "parallel", "arbitrary")),
        name="moba",
    )(q, k, v, dbias, far)


def _merge_kernel(ohg_ref, oat_ref, gate_ref, x_ref, whg_ref, wat_ref, wout_ref, g2_ref,
                  x1_ref, h2_ref, *, D):
    y_hg = _dot(ohg_ref[...], whg_ref[...])
    y_at = _dot(oat_ref[...], wat_ref[...])
    merged = (gate_ref[:, 0:D].astype(F32) * y_hg + gate_ref[:, D:2 * D].astype(F32) * y_at)
    x1 = x_ref[...] + _dot(merged.astype(BF16), wout_ref[...])
    x1_ref[...] = x1
    ms = jnp.mean(x1 * x1, axis=-1, keepdims=True)
    h2_ref[...] = (x1 * lax.rsqrt(ms + NORM_EPS) * g2_ref[...]).astype(h2_ref.dtype)


def _merge(ohg, oat, gates, x, whg, wat, wout, g2, tm=512):
    T, D = x.shape
    tok = lambda w: pl.BlockSpec((tm, w), lambda i: (i, 0))
    full = lambda a: pl.BlockSpec(a.shape, lambda i: (0, 0))
    return pl.pallas_call(
        functools.partial(_merge_kernel, D=D),
        grid=(T // tm,),
        in_specs=[tok(ohg.shape[1]), tok(oat.shape[1]), tok(2 * D), tok(D),
                  full(whg), full(wat), full(wout), pl.BlockSpec((1, D), lambda i: (0, 0))],
        out_specs=[tok(D), tok(D)],
        out_shape=[jax.ShapeDtypeStruct((T, D), F32), jax.ShapeDtypeStruct((T, D), BF16)],
        compiler_params=_cparams(("parallel",)),
        name="merge",
    )(ohg, oat, gates, x, whg, wat, wout, g2.reshape(1, D).astype(F32))


def _ffn_kernel(h_ref, x_ref, wg_ref, wu_ref, wd_ref, o_ref, *, tf):
    h = h_ref[...]
    acc = x_ref[...]
    for f0 in range(0, wg_ref.shape[1], tf):
        a = _dot(h, wg_ref[:, f0:f0 + tf])
        u = _dot(h, wu_ref[:, f0:f0 + tf])
        acc = acc + _dot((a * jax.nn.sigmoid(a) * u).astype(BF16), wd_ref[f0:f0 + tf, :])
    o_ref[...] = acc


def _ffn(h2, x1, wg, wu, wd, tm=512, tf=2816):
    T, D = x1.shape
    FF = wg.shape[1]
    assert FF % tf == 0
    tok = lambda: pl.BlockSpec((tm, D), lambda i: (i, 0))
    return pl.pallas_call(
        functools.partial(_ffn_kernel, tf=tf),
        grid=(T // tm,),
        in_specs=[tok(), tok(), _resident(wg), _resident(wu), _resident(wd)],
        out_specs=tok(),
        out_shape=jax.ShapeDtypeStruct((T, D), F32),
        compiler_params=_cparams(("parallel",)),
        name="ffn",
    )(h2, x1, wg, wu, wd)


def kernel(x, attn_norm_g, w_in, hg_lb_gamma, hg_out_norm_g, q_norm_g, k_norm_g, rel_bias_table,
           w_branch_hg, w_branch_attn, w_out, ffn_norm_g, w_ffn_gate, w_ffn_up, w_ffn_down):
    B, S, D = x.shape
    T = B * S
    depth = attn_norm_g.shape[0]
    assert depth == 1 and S % MOBA_BLOCK == 0 and S % HG_CHUNK == 0
    WH = HG_HEADS * HG_DK
    WV = HG_HEADS * HG_DV
    WA = AT_HEADS * AT_DH
    assert w_in.shape[2] == 2 * WH + 2 * WV + 3 * WA + 2 * D

    xt = x.reshape(T, D)
    for l in range(depth):
        hq, lf, hk, hi, hg, aq, ak, av, gates = _inproj(
            xt, attn_norm_g[l], w_in[l].astype(BF16), hg_lb_gamma, q_norm_g[l], k_norm_g[l],
            WH, WV, WA)
        o_hg = _hgrn(hq, lf, hk, hi, hg, hg_out_norm_g[l], B, S)
        o_at = _moba(aq, ak, av, rel_bias_table, B, S)
        x1, h2 = _merge(o_hg, o_at, gates, xt, w_branch_hg[l].astype(BF16),
                        w_branch_attn[l].astype(BF16), w_out[l].astype(BF16), ffn_norm_g[l])
        xt = _ffn(h2, x1, w_ffn_gate[l].astype(BF16), w_ffn_up[l].astype(BF16),
                  w_ffn_down[l].astype(BF16))
    return xt.reshape(B, S, D)
```

```python
import functools
import math

import numpy as np
import jax
import jax.numpy as jnp
from jax import lax
from jax.experimental import pallas as pl
from jax.experimental.pallas import tpu as pltpu

F32 = jnp.float32
BF16 = jnp.bfloat16

LANES = 128
NORM_EPS = 1e-6

HG_HEADS = 8
HG_DK = 128
HG_DV = 128
HG_CHUNK = 64
HG_STEP_TOKENS = 512
HG_HEAD_GROUP = 8
AT_HEADS = 8
AT_DH = 128
MOBA_BLOCK = 256
MOBA_TOPK = 3
REL_BUCKETS = 32
REL_MAX_DIST = 128
MASKED_LOGIT = -1e30
LOG2E = math.log2(math.e)

VMEM_LIMIT = 56 * 1024 * 1024


def _cparams(sem):
    return pltpu.CompilerParams(dimension_semantics=sem, vmem_limit_bytes=VMEM_LIMIT)


def _dot(a, b):
    return jnp.dot(a, b, preferred_element_type=F32)


def _dot_nt(a, b):
    return lax.dot_general(a, b, (((1,), (1,)), ((), ())), preferred_element_type=F32)


def _dot_tn(a, b):
    return lax.dot_general(a, b, (((0,), (0,)), ((), ())), preferred_element_type=F32)


def _resident(a):
    return pl.BlockSpec(a.shape, lambda *_: (0,) * a.ndim, pipeline_mode=pl.Buffered(1))


def _head_rmsnorm(acc, gain, scale):
    cols = []
    for c in range(acc.shape[1] // LANES):
        blk = acc[:, c * LANES:(c + 1) * LANES]
        ms = jnp.mean(blk * blk, axis=-1, keepdims=True)
        cols.append(blk * lax.rsqrt(ms + NORM_EPS) * (gain * scale))
    return jnp.concatenate(cols, axis=1)


def _inproj_kernel(x_ref, g_ref, w_ref, gamma_ref, qg_ref, kg_ref,
                   hq_ref, lf_ref, hk_ref, hi_ref, hg_ref, aq_ref, ak_ref, av_ref, gate_ref,
                   *, WH, WV, WA):
    x = x_ref[...]
    ms = jnp.mean(x * x, axis=-1, keepdims=True)
    h = (x * lax.rsqrt(ms + NORM_EPS) * g_ref[...]).astype(BF16)
    col = [0]

    def segment(width):
        acc = _dot(h, w_ref[:, col[0]:col[0] + width])
        col[0] += width
        return acc

    hq_ref[...] = (segment(WH) * HG_DK ** -0.5).astype(hq_ref.dtype)

    z = segment(WH)
    gamma = gamma_ref[...]
    eg = jnp.exp(gamma - jnp.max(gamma, axis=0, keepdims=True))
    lb = eg[0:1, :] / jnp.sum(eg, axis=0, keepdims=True)
    lf_ref[...] = jnp.log2(lb + (1.0 - lb) * jax.nn.sigmoid(z))
    hk_ref[...] = ((1.0 - lb) * jax.nn.sigmoid(-z)).astype(hk_ref.dtype)

    hi_ref[...] = segment(WV).astype(hi_ref.dtype)
    g = segment(WV)
    hg_ref[...] = (g * jax.nn.sigmoid(g)).astype(hg_ref.dtype)
    aq_ref[...] = _head_rmsnorm(segment(WA), qg_ref[...],
                                AT_DH ** -0.5 * LOG2E).astype(aq_ref.dtype)
    ak_ref[...] = _head_rmsnorm(segment(WA), kg_ref[...], 1.0).astype(ak_ref.dtype)
    av_ref[...] = segment(WA).astype(av_ref.dtype)
    half = gate_ref.shape[1] // 2
    gate_ref[:, 0:half] = jax.nn.sigmoid(segment(half)).astype(gate_ref.dtype)
    gate_ref[:, half:] = jax.nn.sigmoid(segment(half)).astype(gate_ref.dtype)


def _inproj(x, g, w, gamma, qg, kg, WH, WV, WA, tm=512):
    T, D = x.shape
    tok = lambda width: pl.BlockSpec((tm, width), lambda i: (i, 0))
    small = lambda a: pl.BlockSpec(a.shape, lambda i: (0, 0))
    widths = [WH, WH, WH, WV, WV, WA, WA, WA, 2 * D]
    dtypes = [BF16, F32, BF16, BF16, BF16, BF16, BF16, BF16, BF16]
    g = g.reshape(1, D).astype(F32)
    gamma = gamma.astype(F32)
    qg = qg.reshape(1, AT_DH).astype(F32)
    kg = kg.reshape(1, AT_DH).astype(F32)
    return pl.pallas_call(
        functools.partial(_inproj_kernel, WH=WH, WV=WV, WA=WA),
        grid=(T // tm,),
        in_specs=[tok(D), small(g), _resident(w), small(gamma), small(qg), small(kg)],
        out_specs=[tok(wd) for wd in widths],
        out_shape=[jax.ShapeDtypeStruct((T, wd), dt) for wd, dt in zip(widths, dtypes)],
        compiler_params=_cparams(("parallel",)),
        name="inproj",
    )(x, g, w, gamma, qg, kg)


def _hgrn_tables(C):
    nl = int(math.log2(C))
    assert 1 << nl == C and nl >= 3
    masks = np.zeros((nl + 1, C, C), np.float32)
    masks[0] = np.eye(C)
    t = np.arange(C)
    for L in range(1, nl + 1):
        blk, half = 1 << L, 1 << (L - 1)
        base = (t // blk) * blk
        upper = (t - base) >= half
        same = base[:, None] == base[None, :]
        masks[L] = (same & upper[:, None] & (~upper)[None, :]).astype(np.float32)
    return np.tril(np.ones((C, C), np.float32)), masks, nl


def _hgrn_kernel(q_ref, lf_ref, k_ref, v_ref, g_ref, gain_ref, tri_ref, m_ref, o_ref,
                 e_sc, d_sc, st_sc, *, C, nl, heads, chunks):
    @pl.when(pl.program_id(1) == 0)
    def _():
        st_sc[...] = jnp.zeros_like(st_sc)

    tri = tri_ref[...]
    gain = gain_ref[...]
    row = lax.broadcasted_iota(jnp.int32, (C, lf_ref.shape[1]), 0)
    odd = (row & 1) == 1
    ph = row & 3

    def chunk_rows(c):
        return pl.ds(pl.multiple_of(c * C, C), C)

    def prepare(c, buf):
        lf = lf_ref[chunk_rows(c), :]
        hi = lf.astype(BF16)
        r1 = lf - hi.astype(F32)
        mid = r1.astype(BF16)
        lo = (r1 - mid.astype(F32)).astype(BF16)
        b = _dot(tri, hi) + _dot(tri, mid) + _dot(tri, lo)
        e_sc[buf, 0:C, :] = jnp.exp2(jnp.where(odd, lf, 0.0)).astype(BF16)
        prev = pltpu.roll(lf, 1, axis=0)
        nxt = pltpu.roll(lf, C - 1, axis=0)
        x2 = jnp.where(ph == 0, nxt, jnp.where(ph == 1, 0.0, jnp.where(ph == 2, lf, lf + prev)))
        e_sc[buf, C:2 * C, :] = jnp.exp2(x2).astype(BF16)
        for L in range(3, nl + 1):
            blk, half = 1 << L, 1 << (L - 1)
            pieces = [b[base:base + blk, :] - b[base + half - 1:base + half, :]
                      for base in range(0, C, blk)]
            d = pieces[0] if len(pieces) == 1 else jnp.concatenate(pieces, axis=0)
            e_sc[buf, (L - 1) * C:L * C, :] = jnp.exp2(-jnp.abs(d)).astype(BF16)
        b_last = b[C - 1:C, :]
        e_sc[buf, nl * C:(nl + 1) * C, :] = jnp.exp2(b).astype(BF16)
        e_sc[buf, (nl + 1) * C:(nl + 2) * C, :] = jnp.exp2(b_last - b).astype(BF16)
        d_sc[buf] = jnp.exp2(b_last)

    def heads_of(c, buf):
        rows = chunk_rows(c)
        for h0 in range(0, heads, HG_HEAD_GROUP):
            hs = range(h0, h0 + HG_HEAD_GROUP)
            sls = [slice(h * HG_DK, (h + 1) * HG_DK) for h in hs]
            qb = [q_ref[rows, sl] for sl in sls]
            kb = [k_ref[rows, sl] for sl in sls]
            vb = [v_ref[rows, sl] for sl in sls]
            A = [m_ref[0] * _dot_nt(q, k) for q, k in zip(qb, kb)]
            for L in range(1, nl + 1):
                for j, sl in enumerate(sls):
                    eL = e_sc[buf, (L - 1) * C:L * C, sl]
                    A[j] = A[j] + m_ref[L] * _dot_nt(qb[j] * eL, kb[j] * eL)
            for j, (h, sl) in enumerate(zip(hs, sls)):
                eb = e_sc[buf, nl * C:(nl + 1) * C, sl]
                ek = e_sc[buf, (nl + 1) * C:(nl + 2) * C, sl]
                st = st_sc[h]
                o = _dot(A[j].astype(BF16), vb[j]) + _dot_nt(qb[j] * eb, st.astype(BF16))
                st_sc[h] = st * d_sc[buf, :, sl] + _dot_tn(vb[j], kb[j] * ek)
                ms = jnp.mean(o * o, axis=-1, keepdims=True)
                y = o * lax.rsqrt(ms + NORM_EPS) * gain * g_ref[rows, sl].astype(F32)
                o_ref[rows, sl] = y.astype(o_ref.dtype)

    prepare(0, 0)

    def step(c, carry):
        prepare(jnp.minimum(c + 1, chunks - 1), (c + 1) & 1)
        heads_of(c, c & 1)
        return carry

    lax.fori_loop(0, chunks, step, 0)


def _hgrn(q, lf, k, v, g, gain, B, S):
    T, W = q.shape
    C = HG_CHUNK
    TS = HG_STEP_TOKENS
    assert S % TS == 0 and TS % C == 0
    N = S // TS
    tri, masks, nl = _hgrn_tables(C)
    row = lambda b, c: (b * N + c, 0)
    tok = lambda: pl.BlockSpec((TS, W), row)
    return pl.pallas_call(
        functools.partial(_hgrn_kernel, C=C, nl=nl, heads=HG_HEADS, chunks=TS // C),
        grid=(B, N),
        in_specs=[tok(), tok(), tok(), tok(), tok(),
                  pl.BlockSpec((1, HG_DV), lambda b, c: (0, 0)),
                  pl.BlockSpec(tri.shape, lambda b, c: (0, 0)),
                  pl.BlockSpec(masks.shape, lambda b, c: (0, 0, 0))],
        out_specs=tok(),
        out_shape=jax.ShapeDtypeStruct((T, W), BF16),
        scratch_shapes=[pltpu.VMEM((2, (nl + 2) * C, W), BF16),
                        pltpu.VMEM((2, 1, W), F32),
                        pltpu.VMEM((HG_HEADS, HG_DV, HG_DK), F32)],
        compiler_params=_cparams(("parallel", "arbitrary")),
        name="hgrn2",
    )(q, lf, k, v, g, gain.reshape(1, HG_DV).astype(F32),
      jnp.asarray(tri, BF16), jnp.asarray(masks, F32))


def _t5_bucket_np(dist):
    n = np.maximum(dist, 0)
    max_exact = REL_BUCKETS // 2
    nf = np.maximum(n, 1).astype(np.float32)
    large = max_exact + (np.log(nf / max_exact) / math.log(REL_MAX_DIST / max_exact)
                         * (REL_BUCKETS - max_exact)).astype(np.int32)
    large = np.minimum(large, REL_BUCKETS - 1)
    return np.where(n < max_exact, n, large).astype(np.int32)


def _bias_kernel(tab_ref, bucket_ref, o_ref):
    h = pl.program_id(0)
    bucket = bucket_ref[...]
    acc = jnp.zeros(bucket.shape, F32)
    for b in range(REL_BUCKETS):
        acc = jnp.where(bucket == b, tab_ref[b, h], acc)
    delta = (acc - tab_ref[REL_BUCKETS - 1, h]) * LOG2E
    blk = bucket.shape[0]
    r = lax.broadcasted_iota(jnp.int32, bucket.shape, 0)
    c = lax.broadcasted_iota(jnp.int32, bucket.shape, 1)
    o_ref[0] = jnp.where(c - blk <= r, delta, MASKED_LOGIT)


def _bias_tiles(rel_table):
    BLK = MOBA_BLOCK
    t = np.arange(BLK)[:, None]
    s = np.arange(2 * BLK)[None, :]
    bucket = _t5_bucket_np(t + BLK - s)
    H = rel_table.shape[1]
    return pl.pallas_call(
        _bias_kernel,
        grid=(H,),
        in_specs=[pl.BlockSpec(memory_space=pltpu.SMEM),
                  pl.BlockSpec(bucket.shape, lambda h: (0, 0))],
        out_specs=pl.BlockSpec((1, BLK, 2 * BLK), lambda h: (h, 0, 0)),
        out_shape=jax.ShapeDtypeStruct((H, BLK, 2 * BLK), F32),
        compiler_params=_cparams(("arbitrary",)),
        name="relbias",
    )(rel_table.astype(F32), jnp.asarray(bucket))


MOBA_HEADS_PER_STEP = 2
MOBA_BOUND_SLACK = 1.0 + 2.0 ** -6
MOBA_SAFE_GAP = 100.0


def _moba_kernel(q_ref, k_ref, v_ref, dbias_ref, far_ref, qg_ref, kg_ref, o_ref,
                 qnear_sc, qfar_sc, kaug_sc, vaug_sc, near_sc, far_sc, mx_sc, mb_sc, acc_sc,
                 safe_sc, *, NB, HPS):
    BLK = MOBA_BLOCK
    W2 = 2 * BLK
    PAD = NB + 2
    SHIFT = NB + 3
    NS = NB // 2 - 1
    hg = pl.program_id(1)
    ip = pl.program_id(2)

    def prepare(hh):
        cols = slice(hh * AT_DH, (hh + 1) * AT_DH)
        kaug, vaug, qnear, qfar = kaug_sc.at[hh], vaug_sc.at[hh], qnear_sc.at[hh], qfar_sc.at[hh]
        far = far_ref[pl.ds(hg * HPS + hh, 1), :]
        far_hi = far.astype(BF16).astype(F32)
        far_lo = far - far_hi
        lane = lax.broadcasted_iota(jnp.int32, (1, LANES), 1)
        tail = jnp.where(lane == NB, far_hi, jnp.where(lane == NB + 1, far_lo, 0.0))
        tail = jnp.where((lane == SHIFT) | (lane == SHIFT + 1), 1.0, tail)
        kaug[0:BLK, 0:AT_DH] = jnp.zeros((BLK, AT_DH), BF16)
        kaug[0:BLK, AT_DH:] = jnp.broadcast_to(
            jnp.where(lane == PAD, 1.0, 0.0), (BLK, LANES)).astype(BF16)
        vaug[0:BLK, :] = jnp.zeros((BLK, AT_DH + LANES), BF16)
        kms = []
        ones8 = jnp.ones((8, BLK), BF16)
        for n in range(NB):
            kb = k_ref[n * BLK:(n + 1) * BLK, cols]
            kms.append(_dot(ones8, kb)[0:1, :] * (1.0 / BLK))
            kaug[(n + 1) * BLK:(n + 2) * BLK, 0:AT_DH] = kb
            kaug[(n + 1) * BLK:(n + 2) * BLK, AT_DH:] = jnp.broadcast_to(
                jnp.where(lane == n, 1.0, tail), (BLK, LANES)).astype(BF16)
        vaug[BLK:, 0:AT_DH] = v_ref[:, cols]
        vaug[BLK:, AT_DH:] = jnp.ones((NB * BLK, LANES), BF16)

        qk_bound = (AT_DH * jnp.max(jnp.abs(qg_ref[...]), keepdims=True)
                    * jnp.max(jnp.abs(kg_ref[...]), keepdims=True)
                    * (AT_DH ** -0.5 * LOG2E * MOBA_BOUND_SLACK))
        bias_max = jnp.max(dbias_ref[hh], keepdims=True)[0:1, 0:1] + far[:, 0:1]
        bias_self = dbias_ref[hh][0:1, BLK:BLK + 1] + far[:, 0:1]
        shift = qk_bound + bias_max
        gap = shift + qk_bound - bias_self
        shift_hi = shift.astype(BF16).astype(F32)

        kmean = jnp.concatenate(kms, axis=0)
        km_hi = kmean.astype(BF16)
        km_lo = (kmean - km_hi.astype(F32)).astype(BF16)
        blk = lax.broadcasted_iota(jnp.int32, (NB, BLK), 0)
        rowid = lax.broadcasted_iota(jnp.int32, (LANES - NB, BLK), 0) + NB
        rest = jnp.where(rowid < NB + 2, 1.0, jnp.where(rowid == PAD, MASKED_LOGIT, 0.0))
        rest = jnp.where(rowid == SHIFT, -shift_hi,
                         jnp.where(rowid == SHIFT + 1, shift_hi - shift, rest))
        lane_q = lax.broadcasted_iota(jnp.int32, (BLK, LANES), 1)
        for t in range(NB):
            qt = q_ref[t * BLK:(t + 1) * BLK, cols]
            gate = jnp.where(blk < t, _dot_nt(km_hi, qt) + _dot_nt(km_lo, qt), -jnp.inf)
            rank = jnp.zeros((NB, BLK), jnp.int32)
            for m in range(t):
                gm = gate[m:m + 1, :]
                rank = rank + ((gm > gate) | ((gm == gate) & (blk > m))).astype(jnp.int32)
            sel = ((blk < t) & (rank < MOBA_TOPK)) | (blk == t)
            near = jnp.concatenate([jnp.where(sel, 0.0, MASKED_LOGIT), rest], axis=0).T
            older = jnp.where((lane_q == t) | (lane_q == t - 1), MASKED_LOGIT, near)
            rows = slice(t * BLK, (t + 1) * BLK)
            qnear[rows, 0:AT_DH] = qt
            qfar[rows, 0:AT_DH] = qt
            qnear[rows, AT_DH:] = near.astype(BF16)
            qfar[rows, AT_DH:] = older.astype(BF16)
        return jnp.max(gap)

    @pl.when(ip == 0)
    def _():
        worst = prepare(0)
        for hh in range(1, HPS):
            worst = jnp.maximum(worst, prepare(hh))
        safe_sc[0] = (worst <= MOBA_SAFE_GAP).astype(jnp.int32)

    def lane_max(s):
        out = s[:, 0:LANES]
        for c in range(1, s.shape[1] // LANES):
            out = jnp.maximum(out, s[:, c * LANES:(c + 1) * LANES])
        return out

    def probs(s, m_b):
        return jnp.exp2(s - jnp.concatenate([m_b] * (W2 // LANES), axis=1)).astype(BF16)

    blocks = (ip, NB - 1 - ip)
    rows = [pl.multiple_of(b * BLK, BLK) for b in blocks]
    n_first = jnp.maximum(ip, 1) // 2
    visits = []
    for st in range(NS):
        second = st >= n_first
        jj = jnp.where(second, st - n_first, st)
        visits.append((second.astype(jnp.int32), jnp.where(second, rows[1], rows[0]),
                       pl.multiple_of(jj * W2 + BLK, BLK)))

    def one_pass():
        for hh in range(HPS):
            for slot in range(2):
                s = (_dot_nt(qnear_sc[hh, pl.ds(rows[slot], BLK), :],
                             kaug_sc[hh, pl.ds(rows[slot], W2), :]) + dbias_ref[hh])
                acc_sc[hh, slot] = _dot(jnp.exp2(s).astype(BF16),
                                        vaug_sc[hh, pl.ds(rows[slot], W2), :])
            for st, (slot, qrow, krow) in enumerate(visits):
                s = _dot_nt(qfar_sc[hh, pl.ds(qrow, BLK), :], kaug_sc[hh, pl.ds(krow, W2), :])
                acc_sc[hh, slot] += _dot(jnp.exp2(s).astype(BF16),
                                         vaug_sc[hh, pl.ds(krow, W2), :])

    def two_pass():
        for hh in range(HPS):
            for slot in range(2):
                near_sc[hh, slot] = (_dot_nt(qnear_sc[hh, pl.ds(rows[slot], BLK), :],
                                             kaug_sc[hh, pl.ds(rows[slot], W2), :])
                                     + dbias_ref[hh])
            for st, (slot, qrow, krow) in enumerate(visits):
                far_sc[hh, st] = _dot_nt(qfar_sc[hh, pl.ds(qrow, BLK), :],
                                         kaug_sc[hh, pl.ds(krow, W2), :])
        mx_sc[...] = jnp.full(mx_sc.shape, MASKED_LOGIT, F32)
        for hh in range(HPS):
            for st, (slot, qrow, krow) in enumerate(visits):
                mx_sc[hh, slot * NS + st] = lane_max(far_sc[hh, st])
            for slot in range(2):
                m = lane_max(near_sc[hh, slot])
                for st in range(NS):
                    m = jnp.maximum(m, mx_sc[hh, slot * NS + st])
                mb_sc[hh, slot] = jnp.broadcast_to(jnp.max(m, axis=-1, keepdims=True),
                                                   (BLK, LANES))
        for hh in range(HPS):
            for slot in range(2):
                acc_sc[hh, slot] = _dot(probs(near_sc[hh, slot], mb_sc[hh, slot]),
                                        vaug_sc[hh, pl.ds(rows[slot], W2), :])
            for st, (slot, qrow, krow) in enumerate(visits):
                acc_sc[hh, slot] += _dot(probs(far_sc[hh, st], mb_sc[hh, slot]),
                                         vaug_sc[hh, pl.ds(krow, W2), :])

    lax.cond(safe_sc[0] != 0, one_pass, two_pass)

    for hh in range(HPS):
        for slot in range(2):
            o_ref[pl.ds(rows[slot], BLK), hh * AT_DH:(hh + 1) * AT_DH] = (
                acc_sc[hh, slot, :, 0:AT_DH] / acc_sc[hh, slot, :, AT_DH:]).astype(o_ref.dtype)


def _moba(q, k, v, rel_table, qg, kg, B, S):
    T, W = q.shape
    H = AT_HEADS
    HPS = MOBA_HEADS_PER_STEP
    NB = S // MOBA_BLOCK
    BLK = MOBA_BLOCK
    NS = max(NB // 2 - 1, 1)
    assert NB + 5 <= LANES and NB % 2 == 0 and H % HPS == 0
    assert int(_t5_bucket_np(np.array([BLK + 1]))[0]) == REL_BUCKETS - 1
    dbias = _bias_tiles(rel_table)
    far = jnp.broadcast_to((rel_table[REL_BUCKETS - 1, :].astype(F32) * LOG2E)[:, None],
                           (H, LANES))
    heads = lambda: pl.BlockSpec((S, HPS * AT_DH), lambda b, g, i: (b, g))
    aug = AT_DH + LANES
    return pl.pallas_call(
        functools.partial(_moba_kernel, NB=NB, HPS=HPS),
        grid=(B, H // HPS, NB // 2),
        in_specs=[heads(), heads(), heads(),
                  pl.BlockSpec((HPS, BLK, 2 * BLK), lambda b, g, i: (g, 0, 0)),
                  pl.BlockSpec((H, LANES), lambda b, g, i: (0, 0)),
                  pl.BlockSpec((1, AT_DH), lambda b, g, i: (0, 0)),
                  pl.BlockSpec((1, AT_DH), lambda b, g, i: (0, 0))],
        out_specs=heads(),
        out_shape=jax.ShapeDtypeStruct((T, W), BF16),
        scratch_shapes=[pltpu.VMEM((HPS, S, aug), BF16),
                        pltpu.VMEM((HPS, S, aug), BF16),
                        pltpu.VMEM((HPS, S + BLK, aug), BF16),
                        pltpu.VMEM((HPS, S + BLK, aug), BF16),
                        pltpu.VMEM((HPS, 2, BLK, 2 * BLK), F32),
                        pltpu.VMEM((HPS, NS, BLK, 2 * BLK), F32),
                        pltpu.VMEM((HPS, 2 * NS, BLK, LANES), F32),
                        pltpu.VMEM((HPS, 2, BLK, LANES), F32),
                        pltpu.VMEM((HPS, 2, BLK, aug), F32),
                        pltpu.SMEM((1,), jnp.int32)],
        compiler_params=_cparams(("parallel", "parallel", "arbitrary")),
        name="moba",
    )(q, k, v, dbias, far, qg.reshape(1, AT_DH).astype(F32), kg.reshape(1, AT_DH).astype(F32))


def _merge_kernel(ohg_ref, oat_ref, gate_ref, x_ref, whg_ref, wat_ref, wout_ref, g2_ref,
                  x1_ref, h2_ref, *, D):
    y_hg = _dot(ohg_ref[...], whg_ref[...])
    y_at = _dot(oat_ref[...], wat_ref[...])
    merged = (gate_ref[:, 0:D].astype(F32) * y_hg + gate_ref[:, D:2 * D].astype(F32) * y_at)
    x1 = x_ref[...] + _dot(merged.astype(BF16), wout_ref[...])
    x1_ref[...] = x1
    ms = jnp.mean(x1 * x1, axis=-1, keepdims=True)
    h2_ref[...] = (x1 * lax.rsqrt(ms + NORM_EPS) * g2_ref[...]).astype(h2_ref.dtype)


def _merge(ohg, oat, gates, x, whg, wat, wout, g2, tm=512):
    T, D = x.shape
    tok = lambda w: pl.BlockSpec((tm, w), lambda i: (i, 0))
    full = lambda a: pl.BlockSpec(a.shape, lambda i: (0, 0))
    return pl.pallas_call(
        functools.partial(_merge_kernel, D=D),
        grid=(T // tm,),
        in_specs=[tok(ohg.shape[1]), tok(oat.shape[1]), tok(2 * D), tok(D),
                  full(whg), full(wat), full(wout), pl.BlockSpec((1, D), lambda i: (0, 0))],
        out_specs=[tok(D), tok(D)],
        out_shape=[jax.ShapeDtypeStruct((T, D), F32), jax.ShapeDtypeStruct((T, D), BF16)],
        compiler_params=_cparams(("parallel",)),
        name="merge",
    )(ohg, oat, gates, x, whg, wat, wout, g2.reshape(1, D).astype(F32))


def _ffn_kernel(h_ref, x_ref, wg_ref, wu_ref, wd_ref, o_ref, *, tf):
    h = h_ref[...]
    acc = x_ref[...]
    for f0 in range(0, wg_ref.shape[1], tf):
        a = _dot(h, wg_ref[:, f0:f0 + tf])
        u = _dot(h, wu_ref[:, f0:f0 + tf])
        acc = acc + _dot((a * jax.nn.sigmoid(a) * u).astype(BF16), wd_ref[f0:f0 + tf, :])
    o_ref[...] = acc


def _ffn(h2, x1, wg, wu, wd, tm=512, tf=2816):
    T, D = x1.shape
    FF = wg.shape[1]
    assert FF % tf == 0
    tok = lambda: pl.BlockSpec((tm, D), lambda i: (i, 0))
    return pl.pallas_call(
        functools.partial(_ffn_kernel, tf=tf),
        grid=(T // tm,),
        in_specs=[tok(), tok(), _resident(wg), _resident(wu), _resident(wd)],
        out_specs=tok(),
        out_shape=jax.ShapeDtypeStruct((T, D), F32),
        compiler_params=_cparams(("parallel",)),
        name="ffn",
    )(h2, x1, wg, wu, wd)


def kernel(x, attn_norm_g, w_in, hg_lb_gamma, hg_out_norm_g, q_norm_g, k_norm_g, rel_bias_table,
           w_branch_hg, w_branch_attn, w_out, ffn_norm_g, w_ffn_gate, w_ffn_up, w_ffn_down):
    B, S, D = x.shape
    T = B * S
    depth = attn_norm_g.shape[0]
    assert depth == 1 and S % MOBA_BLOCK == 0 and S % HG_CHUNK == 0
    WH = HG_HEADS * HG_DK
    WV = HG_HEADS * HG_DV
    WA = AT_HEADS * AT_DH
    assert w_in.shape[2] == 2 * WH + 2 * WV + 3 * WA + 2 * D

    xt = x.reshape(T, D)
    for l in range(depth):
        hq, lf, hk, hi, hg, aq, ak, av, gates = _inproj(
            xt, attn_norm_g[l], w_in[l].astype(BF16), hg_lb_gamma, q_norm_g[l], k_norm_g[l],
            WH, WV, WA)
        o_hg = _hgrn(hq, lf, hk, hi, hg, hg_out_norm_g[l], B, S)
        o_at = _moba(aq, ak, av, rel_bias_table, q_norm_g[l], k_norm_g[l], B, S)
        x1, h2 = _merge(o_hg, o_at, gates, xt, w_branch_hg[l].astype(BF16),
                        w_branch_attn[l].astype(BF16), w_out[l].astype(BF16), ffn_norm_g[l])
        xt = _ffn(h2, x1, w_ffn_gate[l].astype(BF16), w_ffn_up[l].astype(BF16),
                  w_ffn_down[l].astype(BF16))
    return xt.reshape(B, S, D)
```

```python
import functools
import math

import numpy as np
import jax
import jax.numpy as jnp
from jax import lax
from jax.experimental import pallas as pl
from jax.experimental.pallas import tpu as pltpu

F32 = jnp.float32
BF16 = jnp.bfloat16

LANES = 128
NORM_EPS = 1e-6

HG_HEADS = 8
HG_DK = 128
HG_DV = 128
HG_CHUNK = 64
HG_STEP_TOKENS = 512
HG_HEAD_GROUP = 8
AT_HEADS = 8
AT_DH = 128
MOBA_BLOCK = 256
MOBA_TOPK = 3
REL_BUCKETS = 32
REL_MAX_DIST = 128
MASKED_LOGIT = -1e30
LOG2E = math.log2(math.e)

VMEM_LIMIT = 56 * 1024 * 1024


def _cparams(sem):
    return pltpu.CompilerParams(dimension_semantics=sem, vmem_limit_bytes=VMEM_LIMIT)


def _dot(a, b):
    return jnp.dot(a, b, preferred_element_type=F32)


def _dot_nt(a, b):
    return lax.dot_general(a, b, (((1,), (1,)), ((), ())), preferred_element_type=F32)


def _dot_tn(a, b):
    return lax.dot_general(a, b, (((0,), (0,)), ((), ())), preferred_element_type=F32)


def _resident(a):
    return pl.BlockSpec(a.shape, lambda *_: (0,) * a.ndim, pipeline_mode=pl.Buffered(1))


def _head_rmsnorm(acc, gain, scale):
    cols = []
    for c in range(acc.shape[1] // LANES):
        blk = acc[:, c * LANES:(c + 1) * LANES]
        ms = jnp.mean(blk * blk, axis=-1, keepdims=True)
        cols.append(blk * lax.rsqrt(ms + NORM_EPS) * (gain * scale))
    return jnp.concatenate(cols, axis=1)


def _inproj_kernel(x_ref, g_ref, w_ref, gamma_ref, qg_ref, kg_ref,
                   hq_ref, lf_ref, hk_ref, hi_ref, hg_ref, aq_ref, ak_ref, av_ref, gate_ref,
                   *, WH, WV, WA):
    x = x_ref[...]
    ms = jnp.mean(x * x, axis=-1, keepdims=True)
    h = (x * lax.rsqrt(ms + NORM_EPS) * g_ref[...]).astype(BF16)
    col = [0]

    def segment(width):
        acc = _dot(h, w_ref[:, col[0]:col[0] + width])
        col[0] += width
        return acc

    hq_ref[...] = (segment(WH) * HG_DK ** -0.5).astype(hq_ref.dtype)

    z = segment(WH)
    gamma = gamma_ref[...]
    eg = jnp.exp(gamma - jnp.max(gamma, axis=0, keepdims=True))
    lb = eg[0:1, :] / jnp.sum(eg, axis=0, keepdims=True)
    lf_ref[...] = jnp.log2(lb + (1.0 - lb) * jax.nn.sigmoid(z))
    hk_ref[...] = ((1.0 - lb) * jax.nn.sigmoid(-z)).astype(hk_ref.dtype)

    hi_ref[...] = segment(WV).astype(hi_ref.dtype)
    g = segment(WV)
    hg_ref[...] = (g * jax.nn.sigmoid(g)).astype(hg_ref.dtype)
    aq_ref[...] = _head_rmsnorm(segment(WA), qg_ref[...],
                                AT_DH ** -0.5 * LOG2E).astype(aq_ref.dtype)
    ak_ref[...] = _head_rmsnorm(segment(WA), kg_ref[...], 1.0).astype(ak_ref.dtype)
    av_ref[...] = segment(WA).astype(av_ref.dtype)
    half = gate_ref.shape[1] // 2
    gate_ref[:, 0:half] = jax.nn.sigmoid(segment(half)).astype(gate_ref.dtype)
    gate_ref[:, half:] = jax.nn.sigmoid(segment(half)).astype(gate_ref.dtype)


def _inproj(x, g, w, gamma, qg, kg, WH, WV, WA, tm=512):
    T, D = x.shape
    tok = lambda width: pl.BlockSpec((tm, width), lambda i: (i, 0))
    small = lambda a: pl.BlockSpec(a.shape, lambda i: (0, 0))
    widths = [WH, WH, WH, WV, WV, WA, WA, WA, 2 * D]
    dtypes = [BF16, F32, BF16, BF16, BF16, BF16, BF16, BF16, BF16]
    g = g.reshape(1, D).astype(F32)
    gamma = gamma.astype(F32)
    qg = qg.reshape(1, AT_DH).astype(F32)
    kg = kg.reshape(1, AT_DH).astype(F32)
    return pl.pallas_call(
        functools.partial(_inproj_kernel, WH=WH, WV=WV, WA=WA),
        grid=(T // tm,),
        in_specs=[tok(D), small(g), _resident(w), small(gamma), small(qg), small(kg)],
        out_specs=[tok(wd) for wd in widths],
        out_shape=[jax.ShapeDtypeStruct((T, wd), dt) for wd, dt in zip(widths, dtypes)],
        compiler_params=_cparams(("parallel",)),
        name="inproj",
    )(x, g, w, gamma, qg, kg)


def _hgrn_tables(C):
    nl = int(math.log2(C))
    assert 1 << nl == C and nl >= 3
    masks = np.zeros((nl + 1, C, C), np.float32)
    masks[0] = np.eye(C)
    t = np.arange(C)
    for L in range(1, nl + 1):
        blk, half = 1 << L, 1 << (L - 1)
        base = (t // blk) * blk
        upper = (t - base) >= half
        same = base[:, None] == base[None, :]
        masks[L] = (same & upper[:, None] & (~upper)[None, :]).astype(np.float32)
    return np.tril(np.ones((C, C), np.float32)), masks, nl


def _hgrn_kernel(q_ref, lf_ref, k_ref, v_ref, g_ref, gain_ref, tri_ref, m_ref, o_ref,
                 e_sc, d_sc, st_sc, *, C, nl, heads, chunks):
    @pl.when(pl.program_id(1) == 0)
    def _():
        st_sc[...] = jnp.zeros_like(st_sc)

    tri = tri_ref[...]
    gain = gain_ref[...]
    row = lax.broadcasted_iota(jnp.int32, (C, lf_ref.shape[1]), 0)
    odd = (row & 1) == 1
    ph = row & 3

    def chunk_rows(c):
        return pl.ds(pl.multiple_of(c * C, C), C)

    def prepare(c, buf):
        lf = lf_ref[chunk_rows(c), :]
        hi = lf.astype(BF16)
        r1 = lf - hi.astype(F32)
        mid = r1.astype(BF16)
        lo = (r1 - mid.astype(F32)).astype(BF16)
        b = _dot(tri, hi) + _dot(tri, mid) + _dot(tri, lo)
        e_sc[buf, 0:C, :] = jnp.exp2(jnp.where(odd, lf, 0.0)).astype(BF16)
        prev = pltpu.roll(lf, 1, axis=0)
        nxt = pltpu.roll(lf, C - 1, axis=0)
        x2 = jnp.where(ph == 0, nxt, jnp.where(ph == 1, 0.0, jnp.where(ph == 2, lf, lf + prev)))
        e_sc[buf, C:2 * C, :] = jnp.exp2(x2).astype(BF16)
        for L in range(3, nl + 1):
            blk, half = 1 << L, 1 << (L - 1)
            pieces = [b[base:base + blk, :] - b[base + half - 1:base + half, :]
                      for base in range(0, C, blk)]
            d = pieces[0] if len(pieces) == 1 else jnp.concatenate(pieces, axis=0)
            e_sc[buf, (L - 1) * C:L * C, :] = jnp.exp2(-jnp.abs(d)).astype(BF16)
        b_last = b[C - 1:C, :]
        e_sc[buf, nl * C:(nl + 1) * C, :] = jnp.exp2(b).astype(BF16)
        e_sc[buf, (nl + 1) * C:(nl + 2) * C, :] = jnp.exp2(b_last - b).astype(BF16)
        d_sc[buf] = jnp.exp2(b_last)

    def block_diag(x):
        n, d2 = x.shape
        z = jnp.zeros((n, d2 // 2), x.dtype)
        return jnp.concatenate([jnp.concatenate([x[:, :d2 // 2], z], axis=1),
                                jnp.concatenate([z, x[:, d2 // 2:]], axis=1)], axis=0)

    W2 = 2 * HG_DK
    pair_mask = [jnp.concatenate([m_ref[L], m_ref[L]], axis=1) for L in range(nl + 1)]

    def heads_of(c, buf):
        rows = chunk_rows(c)
        pairs = [slice(p * W2, (p + 1) * W2) for p in range(heads // 2)]
        qb = [q_ref[rows, sl] for sl in pairs]
        kb = [k_ref[rows, sl] for sl in pairs]
        vb = [v_ref[rows, sl] for sl in pairs]
        A = [pair_mask[0] * _dot_nt(q, block_diag(k)) for q, k in zip(qb, kb)]
        for L in range(1, nl + 1):
            for p, sl in enumerate(pairs):
                eL = e_sc[buf, (L - 1) * C:L * C, sl]
                A[p] = A[p] + pair_mask[L] * _dot_nt(qb[p] * eL, block_diag(kb[p] * eL))
        for p, sl in enumerate(pairs):
            eb = e_sc[buf, nl * C:(nl + 1) * C, sl]
            ek = e_sc[buf, (nl + 1) * C:(nl + 2) * C, sl]
            st = [st_sc[2 * p], st_sc[2 * p + 1]]
            st_pair = jnp.concatenate([s.astype(BF16) for s in st], axis=1)
            o = (_dot(A[p].astype(BF16), block_diag(vb[p]))
                 + _dot_nt(qb[p] * eb, block_diag(st_pair)))
            upd = _dot_tn(vb[p], kb[p] * ek)
            for j in range(2):
                h = 2 * p + j
                hs = slice(h * HG_DK, (h + 1) * HG_DK)
                js = slice(j * HG_DK, (j + 1) * HG_DK)
                st_sc[h] = st[j] * d_sc[buf, :, hs] + upd[js, js]
                oj = o[:, js]
                ms = jnp.mean(oj * oj, axis=-1, keepdims=True)
                y = oj * lax.rsqrt(ms + NORM_EPS) * gain * g_ref[rows, hs].astype(F32)
                o_ref[rows, hs] = y.astype(o_ref.dtype)

    prepare(0, 0)

    def step(c, carry):
        prepare(jnp.minimum(c + 1, chunks - 1), (c + 1) & 1)
        heads_of(c, c & 1)
        return carry

    lax.fori_loop(0, chunks, step, 0)


def _hgrn(q, lf, k, v, g, gain, B, S):
    T, W = q.shape
    C = HG_CHUNK
    TS = HG_STEP_TOKENS
    assert S % TS == 0 and TS % C == 0
    N = S // TS
    tri, masks, nl = _hgrn_tables(C)
    row = lambda b, c: (b * N + c, 0)
    tok = lambda: pl.BlockSpec((TS, W), row)
    return pl.pallas_call(
        functools.partial(_hgrn_kernel, C=C, nl=nl, heads=HG_HEADS, chunks=TS // C),
        grid=(B, N),
        in_specs=[tok(), tok(), tok(), tok(), tok(),
                  pl.BlockSpec((1, HG_DV), lambda b, c: (0, 0)),
                  pl.BlockSpec(tri.shape, lambda b, c: (0, 0)),
                  pl.BlockSpec(masks.shape, lambda b, c: (0, 0, 0))],
        out_specs=tok(),
        out_shape=jax.ShapeDtypeStruct((T, W), BF16),
        scratch_shapes=[pltpu.VMEM((2, (nl + 2) * C, W), BF16),
                        pltpu.VMEM((2, 1, W), F32),
                        pltpu.VMEM((HG_HEADS, HG_DV, HG_DK), F32)],
        compiler_params=_cparams(("parallel", "arbitrary")),
        name="hgrn2",
    )(q, lf, k, v, g, gain.reshape(1, HG_DV).astype(F32),
      jnp.asarray(tri, BF16), jnp.asarray(masks, F32))


def _t5_bucket_np(dist):
    n = np.maximum(dist, 0)
    max_exact = REL_BUCKETS // 2
    nf = np.maximum(n, 1).astype(np.float32)
    large = max_exact + (np.log(nf / max_exact) / math.log(REL_MAX_DIST / max_exact)
                         * (REL_BUCKETS - max_exact)).astype(np.int32)
    large = np.minimum(large, REL_BUCKETS - 1)
    return np.where(n < max_exact, n, large).astype(np.int32)


def _bias_kernel(tab_ref, bucket_ref, o_ref):
    h = pl.program_id(0)
    bucket = bucket_ref[...]
    acc = jnp.zeros(bucket.shape, F32)
    for b in range(REL_BUCKETS):
        acc = jnp.where(bucket == b, tab_ref[b, h], acc)
    delta = (acc - tab_ref[REL_BUCKETS - 1, h]) * LOG2E
    blk = bucket.shape[0]
    r = lax.broadcasted_iota(jnp.int32, bucket.shape, 0)
    c = lax.broadcasted_iota(jnp.int32, bucket.shape, 1)
    o_ref[0] = jnp.where(c - blk <= r, delta, MASKED_LOGIT)


def _bias_tiles(rel_table):
    BLK = MOBA_BLOCK
    t = np.arange(BLK)[:, None]
    s = np.arange(2 * BLK)[None, :]
    bucket = _t5_bucket_np(t + BLK - s)
    H = rel_table.shape[1]
    return pl.pallas_call(
        _bias_kernel,
        grid=(H,),
        in_specs=[pl.BlockSpec(memory_space=pltpu.SMEM),
                  pl.BlockSpec(bucket.shape, lambda h: (0, 0))],
        out_specs=pl.BlockSpec((1, BLK, 2 * BLK), lambda h: (h, 0, 0)),
        out_shape=jax.ShapeDtypeStruct((H, BLK, 2 * BLK), F32),
        compiler_params=_cparams(("arbitrary",)),
        name="relbias",
    )(rel_table.astype(F32), jnp.asarray(bucket))


MOBA_HEADS_PER_STEP = 2
MOBA_BOUND_SLACK = 1.0 + 2.0 ** -6
MOBA_SAFE_GAP = 100.0


def _moba_kernel(q_ref, k_ref, v_ref, dbias_ref, far_ref, qg_ref, kg_ref, o_ref,
                 qnear_sc, qfar_sc, kaug_sc, vaug_sc, near_sc, far_sc, mx_sc, mb_sc, acc_sc,
                 safe_sc, *, NB, HPS):
    BLK = MOBA_BLOCK
    W2 = 2 * BLK
    PAD = NB + 2
    SHIFT = NB + 3
    NS = NB // 2 - 1
    hg = pl.program_id(1)
    ip = pl.program_id(2)

    def prepare(hh):
        cols = slice(hh * AT_DH, (hh + 1) * AT_DH)
        kaug, vaug, qnear, qfar = kaug_sc.at[hh], vaug_sc.at[hh], qnear_sc.at[hh], qfar_sc.at[hh]
        far = far_ref[pl.ds(hg * HPS + hh, 1), :]
        far_hi = far.astype(BF16).astype(F32)
        far_lo = far - far_hi
        lane = lax.broadcasted_iota(jnp.int32, (1, LANES), 1)
        tail = jnp.where(lane == NB, far_hi, jnp.where(lane == NB + 1, far_lo, 0.0))
        tail = jnp.where((lane == SHIFT) | (lane == SHIFT + 1), 1.0, tail)
        kaug[0:BLK, 0:AT_DH] = jnp.zeros((BLK, AT_DH), BF16)
        kaug[0:BLK, AT_DH:] = jnp.broadcast_to(
            jnp.where(lane == PAD, 1.0, 0.0), (BLK, LANES)).astype(BF16)
        vaug[0:BLK, :] = jnp.zeros((BLK, AT_DH + LANES), BF16)
        kms = []
        ones8 = jnp.ones((8, BLK), BF16)
        for n in range(NB):
            kb = k_ref[n * BLK:(n + 1) * BLK, cols]
            kms.append(_dot(ones8, kb)[0:1, :] * (1.0 / BLK))
            kaug[(n + 1) * BLK:(n + 2) * BLK, 0:AT_DH] = kb
            kaug[(n + 1) * BLK:(n + 2) * BLK, AT_DH:] = jnp.broadcast_to(
                jnp.where(lane == n, 1.0, tail), (BLK, LANES)).astype(BF16)
        vaug[BLK:, 0:AT_DH] = v_ref[:, cols]
        vaug[BLK:, AT_DH:] = jnp.ones((NB * BLK, LANES), BF16)

        qk_bound = (AT_DH * jnp.max(jnp.abs(qg_ref[...]), keepdims=True)
                    * jnp.max(jnp.abs(kg_ref[...]), keepdims=True)
                    * (AT_DH ** -0.5 * LOG2E * MOBA_BOUND_SLACK))
        bias_max = jnp.max(dbias_ref[hh], keepdims=True)[0:1, 0:1] + far[:, 0:1]
        bias_self = dbias_ref[hh][0:1, BLK:BLK + 1] + far[:, 0:1]
        shift = qk_bound + bias_max
        gap = shift + qk_bound - bias_self
        shift_hi = shift.astype(BF16).astype(F32)

        kmean = jnp.concatenate(kms, axis=0)
        km_hi = kmean.astype(BF16)
        km_lo = (kmean - km_hi.astype(F32)).astype(BF16)
        blk = lax.broadcasted_iota(jnp.int32, (NB, BLK), 0)
        rowid = lax.broadcasted_iota(jnp.int32, (LANES - NB, BLK), 0) + NB
        rest = jnp.where(rowid < NB + 2, 1.0, jnp.where(rowid == PAD, MASKED_LOGIT, 0.0))
        rest = jnp.where(rowid == SHIFT, -shift_hi,
                         jnp.where(rowid == SHIFT + 1, shift_hi - shift, rest))
        lane_q = lax.broadcasted_iota(jnp.int32, (BLK, LANES), 1)
        for t in range(NB):
            qt = q_ref[t * BLK:(t + 1) * BLK, cols]
            gate = jnp.where(blk < t, _dot_nt(km_hi, qt) + _dot_nt(km_lo, qt), -jnp.inf)
            rank = jnp.zeros((NB, BLK), jnp.int32)
            for m in range(t):
                gm = gate[m:m + 1, :]
                rank = rank + ((gm > gate) | ((gm == gate) & (blk > m))).astype(jnp.int32)
            sel = ((blk < t) & (rank < MOBA_TOPK)) | (blk == t)
            near = jnp.concatenate([jnp.where(sel, 0.0, MASKED_LOGIT), rest], axis=0).T
            older = jnp.where((lane_q == t) | (lane_q == t - 1), MASKED_LOGIT, near)
            rows = slice(t * BLK, (t + 1) * BLK)
            qnear[rows, 0:AT_DH] = qt
            qfar[rows, 0:AT_DH] = qt
            qnear[rows, AT_DH:] = near.astype(BF16)
            qfar[rows, AT_DH:] = older.astype(BF16)
        return jnp.max(gap)

    @pl.when(ip == 0)
    def _():
        worst = prepare(0)
        for hh in range(1, HPS):
            worst = jnp.maximum(worst, prepare(hh))
        safe_sc[0] = (worst <= MOBA_SAFE_GAP).astype(jnp.int32)

    def lane_max(s):
        out = s[:, 0:LANES]
        for c in range(1, s.shape[1] // LANES):
            out = jnp.maximum(out, s[:, c * LANES:(c + 1) * LANES])
        return out

    def probs(s, m_b):
        return jnp.exp2(s - jnp.concatenate([m_b] * (W2 // LANES), axis=1)).astype(BF16)

    blocks = (ip, NB - 1 - ip)
    rows = [pl.multiple_of(b * BLK, BLK) for b in blocks]
    n_first = jnp.maximum(ip, 1) // 2
    visits = []
    for st in range(NS):
        second = st >= n_first
        jj = jnp.where(second, st - n_first, st)
        visits.append((second.astype(jnp.int32), jnp.where(second, rows[1], rows[0]),
                       pl.multiple_of(jj * W2 + BLK, BLK)))

    def one_pass():
        for hh in range(HPS):
            for slot in range(2):
                s = (_dot_nt(qnear_sc[hh, pl.ds(rows[slot], BLK), :],
                             kaug_sc[hh, pl.ds(rows[slot], W2), :]) + dbias_ref[hh])
                acc_sc[hh, slot] = _dot(jnp.exp2(s).astype(BF16),
                                        vaug_sc[hh, pl.ds(rows[slot], W2), :])
            for st, (slot, qrow, krow) in enumerate(visits):
                s = _dot_nt(qfar_sc[hh, pl.ds(qrow, BLK), :], kaug_sc[hh, pl.ds(krow, W2), :])
                acc_sc[hh, slot] += _dot(jnp.exp2(s).astype(BF16),
                                         vaug_sc[hh, pl.ds(krow, W2), :])

    def two_pass():
        for hh in range(HPS):
            for slot in range(2):
                near_sc[hh, slot] = (_dot_nt(qnear_sc[hh, pl.ds(rows[slot], BLK), :],
                                             kaug_sc[hh, pl.ds(rows[slot], W2), :])
                                     + dbias_ref[hh])
            for st, (slot, qrow, krow) in enumerate(visits):
                far_sc[hh, st] = _dot_nt(qfar_sc[hh, pl.ds(qrow, BLK), :],
                                         kaug_sc[hh, pl.ds(krow, W2), :])
        mx_sc[...] = jnp.full(mx_sc.shape, MASKED_LOGIT, F32)
        for hh in range(HPS):
            for st, (slot, qrow, krow) in enumerate(visits):
                mx_sc[hh, slot * NS + st] = lane_max(far_sc[hh, st])
            for slot in range(2):
                m = lane_max(near_sc[hh, slot])
                for st in range(NS):
                    m = jnp.maximum(m, mx_sc[hh, slot * NS + st])
                mb_sc[hh, slot] = jnp.broadcast_to(jnp.max(m, axis=-1, keepdims=True),
                                                   (BLK, LANES))
        for hh in range(HPS):
            for slot in range(2):
                acc_sc[hh, slot] = _dot(probs(near_sc[hh, slot], mb_sc[hh, slot]),
                                        vaug_sc[hh, pl.ds(rows[slot], W2), :])
            for st, (slot, qrow, krow) in enumerate(visits):
                acc_sc[hh, slot] += _dot(probs(far_sc[hh, st], mb_sc[hh, slot]),
                                         vaug_sc[hh, pl.ds(krow, W2), :])

    lax.cond(safe_sc[0] != 0, one_pass, two_pass)

    for hh in range(HPS):
        for slot in range(2):
            o_ref[pl.ds(rows[slot], BLK), hh * AT_DH:(hh + 1) * AT_DH] = (
                acc_sc[hh, slot, :, 0:AT_DH] / acc_sc[hh, slot, :, AT_DH:]).astype(o_ref.dtype)


def _moba(q, k, v, rel_table, qg, kg, B, S):
    T, W = q.shape
    H = AT_HEADS
    HPS = MOBA_HEADS_PER_STEP
    NB = S // MOBA_BLOCK
    BLK = MOBA_BLOCK
    NS = max(NB // 2 - 1, 1)
    assert NB + 5 <= LANES and NB % 2 == 0 and H % HPS == 0
    assert int(_t5_bucket_np(np.array([BLK + 1]))[0]) == REL_BUCKETS - 1
    dbias = _bias_tiles(rel_table)
    far = jnp.broadcast_to((rel_table[REL_BUCKETS - 1, :].astype(F32) * LOG2E)[:, None],
                           (H, LANES))
    heads = lambda: pl.BlockSpec((S, HPS * AT_DH), lambda b, g, i: (b, g))
    aug = AT_DH + LANES
    return pl.pallas_call(
        functools.partial(_moba_kernel, NB=NB, HPS=HPS),
        grid=(B, H // HPS, NB // 2),
        in_specs=[heads(), heads(), heads(),
                  pl.BlockSpec((HPS, BLK, 2 * BLK), lambda b, g, i: (g, 0, 0)),
                  pl.BlockSpec((H, LANES), lambda b, g, i: (0, 0)),
                  pl.BlockSpec((1, AT_DH), lambda b, g, i: (0, 0)),
                  pl.BlockSpec((1, AT_DH), lambda b, g, i: (0, 0))],
        out_specs=heads(),
        out_shape=jax.ShapeDtypeStruct((T, W), BF16),
        scratch_shapes=[pltpu.VMEM((HPS, S, aug), BF16),
                        pltpu.VMEM((HPS, S, aug), BF16),
                        pltpu.VMEM((HPS, S + BLK, aug), BF16),
                        pltpu.VMEM((HPS, S + BLK, aug), BF16),
                        pltpu.VMEM((HPS, 2, BLK, 2 * BLK), F32),
                        pltpu.VMEM((HPS, NS, BLK, 2 * BLK), F32),
                        pltpu.VMEM((HPS, 2 * NS, BLK, LANES), F32),
                        pltpu.VMEM((HPS, 2, BLK, LANES), F32),
                        pltpu.VMEM((HPS, 2, BLK, aug), F32),
                        pltpu.SMEM((1,), jnp.int32)],
        compiler_params=_cparams(("parallel", "parallel", "arbitrary")),
        name="moba",
    )(q, k, v, dbias, far, qg.reshape(1, AT_DH).astype(F32), kg.reshape(1, AT_DH).astype(F32))


def _merge_kernel(ohg_ref, oat_ref, gate_ref, x_ref, whg_ref, wat_ref, wout_ref, g2_ref,
                  x1_ref, h2_ref, *, D):
    y_hg = _dot(ohg_ref[...], whg_ref[...])
    y_at = _dot(oat_ref[...], wat_ref[...])
    merged = (gate_ref[:, 0:D].astype(F32) * y_hg + gate_ref[:, D:2 * D].astype(F32) * y_at)
    x1 = x_ref[...] + _dot(merged.astype(BF16), wout_ref[...])
    x1_ref[...] = x1
    ms = jnp.mean(x1 * x1, axis=-1, keepdims=True)
    h2_ref[...] = (x1 * lax.rsqrt(ms + NORM_EPS) * g2_ref[...]).astype(h2_ref.dtype)


def _merge(ohg, oat, gates, x, whg, wat, wout, g2, tm=512):
    T, D = x.shape
    tok = lambda w: pl.BlockSpec((tm, w), lambda i: (i, 0))
    full = lambda a: pl.BlockSpec(a.shape, lambda i: (0, 0))
    return pl.pallas_call(
        functools.partial(_merge_kernel, D=D),
        grid=(T // tm,),
        in_specs=[tok(ohg.shape[1]), tok(oat.shape[1]), tok(2 * D), tok(D),
                  full(whg), full(wat), full(wout), pl.BlockSpec((1, D), lambda i: (0, 0))],
        out_specs=[tok(D), tok(D)],
        out_shape=[jax.ShapeDtypeStruct((T, D), F32), jax.ShapeDtypeStruct((T, D), BF16)],
        compiler_params=_cparams(("parallel",)),
        name="merge",
    )(ohg, oat, gates, x, whg, wat, wout, g2.reshape(1, D).astype(F32))


def _ffn_kernel(h_ref, x_ref, wg_ref, wu_ref, wd_ref, o_ref, *, tf):
    h = h_ref[...]
    acc = x_ref[...]
    for f0 in range(0, wg_ref.shape[1], tf):
        a = _dot(h, wg_ref[:, f0:f0 + tf])
        u = _dot(h, wu_ref[:, f0:f0 + tf])
        acc = acc + _dot((a * jax.nn.sigmoid(a) * u).astype(BF16), wd_ref[f0:f0 + tf, :])
    o_ref[...] = acc


def _ffn(h2, x1, wg, wu, wd, tm=512, tf=2816):
    T, D = x1.shape
    FF = wg.shape[1]
    assert FF % tf == 0
    tok = lambda: pl.BlockSpec((tm, D), lambda i: (i, 0))
    return pl.pallas_call(
        functools.partial(_ffn_kernel, tf=tf),
        grid=(T // tm,),
        in_specs=[tok(), tok(), _resident(wg), _resident(wu), _resident(wd)],
        out_specs=tok(),
        out_shape=jax.ShapeDtypeStruct((T, D), F32),
        compiler_params=_cparams(("parallel",)),
        name="ffn",
    )(h2, x1, wg, wu, wd)


def kernel(x, attn_norm_g, w_in, hg_lb_gamma, hg_out_norm_g, q_norm_g, k_norm_g, rel_bias_table,
           w_branch_hg, w_branch_attn, w_out, ffn_norm_g, w_ffn_gate, w_ffn_up, w_ffn_down):
    B, S, D = x.shape
    T = B * S
    depth = attn_norm_g.shape[0]
    assert depth == 1 and S % MOBA_BLOCK == 0 and S % HG_CHUNK == 0
    WH = HG_HEADS * HG_DK
    WV = HG_HEADS * HG_DV
    WA = AT_HEADS * AT_DH
    assert w_in.shape[2] == 2 * WH + 2 * WV + 3 * WA + 2 * D

    xt = x.reshape(T, D)
    for l in range(depth):
        hq, lf, hk, hi, hg, aq, ak, av, gates = _inproj(
            xt, attn_norm_g[l], w_in[l].astype(BF16), hg_lb_gamma, q_norm_g[l], k_norm_g[l],
            WH, WV, WA)
        o_hg = _hgrn(hq, lf, hk, hi, hg, hg_out_norm_g[l], B, S)
        o_at = _moba(aq, ak, av, rel_bias_table, q_norm_g[l], k_norm_g[l], B, S)
        x1, h2 = _merge(o_hg, o_at, gates, xt, w_branch_hg[l].astype(BF16),
                        w_branch_attn[l].astype(BF16), w_out[l].astype(BF16), ffn_norm_g[l])
        xt = _ffn(h2, x1, w_ffn_gate[l].astype(BF16), w_ffn_up[l].astype(BF16),
                  w_ffn_down[l].astype(BF16))
    return xt.reshape(B, S, D)
```

```python
import functools
import math

import numpy as np
import jax
import jax.numpy as jnp
from jax import lax
from jax.experimental import pallas as pl
from jax.experimental.pallas import tpu as pltpu

F32 = jnp.float32
BF16 = jnp.bfloat16

LANES = 128
NORM_EPS = 1e-6

HG_HEADS = 8
HG_DK = 128
HG_DV = 128
HG_CHUNK = 64
HG_STEP_TOKENS = 512
HG_HEAD_GROUP = 8
AT_HEADS = 8
AT_DH = 128
MOBA_BLOCK = 256
MOBA_TOPK = 3
REL_BUCKETS = 32
REL_MAX_DIST = 128
MASKED_LOGIT = -1e30
LOG2E = math.log2(math.e)

VMEM_LIMIT = 56 * 1024 * 1024


def _cparams(sem):
    return pltpu.CompilerParams(dimension_semantics=sem, vmem_limit_bytes=VMEM_LIMIT)


def _dot(a, b):
    return jnp.dot(a, b, preferred_element_type=F32)


def _dot_nt(a, b):
    return lax.dot_general(a, b, (((1,), (1,)), ((), ())), preferred_element_type=F32)


def _dot_tn(a, b):
    return lax.dot_general(a, b, (((0,), (0,)), ((), ())), preferred_element_type=F32)


def _resident(a):
    return pl.BlockSpec(a.shape, lambda *_: (0,) * a.ndim, pipeline_mode=pl.Buffered(1))


def _head_rmsnorm(acc, gain, scale):
    cols = []
    for c in range(acc.shape[1] // LANES):
        blk = acc[:, c * LANES:(c + 1) * LANES]
        ms = jnp.mean(blk * blk, axis=-1, keepdims=True)
        cols.append(blk * lax.rsqrt(ms + NORM_EPS) * (gain * scale))
    return jnp.concatenate(cols, axis=1)


def _inproj_kernel(x_ref, g_ref, w_ref, gamma_ref, qg_ref, kg_ref,
                   hq_ref, lf_ref, hk_ref, hi_ref, hg_ref, aq_ref, ak_ref, av_ref, gate_ref,
                   *, WH, WV, WA):
    x = x_ref[...]
    ms = jnp.mean(x * x, axis=-1, keepdims=True)
    h = (x * lax.rsqrt(ms + NORM_EPS) * g_ref[...]).astype(BF16)
    col = [0]

    def segment(width):
        acc = _dot(h, w_ref[:, col[0]:col[0] + width])
        col[0] += width
        return acc

    hq_ref[...] = (segment(WH) * HG_DK ** -0.5).astype(hq_ref.dtype)

    z = segment(WH)
    gamma = gamma_ref[...]
    eg = jnp.exp(gamma - jnp.max(gamma, axis=0, keepdims=True))
    lb = eg[0:1, :] / jnp.sum(eg, axis=0, keepdims=True)
    lf_ref[...] = jnp.log2(lb + (1.0 - lb) * jax.nn.sigmoid(z))
    hk_ref[...] = ((1.0 - lb) * jax.nn.sigmoid(-z)).astype(hk_ref.dtype)

    hi_ref[...] = segment(WV).astype(hi_ref.dtype)
    g = segment(WV)
    hg_ref[...] = (g * jax.nn.sigmoid(g)).astype(hg_ref.dtype)
    aq_ref[...] = _head_rmsnorm(segment(WA), qg_ref[...],
                                AT_DH ** -0.5 * LOG2E).astype(aq_ref.dtype)
    ak_ref[...] = _head_rmsnorm(segment(WA), kg_ref[...], 1.0).astype(ak_ref.dtype)
    av_ref[...] = segment(WA).astype(av_ref.dtype)
    half = gate_ref.shape[1] // 2
    gate_ref[:, 0:half] = jax.nn.sigmoid(segment(half)).astype(gate_ref.dtype)
    gate_ref[:, half:] = jax.nn.sigmoid(segment(half)).astype(gate_ref.dtype)


def _inproj(x, g, w, gamma, qg, kg, WH, WV, WA, tm=512):
    T, D = x.shape
    tok = lambda width: pl.BlockSpec((tm, width), lambda i: (i, 0))
    small = lambda a: pl.BlockSpec(a.shape, lambda i: (0, 0))
    widths = [WH, WH, WH, WV, WV, WA, WA, WA, 2 * D]
    dtypes = [BF16, F32, BF16, BF16, BF16, BF16, BF16, BF16, BF16]
    g = g.reshape(1, D).astype(F32)
    gamma = gamma.astype(F32)
    qg = qg.reshape(1, AT_DH).astype(F32)
    kg = kg.reshape(1, AT_DH).astype(F32)
    return pl.pallas_call(
        functools.partial(_inproj_kernel, WH=WH, WV=WV, WA=WA),
        grid=(T // tm,),
        in_specs=[tok(D), small(g), _resident(w), small(gamma), small(qg), small(kg)],
        out_specs=[tok(wd) for wd in widths],
        out_shape=[jax.ShapeDtypeStruct((T, wd), dt) for wd, dt in zip(widths, dtypes)],
        compiler_params=_cparams(("parallel",)),
        name="inproj",
    )(x, g, w, gamma, qg, kg)


def _hgrn_tables(C):
    nl = int(math.log2(C))
    assert 1 << nl == C and nl >= 3
    masks = np.zeros((nl + 1, C, C), np.float32)
    masks[0] = np.eye(C)
    t = np.arange(C)
    for L in range(1, nl + 1):
        blk, half = 1 << L, 1 << (L - 1)
        base = (t // blk) * blk
        upper = (t - base) >= half
        same = base[:, None] == base[None, :]
        masks[L] = (same & upper[:, None] & (~upper)[None, :]).astype(np.float32)
    return masks, nl


def _hgrn_kernel(q_ref, lf_ref, k_ref, v_ref, g_ref, gain_ref, m_ref, o_ref,
                 e_sc, d_sc, st_sc, *, C, nl, heads, chunks):
    @pl.when(pl.program_id(1) == 0)
    def _():
        st_sc[...] = jnp.zeros_like(st_sc)

    gain = gain_ref[...]
    scan_shifts = [1 << j for j in range(nl)]
    row = lax.broadcasted_iota(jnp.int32, (C, lf_ref.shape[1]), 0)
    odd = (row & 1) == 1
    ph = row & 3

    def chunk_rows(c):
        return pl.ds(pl.multiple_of(c * C, C), C)

    def prepare(c, buf):
        lf = lf_ref[chunk_rows(c), :]
        b = lf
        for sh in scan_shifts:
            b = b + jnp.where(row >= sh, pltpu.roll(b, sh, axis=0), 0.0)
        e_sc[buf, 0:C, :] = jnp.exp2(jnp.where(odd, lf, 0.0)).astype(BF16)
        prev = pltpu.roll(lf, 1, axis=0)
        nxt = pltpu.roll(lf, C - 1, axis=0)
        x2 = jnp.where(ph == 0, nxt, jnp.where(ph == 1, 0.0, jnp.where(ph == 2, lf, lf + prev)))
        e_sc[buf, C:2 * C, :] = jnp.exp2(x2).astype(BF16)
        for L in range(3, nl + 1):
            blk, half = 1 << L, 1 << (L - 1)
            pieces = [b[base:base + blk, :] - b[base + half - 1:base + half, :]
                      for base in range(0, C, blk)]
            d = pieces[0] if len(pieces) == 1 else jnp.concatenate(pieces, axis=0)
            e_sc[buf, (L - 1) * C:L * C, :] = jnp.exp2(-jnp.abs(d)).astype(BF16)
        b_last = b[C - 1:C, :]
        e_sc[buf, nl * C:(nl + 1) * C, :] = jnp.exp2(b).astype(BF16)
        e_sc[buf, (nl + 1) * C:(nl + 2) * C, :] = jnp.exp2(b_last - b).astype(BF16)
        d_sc[buf] = jnp.exp2(b_last)

    def block_diag(x):
        n, d2 = x.shape
        z = jnp.zeros((n, d2 // 2), x.dtype)
        return jnp.concatenate([jnp.concatenate([x[:, :d2 // 2], z], axis=1),
                                jnp.concatenate([z, x[:, d2 // 2:]], axis=1)], axis=0)

    W2 = 2 * HG_DK
    pair_mask = [jnp.concatenate([m_ref[L], m_ref[L]], axis=1) for L in range(nl + 1)]

    def heads_of(c, buf):
        rows = chunk_rows(c)
        pairs = [slice(p * W2, (p + 1) * W2) for p in range(heads // 2)]
        qb = [q_ref[rows, sl] for sl in pairs]
        kb = [k_ref[rows, sl] for sl in pairs]
        vb = [v_ref[rows, sl] for sl in pairs]
        A = [pair_mask[0] * _dot_nt(q, block_diag(k)) for q, k in zip(qb, kb)]
        for L in range(1, nl + 1):
            for p, sl in enumerate(pairs):
                eL = e_sc[buf, (L - 1) * C:L * C, sl]
                A[p] = A[p] + pair_mask[L] * _dot_nt(qb[p] * eL, block_diag(kb[p] * eL))
        for p, sl in enumerate(pairs):
            eb = e_sc[buf, nl * C:(nl + 1) * C, sl]
            ek = e_sc[buf, (nl + 1) * C:(nl + 2) * C, sl]
            st = [st_sc[2 * p], st_sc[2 * p + 1]]
            st_pair = jnp.concatenate([s.astype(BF16) for s in st], axis=1)
            o = (_dot(A[p].astype(BF16), block_diag(vb[p]))
                 + _dot_nt(qb[p] * eb, block_diag(st_pair)))
            upd = _dot_tn(vb[p], kb[p] * ek)
            for j in range(2):
                h = 2 * p + j
                hs = slice(h * HG_DK, (h + 1) * HG_DK)
                js = slice(j * HG_DK, (j + 1) * HG_DK)
                st_sc[h] = st[j] * d_sc[buf, :, hs] + upd[js, js]
                oj = o[:, js]
                ms = jnp.mean(oj * oj, axis=-1, keepdims=True)
                y = oj * lax.rsqrt(ms + NORM_EPS) * gain * g_ref[rows, hs].astype(F32)
                o_ref[rows, hs] = y.astype(o_ref.dtype)

    prepare(0, 0)

    def step(c, carry):
        prepare(jnp.minimum(c + 1, chunks - 1), (c + 1) & 1)
        heads_of(c, c & 1)
        return carry

    lax.fori_loop(0, chunks, step, 0)


def _hgrn(q, lf, k, v, g, gain, B, S):
    T, W = q.shape
    C = HG_CHUNK
    TS = HG_STEP_TOKENS
    assert S % TS == 0 and TS % C == 0
    N = S // TS
    masks, nl = _hgrn_tables(C)
    row = lambda b, c: (b * N + c, 0)
    tok = lambda: pl.BlockSpec((TS, W), row)
    return pl.pallas_call(
        functools.partial(_hgrn_kernel, C=C, nl=nl, heads=HG_HEADS, chunks=TS // C),
        grid=(B, N),
        in_specs=[tok(), tok(), tok(), tok(), tok(),
                  pl.BlockSpec((1, HG_DV), lambda b, c: (0, 0)),
                  pl.BlockSpec(masks.shape, lambda b, c: (0, 0, 0))],
        out_specs=tok(),
        out_shape=jax.ShapeDtypeStruct((T, W), BF16),
        scratch_shapes=[pltpu.VMEM((2, (nl + 2) * C, W), BF16),
                        pltpu.VMEM((2, 1, W), F32),
                        pltpu.VMEM((HG_HEADS, HG_DV, HG_DK), F32)],
        compiler_params=_cparams(("parallel", "arbitrary")),
        name="hgrn2",
    )(q, lf, k, v, g, gain.reshape(1, HG_DV).astype(F32), jnp.asarray(masks, F32))


def _t5_bucket_np(dist):
    n = np.maximum(dist, 0)
    max_exact = REL_BUCKETS // 2
    nf = np.maximum(n, 1).astype(np.float32)
    large = max_exact + (np.log(nf / max_exact) / math.log(REL_MAX_DIST / max_exact)
                         * (REL_BUCKETS - max_exact)).astype(np.int32)
    large = np.minimum(large, REL_BUCKETS - 1)
    return np.where(n < max_exact, n, large).astype(np.int32)


def _bias_kernel(tab_ref, bucket_ref, o_ref):
    h = pl.program_id(0)
    bucket = bucket_ref[...]
    acc = jnp.zeros(bucket.shape, F32)
    for b in range(REL_BUCKETS):
        acc = jnp.where(bucket == b, tab_ref[b, h], acc)
    delta = (acc - tab_ref[REL_BUCKETS - 1, h]) * LOG2E
    blk = bucket.shape[0]
    r = lax.broadcasted_iota(jnp.int32, bucket.shape, 0)
    c = lax.broadcasted_iota(jnp.int32, bucket.shape, 1)
    o_ref[0] = jnp.where(c - blk <= r, delta, MASKED_LOGIT)


def _bias_tiles(rel_table):
    BLK = MOBA_BLOCK
    t = np.arange(BLK)[:, None]
    s = np.arange(2 * BLK)[None, :]
    bucket = _t5_bucket_np(t + BLK - s)
    H = rel_table.shape[1]
    return pl.pallas_call(
        _bias_kernel,
        grid=(H,),
        in_specs=[pl.BlockSpec(memory_space=pltpu.SMEM),
                  pl.BlockSpec(bucket.shape, lambda h: (0, 0))],
        out_specs=pl.BlockSpec((1, BLK, 2 * BLK), lambda h: (h, 0, 0)),
        out_shape=jax.ShapeDtypeStruct((H, BLK, 2 * BLK), F32),
        compiler_params=_cparams(("arbitrary",)),
        name="relbias",
    )(rel_table.astype(F32), jnp.asarray(bucket))


MOBA_HEADS_PER_STEP = 2
MOBA_BOUND_SLACK = 1.0 + 2.0 ** -6
MOBA_SAFE_GAP = 100.0


def _moba_kernel(q_ref, k_ref, v_ref, dbias_ref, far_ref, qg_ref, kg_ref, o_ref,
                 qnear_sc, qfar_sc, kaug_sc, vaug_sc, near_sc, far_sc, mx_sc, mb_sc, acc_sc,
                 safe_sc, *, NB, HPS):
    BLK = MOBA_BLOCK
    W2 = 2 * BLK
    PAD = NB + 2
    SHIFT = NB + 3
    NS = NB // 2 - 1
    hg = pl.program_id(1)
    ip = pl.program_id(2)

    def prepare(hh):
        cols = slice(hh * AT_DH, (hh + 1) * AT_DH)
        kaug, vaug, qnear, qfar = kaug_sc.at[hh], vaug_sc.at[hh], qnear_sc.at[hh], qfar_sc.at[hh]
        far = far_ref[pl.ds(hg * HPS + hh, 1), :]
        far_hi = far.astype(BF16).astype(F32)
        far_lo = far - far_hi
        lane = lax.broadcasted_iota(jnp.int32, (1, LANES), 1)
        tail = jnp.where(lane == NB, far_hi, jnp.where(lane == NB + 1, far_lo, 0.0))
        tail = jnp.where((lane == SHIFT) | (lane == SHIFT + 1), 1.0, tail)
        kaug[0:BLK, 0:AT_DH] = jnp.zeros((BLK, AT_DH), BF16)
        kaug[0:BLK, AT_DH:] = jnp.broadcast_to(
            jnp.where(lane == PAD, 1.0, 0.0), (BLK, LANES)).astype(BF16)
        vaug[0:BLK, :] = jnp.zeros((BLK, AT_DH + LANES), BF16)
        kms = []
        ones8 = jnp.ones((8, BLK), BF16)
        for n in range(NB):
            kb = k_ref[n * BLK:(n + 1) * BLK, cols]
            kms.append(_dot(ones8, kb)[0:1, :] * (1.0 / BLK))
            kaug[(n + 1) * BLK:(n + 2) * BLK, 0:AT_DH] = kb
            kaug[(n + 1) * BLK:(n + 2) * BLK, AT_DH:] = jnp.broadcast_to(
                jnp.where(lane == n, 1.0, tail), (BLK, LANES)).astype(BF16)
        vaug[BLK:, 0:AT_DH] = v_ref[:, cols]
        vaug[BLK:, AT_DH:] = jnp.ones((NB * BLK, LANES), BF16)

        qk_bound = (AT_DH * jnp.max(jnp.abs(qg_ref[...]), keepdims=True)
                    * jnp.max(jnp.abs(kg_ref[...]), keepdims=True)
                    * (AT_DH ** -0.5 * LOG2E * MOBA_BOUND_SLACK))
        bias_max = jnp.max(dbias_ref[hh], keepdims=True)[0:1, 0:1] + far[:, 0:1]
        bias_self = dbias_ref[hh][0:1, BLK:BLK + 1] + far[:, 0:1]
        shift = qk_bound + bias_max
        gap = shift + qk_bound - bias_self
        shift_hi = shift.astype(BF16).astype(F32)

        kmean = jnp.concatenate(kms, axis=0)
        km_hi = kmean.astype(BF16)
        km_lo = (kmean - km_hi.astype(F32)).astype(BF16)
        blk = lax.broadcasted_iota(jnp.int32, (NB, BLK), 0)
        rowid = lax.broadcasted_iota(jnp.int32, (LANES - NB, BLK), 0) + NB
        rest = jnp.where(rowid < NB + 2, 1.0, jnp.where(rowid == PAD, MASKED_LOGIT, 0.0))
        rest = jnp.where(rowid == SHIFT, -shift_hi,
                         jnp.where(rowid == SHIFT + 1, shift_hi - shift, rest))
        lane_q = lax.broadcasted_iota(jnp.int32, (BLK, LANES), 1)
        for t in range(NB):
            qt = q_ref[t * BLK:(t + 1) * BLK, cols]
            gate = jnp.where(blk < t, _dot_nt(km_hi, qt) + _dot_nt(km_lo, qt), -jnp.inf)
            rank = jnp.zeros((NB, BLK), jnp.int32)
            for m in range(t):
                gm = gate[m:m + 1, :]
                rank = rank + ((gm > gate) | ((gm == gate) & (blk > m))).astype(jnp.int32)
            sel = ((blk < t) & (rank < MOBA_TOPK)) | (blk == t)
            near = jnp.concatenate([jnp.where(sel, 0.0, MASKED_LOGIT), rest], axis=0).T
            older = jnp.where((lane_q == t) | (lane_q == t - 1), MASKED_LOGIT, near)
            rows = slice(t * BLK, (t + 1) * BLK)
            qnear[rows, 0:AT_DH] = qt
            qfar[rows, 0:AT_DH] = qt
            qnear[rows, AT_DH:] = near.astype(BF16)
            qfar[rows, AT_DH:] = older.astype(BF16)
        return jnp.max(gap)

    @pl.when(ip == 0)
    def _():
        worst = prepare(0)
        for hh in range(1, HPS):
            worst = jnp.maximum(worst, prepare(hh))
        safe_sc[0] = (worst <= MOBA_SAFE_GAP).astype(jnp.int32)

    def lane_max(s):
        out = s[:, 0:LANES]
        for c in range(1, s.shape[1] // LANES):
            out = jnp.maximum(out, s[:, c * LANES:(c + 1) * LANES])
        return out

    def probs(s, m_b):
        return jnp.exp2(s - jnp.concatenate([m_b] * (W2 // LANES), axis=1)).astype(BF16)

    blocks = (ip, NB - 1 - ip)
    rows = [pl.multiple_of(b * BLK, BLK) for b in blocks]
    n_first = jnp.maximum(ip, 1) // 2
    visits = []
    for st in range(NS):
        second = st >= n_first
        jj = jnp.where(second, st - n_first, st)
        visits.append((second.astype(jnp.int32), jnp.where(second, rows[1], rows[0]),
                       pl.multiple_of(jj * W2 + BLK, BLK)))

    def one_pass():
        for hh in range(HPS):
            for slot in range(2):
                s = (_dot_nt(qnear_sc[hh, pl.ds(rows[slot], BLK), :],
                             kaug_sc[hh, pl.ds(rows[slot], W2), :]) + dbias_ref[hh])
                acc_sc[hh, slot] = _dot(jnp.exp2(s).astype(BF16),
                                        vaug_sc[hh, pl.ds(rows[slot], W2), :])
            for st, (slot, qrow, krow) in enumerate(visits):
                s = _dot_nt(qfar_sc[hh, pl.ds(qrow, BLK), :], kaug_sc[hh, pl.ds(krow, W2), :])
                acc_sc[hh, slot] += _dot(jnp.exp2(s).astype(BF16),
                                         vaug_sc[hh, pl.ds(krow, W2), :])

    def two_pass():
        for hh in range(HPS):
            for slot in range(2):
                near_sc[hh, slot] = (_dot_nt(qnear_sc[hh, pl.ds(rows[slot], BLK), :],
                                             kaug_sc[hh, pl.ds(rows[slot], W2), :])
                                     + dbias_ref[hh])
            for st, (slot, qrow, krow) in enumerate(visits):
                far_sc[hh, st] = _dot_nt(qfar_sc[hh, pl.ds(qrow, BLK), :],
                                         kaug_sc[hh, pl.ds(krow, W2), :])
        mx_sc[...] = jnp.full(mx_sc.shape, MASKED_LOGIT, F32)
        for hh in range(HPS):
            for st, (slot, qrow, krow) in enumerate(visits):
                mx_sc[hh, slot * NS + st] = lane_max(far_sc[hh, st])
            for slot in range(2):
                m = lane_max(near_sc[hh, slot])
                for st in range(NS):
                    m = jnp.maximum(m, mx_sc[hh, slot * NS + st])
                mb_sc[hh, slot] = jnp.broadcast_to(jnp.max(m, axis=-1, keepdims=True),
                                                   (BLK, LANES))
        for hh in range(HPS):
            for slot in range(2):
                acc_sc[hh, slot] = _dot(probs(near_sc[hh, slot], mb_sc[hh, slot]),
                                        vaug_sc[hh, pl.ds(rows[slot], W2), :])
            for st, (slot, qrow, krow) in enumerate(visits):
                acc_sc[hh, slot] += _dot(probs(far_sc[hh, st], mb_sc[hh, slot]),
                                         vaug_sc[hh, pl.ds(krow, W2), :])

    lax.cond(safe_sc[0] != 0, one_pass, two_pass)

    for hh in range(HPS):
        for slot in range(2):
            o_ref[pl.ds(rows[slot], BLK), hh * AT_DH:(hh + 1) * AT_DH] = (
                acc_sc[hh, slot, :, 0:AT_DH] / acc_sc[hh, slot, :, AT_DH:]).astype(o_ref.dtype)


def _moba(q, k, v, rel_table, qg, kg, B, S):
    T, W = q.shape
    H = AT_HEADS
    HPS = MOBA_HEADS_PER_STEP
    NB = S // MOBA_BLOCK
    BLK = MOBA_BLOCK
    NS = max(NB // 2 - 1, 1)
    assert NB + 5 <= LANES and NB % 2 == 0 and H % HPS == 0
    assert int(_t5_bucket_np(np.array([BLK + 1]))[0]) == REL_BUCKETS - 1
    dbias = _bias_tiles(rel_table)
    far = jnp.broadcast_to((rel_table[REL_BUCKETS - 1, :].astype(F32) * LOG2E)[:, None],
                           (H, LANES))
    heads = lambda: pl.BlockSpec((S, HPS * AT_DH), lambda b, g, i: (b, g))
    aug = AT_DH + LANES
    return pl.pallas_call(
        functools.partial(_moba_kernel, NB=NB, HPS=HPS),
        grid=(B, H // HPS, NB // 2),
        in_specs=[heads(), heads(), heads(),
                  pl.BlockSpec((HPS, BLK, 2 * BLK), lambda b, g, i: (g, 0, 0)),
                  pl.BlockSpec((H, LANES), lambda b, g, i: (0, 0)),
                  pl.BlockSpec((1, AT_DH), lambda b, g, i: (0, 0)),
                  pl.BlockSpec((1, AT_DH), lambda b, g, i: (0, 0))],
        out_specs=heads(),
        out_shape=jax.ShapeDtypeStruct((T, W), BF16),
        scratch_shapes=[pltpu.VMEM((HPS, S, aug), BF16),
                        pltpu.VMEM((HPS, S, aug), BF16),
                        pltpu.VMEM((HPS, S + BLK, aug), BF16),
                        pltpu.VMEM((HPS, S + BLK, aug), BF16),
                        pltpu.VMEM((HPS, 2, BLK, 2 * BLK), F32),
                        pltpu.VMEM((HPS, NS, BLK, 2 * BLK), F32),
                        pltpu.VMEM((HPS, 2 * NS, BLK, LANES), F32),
                        pltpu.VMEM((HPS, 2, BLK, LANES), F32),
                        pltpu.VMEM((HPS, 2, BLK, aug), F32),
                        pltpu.SMEM((1,), jnp.int32)],
        compiler_params=_cparams(("parallel", "parallel", "arbitrary")),
        name="moba",
    )(q, k, v, dbias, far, qg.reshape(1, AT_DH).astype(F32), kg.reshape(1, AT_DH).astype(F32))


def _merge_kernel(ohg_ref, oat_ref, gate_ref, x_ref, whg_ref, wat_ref, wout_ref, g2_ref,
                  x1_ref, h2_ref, *, D):
    y_hg = _dot(ohg_ref[...], whg_ref[...])
    y_at = _dot(oat_ref[...], wat_ref[...])
    merged = (gate_ref[:, 0:D].astype(F32) * y_hg + gate_ref[:, D:2 * D].astype(F32) * y_at)
    x1 = x_ref[...] + _dot(merged.astype(BF16), wout_ref[...])
    x1_ref[...] = x1
    ms = jnp.mean(x1 * x1, axis=-1, keepdims=True)
    h2_ref[...] = (x1 * lax.rsqrt(ms + NORM_EPS) * g2_ref[...]).astype(h2_ref.dtype)


def _merge(ohg, oat, gates, x, whg, wat, wout, g2, tm=512):
    T, D = x.shape
    tok = lambda w: pl.BlockSpec((tm, w), lambda i: (i, 0))
    full = lambda a: pl.BlockSpec(a.shape, lambda i: (0, 0))
    return pl.pallas_call(
        functools.partial(_merge_kernel, D=D),
        grid=(T // tm,),
        in_specs=[tok(ohg.shape[1]), tok(oat.shape[1]), tok(2 * D), tok(D),
                  full(whg), full(wat), full(wout), pl.BlockSpec((1, D), lambda i: (0, 0))],
        out_specs=[tok(D), tok(D)],
        out_shape=[jax.ShapeDtypeStruct((T, D), F32), jax.ShapeDtypeStruct((T, D), BF16)],
        compiler_params=_cparams(("parallel",)),
        name="merge",
    )(ohg, oat, gates, x, whg, wat, wout, g2.reshape(1, D).astype(F32))


def _ffn_kernel(h_ref, x_ref, wg_ref, wu_ref, wd_ref, o_ref, *, tf):
    h = h_ref[...]
    acc = x_ref[...]
    for f0 in range(0, wg_ref.shape[1], tf):
        a = _dot(h, wg_ref[:, f0:f0 + tf])
        u = _dot(h, wu_ref[:, f0:f0 + tf])
        acc = acc + _dot((a * jax.nn.sigmoid(a) * u).astype(BF16), wd_ref[f0:f0 + tf, :])
    o_ref[...] = acc


def _ffn(h2, x1, wg, wu, wd, tm=512, tf=2816):
    T, D = x1.shape
    FF = wg.shape[1]
    assert FF % tf == 0
    tok = lambda: pl.BlockSpec((tm, D), lambda i: (i, 0))
    return pl.pallas_call(
        functools.partial(_ffn_kernel, tf=tf),
        grid=(T // tm,),
        in_specs=[tok(), tok(), _resident(wg), _resident(wu), _resident(wd)],
        out_specs=tok(),
        out_shape=jax.ShapeDtypeStruct((T, D), F32),
        compiler_params=_cparams(("parallel",)),
        name="ffn",
    )(h2, x1, wg, wu, wd)


def kernel(x, attn_norm_g, w_in, hg_lb_gamma, hg_out_norm_g, q_norm_g, k_norm_g, rel_bias_table,
           w_branch_hg, w_branch_attn, w_out, ffn_norm_g, w_ffn_gate, w_ffn_up, w_ffn_down):
    B, S, D = x.shape
    T = B * S
    depth = attn_norm_g.shape[0]
    assert depth == 1 and S % MOBA_BLOCK == 0 and S % HG_CHUNK == 0
    WH = HG_HEADS * HG_DK
    WV = HG_HEADS * HG_DV
    WA = AT_HEADS * AT_DH
    assert w_in.shape[2] == 2 * WH + 2 * WV + 3 * WA + 2 * D

    xt = x.reshape(T, D)
    for l in range(depth):
        hq, lf, hk, hi, hg, aq, ak, av, gates = _inproj(
            xt, attn_norm_g[l], w_in[l].astype(BF16), hg_lb_gamma, q_norm_g[l], k_norm_g[l],
            WH, WV, WA)
        o_hg = _hgrn(hq, lf, hk, hi, hg, hg_out_norm_g[l], B, S)
        o_at = _moba(aq, ak, av, rel_bias_table, q_norm_g[l], k_norm_g[l], B, S)
        x1, h2 = _merge(o_hg, o_at, gates, xt, w_branch_hg[l].astype(BF16),
                        w_branch_attn[l].astype(BF16), w_out[l].astype(BF16), ffn_norm_g[l])
        xt = _ffn(h2, x1, w_ffn_gate[l].astype(BF16), w_ffn_up[l].astype(BF16),
                  w_ffn_down[l].astype(BF16))
    return xt.reshape(B, S, D)
```

```python
import functools
import math

import numpy as np
import jax
import jax.numpy as jnp
from jax import lax
from jax.experimental import pallas as pl
from jax.experimental.pallas import tpu as pltpu

F32 = jnp.float32
BF16 = jnp.bfloat16

LANES = 128
NORM_EPS = 1e-6

HG_HEADS = 8
HG_DK = 128
HG_DV = 128
HG_CHUNK = 64
HG_STEP_TOKENS = 512
HG_HEAD_GROUP = 8
AT_HEADS = 8
AT_DH = 128
MOBA_BLOCK = 256
MOBA_TOPK = 3
REL_BUCKETS = 32
REL_MAX_DIST = 128
MASKED_LOGIT = -1e30
LOG2E = math.log2(math.e)

VMEM_LIMIT = 56 * 1024 * 1024


def _cparams(sem):
    return pltpu.CompilerParams(dimension_semantics=sem, vmem_limit_bytes=VMEM_LIMIT)


def _dot(a, b):
    return jnp.dot(a, b, preferred_element_type=F32)


def _dot_nt(a, b):
    return lax.dot_general(a, b, (((1,), (1,)), ((), ())), preferred_element_type=F32)


def _dot_tn(a, b):
    return lax.dot_general(a, b, (((0,), (0,)), ((), ())), preferred_element_type=F32)


def _resident(a):
    return pl.BlockSpec(a.shape, lambda *_: (0,) * a.ndim, pipeline_mode=pl.Buffered(1))


def _sigmoid(x):
    return 0.5 * jnp.tanh(0.5 * x) + 0.5


def _head_rmsnorm(acc, gain, scale):
    cols = []
    for c in range(acc.shape[1] // LANES):
        blk = acc[:, c * LANES:(c + 1) * LANES]
        ms = jnp.mean(blk * blk, axis=-1, keepdims=True)
        cols.append(blk * lax.rsqrt(ms + NORM_EPS) * (gain * scale))
    return jnp.concatenate(cols, axis=1)


def _inproj_kernel(x_ref, g_ref, w_ref, gamma_ref, qg_ref, kg_ref,
                   hq_ref, lf_ref, hk_ref, hi_ref, hg_ref, aq_ref, ak_ref, av_ref, gate_ref,
                   *, WH, WV, WA):
    x = x_ref[...]
    ms = jnp.mean(x * x, axis=-1, keepdims=True)
    h = (x * lax.rsqrt(ms + NORM_EPS) * g_ref[...]).astype(BF16)
    D = x.shape[1]
    starts = np.cumsum([0, WH, WH, WV, WV, WA, WA, WA])

    def segment(idx, width, off=0):
        c0 = int(starts[idx]) + off
        return _dot(h, w_ref[:, c0:c0 + width])

    for half in range(2):
        gate_ref[:, half * D:(half + 1) * D] = _sigmoid(
            segment(7, D, half * D)).astype(gate_ref.dtype)

    s = _sigmoid(segment(1, WH))
    gamma = gamma_ref[...]
    eg = jnp.exp(gamma - jnp.max(gamma, axis=0, keepdims=True))
    lb = eg[0:1, :] / jnp.sum(eg, axis=0, keepdims=True)
    lf_ref[...] = jnp.log2(lb + (1.0 - lb) * s)
    hk_ref[...] = ((1.0 - lb) * (1.0 - s)).astype(hk_ref.dtype)

    g = segment(3, WV)
    hg_ref[...] = (g * _sigmoid(g)).astype(hg_ref.dtype)
    aq_ref[...] = _head_rmsnorm(segment(4, WA), qg_ref[...],
                                AT_DH ** -0.5 * LOG2E).astype(aq_ref.dtype)
    ak_ref[...] = _head_rmsnorm(segment(5, WA), kg_ref[...], 1.0).astype(ak_ref.dtype)
    hq_ref[...] = (segment(0, WH) * HG_DK ** -0.5).astype(hq_ref.dtype)
    hi_ref[...] = segment(2, WV).astype(hi_ref.dtype)
    av_ref[...] = segment(6, WA).astype(av_ref.dtype)


def _inproj(x, g, w, gamma, qg, kg, WH, WV, WA, tm=512):
    T, D = x.shape
    tok = lambda width: pl.BlockSpec((tm, width), lambda i: (i, 0))
    small = lambda a: pl.BlockSpec(a.shape, lambda i: (0, 0))
    widths = [WH, WH, WH, WV, WV, WA, WA, WA, 2 * D]
    dtypes = [BF16, F32, BF16, BF16, BF16, BF16, BF16, BF16, BF16]
    g = g.reshape(1, D).astype(F32)
    gamma = gamma.astype(F32)
    qg = qg.reshape(1, AT_DH).astype(F32)
    kg = kg.reshape(1, AT_DH).astype(F32)
    return pl.pallas_call(
        functools.partial(_inproj_kernel, WH=WH, WV=WV, WA=WA),
        grid=(T // tm,),
        in_specs=[tok(D), small(g), _resident(w), small(gamma), small(qg), small(kg)],
        out_specs=[tok(wd) for wd in widths],
        out_shape=[jax.ShapeDtypeStruct((T, wd), dt) for wd, dt in zip(widths, dtypes)],
        compiler_params=_cparams(("parallel",)),
        name="inproj",
    )(x, g, w, gamma, qg, kg)


def _hgrn_tables(C):
    nl = int(math.log2(C))
    assert 1 << nl == C and nl >= 3
    masks = np.zeros((nl + 1, C, C), np.float32)
    masks[0] = np.eye(C)
    t = np.arange(C)
    for L in range(1, nl + 1):
        blk, half = 1 << L, 1 << (L - 1)
        base = (t // blk) * blk
        upper = (t - base) >= half
        same = base[:, None] == base[None, :]
        masks[L] = (same & upper[:, None] & (~upper)[None, :]).astype(np.float32)
    return masks, nl


def _hgrn_kernel(q_ref, lf_ref, k_ref, v_ref, g_ref, gain_ref, m_ref, o_ref,
                 e_sc, d_sc, st_sc, *, C, nl, heads, chunks):
    @pl.when(pl.program_id(1) == 0)
    def _():
        st_sc[...] = jnp.zeros_like(st_sc)

    gain = gain_ref[...]
    scan_shifts = [1 << j for j in range(nl)]
    row = lax.broadcasted_iota(jnp.int32, (C, lf_ref.shape[1]), 0)
    odd = (row & 1) == 1
    ph = row & 3

    def chunk_rows(c):
        return pl.ds(pl.multiple_of(c * C, C), C)

    def prepare(c, buf):
        lf = lf_ref[chunk_rows(c), :]
        b = lf
        for sh in scan_shifts:
            b = b + jnp.where(row >= sh, pltpu.roll(b, sh, axis=0), 0.0)
        e_sc[buf, 0:C, :] = jnp.exp2(jnp.where(odd, lf, 0.0)).astype(BF16)
        prev = pltpu.roll(lf, 1, axis=0)
        nxt = pltpu.roll(lf, C - 1, axis=0)
        x2 = jnp.where(ph == 0, nxt, jnp.where(ph == 1, 0.0, jnp.where(ph == 2, lf, lf + prev)))
        e_sc[buf, C:2 * C, :] = jnp.exp2(x2).astype(BF16)
        for L in range(3, nl + 1):
            blk, half = 1 << L, 1 << (L - 1)
            pieces = [b[base:base + blk, :] - b[base + half - 1:base + half, :]
                      for base in range(0, C, blk)]
            d = pieces[0] if len(pieces) == 1 else jnp.concatenate(pieces, axis=0)
            e_sc[buf, (L - 1) * C:L * C, :] = jnp.exp2(-jnp.abs(d)).astype(BF16)
        b_last = b[C - 1:C, :]
        e_sc[buf, nl * C:(nl + 1) * C, :] = jnp.exp2(b).astype(BF16)
        e_sc[buf, (nl + 1) * C:(nl + 2) * C, :] = jnp.exp2(b_last - b).astype(BF16)
        d_sc[buf] = jnp.exp2(b_last)

    def block_diag(x):
        n, d2 = x.shape
        z = jnp.zeros((n, d2 // 2), x.dtype)
        return jnp.concatenate([jnp.concatenate([x[:, :d2 // 2], z], axis=1),
                                jnp.concatenate([z, x[:, d2 // 2:]], axis=1)], axis=0)

    W2 = 2 * HG_DK
    pair_mask = [jnp.concatenate([m_ref[L], m_ref[L]], axis=1) for L in range(nl + 1)]

    def heads_of(c, buf):
        rows = chunk_rows(c)
        pairs = [slice(p * W2, (p + 1) * W2) for p in range(heads // 2)]
        qb = [q_ref[rows, sl] for sl in pairs]
        kb = [k_ref[rows, sl] for sl in pairs]
        vb = [v_ref[rows, sl] for sl in pairs]
        A = [pair_mask[0] * _dot_nt(q, block_diag(k)) for q, k in zip(qb, kb)]
        for L in range(1, nl + 1):
            for p, sl in enumerate(pairs):
                eL = e_sc[buf, (L - 1) * C:L * C, sl]
                A[p] = A[p] + pair_mask[L] * _dot_nt(qb[p] * eL, block_diag(kb[p] * eL))
        for p, sl in enumerate(pairs):
            eb = e_sc[buf, nl * C:(nl + 1) * C, sl]
            ek = e_sc[buf, (nl + 1) * C:(nl + 2) * C, sl]
            st = [st_sc[2 * p], st_sc[2 * p + 1]]
            st_pair = jnp.concatenate([s.astype(BF16) for s in st], axis=1)
            o = (_dot(A[p].astype(BF16), block_diag(vb[p]))
                 + _dot_nt(qb[p] * eb, block_diag(st_pair)))
            upd = _dot_tn(vb[p], kb[p] * ek)
            for j in range(2):
                h = 2 * p + j
                hs = slice(h * HG_DK, (h + 1) * HG_DK)
                js = slice(j * HG_DK, (j + 1) * HG_DK)
                st_sc[h] = st[j] * d_sc[buf, :, hs] + upd[js, js]
                oj = o[:, js]
                ms = jnp.mean(oj * oj, axis=-1, keepdims=True)
                y = oj * lax.rsqrt(ms + NORM_EPS) * gain * g_ref[rows, hs].astype(F32)
                o_ref[rows, hs] = y.astype(o_ref.dtype)

    prepare(0, 0)

    def step(c, carry):
        prepare(jnp.minimum(c + 1, chunks - 1), (c + 1) & 1)
        heads_of(c, c & 1)
        return carry

    lax.fori_loop(0, chunks, step, 0)


def _hgrn(q, lf, k, v, g, gain, B, S):
    T, W = q.shape
    C = HG_CHUNK
    TS = HG_STEP_TOKENS
    assert S % TS == 0 and TS % C == 0
    N = S // TS
    masks, nl = _hgrn_tables(C)
    row = lambda b, c: (b * N + c, 0)
    tok = lambda: pl.BlockSpec((TS, W), row)
    return pl.pallas_call(
        functools.partial(_hgrn_kernel, C=C, nl=nl, heads=HG_HEADS, chunks=TS // C),
        grid=(B, N),
        in_specs=[tok(), tok(), tok(), tok(), tok(),
                  pl.BlockSpec((1, HG_DV), lambda b, c: (0, 0)),
                  pl.BlockSpec(masks.shape, lambda b, c: (0, 0, 0))],
        out_specs=tok(),
        out_shape=jax.ShapeDtypeStruct((T, W), BF16),
        scratch_shapes=[pltpu.VMEM((2, (nl + 2) * C, W), BF16),
                        pltpu.VMEM((2, 1, W), F32),
                        pltpu.VMEM((HG_HEADS, HG_DV, HG_DK), F32)],
        compiler_params=_cparams(("parallel", "arbitrary")),
        name="hgrn2",
    )(q, lf, k, v, g, gain.reshape(1, HG_DV).astype(F32), jnp.asarray(masks, F32))


def _t5_bucket_np(dist):
    n = np.maximum(dist, 0)
    max_exact = REL_BUCKETS // 2
    nf = np.maximum(n, 1).astype(np.float32)
    large = max_exact + (np.log(nf / max_exact) / math.log(REL_MAX_DIST / max_exact)
                         * (REL_BUCKETS - max_exact)).astype(np.int32)
    large = np.minimum(large, REL_BUCKETS - 1)
    return np.where(n < max_exact, n, large).astype(np.int32)


def _bias_kernel(tab_ref, bucket_ref, o_ref):
    h = pl.program_id(0)
    bucket = bucket_ref[...]
    acc = jnp.zeros(bucket.shape, F32)
    for b in range(REL_BUCKETS):
        acc = jnp.where(bucket == b, tab_ref[b, h], acc)
    delta = (acc - tab_ref[REL_BUCKETS - 1, h]) * LOG2E
    blk = bucket.shape[0]
    r = lax.broadcasted_iota(jnp.int32, bucket.shape, 0)
    c = lax.broadcasted_iota(jnp.int32, bucket.shape, 1)
    o_ref[0] = jnp.where(c - blk <= r, delta, MASKED_LOGIT)


def _bias_tiles(rel_table):
    BLK = MOBA_BLOCK
    t = np.arange(BLK)[:, None]
    s = np.arange(2 * BLK)[None, :]
    bucket = _t5_bucket_np(t + BLK - s)
    H = rel_table.shape[1]
    return pl.pallas_call(
        _bias_kernel,
        grid=(H,),
        in_specs=[pl.BlockSpec(memory_space=pltpu.SMEM),
                  pl.BlockSpec(bucket.shape, lambda h: (0, 0))],
        out_specs=pl.BlockSpec((1, BLK, 2 * BLK), lambda h: (h, 0, 0)),
        out_shape=jax.ShapeDtypeStruct((H, BLK, 2 * BLK), F32),
        compiler_params=_cparams(("arbitrary",)),
        name="relbias",
    )(rel_table.astype(F32), jnp.asarray(bucket))


MOBA_HEADS_PER_STEP = 2
MOBA_BOUND_SLACK = 1.0 + 2.0 ** -6
MOBA_SAFE_GAP = 100.0


def _moba_kernel(q_ref, k_ref, v_ref, dbias_ref, far_ref, qg_ref, kg_ref, o_ref,
                 qnear_sc, qfar_sc, kaug_sc, vaug_sc, near_sc, far_sc, mx_sc, mb_sc, acc_sc,
                 safe_sc, *, NB, HPS):
    BLK = MOBA_BLOCK
    W2 = 2 * BLK
    PAD = NB + 2
    SHIFT = NB + 3
    NS = NB // 2 - 1
    hg = pl.program_id(1)
    ip = pl.program_id(2)

    def prepare(hh):
        cols = slice(hh * AT_DH, (hh + 1) * AT_DH)
        kaug, vaug, qnear, qfar = kaug_sc.at[hh], vaug_sc.at[hh], qnear_sc.at[hh], qfar_sc.at[hh]
        far = far_ref[pl.ds(hg * HPS + hh, 1), :]
        far_hi = far.astype(BF16).astype(F32)
        far_lo = far - far_hi
        lane = lax.broadcasted_iota(jnp.int32, (1, LANES), 1)
        tail = jnp.where(lane == NB, far_hi, jnp.where(lane == NB + 1, far_lo, 0.0))
        tail = jnp.where((lane == SHIFT) | (lane == SHIFT + 1), 1.0, tail)
        kaug[0:BLK, 0:AT_DH] = jnp.zeros((BLK, AT_DH), BF16)
        kaug[0:BLK, AT_DH:] = jnp.broadcast_to(
            jnp.where(lane == PAD, 1.0, 0.0), (BLK, LANES)).astype(BF16)
        vaug[0:BLK, :] = jnp.zeros((BLK, AT_DH + LANES), BF16)
        kms = []
        ones8 = jnp.ones((8, BLK), BF16)
        for n in range(NB):
            kb = k_ref[n * BLK:(n + 1) * BLK, cols]
            kms.append(_dot(ones8, kb)[0:1, :] * (1.0 / BLK))
            kaug[(n + 1) * BLK:(n + 2) * BLK, 0:AT_DH] = kb
            kaug[(n + 1) * BLK:(n + 2) * BLK, AT_DH:] = jnp.broadcast_to(
                jnp.where(lane == n, 1.0, tail), (BLK, LANES)).astype(BF16)
        vaug[BLK:, 0:AT_DH] = v_ref[:, cols]
        vaug[BLK:, AT_DH:] = jnp.ones((NB * BLK, LANES), BF16)

        qk_bound = (AT_DH * jnp.max(jnp.abs(qg_ref[...]), keepdims=True)
                    * jnp.max(jnp.abs(kg_ref[...]), keepdims=True)
                    * (AT_DH ** -0.5 * LOG2E * MOBA_BOUND_SLACK))
        bias_max = jnp.max(dbias_ref[hh], keepdims=True)[0:1, 0:1] + far[:, 0:1]
        bias_self = dbias_ref[hh][0:1, BLK:BLK + 1] + far[:, 0:1]
        shift = qk_bound + bias_max
        gap = shift + qk_bound - bias_self
        shift_hi = shift.astype(BF16).astype(F32)

        kmean = jnp.concatenate(kms, axis=0)
        km_hi = kmean.astype(BF16)
        km_lo = (kmean - km_hi.astype(F32)).astype(BF16)
        blk = lax.broadcasted_iota(jnp.int32, (NB, BLK), 0)
        rowid = lax.broadcasted_iota(jnp.int32, (LANES - NB, BLK), 0) + NB
        rest = jnp.where(rowid < NB + 2, 1.0, jnp.where(rowid == PAD, MASKED_LOGIT, 0.0))
        rest = jnp.where(rowid == SHIFT, -shift_hi,
                         jnp.where(rowid == SHIFT + 1, shift_hi - shift, rest))
        lane_q = lax.broadcasted_iota(jnp.int32, (BLK, LANES), 1)
        for t in range(NB):
            qt = q_ref[t * BLK:(t + 1) * BLK, cols]
            gate = jnp.where(blk < t, _dot_nt(km_hi, qt) + _dot_nt(km_lo, qt), -jnp.inf)
            rank = jnp.zeros((NB, BLK), jnp.int32)
            for m in range(t):
                gm = gate[m:m + 1, :]
                rank = rank + ((gm > gate) | ((gm == gate) & (blk > m))).astype(jnp.int32)
            sel = ((blk < t) & (rank < MOBA_TOPK)) | (blk == t)
            near = jnp.concatenate([jnp.where(sel, 0.0, MASKED_LOGIT), rest], axis=0).T
            older = jnp.where((lane_q == t) | (lane_q == t - 1), MASKED_LOGIT, near)
            rows = slice(t * BLK, (t + 1) * BLK)
            qnear[rows, 0:AT_DH] = qt
            qfar[rows, 0:AT_DH] = qt
            qnear[rows, AT_DH:] = near.astype(BF16)
            qfar[rows, AT_DH:] = older.astype(BF16)
        return jnp.max(gap)

    @pl.when(ip == 0)
    def _():
        worst = prepare(0)
        for hh in range(1, HPS):
            worst = jnp.maximum(worst, prepare(hh))
        safe_sc[0] = (worst <= MOBA_SAFE_GAP).astype(jnp.int32)

    def lane_max(s):
        out = s[:, 0:LANES]
        for c in range(1, s.shape[1] // LANES):
            out = jnp.maximum(out, s[:, c * LANES:(c + 1) * LANES])
        return out

    def probs(s, m_b):
        return jnp.exp2(s - jnp.concatenate([m_b] * (W2 // LANES), axis=1)).astype(BF16)

    blocks = (ip, NB - 1 - ip)
    rows = [pl.multiple_of(b * BLK, BLK) for b in blocks]
    n_first = jnp.maximum(ip, 1) // 2
    visits = []
    for st in range(NS):
        second = st >= n_first
        jj = jnp.where(second, st - n_first, st)
        visits.append((second.astype(jnp.int32), jnp.where(second, rows[1], rows[0]),
                       pl.multiple_of(jj * W2 + BLK, BLK)))

    def one_pass():
        for hh in range(HPS):
            for slot in range(2):
                s = (_dot_nt(qnear_sc[hh, pl.ds(rows[slot], BLK), :],
                             kaug_sc[hh, pl.ds(rows[slot], W2), :]) + dbias_ref[hh])
                acc_sc[hh, slot] = _dot(jnp.exp2(s).astype(BF16),
                                        vaug_sc[hh, pl.ds(rows[slot], W2), :])
            for st, (slot, qrow, krow) in enumerate(visits):
                s = _dot_nt(qfar_sc[hh, pl.ds(qrow, BLK), :], kaug_sc[hh, pl.ds(krow, W2), :])
                acc_sc[hh, slot] += _dot(jnp.exp2(s).astype(BF16),
                                         vaug_sc[hh, pl.ds(krow, W2), :])

    def two_pass():
        for hh in range(HPS):
            for slot in range(2):
                near_sc[hh, slot] = (_dot_nt(qnear_sc[hh, pl.ds(rows[slot], BLK), :],
                                             kaug_sc[hh, pl.ds(rows[slot], W2), :])
                                     + dbias_ref[hh])
            for st, (slot, qrow, krow) in enumerate(visits):
                far_sc[hh, st] = _dot_nt(qfar_sc[hh, pl.ds(qrow, BLK), :],
                                         kaug_sc[hh, pl.ds(krow, W2), :])
        mx_sc[...] = jnp.full(mx_sc.shape, MASKED_LOGIT, F32)
        for hh in range(HPS):
            for st, (slot, qrow, krow) in enumerate(visits):
                mx_sc[hh, slot * NS + st] = lane_max(far_sc[hh, st])
            for slot in range(2):
                m = lane_max(near_sc[hh, slot])
                for st in range(NS):
                    m = jnp.maximum(m, mx_sc[hh, slot * NS + st])
                mb_sc[hh, slot] = jnp.broadcast_to(jnp.max(m, axis=-1, keepdims=True),
                                                   (BLK, LANES))
        for hh in range(HPS):
            for slot in range(2):
                acc_sc[hh, slot] = _dot(probs(near_sc[hh, slot], mb_sc[hh, slot]),
                                        vaug_sc[hh, pl.ds(rows[slot], W2), :])
            for st, (slot, qrow, krow) in enumerate(visits):
                acc_sc[hh, slot] += _dot(probs(far_sc[hh, st], mb_sc[hh, slot]),
                                         vaug_sc[hh, pl.ds(krow, W2), :])

    lax.cond(safe_sc[0] != 0, one_pass, two_pass)

    for hh in range(HPS):
        for slot in range(2):
            o_ref[pl.ds(rows[slot], BLK), hh * AT_DH:(hh + 1) * AT_DH] = (
                acc_sc[hh, slot, :, 0:AT_DH] / acc_sc[hh, slot, :, AT_DH:]).astype(o_ref.dtype)


def _moba(q, k, v, rel_table, qg, kg, B, S):
    T, W = q.shape
    H = AT_HEADS
    HPS = MOBA_HEADS_PER_STEP
    NB = S // MOBA_BLOCK
    BLK = MOBA_BLOCK
    NS = max(NB // 2 - 1, 1)
    assert NB + 5 <= LANES and NB % 2 == 0 and H % HPS == 0
    assert int(_t5_bucket_np(np.array([BLK + 1]))[0]) == REL_BUCKETS - 1
    dbias = _bias_tiles(rel_table)
    far = jnp.broadcast_to((rel_table[REL_BUCKETS - 1, :].astype(F32) * LOG2E)[:, None],
                           (H, LANES))
    heads = lambda: pl.BlockSpec((S, HPS * AT_DH), lambda b, g, i: (b, g))
    aug = AT_DH + LANES
    return pl.pallas_call(
        functools.partial(_moba_kernel, NB=NB, HPS=HPS),
        grid=(B, H // HPS, NB // 2),
        in_specs=[heads(), heads(), heads(),
                  pl.BlockSpec((HPS, BLK, 2 * BLK), lambda b, g, i: (g, 0, 0)),
                  pl.BlockSpec((H, LANES), lambda b, g, i: (0, 0)),
                  pl.BlockSpec((1, AT_DH), lambda b, g, i: (0, 0)),
                  pl.BlockSpec((1, AT_DH), lambda b, g, i: (0, 0))],
        out_specs=heads(),
        out_shape=jax.ShapeDtypeStruct((T, W), BF16),
        scratch_shapes=[pltpu.VMEM((HPS, S, aug), BF16),
                        pltpu.VMEM((HPS, S, aug), BF16),
                        pltpu.VMEM((HPS, S + BLK, aug), BF16),
                        pltpu.VMEM((HPS, S + BLK, aug), BF16),
                        pltpu.VMEM((HPS, 2, BLK, 2 * BLK), F32),
                        pltpu.VMEM((HPS, NS, BLK, 2 * BLK), F32),
                        pltpu.VMEM((HPS, 2 * NS, BLK, LANES), F32),
                        pltpu.VMEM((HPS, 2, BLK, LANES), F32),
                        pltpu.VMEM((HPS, 2, BLK, aug), F32),
                        pltpu.SMEM((1,), jnp.int32)],
        compiler_params=_cparams(("parallel", "parallel", "arbitrary")),
        name="moba",
    )(q, k, v, dbias, far, qg.reshape(1, AT_DH).astype(F32), kg.reshape(1, AT_DH).astype(F32))


def _merge_kernel(ohg_ref, oat_ref, gate_ref, x_ref, whg_ref, wat_ref, wout_ref, g2_ref,
                  x1_ref, h2_ref, *, D, sub):
    for r0 in range(0, x_ref.shape[0], sub):
        rows = slice(r0, r0 + sub)
        y_hg = _dot(ohg_ref[rows, :], whg_ref[...])
        y_at = _dot(oat_ref[rows, :], wat_ref[...])
        merged = (gate_ref[rows, 0:D].astype(F32) * y_hg
                  + gate_ref[rows, D:2 * D].astype(F32) * y_at)
        x1 = x_ref[rows, :] + _dot(merged.astype(BF16), wout_ref[...])
        x1_ref[rows, :] = x1
        ms = jnp.mean(x1 * x1, axis=-1, keepdims=True)
        h2_ref[rows, :] = (x1 * lax.rsqrt(ms + NORM_EPS) * g2_ref[...]).astype(h2_ref.dtype)


def _merge(ohg, oat, gates, x, whg, wat, wout, g2, tm=1024, sub=512):
    T, D = x.shape
    tok = lambda w: pl.BlockSpec((tm, w), lambda i: (i, 0))
    return pl.pallas_call(
        functools.partial(_merge_kernel, D=D, sub=sub),
        grid=(T // tm,),
        in_specs=[tok(ohg.shape[1]), tok(oat.shape[1]), tok(2 * D), tok(D),
                  _resident(whg), _resident(wat), _resident(wout),
                  pl.BlockSpec((1, D), lambda i: (0, 0))],
        out_specs=[tok(D), tok(D)],
        out_shape=[jax.ShapeDtypeStruct((T, D), F32), jax.ShapeDtypeStruct((T, D), BF16)],
        compiler_params=_cparams(("parallel",)),
        name="merge",
    )(ohg, oat, gates, x, whg, wat, wout, g2.reshape(1, D).astype(F32))


def _ffn_kernel(h_ref, x_ref, wg_ref, wu_ref, wd_ref, o_ref, *, tf):
    h = h_ref[...]
    acc = x_ref[...]
    for f0 in range(0, wg_ref.shape[1], tf):
        a = _dot(h, wg_ref[:, f0:f0 + tf])
        u = _dot(h, wu_ref[:, f0:f0 + tf])
        acc = acc + _dot((a * _sigmoid(a) * u).astype(BF16), wd_ref[f0:f0 + tf, :])
    o_ref[...] = acc


def _ffn(h2, x1, wg, wu, wd, tm=512, tf=2816):
    T, D = x1.shape
    FF = wg.shape[1]
    assert FF % tf == 0
    tok = lambda: pl.BlockSpec((tm, D), lambda i: (i, 0))
    return pl.pallas_call(
        functools.partial(_ffn_kernel, tf=tf),
        grid=(T // tm,),
        in_specs=[tok(), tok(), _resident(wg), _resident(wu), _resident(wd)],
        out_specs=tok(),
        out_shape=jax.ShapeDtypeStruct((T, D), F32),
        compiler_params=_cparams(("parallel",)),
        name="ffn",
    )(h2, x1, wg, wu, wd)


def kernel(x, attn_norm_g, w_in, hg_lb_gamma, hg_out_norm_g, q_norm_g, k_norm_g, rel_bias_table,
           w_branch_hg, w_branch_attn, w_out, ffn_norm_g, w_ffn_gate, w_ffn_up, w_ffn_down):
    B, S, D = x.shape
    T = B * S
    depth = attn_norm_g.shape[0]
    assert depth == 1 and S % MOBA_BLOCK == 0 and S % HG_CHUNK == 0
    WH = HG_HEADS * HG_DK
    WV = HG_HEADS * HG_DV
    WA = AT_HEADS * AT_DH
    assert w_in.shape[2] == 2 * WH + 2 * WV + 3 * WA + 2 * D

    xt = x.reshape(T, D)
    for l in range(depth):
        hq, lf, hk, hi, hg, aq, ak, av, gates = _inproj(
            xt, attn_norm_g[l], w_in[l].astype(BF16), hg_lb_gamma, q_norm_g[l], k_norm_g[l],
            WH, WV, WA)
        o_hg = _hgrn(hq, lf, hk, hi, hg, hg_out_norm_g[l], B, S)
        o_at = _moba(aq, ak, av, rel_bias_table, q_norm_g[l], k_norm_g[l], B, S)
        x1, h2 = _merge(o_hg, o_at, gates, xt, w_branch_hg[l].astype(BF16),
                        w_branch_attn[l].astype(BF16), w_out[l].astype(BF16), ffn_norm_g[l])
        xt = _ffn(h2, x1, w_ffn_gate[l].astype(BF16), w_ffn_up[l].astype(BF16),
                  w_ffn_down[l].astype(BF16))
    return xt.reshape(B, S, D)
```

```python
import functools
import math

import numpy as np
import jax
import jax.numpy as jnp
from jax import lax
from jax.experimental import pallas as pl
from jax.experimental.pallas import tpu as pltpu

F32 = jnp.float32
BF16 = jnp.bfloat16

LANES = 128
NORM_EPS = 1e-6

HG_HEADS = 8
HG_DK = 128
HG_DV = 128
HG_CHUNK = 64
HG_STEP_TOKENS = 512
HG_HEAD_GROUP = 8
AT_HEADS = 8
AT_DH = 128
MOBA_BLOCK = 256
MOBA_TOPK = 3
REL_BUCKETS = 32
REL_MAX_DIST = 128
MASKED_LOGIT = -1e30
LOG2E = math.log2(math.e)

VMEM_LIMIT = 56 * 1024 * 1024


def _cparams(sem):
    return pltpu.CompilerParams(dimension_semantics=sem, vmem_limit_bytes=VMEM_LIMIT)


def _dot(a, b):
    return jnp.dot(a, b, preferred_element_type=F32)


def _dot_nt(a, b):
    return lax.dot_general(a, b, (((1,), (1,)), ((), ())), preferred_element_type=F32)


def _dot_tn(a, b):
    return lax.dot_general(a, b, (((0,), (0,)), ((), ())), preferred_element_type=F32)


def _resident(a):
    return pl.BlockSpec(a.shape, lambda *_: (0,) * a.ndim, pipeline_mode=pl.Buffered(1))


def _sigmoid(x):
    return 0.5 * jnp.tanh(0.5 * x) + 0.5


def _head_rmsnorm(acc, gain, scale):
    cols = []
    for c in range(acc.shape[1] // LANES):
        blk = acc[:, c * LANES:(c + 1) * LANES]
        ms = jnp.mean(blk * blk, axis=-1, keepdims=True)
        cols.append(blk * lax.rsqrt(ms + NORM_EPS) * (gain * scale))
    return jnp.concatenate(cols, axis=1)


def _inproj_kernel(x_ref, g_ref, w_ref, gamma_ref, qg_ref, kg_ref,
                   hq_ref, lf_ref, hk_ref, hi_ref, hg_ref, aq_ref, ak_ref, av_ref, gate_ref,
                   *, WH, WV, WA):
    x = x_ref[...]
    ms = jnp.mean(x * x, axis=-1, keepdims=True)
    h = (x * lax.rsqrt(ms + NORM_EPS) * g_ref[...]).astype(BF16)
    D = x.shape[1]
    starts = np.cumsum([0, WH, WH, WV, WV, WA, WA, WA])

    def segment(idx, width, off=0):
        c0 = int(starts[idx]) + off
        return _dot(h, w_ref[:, c0:c0 + width])

    for half in range(2):
        gate_ref[:, half * D:(half + 1) * D] = _sigmoid(
            segment(7, D, half * D)).astype(gate_ref.dtype)

    s = _sigmoid(segment(1, WH))
    gamma = gamma_ref[...]
    eg = jnp.exp(gamma - jnp.max(gamma, axis=0, keepdims=True))
    lb = eg[0:1, :] / jnp.sum(eg, axis=0, keepdims=True)
    lf_ref[...] = jnp.log2(lb + (1.0 - lb) * s)
    hk_ref[...] = ((1.0 - lb) * (1.0 - s)).astype(hk_ref.dtype)

    g = segment(3, WV)
    hg_ref[...] = (g * _sigmoid(g)).astype(hg_ref.dtype)
    aq_ref[...] = _head_rmsnorm(segment(4, WA), qg_ref[...],
                                AT_DH ** -0.5 * LOG2E).astype(aq_ref.dtype)
    ak_ref[...] = _head_rmsnorm(segment(5, WA), kg_ref[...], 1.0).astype(ak_ref.dtype)
    hq_ref[...] = (segment(0, WH) * HG_DK ** -0.5).astype(hq_ref.dtype)
    hi_ref[...] = segment(2, WV).astype(hi_ref.dtype)
    av_ref[...] = segment(6, WA).astype(av_ref.dtype)


def _inproj(x, g, w, gamma, qg, kg, WH, WV, WA, tm=512):
    T, D = x.shape
    tok = lambda width: pl.BlockSpec((tm, width), lambda i: (i, 0))
    small = lambda a: pl.BlockSpec(a.shape, lambda i: (0, 0))
    widths = [WH, WH, WH, WV, WV, WA, WA, WA, 2 * D]
    dtypes = [BF16, F32, BF16, BF16, BF16, BF16, BF16, BF16, BF16]
    g = g.reshape(1, D).astype(F32)
    gamma = gamma.astype(F32)
    qg = qg.reshape(1, AT_DH).astype(F32)
    kg = kg.reshape(1, AT_DH).astype(F32)
    return pl.pallas_call(
        functools.partial(_inproj_kernel, WH=WH, WV=WV, WA=WA),
        grid=(T // tm,),
        in_specs=[tok(D), small(g), _resident(w), small(gamma), small(qg), small(kg)],
        out_specs=[tok(wd) for wd in widths],
        out_shape=[jax.ShapeDtypeStruct((T, wd), dt) for wd, dt in zip(widths, dtypes)],
        compiler_params=_cparams(("parallel",)),
        name="inproj",
    )(x, g, w, gamma, qg, kg)


def _hgrn_tables(C):
    nl = int(math.log2(C))
    assert 1 << nl == C and nl >= 3
    masks = np.zeros((nl + 1, C, C), np.float32)
    masks[0] = np.eye(C)
    t = np.arange(C)
    for L in range(1, nl + 1):
        blk, half = 1 << L, 1 << (L - 1)
        base = (t // blk) * blk
        upper = (t - base) >= half
        same = base[:, None] == base[None, :]
        masks[L] = (same & upper[:, None] & (~upper)[None, :]).astype(np.float32)
    return masks, nl


def _hgrn_kernel(q_ref, lf_ref, k_ref, v_ref, g_ref, gain_ref, m_ref, o_ref,
                 e_sc, d_sc, st_sc, *, C, nl, heads, chunks):
    @pl.when(pl.program_id(1) == 0)
    def _():
        st_sc[...] = jnp.zeros_like(st_sc)

    gain = gain_ref[...]
    scan_shifts = [1 << j for j in range(nl)]
    row = lax.broadcasted_iota(jnp.int32, (C, lf_ref.shape[1]), 0)
    odd = (row & 1) == 1
    ph = row & 3

    def chunk_rows(c):
        return pl.ds(pl.multiple_of(c * C, C), C)

    def prepare(c, buf):
        lf = lf_ref[chunk_rows(c), :]
        b = lf
        for sh in scan_shifts:
            b = b + jnp.where(row >= sh, pltpu.roll(b, sh, axis=0), 0.0)
        e_sc[buf, 0:C, :] = jnp.exp2(jnp.where(odd, lf, 0.0)).astype(BF16)
        prev = pltpu.roll(lf, 1, axis=0)
        nxt = pltpu.roll(lf, C - 1, axis=0)
        x2 = jnp.where(ph == 0, nxt, jnp.where(ph == 1, 0.0, jnp.where(ph == 2, lf, lf + prev)))
        e_sc[buf, C:2 * C, :] = jnp.exp2(x2).astype(BF16)
        for L in range(3, nl + 1):
            blk, half = 1 << L, 1 << (L - 1)
            pieces = [b[base:base + blk, :] - b[base + half - 1:base + half, :]
                      for base in range(0, C, blk)]
            d = pieces[0] if len(pieces) == 1 else jnp.concatenate(pieces, axis=0)
            e_sc[buf, (L - 1) * C:L * C, :] = jnp.exp2(-jnp.abs(d)).astype(BF16)
        b_last = b[C - 1:C, :]
        e_sc[buf, nl * C:(nl + 1) * C, :] = jnp.exp2(b).astype(BF16)
        e_sc[buf, (nl + 1) * C:(nl + 2) * C, :] = jnp.exp2(b_last - b).astype(BF16)
        d_sc[buf] = jnp.exp2(b_last)

    def block_diag(x):
        n, d2 = x.shape
        z = jnp.zeros((n, d2 // 2), x.dtype)
        return jnp.concatenate([jnp.concatenate([x[:, :d2 // 2], z], axis=1),
                                jnp.concatenate([z, x[:, d2 // 2:]], axis=1)], axis=0)

    W2 = 2 * HG_DK
    pair_mask = [jnp.concatenate([m_ref[L], m_ref[L]], axis=1) for L in range(nl + 1)]

    def heads_of(c, buf):
        rows = chunk_rows(c)
        pairs = [slice(p * W2, (p + 1) * W2) for p in range(heads // 2)]
        qb = [q_ref[rows, sl] for sl in pairs]
        kb = [k_ref[rows, sl] for sl in pairs]
        vb = [v_ref[rows, sl] for sl in pairs]
        A = [pair_mask[0] * _dot_nt(q, block_diag(k)) for q, k in zip(qb, kb)]
        for L in range(1, nl + 1):
            for p, sl in enumerate(pairs):
                eL = e_sc[buf, (L - 1) * C:L * C, sl]
                A[p] = A[p] + pair_mask[L] * _dot_nt(qb[p] * eL, block_diag(kb[p] * eL))
        for p, sl in enumerate(pairs):
            eb = e_sc[buf, nl * C:(nl + 1) * C, sl]
            ek = e_sc[buf, (nl + 1) * C:(nl + 2) * C, sl]
            st = [st_sc[2 * p], st_sc[2 * p + 1]]
            st_pair = jnp.concatenate([s.astype(BF16) for s in st], axis=1)
            o = (_dot(A[p].astype(BF16), block_diag(vb[p]))
                 + _dot_nt(qb[p] * eb, block_diag(st_pair)))
            upd = _dot_tn(vb[p], kb[p] * ek)
            for j in range(2):
                h = 2 * p + j
                hs = slice(h * HG_DK, (h + 1) * HG_DK)
                js = slice(j * HG_DK, (j + 1) * HG_DK)
                st_sc[h] = st[j] * d_sc[buf, :, hs] + upd[js, js]
                oj = o[:, js]
                ms = jnp.mean(oj * oj, axis=-1, keepdims=True)
                y = oj * lax.rsqrt(ms + NORM_EPS) * gain * g_ref[rows, hs].astype(F32)
                o_ref[rows, hs] = y.astype(o_ref.dtype)

    prepare(0, 0)

    def step(c, carry):
        prepare(jnp.minimum(c + 1, chunks - 1), (c + 1) & 1)
        heads_of(c, c & 1)
        return carry

    lax.fori_loop(0, chunks, step, 0)


def _hgrn(q, lf, k, v, g, gain, B, S):
    T, W = q.shape
    C = HG_CHUNK
    TS = HG_STEP_TOKENS
    assert S % TS == 0 and TS % C == 0
    N = S // TS
    masks, nl = _hgrn_tables(C)
    row = lambda b, c: (b * N + c, 0)
    tok = lambda: pl.BlockSpec((TS, W), row)
    return pl.pallas_call(
        functools.partial(_hgrn_kernel, C=C, nl=nl, heads=HG_HEADS, chunks=TS // C),
        grid=(B, N),
        in_specs=[tok(), tok(), tok(), tok(), tok(),
                  pl.BlockSpec((1, HG_DV), lambda b, c: (0, 0)),
                  pl.BlockSpec(masks.shape, lambda b, c: (0, 0, 0))],
        out_specs=tok(),
        out_shape=jax.ShapeDtypeStruct((T, W), BF16),
        scratch_shapes=[pltpu.VMEM((2, (nl + 2) * C, W), BF16),
                        pltpu.VMEM((2, 1, W), F32),
                        pltpu.VMEM((HG_HEADS, HG_DV, HG_DK), F32)],
        compiler_params=_cparams(("parallel", "arbitrary")),
        name="hgrn2",
    )(q, lf, k, v, g, gain.reshape(1, HG_DV).astype(F32), jnp.asarray(masks, F32))


def _t5_bucket_np(dist):
    n = np.maximum(dist, 0)
    max_exact = REL_BUCKETS // 2
    nf = np.maximum(n, 1).astype(np.float32)
    large = max_exact + (np.log(nf / max_exact) / math.log(REL_MAX_DIST / max_exact)
                         * (REL_BUCKETS - max_exact)).astype(np.int32)
    large = np.minimum(large, REL_BUCKETS - 1)
    return np.where(n < max_exact, n, large).astype(np.int32)


def _bias_kernel(tab_ref, bucket_ref, o_ref):
    h = pl.program_id(0)
    bucket = bucket_ref[...]
    acc = jnp.zeros(bucket.shape, F32)
    for b in range(REL_BUCKETS):
        acc = jnp.where(bucket == b, tab_ref[b, h], acc)
    delta = (acc - tab_ref[REL_BUCKETS - 1, h]) * LOG2E
    blk = bucket.shape[0]
    r = lax.broadcasted_iota(jnp.int32, bucket.shape, 0)
    c = lax.broadcasted_iota(jnp.int32, bucket.shape, 1)
    o_ref[0] = jnp.where(c - blk <= r, delta, MASKED_LOGIT)


def _bias_tiles(rel_table):
    BLK = MOBA_BLOCK
    t = np.arange(BLK)[:, None]
    s = np.arange(2 * BLK)[None, :]
    bucket = _t5_bucket_np(t + BLK - s)
    H = rel_table.shape[1]
    return pl.pallas_call(
        _bias_kernel,
        grid=(H,),
        in_specs=[pl.BlockSpec(memory_space=pltpu.SMEM),
                  pl.BlockSpec(bucket.shape, lambda h: (0, 0))],
        out_specs=pl.BlockSpec((1, BLK, 2 * BLK), lambda h: (h, 0, 0)),
        out_shape=jax.ShapeDtypeStruct((H, BLK, 2 * BLK), F32),
        compiler_params=_cparams(("arbitrary",)),
        name="relbias",
    )(rel_table.astype(F32), jnp.asarray(bucket))


MOBA_HEADS_PER_STEP = 2
MOBA_PAIRS_PER_STEP = 2
MOBA_BOUND_SLACK = 1.0 + 2.0 ** -6
MOBA_SAFE_GAP = 100.0


def _moba_kernel(q_ref, k_ref, v_ref, dbias_ref, far_ref, qg_ref, kg_ref, o_ref,
                 qnear_sc, qfar_sc, kaug_sc, vaug_sc, near_sc, far_sc, mx_sc, mb_sc, acc_sc,
                 safe_sc, *, NB, HPS, PPS):
    BLK = MOBA_BLOCK
    W2 = 2 * BLK
    PAD = NB + 2
    SHIFT = NB + 3
    NS = NB // 2 - 1
    hg = pl.program_id(1)
    ip = pl.program_id(2)

    def prepare(hh):
        cols = slice(hh * AT_DH, (hh + 1) * AT_DH)
        kaug, vaug, qnear, qfar = kaug_sc.at[hh], vaug_sc.at[hh], qnear_sc.at[hh], qfar_sc.at[hh]
        far = far_ref[pl.ds(hg * HPS + hh, 1), :]
        far_hi = far.astype(BF16).astype(F32)
        far_lo = far - far_hi
        lane = lax.broadcasted_iota(jnp.int32, (1, LANES), 1)
        tail = jnp.where(lane == NB, far_hi, jnp.where(lane == NB + 1, far_lo, 0.0))
        tail = jnp.where((lane == SHIFT) | (lane == SHIFT + 1), 1.0, tail)
        kaug[0:BLK, 0:AT_DH] = jnp.zeros((BLK, AT_DH), BF16)
        kaug[0:BLK, AT_DH:] = jnp.broadcast_to(
            jnp.where(lane == PAD, 1.0, 0.0), (BLK, LANES)).astype(BF16)
        vaug[0:BLK, :] = jnp.zeros((BLK, AT_DH + LANES), BF16)
        kms = []
        ones8 = jnp.ones((8, BLK), BF16)
        for n in range(NB):
            kb = k_ref[n * BLK:(n + 1) * BLK, cols]
            kms.append(_dot(ones8, kb)[0:1, :] * (1.0 / BLK))
            kaug[(n + 1) * BLK:(n + 2) * BLK, 0:AT_DH] = kb
            kaug[(n + 1) * BLK:(n + 2) * BLK, AT_DH:] = jnp.broadcast_to(
                jnp.where(lane == n, 1.0, tail), (BLK, LANES)).astype(BF16)
        vaug[BLK:, 0:AT_DH] = v_ref[:, cols]
        vaug[BLK:, AT_DH:] = jnp.ones((NB * BLK, LANES), BF16)

        qk_bound = (AT_DH * jnp.max(jnp.abs(qg_ref[...]), keepdims=True)
                    * jnp.max(jnp.abs(kg_ref[...]), keepdims=True)
                    * (AT_DH ** -0.5 * LOG2E * MOBA_BOUND_SLACK))
        bias_max = jnp.max(dbias_ref[hh], keepdims=True)[0:1, 0:1] + far[:, 0:1]
        bias_self = dbias_ref[hh][0:1, BLK:BLK + 1] + far[:, 0:1]
        shift = qk_bound + bias_max
        gap = shift + qk_bound - bias_self
        shift_hi = shift.astype(BF16).astype(F32)

        kmean = jnp.concatenate(kms, axis=0)
        km_hi = kmean.astype(BF16)
        km_lo = (kmean - km_hi.astype(F32)).astype(BF16)
        blk = lax.broadcasted_iota(jnp.int32, (NB, BLK), 0)
        rowid = lax.broadcasted_iota(jnp.int32, (LANES - NB, BLK), 0) + NB
        rest = jnp.where(rowid < NB + 2, 1.0, jnp.where(rowid == PAD, MASKED_LOGIT, 0.0))
        rest = jnp.where(rowid == SHIFT, -shift_hi,
                         jnp.where(rowid == SHIFT + 1, shift_hi - shift, rest))
        lane_q = lax.broadcasted_iota(jnp.int32, (BLK, LANES), 1)
        for t in range(NB):
            qt = q_ref[t * BLK:(t + 1) * BLK, cols]
            gate = jnp.where(blk < t, _dot_nt(km_hi, qt) + _dot_nt(km_lo, qt), -jnp.inf)
            rank = jnp.zeros((NB, BLK), jnp.int32)
            for m in range(t):
                gm = gate[m:m + 1, :]
                rank = rank + ((gm > gate) | ((gm == gate) & (blk > m))).astype(jnp.int32)
            sel = ((blk < t) & (rank < MOBA_TOPK)) | (blk == t)
            near = jnp.concatenate([jnp.where(sel, 0.0, MASKED_LOGIT), rest], axis=0).T
            older = jnp.where((lane_q == t) | (lane_q == t - 1), MASKED_LOGIT, near)
            rows = slice(t * BLK, (t + 1) * BLK)
            qnear[rows, 0:AT_DH] = qt
            qfar[rows, 0:AT_DH] = qt
            qnear[rows, AT_DH:] = near.astype(BF16)
            qfar[rows, AT_DH:] = older.astype(BF16)
        return jnp.max(gap)

    @pl.when(ip == 0)
    def _():
        worst = prepare(0)
        for hh in range(1, HPS):
            worst = jnp.maximum(worst, prepare(hh))
        safe_sc[0] = (worst <= MOBA_SAFE_GAP).astype(jnp.int32)

    def lane_max(s):
        out = s[:, 0:LANES]
        for c in range(1, s.shape[1] // LANES):
            out = jnp.maximum(out, s[:, c * LANES:(c + 1) * LANES])
        return out

    def probs(s, m_b):
        return jnp.exp2(s - jnp.concatenate([m_b] * (W2 // LANES), axis=1)).astype(BF16)

    def plan(j):
        rows = [pl.multiple_of(b * BLK, BLK) for b in (j, NB - 1 - j)]
        n_first = jnp.maximum(j, 1) // 2
        visits = []
        for st in range(NS):
            second = st >= n_first
            jj = jnp.where(second, st - n_first, st)
            visits.append((second.astype(jnp.int32), jnp.where(second, rows[1], rows[0]),
                           pl.multiple_of(jj * W2 + BLK, BLK)))
        return rows, visits

    plans = [plan(ip * PPS + pp) for pp in range(PPS)]

    def one_pass():
        for pp, (rows, visits) in enumerate(plans):
            for hh in range(HPS):
                for slot in range(2):
                    s = (_dot_nt(qnear_sc[hh, pl.ds(rows[slot], BLK), :],
                                 kaug_sc[hh, pl.ds(rows[slot], W2), :]) + dbias_ref[hh])
                    acc_sc[pp, hh, slot] = _dot(jnp.exp2(s).astype(BF16),
                                                vaug_sc[hh, pl.ds(rows[slot], W2), :])
                for st, (slot, qrow, krow) in enumerate(visits):
                    s = _dot_nt(qfar_sc[hh, pl.ds(qrow, BLK), :],
                                kaug_sc[hh, pl.ds(krow, W2), :])
                    acc_sc[pp, hh, slot] += _dot(jnp.exp2(s).astype(BF16),
                                                 vaug_sc[hh, pl.ds(krow, W2), :])

    def two_pass():
        for pp, (rows, visits) in enumerate(plans):
            for hh in range(HPS):
                for slot in range(2):
                    near_sc[hh, slot] = (_dot_nt(qnear_sc[hh, pl.ds(rows[slot], BLK), :],
                                                 kaug_sc[hh, pl.ds(rows[slot], W2), :])
                                         + dbias_ref[hh])
                for st, (slot, qrow, krow) in enumerate(visits):
                    far_sc[hh, st] = _dot_nt(qfar_sc[hh, pl.ds(qrow, BLK), :],
                                             kaug_sc[hh, pl.ds(krow, W2), :])
            mx_sc[...] = jnp.full(mx_sc.shape, MASKED_LOGIT, F32)
            for hh in range(HPS):
                for st, (slot, qrow, krow) in enumerate(visits):
                    mx_sc[hh, slot * NS + st] = lane_max(far_sc[hh, st])
                for slot in range(2):
                    m = lane_max(near_sc[hh, slot])
                    for st in range(NS):
                        m = jnp.maximum(m, mx_sc[hh, slot * NS + st])
                    mb_sc[hh, slot] = jnp.broadcast_to(jnp.max(m, axis=-1, keepdims=True),
                                                       (BLK, LANES))
            for hh in range(HPS):
                for slot in range(2):
                    acc_sc[pp, hh, slot] = _dot(probs(near_sc[hh, slot], mb_sc[hh, slot]),
                                                vaug_sc[hh, pl.ds(rows[slot], W2), :])
                for st, (slot, qrow, krow) in enumerate(visits):
                    acc_sc[pp, hh, slot] += _dot(probs(far_sc[hh, st], mb_sc[hh, slot]),
                                                 vaug_sc[hh, pl.ds(krow, W2), :])

    lax.cond(safe_sc[0] != 0, one_pass, two_pass)

    for pp, (rows, visits) in enumerate(plans):
        for hh in range(HPS):
            for slot in range(2):
                acc = acc_sc[pp, hh, slot]
                o_ref[pl.ds(rows[slot], BLK), hh * AT_DH:(hh + 1) * AT_DH] = (
                    acc[:, 0:AT_DH] / acc[:, AT_DH:]).astype(o_ref.dtype)


def _moba(q, k, v, rel_table, qg, kg, B, S):
    T, W = q.shape
    H = AT_HEADS
    HPS = MOBA_HEADS_PER_STEP
    PPS = MOBA_PAIRS_PER_STEP
    NB = S // MOBA_BLOCK
    BLK = MOBA_BLOCK
    NS = max(NB // 2 - 1, 1)
    assert NB + 5 <= LANES and NB % (2 * PPS) == 0 and H % HPS == 0
    assert int(_t5_bucket_np(np.array([BLK + 1]))[0]) == REL_BUCKETS - 1
    dbias = _bias_tiles(rel_table)
    far = jnp.broadcast_to((rel_table[REL_BUCKETS - 1, :].astype(F32) * LOG2E)[:, None],
                           (H, LANES))
    heads = lambda: pl.BlockSpec((S, HPS * AT_DH), lambda b, g, i: (b, g))
    aug = AT_DH + LANES
    return pl.pallas_call(
        functools.partial(_moba_kernel, NB=NB, HPS=HPS, PPS=PPS),
        grid=(B, H // HPS, NB // (2 * PPS)),
        in_specs=[heads(), heads(), heads(),
                  pl.BlockSpec((HPS, BLK, 2 * BLK), lambda b, g, i: (g, 0, 0)),
                  pl.BlockSpec((H, LANES), lambda b, g, i: (0, 0)),
                  pl.BlockSpec((1, AT_DH), lambda b, g, i: (0, 0)),
                  pl.BlockSpec((1, AT_DH), lambda b, g, i: (0, 0))],
        out_specs=heads(),
        out_shape=jax.ShapeDtypeStruct((T, W), BF16),
        scratch_shapes=[pltpu.VMEM((HPS, S, aug), BF16),
                        pltpu.VMEM((HPS, S, aug), BF16),
                        pltpu.VMEM((HPS, S + BLK, aug), BF16),
                        pltpu.VMEM((HPS, S + BLK, aug), BF16),
                        pltpu.VMEM((HPS, 2, BLK, 2 * BLK), F32),
                        pltpu.VMEM((HPS, NS, BLK, 2 * BLK), F32),
                        pltpu.VMEM((HPS, 2 * NS, BLK, LANES), F32),
                        pltpu.VMEM((HPS, 2, BLK, LANES), F32),
                        pltpu.VMEM((PPS, HPS, 2, BLK, aug), F32),
                        pltpu.SMEM((1,), jnp.int32)],
        compiler_params=_cparams(("parallel", "parallel", "arbitrary")),
        name="moba",
    )(q, k, v, dbias, far, qg.reshape(1, AT_DH).astype(F32), kg.reshape(1, AT_DH).astype(F32))


def _merge_kernel(ohg_ref, oat_ref, gate_ref, x_ref, whg_ref, wat_ref, wout_ref, g2_ref,
                  x1_ref, h2_ref, *, D, sub):
    for r0 in range(0, x_ref.shape[0], sub):
        rows = slice(r0, r0 + sub)
        y_hg = _dot(ohg_ref[rows, :], whg_ref[...])
        y_at = _dot(oat_ref[rows, :], wat_ref[...])
        merged = (gate_ref[rows, 0:D].astype(F32) * y_hg
                  + gate_ref[rows, D:2 * D].astype(F32) * y_at)
        x1 = x_ref[rows, :] + _dot(merged.astype(BF16), wout_ref[...])
        x1_ref[rows, :] = x1
        ms = jnp.mean(x1 * x1, axis=-1, keepdims=True)
        h2_ref[rows, :] = (x1 * lax.rsqrt(ms + NORM_EPS) * g2_ref[...]).astype(h2_ref.dtype)


def _merge(ohg, oat, gates, x, whg, wat, wout, g2, tm=1024, sub=512):
    T, D = x.shape
    tok = lambda w: pl.BlockSpec((tm, w), lambda i: (i, 0))
    return pl.pallas_call(
        functools.partial(_merge_kernel, D=D, sub=sub),
        grid=(T // tm,),
        in_specs=[tok(ohg.shape[1]), tok(oat.shape[1]), tok(2 * D), tok(D),
                  _resident(whg), _resident(wat), _resident(wout),
                  pl.BlockSpec((1, D), lambda i: (0, 0))],
        out_specs=[tok(D), tok(D)],
        out_shape=[jax.ShapeDtypeStruct((T, D), F32), jax.ShapeDtypeStruct((T, D), BF16)],
        compiler_params=_cparams(("parallel",)),
        name="merge",
    )(ohg, oat, gates, x, whg, wat, wout, g2.reshape(1, D).astype(F32))


def _ffn_kernel(h_ref, x_ref, wg_ref, wu_ref, wd_ref, o_ref, *, tf):
    h = h_ref[...]
    acc = x_ref[...]
    for f0 in range(0, wg_ref.shape[1], tf):
        a = _dot(h, wg_ref[:, f0:f0 + tf])
        u = _dot(h, wu_ref[:, f0:f0 + tf])
        acc = acc + _dot((a * _sigmoid(a) * u).astype(BF16), wd_ref[f0:f0 + tf, :])
    o_ref[...] = acc


def _ffn(h2, x1, wg, wu, wd, tm=512, tf=2816):
    T, D = x1.shape
    FF = wg.shape[1]
    assert FF % tf == 0
    tok = lambda: pl.BlockSpec((tm, D), lambda i: (i, 0))
    return pl.pallas_call(
        functools.partial(_ffn_kernel, tf=tf),
        grid=(T // tm,),
        in_specs=[tok(), tok(), _resident(wg), _resident(wu), _resident(wd)],
        out_specs=tok(),
        out_shape=jax.ShapeDtypeStruct((T, D), F32),
        compiler_params=_cparams(("parallel",)),
        name="ffn",
    )(h2, x1, wg, wu, wd)


def kernel(x, attn_norm_g, w_in, hg_lb_gamma, hg_out_norm_g, q_norm_g, k_norm_g, rel_bias_table,
           w_branch_hg, w_branch_attn, w_out, ffn_norm_g, w_ffn_gate, w_ffn_up, w_ffn_down):
    B, S, D = x.shape
    T = B * S
    depth = attn_norm_g.shape[0]
    assert depth == 1 and S % MOBA_BLOCK == 0 and S % HG_CHUNK == 0
    WH = HG_HEADS * HG_DK
    WV = HG_HEADS * HG_DV
    WA = AT_HEADS * AT_DH
    assert w_in.shape[2] == 2 * WH + 2 * WV + 3 * WA + 2 * D

    xt = x.reshape(T, D)
    for l in range(depth):
        hq, lf, hk, hi, hg, aq, ak, av, gates = _inproj(
            xt, attn_norm_g[l], w_in[l].astype(BF16), hg_lb_gamma, q_norm_g[l], k_norm_g[l],
            WH, WV, WA)
        o_hg = _hgrn(hq, lf, hk, hi, hg, hg_out_norm_g[l], B, S)
        o_at = _moba(aq, ak, av, rel_bias_table, q_norm_g[l], k_norm_g[l], B, S)
        x1, h2 = _merge(o_hg, o_at, gates, xt, w_branch_hg[l].astype(BF16),
                        w_branch_attn[l].astype(BF16), w_out[l].astype(BF16), ffn_norm_g[l])
        xt = _ffn(h2, x1, w_ffn_gate[l].astype(BF16), w_ffn_up[l].astype(BF16),
                  w_ffn_down[l].astype(BF16))
    return xt.reshape(B, S, D)
```

```python
import functools
import math

import numpy as np
import jax
import jax.numpy as jnp
from jax import lax
from jax.experimental import pallas as pl
from jax.experimental.pallas import tpu as pltpu

F32 = jnp.float32
BF16 = jnp.bfloat16

LANES = 128
NORM_EPS = 1e-6

HG_HEADS = 8
HG_DK = 128
HG_DV = 128
HG_CHUNK = 64
HG_STEP_TOKENS = 512
HG_HEAD_GROUP = 8
AT_HEADS = 8
AT_DH = 128
MOBA_BLOCK = 256
MOBA_TOPK = 3
REL_BUCKETS = 32
REL_MAX_DIST = 128
MASKED_LOGIT = -1e30
LOG2E = math.log2(math.e)

VMEM_LIMIT = 56 * 1024 * 1024


def _cparams(sem):
    return pltpu.CompilerParams(dimension_semantics=sem, vmem_limit_bytes=VMEM_LIMIT)


def _dot(a, b):
    return jnp.dot(a, b, preferred_element_type=F32)


def _dot_nt(a, b):
    return lax.dot_general(a, b, (((1,), (1,)), ((), ())), preferred_element_type=F32)


def _dot_tn(a, b):
    return lax.dot_general(a, b, (((0,), (0,)), ((), ())), preferred_element_type=F32)


def _resident(a):
    return pl.BlockSpec(a.shape, lambda *_: (0,) * a.ndim, pipeline_mode=pl.Buffered(1))


def _sigmoid(x):
    return 0.5 * jnp.tanh(0.5 * x) + 0.5


def _head_rmsnorm(acc, gain, scale):
    cols = []
    for c in range(acc.shape[1] // LANES):
        blk = acc[:, c * LANES:(c + 1) * LANES]
        ms = jnp.mean(blk * blk, axis=-1, keepdims=True)
        cols.append(blk * lax.rsqrt(ms + NORM_EPS) * (gain * scale))
    return jnp.concatenate(cols, axis=1)


def _inproj_kernel(x_ref, g_ref, w_ref, gamma_ref, qg_ref, kg_ref,
                   hq_ref, lf_ref, hk_ref, hi_ref, hg_ref, aq_ref, ak_ref, av_ref, gate_ref,
                   *, WH, WV, WA):
    x = x_ref[...]
    ms = jnp.mean(x * x, axis=-1, keepdims=True)
    h = (x * lax.rsqrt(ms + NORM_EPS) * g_ref[...]).astype(BF16)
    D = x.shape[1]
    starts = np.cumsum([0, WH, WH, WV, WV, WA, WA, WA])

    def segment(idx, width, off=0):
        c0 = int(starts[idx]) + off
        return _dot(h, w_ref[:, c0:c0 + width])

    for half in range(2):
        gate_ref[:, half * D:(half + 1) * D] = _sigmoid(
            segment(7, D, half * D)).astype(gate_ref.dtype)

    s = _sigmoid(segment(1, WH))
    gamma = gamma_ref[...]
    eg = jnp.exp(gamma - jnp.max(gamma, axis=0, keepdims=True))
    lb = eg[0:1, :] / jnp.sum(eg, axis=0, keepdims=True)
    lf_ref[...] = jnp.log2(lb + (1.0 - lb) * s)
    hk_ref[...] = ((1.0 - lb) * (1.0 - s)).astype(hk_ref.dtype)

    g = segment(3, WV)
    hg_ref[...] = (g * _sigmoid(g)).astype(hg_ref.dtype)
    aq_ref[...] = _head_rmsnorm(segment(4, WA), qg_ref[...],
                                AT_DH ** -0.5 * LOG2E).astype(aq_ref.dtype)
    ak_ref[...] = _head_rmsnorm(segment(5, WA), kg_ref[...], 1.0).astype(ak_ref.dtype)
    hq_ref[...] = (segment(0, WH) * HG_DK ** -0.5).astype(hq_ref.dtype)
    hi_ref[...] = segment(2, WV).astype(hi_ref.dtype)
    av_ref[...] = segment(6, WA).astype(av_ref.dtype)


def _inproj(x, g, w, gamma, qg, kg, WH, WV, WA, tm=512):
    T, D = x.shape
    tok = lambda width: pl.BlockSpec((tm, width), lambda i: (i, 0))
    small = lambda a: pl.BlockSpec(a.shape, lambda i: (0, 0))
    widths = [WH, WH, WH, WV, WV, WA, WA, WA, 2 * D]
    dtypes = [BF16, F32, BF16, BF16, BF16, BF16, BF16, BF16, BF16]
    g = g.reshape(1, D).astype(F32)
    gamma = gamma.astype(F32)
    qg = qg.reshape(1, AT_DH).astype(F32)
    kg = kg.reshape(1, AT_DH).astype(F32)
    return pl.pallas_call(
        functools.partial(_inproj_kernel, WH=WH, WV=WV, WA=WA),
        grid=(T // tm,),
        in_specs=[tok(D), small(g), _resident(w), small(gamma), small(qg), small(kg)],
        out_specs=[tok(wd) for wd in widths],
        out_shape=[jax.ShapeDtypeStruct((T, wd), dt) for wd, dt in zip(widths, dtypes)],
        compiler_params=_cparams(("parallel",)),
        name="inproj",
    )(x, g, w, gamma, qg, kg)


def _hgrn_tables(C):
    nl = int(math.log2(C))
    assert 1 << nl == C and nl >= 3
    masks = np.zeros((nl + 1, C, C), np.float32)
    masks[0] = np.eye(C)
    t = np.arange(C)
    for L in range(1, nl + 1):
        blk, half = 1 << L, 1 << (L - 1)
        base = (t // blk) * blk
        upper = (t - base) >= half
        same = base[:, None] == base[None, :]
        masks[L] = (same & upper[:, None] & (~upper)[None, :]).astype(np.float32)
    return masks, nl


def _hgrn_kernel(q_ref, lf_ref, k_ref, v_ref, g_ref, gain_ref, m_ref, o_ref,
                 e_sc, d_sc, st_sc, *, C, nl, heads, chunks):
    @pl.when(pl.program_id(1) == 0)
    def _():
        st_sc[...] = jnp.zeros_like(st_sc)

    gain = gain_ref[...]
    scan_shifts = [1 << j for j in range(nl)]
    row = lax.broadcasted_iota(jnp.int32, (C, lf_ref.shape[1]), 0)
    odd = (row & 1) == 1
    ph = row & 3

    def chunk_rows(c):
        return pl.ds(pl.multiple_of(c * C, C), C)

    def prepare(c, buf):
        lf = lf_ref[chunk_rows(c), :]
        b = lf
        for sh in scan_shifts:
            b = b + jnp.where(row >= sh, pltpu.roll(b, sh, axis=0), 0.0)
        e_sc[buf, 0:C, :] = jnp.exp2(jnp.where(odd, lf, 0.0)).astype(BF16)
        prev = pltpu.roll(lf, 1, axis=0)
        nxt = pltpu.roll(lf, C - 1, axis=0)
        x2 = jnp.where(ph == 0, nxt, jnp.where(ph == 1, 0.0, jnp.where(ph == 2, lf, lf + prev)))
        e_sc[buf, C:2 * C, :] = jnp.exp2(x2).astype(BF16)
        for L in range(3, nl + 1):
            blk, half = 1 << L, 1 << (L - 1)
            pieces = [b[base:base + blk, :] - b[base + half - 1:base + half, :]
                      for base in range(0, C, blk)]
            d = pieces[0] if len(pieces) == 1 else jnp.concatenate(pieces, axis=0)
            e_sc[buf, (L - 1) * C:L * C, :] = jnp.exp2(-jnp.abs(d)).astype(BF16)
        b_last = b[C - 1:C, :]
        e_sc[buf, nl * C:(nl + 1) * C, :] = jnp.exp2(b).astype(BF16)
        e_sc[buf, (nl + 1) * C:(nl + 2) * C, :] = jnp.exp2(b_last - b).astype(BF16)
        d_sc[buf] = jnp.exp2(b_last)

    def block_diag(x):
        n, d2 = x.shape
        z = jnp.zeros((n, d2 // 2), x.dtype)
        return jnp.concatenate([jnp.concatenate([x[:, :d2 // 2], z], axis=1),
                                jnp.concatenate([z, x[:, d2 // 2:]], axis=1)], axis=0)

    W2 = 2 * HG_DK
    pair_mask = [jnp.concatenate([m_ref[L], m_ref[L]], axis=1) for L in range(nl + 1)]

    def heads_of(c, buf):
        rows = chunk_rows(c)
        pairs = [slice(p * W2, (p + 1) * W2) for p in range(heads // 2)]
        qb = [q_ref[rows, sl] for sl in pairs]
        kb = [k_ref[rows, sl] for sl in pairs]
        vb = [v_ref[rows, sl] for sl in pairs]
        A = [pair_mask[0] * _dot_nt(q, block_diag(k)) for q, k in zip(qb, kb)]
        for L in range(1, nl + 1):
            for p, sl in enumerate(pairs):
                eL = e_sc[buf, (L - 1) * C:L * C, sl]
                A[p] = A[p] + pair_mask[L] * _dot_nt(qb[p] * eL, block_diag(kb[p] * eL))
        for p, sl in enumerate(pairs):
            eb = e_sc[buf, nl * C:(nl + 1) * C, sl]
            ek = e_sc[buf, (nl + 1) * C:(nl + 2) * C, sl]
            st = [st_sc[2 * p], st_sc[2 * p + 1]]
            st_pair = jnp.concatenate([s.astype(BF16) for s in st], axis=1)
            o = (_dot(A[p].astype(BF16), block_diag(vb[p]))
                 + _dot_nt(qb[p] * eb, block_diag(st_pair)))
            upd = _dot_tn(vb[p], kb[p] * ek)
            for j in range(2):
                h = 2 * p + j
                hs = slice(h * HG_DK, (h + 1) * HG_DK)
                js = slice(j * HG_DK, (j + 1) * HG_DK)
                st_sc[h] = st[j] * d_sc[buf, :, hs] + upd[js, js]
                oj = o[:, js]
                ms = jnp.mean(oj * oj, axis=-1, keepdims=True)
                y = oj * lax.rsqrt(ms + NORM_EPS) * gain * g_ref[rows, hs].astype(F32)
                o_ref[rows, hs] = y.astype(o_ref.dtype)

    prepare(0, 0)

    def step(c, carry):
        prepare(jnp.minimum(c + 1, chunks - 1), (c + 1) & 1)
        heads_of(c, c & 1)
        return carry

    lax.fori_loop(0, chunks, step, 0)


def _hgrn(q, lf, k, v, g, gain, B, S):
    T, W = q.shape
    C = HG_CHUNK
    TS = HG_STEP_TOKENS
    assert S % TS == 0 and TS % C == 0
    N = S // TS
    masks, nl = _hgrn_tables(C)
    row = lambda b, c: (b * N + c, 0)
    tok = lambda: pl.BlockSpec((TS, W), row)
    return pl.pallas_call(
        functools.partial(_hgrn_kernel, C=C, nl=nl, heads=HG_HEADS, chunks=TS // C),
        grid=(B, N),
        in_specs=[tok(), tok(), tok(), tok(), tok(),
                  pl.BlockSpec((1, HG_DV), lambda b, c: (0, 0)),
                  pl.BlockSpec(masks.shape, lambda b, c: (0, 0, 0))],
        out_specs=tok(),
        out_shape=jax.ShapeDtypeStruct((T, W), BF16),
        scratch_shapes=[pltpu.VMEM((2, (nl + 2) * C, W), BF16),
                        pltpu.VMEM((2, 1, W), F32),
                        pltpu.VMEM((HG_HEADS, HG_DV, HG_DK), F32)],
        compiler_params=_cparams(("parallel", "arbitrary")),
        name="hgrn2",
    )(q, lf, k, v, g, gain.reshape(1, HG_DV).astype(F32), jnp.asarray(masks, F32))


def _t5_bucket_np(dist):
    n = np.maximum(dist, 0)
    max_exact = REL_BUCKETS // 2
    nf = np.maximum(n, 1).astype(np.float32)
    large = max_exact + (np.log(nf / max_exact) / math.log(REL_MAX_DIST / max_exact)
                         * (REL_BUCKETS - max_exact)).astype(np.int32)
    large = np.minimum(large, REL_BUCKETS - 1)
    return np.where(n < max_exact, n, large).astype(np.int32)


def _bias_kernel(tab_ref, bucket_ref, o_ref):
    h = pl.program_id(0)
    bucket = bucket_ref[...]
    acc = jnp.zeros(bucket.shape, F32)
    for b in range(REL_BUCKETS):
        acc = jnp.where(bucket == b, tab_ref[b, h], acc)
    delta = (acc - tab_ref[REL_BUCKETS - 1, h]) * LOG2E
    blk = bucket.shape[0]
    r = lax.broadcasted_iota(jnp.int32, bucket.shape, 0)
    c = lax.broadcasted_iota(jnp.int32, bucket.shape, 1)
    o_ref[0] = jnp.where(c - blk <= r, delta, MASKED_LOGIT)


def _bias_tiles(rel_table):
    BLK = MOBA_BLOCK
    t = np.arange(BLK)[:, None]
    s = np.arange(2 * BLK)[None, :]
    bucket = _t5_bucket_np(t + BLK - s)
    H = rel_table.shape[1]
    return pl.pallas_call(
        _bias_kernel,
        grid=(H,),
        in_specs=[pl.BlockSpec(memory_space=pltpu.SMEM),
                  pl.BlockSpec(bucket.shape, lambda h: (0, 0))],
        out_specs=pl.BlockSpec((1, BLK, 2 * BLK), lambda h: (h, 0, 0)),
        out_shape=jax.ShapeDtypeStruct((H, BLK, 2 * BLK), F32),
        compiler_params=_cparams(("arbitrary",)),
        name="relbias",
    )(rel_table.astype(F32), jnp.asarray(bucket))


MOBA_HEADS_PER_STEP = 2
MOBA_PAIRS_PER_STEP = 2
MOBA_BOUND_SLACK = 1.0 + 2.0 ** -6
MOBA_SAFE_GAP = 100.0


def _moba_kernel(q_ref, k_ref, v_ref, dbias_ref, far_ref, qg_ref, kg_ref, o_ref,
                 qnear_sc, qfar_sc, kaug_sc, vaug_sc, near_sc, far_sc, mx_sc, mb_sc, acc_sc,
                 safe_sc, *, NB, HPS, PPS):
    BLK = MOBA_BLOCK
    W2 = 2 * BLK
    PAD = NB + 2
    SHIFT = NB + 3
    NS = NB // 2 - 1
    hg = pl.program_id(1)
    ip = pl.program_id(2)

    def prepare(hh):
        cols = slice(hh * AT_DH, (hh + 1) * AT_DH)
        kaug, vaug, qnear, qfar = kaug_sc.at[hh], vaug_sc.at[hh], qnear_sc.at[hh], qfar_sc.at[hh]
        far = far_ref[pl.ds(hg * HPS + hh, 1), :]
        far_hi = far.astype(BF16).astype(F32)
        far_lo = far - far_hi
        lane = lax.broadcasted_iota(jnp.int32, (1, LANES), 1)
        tail = jnp.where(lane == NB, far_hi, jnp.where(lane == NB + 1, far_lo, 0.0))
        tail = jnp.where((lane == SHIFT) | (lane == SHIFT + 1), 1.0, tail)
        kaug[0:BLK, 0:AT_DH] = jnp.zeros((BLK, AT_DH), BF16)
        kaug[0:BLK, AT_DH:] = jnp.broadcast_to(
            jnp.where(lane == PAD, 1.0, 0.0), (BLK, LANES)).astype(BF16)
        vaug[0:BLK, :] = jnp.zeros((BLK, AT_DH + LANES), BF16)
        kms = []
        ones8 = jnp.ones((8, BLK), BF16)
        for n in range(NB):
            kb = k_ref[n * BLK:(n + 1) * BLK, cols]
            kms.append(_dot(ones8, kb)[0:1, :] * (1.0 / BLK))
            kaug[(n + 1) * BLK:(n + 2) * BLK, 0:AT_DH] = kb
            kaug[(n + 1) * BLK:(n + 2) * BLK, AT_DH:] = jnp.broadcast_to(
                jnp.where(lane == n, 1.0, tail), (BLK, LANES)).astype(BF16)
        vaug[BLK:, 0:AT_DH] = v_ref[:, cols]
        vaug[BLK:, AT_DH:] = jnp.ones((NB * BLK, LANES), BF16)

        qk_bound = (AT_DH * jnp.max(jnp.abs(qg_ref[...]), keepdims=True)
                    * jnp.max(jnp.abs(kg_ref[...]), keepdims=True)
                    * (AT_DH ** -0.5 * LOG2E * MOBA_BOUND_SLACK))
        bias_max = jnp.max(dbias_ref[hh], keepdims=True)[0:1, 0:1] + far[:, 0:1]
        bias_self = dbias_ref[hh][0:1, BLK:BLK + 1] + far[:, 0:1]
        shift = qk_bound + bias_max
        gap = shift + qk_bound - bias_self
        shift_hi = shift.astype(BF16).astype(F32)

        kmean = jnp.concatenate(kms, axis=0)
        km_hi = kmean.astype(BF16)
        km_lo = (kmean - km_hi.astype(F32)).astype(BF16)
        blk = lax.broadcasted_iota(jnp.int32, (NB, BLK), 0)
        rowid = lax.broadcasted_iota(jnp.int32, (LANES - NB, BLK), 0) + NB
        rest = jnp.where(rowid < NB + 2, 1.0, jnp.where(rowid == PAD, MASKED_LOGIT, 0.0))
        rest = jnp.where(rowid == SHIFT, -shift_hi,
                         jnp.where(rowid == SHIFT + 1, shift_hi - shift, rest))
        lane_q = lax.broadcasted_iota(jnp.int32, (BLK, LANES), 1)
        for t in range(NB):
            qt = q_ref[t * BLK:(t + 1) * BLK, cols]
            gate = jnp.where(blk < t, _dot_nt(km_hi, qt) + _dot_nt(km_lo, qt), -jnp.inf)
            rank = jnp.zeros((NB, BLK), jnp.int32)
            for m in range(t):
                gm = gate[m:m + 1, :]
                rank = rank + ((gm > gate) | ((gm == gate) & (blk > m))).astype(jnp.int32)
            sel = ((blk < t) & (rank < MOBA_TOPK)) | (blk == t)
            near = jnp.concatenate([jnp.where(sel, 0.0, MASKED_LOGIT), rest], axis=0).T
            older = jnp.where((lane_q == t) | (lane_q == t - 1), MASKED_LOGIT, near)
            rows = slice(t * BLK, (t + 1) * BLK)
            qnear[rows, 0:AT_DH] = qt
            qfar[rows, 0:AT_DH] = qt
            qnear[rows, AT_DH:] = near.astype(BF16)
            qfar[rows, AT_DH:] = older.astype(BF16)
        return jnp.max(gap)

    @pl.when(ip == 0)
    def _():
        worst = prepare(0)
        for hh in range(1, HPS):
            worst = jnp.maximum(worst, prepare(hh))
        safe_sc[0] = (worst <= MOBA_SAFE_GAP).astype(jnp.int32)

    def lane_max(s):
        out = s[:, 0:LANES]
        for c in range(1, s.shape[1] // LANES):
            out = jnp.maximum(out, s[:, c * LANES:(c + 1) * LANES])
        return out

    def probs(s, m_b):
        return jnp.exp2(s - jnp.concatenate([m_b] * (W2 // LANES), axis=1)).astype(BF16)

    def plan(j):
        rows = [pl.multiple_of(b * BLK, BLK) for b in (j, NB - 1 - j)]
        n_first = jnp.maximum(j, 1) // 2
        visits = []
        for st in range(NS):
            second = st >= n_first
            jj = jnp.where(second, st - n_first, st)
            visits.append((second.astype(jnp.int32), jnp.where(second, rows[1], rows[0]),
                           pl.multiple_of(jj * W2 + BLK, BLK)))
        return rows, visits

    plans = [plan(ip * PPS + pp) for pp in range(PPS)]

    def one_pass():
        for pp, (rows, visits) in enumerate(plans):
            for hh in range(HPS):
                for slot in range(2):
                    s = (_dot_nt(qnear_sc[hh, pl.ds(rows[slot], BLK), :],
                                 kaug_sc[hh, pl.ds(rows[slot], W2), :]) + dbias_ref[hh])
                    acc_sc[pp, hh, slot] = _dot(jnp.exp2(s).astype(BF16),
                                                vaug_sc[hh, pl.ds(rows[slot], W2), :])
                for st, (slot, qrow, krow) in enumerate(visits):
                    s = _dot_nt(qfar_sc[hh, pl.ds(qrow, BLK), :],
                                kaug_sc[hh, pl.ds(krow, W2), :])
                    acc_sc[pp, hh, slot] += _dot(jnp.exp2(s).astype(BF16),
                                                 vaug_sc[hh, pl.ds(krow, W2), :])

    def two_pass():
        for pp, (rows, visits) in enumerate(plans):
            for hh in range(HPS):
                for slot in range(2):
                    near_sc[hh, slot] = (_dot_nt(qnear_sc[hh, pl.ds(rows[slot], BLK), :],
                                                 kaug_sc[hh, pl.ds(rows[slot], W2), :])
                                         + dbias_ref[hh])
                for st, (slot, qrow, krow) in enumerate(visits):
                    far_sc[hh, st] = _dot_nt(qfar_sc[hh, pl.ds(qrow, BLK), :],
                                             kaug_sc[hh, pl.ds(krow, W2), :])
            mx_sc[...] = jnp.full(mx_sc.shape, MASKED_LOGIT, F32)
            for hh in range(HPS):
                for st, (slot, qrow, krow) in enumerate(visits):
                    mx_sc[hh, slot * NS + st] = lane_max(far_sc[hh, st])
                for slot in range(2):
                    m = lane_max(near_sc[hh, slot])
                    for st in range(NS):
                        m = jnp.maximum(m, mx_sc[hh, slot * NS + st])
                    mb_sc[hh, slot] = jnp.broadcast_to(jnp.max(m, axis=-1, keepdims=True),
                                                       (BLK, LANES))
            for hh in range(HPS):
                for slot in range(2):
                    acc_sc[pp, hh, slot] = _dot(probs(near_sc[hh, slot], mb_sc[hh, slot]),
                                                vaug_sc[hh, pl.ds(rows[slot], W2), :])
                for st, (slot, qrow, krow) in enumerate(visits):
                    acc_sc[pp, hh, slot] += _dot(probs(far_sc[hh, st], mb_sc[hh, slot]),
                                                 vaug_sc[hh, pl.ds(krow, W2), :])

    lax.cond(safe_sc[0] != 0, one_pass, two_pass)

    for pp, (rows, visits) in enumerate(plans):
        for hh in range(HPS):
            for slot in range(2):
                acc = acc_sc[pp, hh, slot]
                o_ref[pl.ds(rows[slot], BLK), hh * AT_DH:(hh + 1) * AT_DH] = (
                    acc[:, 0:AT_DH] / acc[:, AT_DH:]).astype(o_ref.dtype)


def _moba(q, k, v, rel_table, qg, kg, B, S):
    T, W = q.shape
    H = AT_HEADS
    HPS = MOBA_HEADS_PER_STEP
    PPS = MOBA_PAIRS_PER_STEP
    NB = S // MOBA_BLOCK
    BLK = MOBA_BLOCK
    NS = max(NB // 2 - 1, 1)
    assert NB + 5 <= LANES and NB % (2 * PPS) == 0 and H % HPS == 0
    assert int(_t5_bucket_np(np.array([BLK + 1]))[0]) == REL_BUCKETS - 1
    dbias = _bias_tiles(rel_table)
    far = jnp.broadcast_to((rel_table[REL_BUCKETS - 1, :].astype(F32) * LOG2E)[:, None],
                           (H, LANES))
    heads = lambda: pl.BlockSpec((S, HPS * AT_DH), lambda b, g, i: (b, g))
    aug = AT_DH + LANES
    return pl.pallas_call(
        functools.partial(_moba_kernel, NB=NB, HPS=HPS, PPS=PPS),
        grid=(B, H // HPS, NB // (2 * PPS)),
        in_specs=[heads(), heads(), heads(),
                  pl.BlockSpec((HPS, BLK, 2 * BLK), lambda b, g, i: (g, 0, 0)),
                  pl.BlockSpec((H, LANES), lambda b, g, i: (0, 0)),
                  pl.BlockSpec((1, AT_DH), lambda b, g, i: (0, 0)),
                  pl.BlockSpec((1, AT_DH), lambda b, g, i: (0, 0))],
        out_specs=heads(),
        out_shape=jax.ShapeDtypeStruct((T, W), BF16),
        scratch_shapes=[pltpu.VMEM((HPS, S, aug), BF16),
                        pltpu.VMEM((HPS, S, aug), BF16),
                        pltpu.VMEM((HPS, S + BLK, aug), BF16),
                        pltpu.VMEM((HPS, S + BLK, aug), BF16),
                        pltpu.VMEM((HPS, 2, BLK, 2 * BLK), F32),
                        pltpu.VMEM((HPS, NS, BLK, 2 * BLK), F32),
                        pltpu.VMEM((HPS, 2 * NS, BLK, LANES), F32),
                        pltpu.VMEM((HPS, 2, BLK, LANES), F32),
                        pltpu.VMEM((PPS, HPS, 2, BLK, aug), F32),
                        pltpu.SMEM((1,), jnp.int32)],
        compiler_params=_cparams(("parallel", "parallel", "arbitrary")),
        name="moba",
    )(q, k, v, dbias, far, qg.reshape(1, AT_DH).astype(F32), kg.reshape(1, AT_DH).astype(F32))


def _merge_kernel(ohg_ref, oat_ref, gate_ref, x_ref, whg_ref, wat_ref, wout_ref, g2_ref,
                  x1_ref, h2_ref, *, D, sub):
    for r0 in range(0, x_ref.shape[0], sub):
        rows = slice(r0, r0 + sub)
        y_hg = _dot(ohg_ref[rows, :], whg_ref[...])
        y_at = _dot(oat_ref[rows, :], wat_ref[...])
        merged = (gate_ref[rows, 0:D].astype(F32) * y_hg
                  + gate_ref[rows, D:2 * D].astype(F32) * y_at)
        x1 = x_ref[rows, :] + _dot(merged.astype(BF16), wout_ref[...])
        x1_ref[rows, :] = x1
        ms = jnp.mean(x1 * x1, axis=-1, keepdims=True)
        h2_ref[rows, :] = (x1 * lax.rsqrt(ms + NORM_EPS) * g2_ref[...]).astype(h2_ref.dtype)


def _merge(ohg, oat, gates, x, whg, wat, wout, g2, tm=1024, sub=512):
    T, D = x.shape
    tok = lambda w: pl.BlockSpec((tm, w), lambda i: (i, 0))
    return pl.pallas_call(
        functools.partial(_merge_kernel, D=D, sub=sub),
        grid=(T // tm,),
        in_specs=[tok(ohg.shape[1]), tok(oat.shape[1]), tok(2 * D), tok(D),
                  _resident(whg), _resident(wat), _resident(wout),
                  pl.BlockSpec((1, D), lambda i: (0, 0))],
        out_specs=[tok(D), tok(D)],
        out_shape=[jax.ShapeDtypeStruct((T, D), F32), jax.ShapeDtypeStruct((T, D), BF16)],
        compiler_params=_cparams(("parallel",)),
        name="merge",
    )(ohg, oat, gates, x, whg, wat, wout, g2.reshape(1, D).astype(F32))


WEIGHT_CHUNKS = 8


def _load_weight_bf16(w_hbm, w_sc, stage, sem):
    rows = w_hbm.shape[0] // WEIGHT_CHUNKS

    def copy(c):
        return pltpu.make_async_copy(w_hbm.at[pl.ds(c * rows, rows), :], stage.at[c % 2],
                                     sem.at[c % 2])

    copy(0).start()
    for c in range(WEIGHT_CHUNKS):
        if c + 1 < WEIGHT_CHUNKS:
            copy(c + 1).start()
        copy(c).wait()
        w_sc[c * rows:(c + 1) * rows, :] = stage[c % 2].astype(w_sc.dtype)


def _ffn_kernel(h_ref, x_ref, wg_hbm, wu_hbm, wd_hbm, o_ref,
                wg_sc, wu_sc, wd_sc, stage_in, stage_out, sem, *, tf):
    @pl.when(pl.program_id(0) == 0)
    def _():
        _load_weight_bf16(wg_hbm, wg_sc, stage_in, sem)
        _load_weight_bf16(wu_hbm, wu_sc, stage_in, sem)
        _load_weight_bf16(wd_hbm, wd_sc, stage_out, sem)

    h = h_ref[...]
    acc = x_ref[...]
    for f0 in range(0, wg_sc.shape[1], tf):
        a = _dot(h, wg_sc[:, f0:f0 + tf])
        u = _dot(h, wu_sc[:, f0:f0 + tf])
        acc = acc + _dot((a * _sigmoid(a) * u).astype(BF16), wd_sc[f0:f0 + tf, :])
    o_ref[...] = acc


def _ffn(h2, x1, wg, wu, wd, tm=512, tf=2816):
    T, D = x1.shape
    FF = wg.shape[1]
    assert FF % tf == 0 and D % (16 * WEIGHT_CHUNKS) == 0 and FF % (16 * WEIGHT_CHUNKS) == 0
    tok = lambda: pl.BlockSpec((tm, D), lambda i: (i, 0))
    hbm = lambda: pl.BlockSpec(memory_space=pl.ANY)
    return pl.pallas_call(
        functools.partial(_ffn_kernel, tf=tf),
        grid=(T // tm,),
        in_specs=[tok(), tok(), hbm(), hbm(), hbm()],
        out_specs=tok(),
        out_shape=jax.ShapeDtypeStruct((T, D), F32),
        scratch_shapes=[pltpu.VMEM((D, FF), BF16), pltpu.VMEM((D, FF), BF16),
                        pltpu.VMEM((FF, D), BF16),
                        pltpu.VMEM((2, D // WEIGHT_CHUNKS, FF), F32),
                        pltpu.VMEM((2, FF // WEIGHT_CHUNKS, D), F32),
                        pltpu.SemaphoreType.DMA((2,))],
        compiler_params=_cparams(("arbitrary",)),
        name="ffn",
    )(h2, x1, wg, wu, wd)


def kernel(x, attn_norm_g, w_in, hg_lb_gamma, hg_out_norm_g, q_norm_g, k_norm_g, rel_bias_table,
           w_branch_hg, w_branch_attn, w_out, ffn_norm_g, w_ffn_gate, w_ffn_up, w_ffn_down):
    B, S, D = x.shape
    T = B * S
    depth = attn_norm_g.shape[0]
    assert depth == 1 and S % MOBA_BLOCK == 0 and S % HG_CHUNK == 0
    WH = HG_HEADS * HG_DK
    WV = HG_HEADS * HG_DV
    WA = AT_HEADS * AT_DH
    assert w_in.shape[2] == 2 * WH + 2 * WV + 3 * WA + 2 * D

    xt = x.reshape(T, D)
    for l in range(depth):
        hq, lf, hk, hi, hg, aq, ak, av, gates = _inproj(
            xt, attn_norm_g[l], w_in[l].astype(BF16), hg_lb_gamma, q_norm_g[l], k_norm_g[l],
            WH, WV, WA)
        o_hg = _hgrn(hq, lf, hk, hi, hg, hg_out_norm_g[l], B, S)
        o_at = _moba(aq, ak, av, rel_bias_table, q_norm_g[l], k_norm_g[l], B, S)
        x1, h2 = _merge(o_hg, o_at, gates, xt, w_branch_hg[l].astype(BF16),
                        w_branch_attn[l].astype(BF16), w_out[l].astype(BF16), ffn_norm_g[l])
        xt = _ffn(h2, x1, w_ffn_gate[l].astype(F32), w_ffn_up[l].astype(F32),
                  w_ffn_down[l].astype(F32))
    return xt.reshape(B, S, D)
```

```python
import functools
import math

import numpy as np
import jax
import jax.numpy as jnp
from jax import lax
from jax.experimental import pallas as pl
from jax.experimental.pallas import tpu as pltpu

F32 = jnp.float32
BF16 = jnp.bfloat16

LANES = 128
NORM_EPS = 1e-6

HG_HEADS = 8
HG_DK = 128
HG_DV = 128
HG_CHUNK = 64
HG_STEP_TOKENS = 512
HG_HEAD_GROUP = 8
AT_HEADS = 8
AT_DH = 128
MOBA_BLOCK = 256
MOBA_TOPK = 3
REL_BUCKETS = 32
REL_MAX_DIST = 128
MASKED_LOGIT = -1e30
LOG2E = math.log2(math.e)

VMEM_LIMIT = 56 * 1024 * 1024


def _cparams(sem):
    return pltpu.CompilerParams(dimension_semantics=sem, vmem_limit_bytes=VMEM_LIMIT)


def _dot(a, b):
    return jnp.dot(a, b, preferred_element_type=F32)


def _dot_nt(a, b):
    return lax.dot_general(a, b, (((1,), (1,)), ((), ())), preferred_element_type=F32)


def _dot_tn(a, b):
    return lax.dot_general(a, b, (((0,), (0,)), ((), ())), preferred_element_type=F32)


def _resident(a):
    return pl.BlockSpec(a.shape, lambda *_: (0,) * a.ndim, pipeline_mode=pl.Buffered(1))


def _sigmoid(x):
    return 0.5 * jnp.tanh(0.5 * x) + 0.5


def _silu(x):
    h = 0.5 * x
    return h + h * jnp.tanh(h)


def _head_rmsnorm(acc, gain, scale):
    cols = []
    for c in range(acc.shape[1] // LANES):
        blk = acc[:, c * LANES:(c + 1) * LANES]
        ms = jnp.mean(blk * blk, axis=-1, keepdims=True)
        cols.append(blk * lax.rsqrt(ms + NORM_EPS) * (gain * scale))
    return jnp.concatenate(cols, axis=1)


def _inproj_kernel(x_ref, g_ref, w_ref, gamma_ref, qg_ref, kg_ref,
                   hq_ref, lf_ref, hk_ref, hi_ref, hg_ref, aq_ref, ak_ref, av_ref, gate_ref,
                   *, WH, WV, WA):
    x = x_ref[...]
    ms = jnp.mean(x * x, axis=-1, keepdims=True)
    h = (x * lax.rsqrt(ms + NORM_EPS) * g_ref[...]).astype(BF16)
    D = x.shape[1]
    starts = np.cumsum([0, WH, WH, WV, WV, WA, WA, WA])

    def segment(idx, width, off=0):
        c0 = int(starts[idx]) + off
        return _dot(h, w_ref[:, c0:c0 + width])

    for half in range(2):
        gate_ref[:, half * D:(half + 1) * D] = _sigmoid(
            segment(7, D, half * D)).astype(gate_ref.dtype)

    gamma = gamma_ref[...]
    eg = jnp.exp(gamma - jnp.max(gamma, axis=0, keepdims=True))
    lb = eg[0:1, :] / jnp.sum(eg, axis=0, keepdims=True)
    c0 = 0.5 + 0.5 * lb
    c1 = 0.5 - 0.5 * lb
    ct = c1 * jnp.tanh(0.5 * segment(1, WH))
    lf_ref[...] = jnp.log2(c0 + ct)
    hk_ref[...] = (c1 - ct).astype(hk_ref.dtype)

    hg_ref[...] = _silu(segment(3, WV)).astype(hg_ref.dtype)
    aq_ref[...] = _head_rmsnorm(segment(4, WA), qg_ref[...],
                                AT_DH ** -0.5 * LOG2E).astype(aq_ref.dtype)
    ak_ref[...] = _head_rmsnorm(segment(5, WA), kg_ref[...], 1.0).astype(ak_ref.dtype)
    hq_ref[...] = (segment(0, WH) * HG_DK ** -0.5).astype(hq_ref.dtype)
    hi_ref[...] = segment(2, WV).astype(hi_ref.dtype)
    av_ref[...] = segment(6, WA).astype(av_ref.dtype)


def _inproj(x, g, w, gamma, qg, kg, WH, WV, WA, tm=512):
    T, D = x.shape
    tok = lambda width: pl.BlockSpec((tm, width), lambda i: (i, 0))
    small = lambda a: pl.BlockSpec(a.shape, lambda i: (0, 0))
    widths = [WH, WH, WH, WV, WV, WA, WA, WA, 2 * D]
    dtypes = [BF16, F32, BF16, BF16, BF16, BF16, BF16, BF16, BF16]
    g = g.reshape(1, D).astype(F32)
    gamma = gamma.astype(F32)
    qg = qg.reshape(1, AT_DH).astype(F32)
    kg = kg.reshape(1, AT_DH).astype(F32)
    return pl.pallas_call(
        functools.partial(_inproj_kernel, WH=WH, WV=WV, WA=WA),
        grid=(T // tm,),
        in_specs=[tok(D), small(g), _resident(w), small(gamma), small(qg), small(kg)],
        out_specs=[tok(wd) for wd in widths],
        out_shape=[jax.ShapeDtypeStruct((T, wd), dt) for wd, dt in zip(widths, dtypes)],
        compiler_params=_cparams(("parallel",)),
        name="inproj",
    )(x, g, w, gamma, qg, kg)


def _hgrn_tables(C):
    nl = int(math.log2(C))
    assert 1 << nl == C and nl >= 3
    masks = np.zeros((nl + 1, C, C), np.float32)
    masks[0] = np.eye(C)
    t = np.arange(C)
    for L in range(1, nl + 1):
        blk, half = 1 << L, 1 << (L - 1)
        base = (t // blk) * blk
        upper = (t - base) >= half
        same = base[:, None] == base[None, :]
        masks[L] = (same & upper[:, None] & (~upper)[None, :]).astype(np.float32)
    return masks, nl


def _hgrn_kernel(q_ref, lf_ref, k_ref, v_ref, g_ref, gain_ref, m_ref, o_ref,
                 e_sc, d_sc, st_sc, *, C, nl, heads, chunks):
    @pl.when(pl.program_id(1) == 0)
    def _():
        st_sc[...] = jnp.zeros_like(st_sc)

    gain = gain_ref[...]
    scan_shifts = [1 << j for j in range(nl)]
    row = lax.broadcasted_iota(jnp.int32, (C, lf_ref.shape[1]), 0)
    odd = (row & 1) == 1
    ph = row & 3

    def chunk_rows(c):
        return pl.ds(pl.multiple_of(c * C, C), C)

    def prepare(c, buf):
        lf = lf_ref[chunk_rows(c), :]
        b = lf
        for sh in scan_shifts:
            b = b + jnp.where(row >= sh, pltpu.roll(b, sh, axis=0), 0.0)
        e_sc[buf, 0:C, :] = jnp.exp2(jnp.where(odd, lf, 0.0)).astype(BF16)
        prev = pltpu.roll(lf, 1, axis=0)
        nxt = pltpu.roll(lf, C - 1, axis=0)
        x2 = jnp.where(ph == 0, nxt, jnp.where(ph == 1, 0.0, jnp.where(ph == 2, lf, lf + prev)))
        e_sc[buf, C:2 * C, :] = jnp.exp2(x2).astype(BF16)
        for L in range(3, nl + 1):
            blk, half = 1 << L, 1 << (L - 1)
            pieces = [b[base:base + blk, :] - b[base + half - 1:base + half, :]
                      for base in range(0, C, blk)]
            d = pieces[0] if len(pieces) == 1 else jnp.concatenate(pieces, axis=0)
            e_sc[buf, (L - 1) * C:L * C, :] = jnp.exp2(-jnp.abs(d)).astype(BF16)
        b_last = b[C - 1:C, :]
        e_sc[buf, nl * C:(nl + 1) * C, :] = jnp.exp2(b).astype(BF16)
        e_sc[buf, (nl + 1) * C:(nl + 2) * C, :] = jnp.exp2(b_last - b).astype(BF16)
        d_sc[buf] = jnp.exp2(b_last)

    def block_diag(x):
        n, d2 = x.shape
        z = jnp.zeros((n, d2 // 2), x.dtype)
        return jnp.concatenate([jnp.concatenate([x[:, :d2 // 2], z], axis=1),
                                jnp.concatenate([z, x[:, d2 // 2:]], axis=1)], axis=0)

    W2 = 2 * HG_DK
    pair_mask = [jnp.concatenate([m_ref[L], m_ref[L]], axis=1) for L in range(nl + 1)]

    def heads_of(c, buf):
        rows = chunk_rows(c)
        pairs = [slice(p * W2, (p + 1) * W2) for p in range(heads // 2)]
        qb = [q_ref[rows, sl] for sl in pairs]
        kb = [k_ref[rows, sl] for sl in pairs]
        vb = [v_ref[rows, sl] for sl in pairs]
        A = [pair_mask[0] * _dot_nt(q, block_diag(k)) for q, k in zip(qb, kb)]
        for L in range(1, nl + 1):
            for p, sl in enumerate(pairs):
                eL = e_sc[buf, (L - 1) * C:L * C, sl]
                A[p] = A[p] + pair_mask[L] * _dot_nt(qb[p] * eL, block_diag(kb[p] * eL))
        for p, sl in enumerate(pairs):
            eb = e_sc[buf, nl * C:(nl + 1) * C, sl]
            ek = e_sc[buf, (nl + 1) * C:(nl + 2) * C, sl]
            st = [st_sc[2 * p], st_sc[2 * p + 1]]
            st_pair = jnp.concatenate([s.astype(BF16) for s in st], axis=1)
            o = (_dot(A[p].astype(BF16), block_diag(vb[p]))
                 + _dot_nt(qb[p] * eb, block_diag(st_pair)))
            upd = _dot_tn(vb[p], kb[p] * ek)
            for j in range(2):
                h = 2 * p + j
                hs = slice(h * HG_DK, (h + 1) * HG_DK)
                js = slice(j * HG_DK, (j + 1) * HG_DK)
                st_sc[h] = st[j] * d_sc[buf, :, hs] + upd[js, js]
                oj = o[:, js]
                ms = jnp.mean(oj * oj, axis=-1, keepdims=True)
                y = oj * lax.rsqrt(ms + NORM_EPS) * gain * g_ref[rows, hs].astype(F32)
                o_ref[rows, hs] = y.astype(o_ref.dtype)

    prepare(0, 0)

    def step(c, carry):
        prepare(jnp.minimum(c + 1, chunks - 1), (c + 1) & 1)
        heads_of(c, c & 1)
        return carry

    lax.fori_loop(0, chunks, step, 0)


def _hgrn(q, lf, k, v, g, gain, B, S):
    T, W = q.shape
    C = HG_CHUNK
    TS = HG_STEP_TOKENS
    assert S % TS == 0 and TS % C == 0
    N = S // TS
    masks, nl = _hgrn_tables(C)
    row = lambda b, c: (b * N + c, 0)
    tok = lambda: pl.BlockSpec((TS, W), row)
    return pl.pallas_call(
        functools.partial(_hgrn_kernel, C=C, nl=nl, heads=HG_HEADS, chunks=TS // C),
        grid=(B, N),
        in_specs=[tok(), tok(), tok(), tok(), tok(),
                  pl.BlockSpec((1, HG_DV), lambda b, c: (0, 0)),
                  pl.BlockSpec(masks.shape, lambda b, c: (0, 0, 0))],
        out_specs=tok(),
        out_shape=jax.ShapeDtypeStruct((T, W), BF16),
        scratch_shapes=[pltpu.VMEM((2, (nl + 2) * C, W), BF16),
                        pltpu.VMEM((2, 1, W), F32),
                        pltpu.VMEM((HG_HEADS, HG_DV, HG_DK), F32)],
        compiler_params=_cparams(("parallel", "arbitrary")),
        name="hgrn2",
    )(q, lf, k, v, g, gain.reshape(1, HG_DV).astype(F32), jnp.asarray(masks, F32))


def _t5_bucket_np(dist):
    n = np.maximum(dist, 0)
    max_exact = REL_BUCKETS // 2
    nf = np.maximum(n, 1).astype(np.float32)
    large = max_exact + (np.log(nf / max_exact) / math.log(REL_MAX_DIST / max_exact)
                         * (REL_BUCKETS - max_exact)).astype(np.int32)
    large = np.minimum(large, REL_BUCKETS - 1)
    return np.where(n < max_exact, n, large).astype(np.int32)


def _bias_kernel(tab_ref, bucket_ref, o_ref):
    h = pl.program_id(0)
    bucket = bucket_ref[...]
    acc = jnp.zeros(bucket.shape, F32)
    for b in range(REL_BUCKETS):
        acc = jnp.where(bucket == b, tab_ref[b, h], acc)
    delta = (acc - tab_ref[REL_BUCKETS - 1, h]) * LOG2E
    blk = bucket.shape[0]
    r = lax.broadcasted_iota(jnp.int32, bucket.shape, 0)
    c = lax.broadcasted_iota(jnp.int32, bucket.shape, 1)
    o_ref[0] = jnp.where(c - blk <= r, delta, MASKED_LOGIT)


def _bias_tiles(rel_table):
    BLK = MOBA_BLOCK
    t = np.arange(BLK)[:, None]
    s = np.arange(2 * BLK)[None, :]
    bucket = _t5_bucket_np(t + BLK - s)
    H = rel_table.shape[1]
    return pl.pallas_call(
        _bias_kernel,
        grid=(H,),
        in_specs=[pl.BlockSpec(memory_space=pltpu.SMEM),
                  pl.BlockSpec(bucket.shape, lambda h: (0, 0))],
        out_specs=pl.BlockSpec((1, BLK, 2 * BLK), lambda h: (h, 0, 0)),
        out_shape=jax.ShapeDtypeStruct((H, BLK, 2 * BLK), F32),
        compiler_params=_cparams(("arbitrary",)),
        name="relbias",
    )(rel_table.astype(F32), jnp.asarray(bucket))


MOBA_HEADS_PER_STEP = 2
MOBA_PAIRS_PER_STEP = 2
MOBA_BOUND_SLACK = 1.0 + 2.0 ** -6
MOBA_SAFE_GAP = 100.0


def _moba_kernel(q_ref, k_ref, v_ref, dbias_ref, far_ref, qg_ref, kg_ref, o_ref,
                 qnear_sc, qfar_sc, kaug_sc, vaug_sc, near_sc, far_sc, mx_sc, mb_sc, acc_sc,
                 safe_sc, *, NB, HPS, PPS):
    BLK = MOBA_BLOCK
    W2 = 2 * BLK
    PAD = NB + 2
    SHIFT = NB + 3
    NS = NB // 2 - 1
    hg = pl.program_id(1)
    ip = pl.program_id(2)

    def prepare(hh):
        cols = slice(hh * AT_DH, (hh + 1) * AT_DH)
        kaug, vaug, qnear, qfar = kaug_sc.at[hh], vaug_sc.at[hh], qnear_sc.at[hh], qfar_sc.at[hh]
        far = far_ref[pl.ds(hg * HPS + hh, 1), :]
        far_hi = far.astype(BF16).astype(F32)
        far_lo = far - far_hi
        lane = lax.broadcasted_iota(jnp.int32, (1, LANES), 1)
        tail = jnp.where(lane == NB, far_hi, jnp.where(lane == NB + 1, far_lo, 0.0))
        tail = jnp.where((lane == SHIFT) | (lane == SHIFT + 1), 1.0, tail)
        kaug[0:BLK, 0:AT_DH] = jnp.zeros((BLK, AT_DH), BF16)
        kaug[0:BLK, AT_DH:] = jnp.broadcast_to(
            jnp.where(lane == PAD, 1.0, 0.0), (BLK, LANES)).astype(BF16)
        vaug[0:BLK, :] = jnp.zeros((BLK, AT_DH + LANES), BF16)
        kms = []
        ones8 = jnp.ones((8, BLK), BF16)
        for n in range(NB):
            kb = k_ref[n * BLK:(n + 1) * BLK, cols]
            kms.append(_dot(ones8, kb)[0:1, :] * (1.0 / BLK))
            kaug[(n + 1) * BLK:(n + 2) * BLK, 0:AT_DH] = kb
            kaug[(n + 1) * BLK:(n + 2) * BLK, AT_DH:] = jnp.broadcast_to(
                jnp.where(lane == n, 1.0, tail), (BLK, LANES)).astype(BF16)
        vaug[BLK:, 0:AT_DH] = v_ref[:, cols]
        vaug[BLK:, AT_DH:] = jnp.ones((NB * BLK, LANES), BF16)

        qk_bound = (AT_DH * jnp.max(jnp.abs(qg_ref[...]), keepdims=True)
                    * jnp.max(jnp.abs(kg_ref[...]), keepdims=True)
                    * (AT_DH ** -0.5 * LOG2E * MOBA_BOUND_SLACK))
        bias_max = jnp.max(dbias_ref[hh], keepdims=True)[0:1, 0:1] + far[:, 0:1]
        bias_self = dbias_ref[hh][0:1, BLK:BLK + 1] + far[:, 0:1]
        shift = qk_bound + bias_max
        gap = shift + qk_bound - bias_self
        shift_hi = shift.astype(BF16).astype(F32)

        kmean = jnp.concatenate(kms, axis=0)
        km_hi = kmean.astype(BF16)
        km_lo = (kmean - km_hi.astype(F32)).astype(BF16)
        blk = lax.broadcasted_iota(jnp.int32, (NB, BLK), 0)
        rowid = lax.broadcasted_iota(jnp.int32, (LANES - NB, BLK), 0) + NB
        rest = jnp.where(rowid < NB + 2, 1.0, jnp.where(rowid == PAD, MASKED_LOGIT, 0.0))
        rest = jnp.where(rowid == SHIFT, -shift_hi,
                         jnp.where(rowid == SHIFT + 1, shift_hi - shift, rest))
        lane_q = lax.broadcasted_iota(jnp.int32, (BLK, LANES), 1)
        for t in range(NB):
            qt = q_ref[t * BLK:(t + 1) * BLK, cols]
            gate = jnp.where(blk < t, _dot_nt(km_hi, qt) + _dot_nt(km_lo, qt), -jnp.inf)
            rank = jnp.zeros((NB, BLK), jnp.int32)
            for m in range(t):
                gm = gate[m:m + 1, :]
                rank = rank + ((gm > gate) | ((gm == gate) & (blk > m))).astype(jnp.int32)
            sel = ((blk < t) & (rank < MOBA_TOPK)) | (blk == t)
            near = jnp.concatenate([jnp.where(sel, 0.0, MASKED_LOGIT), rest], axis=0).T
            older = jnp.where((lane_q == t) | (lane_q == t - 1), MASKED_LOGIT, near)
            rows = slice(t * BLK, (t + 1) * BLK)
            qnear[rows, 0:AT_DH] = qt
            qfar[rows, 0:AT_DH] = qt
            qnear[rows, AT_DH:] = near.astype(BF16)
            qfar[rows, AT_DH:] = older.astype(BF16)
        return jnp.max(gap)

    @pl.when(ip == 0)
    def _():
        worst = prepare(0)
        for hh in range(1, HPS):
            worst = jnp.maximum(worst, prepare(hh))
        safe_sc[0] = (worst <= MOBA_SAFE_GAP).astype(jnp.int32)

    def lane_max(s):
        out = s[:, 0:LANES]
        for c in range(1, s.shape[1] // LANES):
            out = jnp.maximum(out, s[:, c * LANES:(c + 1) * LANES])
        return out

    def probs(s, m_b):
        return jnp.exp2(s - jnp.concatenate([m_b] * (W2 // LANES), axis=1)).astype(BF16)

    def plan(j):
        rows = [pl.multiple_of(b * BLK, BLK) for b in (j, NB - 1 - j)]
        n_first = jnp.maximum(j, 1) // 2
        visits = []
        for st in range(NS):
            second = st >= n_first
            jj = jnp.where(second, st - n_first, st)
            visits.append((second.astype(jnp.int32), jnp.where(second, rows[1], rows[0]),
                           pl.multiple_of(jj * W2 + BLK, BLK)))
        return rows, visits

    plans = [plan(ip * PPS + pp) for pp in range(PPS)]

    def one_pass():
        for pp, (rows, visits) in enumerate(plans):
            for hh in range(HPS):
                for slot in range(2):
                    s = (_dot_nt(qnear_sc[hh, pl.ds(rows[slot], BLK), :],
                                 kaug_sc[hh, pl.ds(rows[slot], W2), :]) + dbias_ref[hh])
                    acc_sc[pp, hh, slot] = _dot(jnp.exp2(s).astype(BF16),
                                                vaug_sc[hh, pl.ds(rows[slot], W2), :])
                for st, (slot, qrow, krow) in enumerate(visits):
                    s = _dot_nt(qfar_sc[hh, pl.ds(qrow, BLK), :],
                                kaug_sc[hh, pl.ds(krow, W2), :])
                    acc_sc[pp, hh, slot] += _dot(jnp.exp2(s).astype(BF16),
                                                 vaug_sc[hh, pl.ds(krow, W2), :])

    def two_pass():
        for pp, (rows, visits) in enumerate(plans):
            for hh in range(HPS):
                for slot in range(2):
                    near_sc[hh, slot] = (_dot_nt(qnear_sc[hh, pl.ds(rows[slot], BLK), :],
                                                 kaug_sc[hh, pl.ds(rows[slot], W2), :])
                                         + dbias_ref[hh])
                for st, (slot, qrow, krow) in enumerate(visits):
                    far_sc[hh, st] = _dot_nt(qfar_sc[hh, pl.ds(qrow, BLK), :],
                                             kaug_sc[hh, pl.ds(krow, W2), :])
            mx_sc[...] = jnp.full(mx_sc.shape, MASKED_LOGIT, F32)
            for hh in range(HPS):
                for st, (slot, qrow, krow) in enumerate(visits):
                    mx_sc[hh, slot * NS + st] = lane_max(far_sc[hh, st])
                for slot in range(2):
                    m = lane_max(near_sc[hh, slot])
                    for st in range(NS):
                        m = jnp.maximum(m, mx_sc[hh, slot * NS + st])
                    mb_sc[hh, slot] = jnp.broadcast_to(jnp.max(m, axis=-1, keepdims=True),
                                                       (BLK, LANES))
            for hh in range(HPS):
                for slot in range(2):
                    acc_sc[pp, hh, slot] = _dot(probs(near_sc[hh, slot], mb_sc[hh, slot]),
                                                vaug_sc[hh, pl.ds(rows[slot], W2), :])
                for st, (slot, qrow, krow) in enumerate(visits):
                    acc_sc[pp, hh, slot] += _dot(probs(far_sc[hh, st], mb_sc[hh, slot]),
                                                 vaug_sc[hh, pl.ds(krow, W2), :])

    lax.cond(safe_sc[0] != 0, one_pass, two_pass)

    for pp, (rows, visits) in enumerate(plans):
        for hh in range(HPS):
            for slot in range(2):
                acc = acc_sc[pp, hh, slot]
                o_ref[pl.ds(rows[slot], BLK), hh * AT_DH:(hh + 1) * AT_DH] = (
                    acc[:, 0:AT_DH] / acc[:, AT_DH:]).astype(o_ref.dtype)


def _moba(q, k, v, rel_table, qg, kg, B, S):
    T, W = q.shape
    H = AT_HEADS
    HPS = MOBA_HEADS_PER_STEP
    PPS = MOBA_PAIRS_PER_STEP
    NB = S // MOBA_BLOCK
    BLK = MOBA_BLOCK
    NS = max(NB // 2 - 1, 1)
    assert NB + 5 <= LANES and NB % (2 * PPS) == 0 and H % HPS == 0
    assert int(_t5_bucket_np(np.array([BLK + 1]))[0]) == REL_BUCKETS - 1
    dbias = _bias_tiles(rel_table)
    far = jnp.broadcast_to((rel_table[REL_BUCKETS - 1, :].astype(F32) * LOG2E)[:, None],
                           (H, LANES))
    heads = lambda: pl.BlockSpec((S, HPS * AT_DH), lambda b, g, i: (b, g))
    aug = AT_DH + LANES
    return pl.pallas_call(
        functools.partial(_moba_kernel, NB=NB, HPS=HPS, PPS=PPS),
        grid=(B, H // HPS, NB // (2 * PPS)),
        in_specs=[heads(), heads(), heads(),
                  pl.BlockSpec((HPS, BLK, 2 * BLK), lambda b, g, i: (g, 0, 0)),
                  pl.BlockSpec((H, LANES), lambda b, g, i: (0, 0)),
                  pl.BlockSpec((1, AT_DH), lambda b, g, i: (0, 0)),
                  pl.BlockSpec((1, AT_DH), lambda b, g, i: (0, 0))],
        out_specs=heads(),
        out_shape=jax.ShapeDtypeStruct((T, W), BF16),
        scratch_shapes=[pltpu.VMEM((HPS, S, aug), BF16),
                        pltpu.VMEM((HPS, S, aug), BF16),
                        pltpu.VMEM((HPS, S + BLK, aug), BF16),
                        pltpu.VMEM((HPS, S + BLK, aug), BF16),
                        pltpu.VMEM((HPS, 2, BLK, 2 * BLK), F32),
                        pltpu.VMEM((HPS, NS, BLK, 2 * BLK), F32),
                        pltpu.VMEM((HPS, 2 * NS, BLK, LANES), F32),
                        pltpu.VMEM((HPS, 2, BLK, LANES), F32),
                        pltpu.VMEM((PPS, HPS, 2, BLK, aug), F32),
                        pltpu.SMEM((1,), jnp.int32)],
        compiler_params=_cparams(("parallel", "parallel", "arbitrary")),
        name="moba",
    )(q, k, v, dbias, far, qg.reshape(1, AT_DH).astype(F32), kg.reshape(1, AT_DH).astype(F32))


def _merge_kernel(ohg_ref, oat_ref, gate_ref, x_ref, whg_ref, wat_ref, wout_ref, g2_ref,
                  x1_ref, h2_ref, *, D, sub):
    for r0 in range(0, x_ref.shape[0], sub):
        rows = slice(r0, r0 + sub)
        y_hg = _dot(ohg_ref[rows, :], whg_ref[...])
        y_at = _dot(oat_ref[rows, :], wat_ref[...])
        merged = (gate_ref[rows, 0:D].astype(F32) * y_hg
                  + gate_ref[rows, D:2 * D].astype(F32) * y_at)
        x1 = x_ref[rows, :] + _dot(merged.astype(BF16), wout_ref[...])
        x1_ref[rows, :] = x1
        ms = jnp.mean(x1 * x1, axis=-1, keepdims=True)
        h2_ref[rows, :] = (x1 * lax.rsqrt(ms + NORM_EPS) * g2_ref[...]).astype(h2_ref.dtype)


def _merge(ohg, oat, gates, x, whg, wat, wout, g2, tm=1024, sub=512):
    T, D = x.shape
    tok = lambda w: pl.BlockSpec((tm, w), lambda i: (i, 0))
    return pl.pallas_call(
        functools.partial(_merge_kernel, D=D, sub=sub),
        grid=(T // tm,),
        in_specs=[tok(ohg.shape[1]), tok(oat.shape[1]), tok(2 * D), tok(D),
                  _resident(whg), _resident(wat), _resident(wout),
                  pl.BlockSpec((1, D), lambda i: (0, 0))],
        out_specs=[tok(D), tok(D)],
        out_shape=[jax.ShapeDtypeStruct((T, D), F32), jax.ShapeDtypeStruct((T, D), BF16)],
        compiler_params=_cparams(("parallel",)),
        name="merge",
    )(ohg, oat, gates, x, whg, wat, wout, g2.reshape(1, D).astype(F32))


def _ffn_kernel(h_ref, x_ref, wg_ref, wu_ref, wd_ref, o_ref, *, tf):
    h = h_ref[...]
    acc = x_ref[...]
    for f0 in range(0, wg_ref.shape[1], tf):
        a = _dot(h, wg_ref[:, f0:f0 + tf])
        u = _dot(h, wu_ref[:, f0:f0 + tf])
        acc = acc + _dot((_silu(a) * u).astype(BF16), wd_ref[f0:f0 + tf, :])
    o_ref[...] = acc


def _ffn(h2, x1, wg, wu, wd, tm=512, tf=2816):
    T, D = x1.shape
    FF = wg.shape[1]
    assert FF % tf == 0
    tok = lambda: pl.BlockSpec((tm, D), lambda i: (i, 0))
    return pl.pallas_call(
        functools.partial(_ffn_kernel, tf=tf),
        grid=(T // tm,),
        in_specs=[tok(), tok(), _resident(wg), _resident(wu), _resident(wd)],
        out_specs=tok(),
        out_shape=jax.ShapeDtypeStruct((T, D), F32),
        compiler_params=_cparams(("parallel",)),
        name="ffn",
    )(h2, x1, wg, wu, wd)


def kernel(x, attn_norm_g, w_in, hg_lb_gamma, hg_out_norm_g, q_norm_g, k_norm_g, rel_bias_table,
           w_branch_hg, w_branch_attn, w_out, ffn_norm_g, w_ffn_gate, w_ffn_up, w_ffn_down):
    B, S, D = x.shape
    T = B * S
    depth = attn_norm_g.shape[0]
    assert depth == 1 and S % MOBA_BLOCK == 0 and S % HG_CHUNK == 0
    WH = HG_HEADS * HG_DK
    WV = HG_HEADS * HG_DV
    WA = AT_HEADS * AT_DH
    assert w_in.shape[2] == 2 * WH + 2 * WV + 3 * WA + 2 * D

    xt = x.reshape(T, D)
    for l in range(depth):
        hq, lf, hk, hi, hg, aq, ak, av, gates = _inproj(
            xt, attn_norm_g[l], w_in[l].astype(BF16), hg_lb_gamma, q_norm_g[l], k_norm_g[l],
            WH, WV, WA)
        o_hg = _hgrn(hq, lf, hk, hi, hg, hg_out_norm_g[l], B, S)
        o_at = _moba(aq, ak, av, rel_bias_table, q_norm_g[l], k_norm_g[l], B, S)
        x1, h2 = _merge(o_hg, o_at, gates, xt, w_branch_hg[l].astype(BF16),
                        w_branch_attn[l].astype(BF16), w_out[l].astype(BF16), ffn_norm_g[l])
        xt = _ffn(h2, x1, w_ffn_gate[l].astype(BF16), w_ffn_up[l].astype(BF16),
                  w_ffn_down[l].astype(BF16))
    return xt.reshape(B, S, D)
```

```python
import functools
import math

import numpy as np
import jax
import jax.numpy as jnp
from jax import lax
from jax.experimental import pallas as pl
from jax.experimental.pallas import tpu as pltpu

F32 = jnp.float32
BF16 = jnp.bfloat16

LANES = 128
NORM_EPS = 1e-6

HG_HEADS = 8
HG_DK = 128
HG_DV = 128
HG_CHUNK = 64
HG_STEP_TOKENS = 512
HG_HEAD_GROUP = 8
AT_HEADS = 8
AT_DH = 128
MOBA_BLOCK = 256
MOBA_TOPK = 3
REL_BUCKETS = 32
REL_MAX_DIST = 128
MASKED_LOGIT = -1e30
LOG2E = math.log2(math.e)

VMEM_LIMIT = 56 * 1024 * 1024


def _cparams(sem):
    return pltpu.CompilerParams(dimension_semantics=sem, vmem_limit_bytes=VMEM_LIMIT)


def _dot(a, b):
    return jnp.dot(a, b, preferred_element_type=F32)


def _dot_nt(a, b):
    return lax.dot_general(a, b, (((1,), (1,)), ((), ())), preferred_element_type=F32)


def _dot_tn(a, b):
    return lax.dot_general(a, b, (((0,), (0,)), ((), ())), preferred_element_type=F32)


def _resident(a):
    return pl.BlockSpec(a.shape, lambda *_: (0,) * a.ndim, pipeline_mode=pl.Buffered(1))


def _sigmoid(x):
    return 0.5 * jnp.tanh(0.5 * x) + 0.5


def _silu(x):
    h = 0.5 * x
    return h + h * jnp.tanh(h)


def _head_rmsnorm(acc, gain, scale):
    cols = []
    for c in range(acc.shape[1] // LANES):
        blk = acc[:, c * LANES:(c + 1) * LANES]
        ms = jnp.mean(blk * blk, axis=-1, keepdims=True)
        cols.append(blk * lax.rsqrt(ms + NORM_EPS) * (gain * scale))
    return jnp.concatenate(cols, axis=1)


def _inproj_kernel(x_ref, g_ref, w_ref, gamma_ref, qg_ref, kg_ref,
                   hq_ref, lf_ref, hk_ref, hi_ref, hg_ref, aq_ref, ak_ref, av_ref, gate_ref,
                   *, WH, WV, WA):
    x = x_ref[...]
    ms = jnp.mean(x * x, axis=-1, keepdims=True)
    h = (x * lax.rsqrt(ms + NORM_EPS) * g_ref[...]).astype(BF16)
    D = x.shape[1]
    starts = np.cumsum([0, WH, WH, WV, WV, WA, WA, WA])

    def segment(idx, width, off=0):
        c0 = int(starts[idx]) + off
        return _dot(h, w_ref[:, c0:c0 + width])

    for half in range(2):
        gate_ref[:, half * D:(half + 1) * D] = _sigmoid(
            segment(7, D, half * D)).astype(gate_ref.dtype)

    gamma = gamma_ref[...]
    eg = jnp.exp(gamma - jnp.max(gamma, axis=0, keepdims=True))
    lb = eg[0:1, :] / jnp.sum(eg, axis=0, keepdims=True)
    c0 = 0.5 + 0.5 * lb
    c1 = 0.5 - 0.5 * lb
    ct = c1 * jnp.tanh(0.5 * segment(1, WH))
    lf_ref[...] = jnp.log2(c0 + ct)
    hk_ref[...] = (c1 - ct).astype(hk_ref.dtype)

    hg_ref[...] = _silu(segment(3, WV)).astype(hg_ref.dtype)
    aq_ref[...] = _head_rmsnorm(segment(4, WA), qg_ref[...],
                                AT_DH ** -0.5 * LOG2E).astype(aq_ref.dtype)
    ak_ref[...] = _head_rmsnorm(segment(5, WA), kg_ref[...], 1.0).astype(ak_ref.dtype)
    hq_ref[...] = (segment(0, WH) * HG_DK ** -0.5).astype(hq_ref.dtype)
    hi_ref[...] = segment(2, WV).astype(hi_ref.dtype)
    av_ref[...] = segment(6, WA).astype(av_ref.dtype)


def _inproj(x, g, w, gamma, qg, kg, WH, WV, WA, tm=512):
    T, D = x.shape
    tok = lambda width: pl.BlockSpec((tm, width), lambda i: (i, 0))
    small = lambda a: pl.BlockSpec(a.shape, lambda i: (0, 0))
    widths = [WH, WH, WH, WV, WV, WA, WA, WA, 2 * D]
    dtypes = [BF16, F32, BF16, BF16, BF16, BF16, BF16, BF16, BF16]
    g = g.reshape(1, D).astype(F32)
    gamma = gamma.astype(F32)
    qg = qg.reshape(1, AT_DH).astype(F32)
    kg = kg.reshape(1, AT_DH).astype(F32)
    return pl.pallas_call(
        functools.partial(_inproj_kernel, WH=WH, WV=WV, WA=WA),
        grid=(T // tm,),
        in_specs=[tok(D), small(g), _resident(w), small(gamma), small(qg), small(kg)],
        out_specs=[tok(wd) for wd in widths],
        out_shape=[jax.ShapeDtypeStruct((T, wd), dt) for wd, dt in zip(widths, dtypes)],
        compiler_params=_cparams(("parallel",)),
        name="inproj",
    )(x, g, w, gamma, qg, kg)


def _hgrn_tables(C):
    nl = int(math.log2(C))
    assert 1 << nl == C and nl >= 3
    masks = np.zeros((nl + 1, C, C), np.float32)
    masks[0] = np.eye(C)
    t = np.arange(C)
    for L in range(1, nl + 1):
        blk, half = 1 << L, 1 << (L - 1)
        base = (t // blk) * blk
        upper = (t - base) >= half
        same = base[:, None] == base[None, :]
        masks[L] = (same & upper[:, None] & (~upper)[None, :]).astype(np.float32)
    return masks, nl


def _hgrn_kernel(q_ref, lf_ref, k_ref, v_ref, g_ref, gain_ref, m_ref, o_ref,
                 e_sc, d_sc, st_sc, *, C, nl, heads, chunks):
    @pl.when(pl.program_id(1) == 0)
    def _():
        st_sc[...] = jnp.zeros_like(st_sc)

    gain = gain_ref[...]
    scan_shifts = [1 << j for j in range(nl)]
    row = lax.broadcasted_iota(jnp.int32, (C, lf_ref.shape[1]), 0)
    odd = (row & 1) == 1
    ph = row & 3

    def chunk_rows(c):
        return pl.ds(pl.multiple_of(c * C, C), C)

    def prepare(c, buf):
        lf = lf_ref[chunk_rows(c), :]
        b = lf
        for sh in scan_shifts:
            b = b + jnp.where(row >= sh, pltpu.roll(b, sh, axis=0), 0.0)
        e_sc[buf, 0:C, :] = jnp.exp2(jnp.where(odd, lf, 0.0)).astype(BF16)
        prev = pltpu.roll(lf, 1, axis=0)
        nxt = pltpu.roll(lf, C - 1, axis=0)
        x2 = jnp.where(ph == 0, nxt, jnp.where(ph == 1, 0.0, jnp.where(ph == 2, lf, lf + prev)))
        e_sc[buf, C:2 * C, :] = jnp.exp2(x2).astype(BF16)
        for L in range(3, nl + 1):
            blk, half = 1 << L, 1 << (L - 1)
            pieces = [b[base:base + blk, :] - b[base + half - 1:base + half, :]
                      for base in range(0, C, blk)]
            d = pieces[0] if len(pieces) == 1 else jnp.concatenate(pieces, axis=0)
            e_sc[buf, (L - 1) * C:L * C, :] = jnp.exp2(-jnp.abs(d)).astype(BF16)
        b_last = b[C - 1:C, :]
        e_sc[buf, nl * C:(nl + 1) * C, :] = jnp.exp2(b).astype(BF16)
        e_sc[buf, (nl + 1) * C:(nl + 2) * C, :] = jnp.exp2(b_last - b).astype(BF16)
        d_sc[buf] = jnp.exp2(b_last)

    def block_diag(x):
        n, d2 = x.shape
        z = jnp.zeros((n, d2 // 2), x.dtype)
        return jnp.concatenate([jnp.concatenate([x[:, :d2 // 2], z], axis=1),
                                jnp.concatenate([z, x[:, d2 // 2:]], axis=1)], axis=0)

    W2 = 2 * HG_DK
    pair_mask = [jnp.concatenate([m_ref[L], m_ref[L]], axis=1) for L in range(nl + 1)]

    def heads_of(c, buf):
        rows = chunk_rows(c)
        pairs = [slice(p * W2, (p + 1) * W2) for p in range(heads // 2)]
        qb = [q_ref[rows, sl] for sl in pairs]
        kb = [k_ref[rows, sl] for sl in pairs]
        vb = [v_ref[rows, sl] for sl in pairs]
        A = [pair_mask[0] * _dot_nt(q, block_diag(k)) for q, k in zip(qb, kb)]
        for L in range(1, nl + 1):
            for p, sl in enumerate(pairs):
                eL = e_sc[buf, (L - 1) * C:L * C, sl]
                A[p] = A[p] + pair_mask[L] * _dot_nt(qb[p] * eL, block_diag(kb[p] * eL))
        for p, sl in enumerate(pairs):
            eb = e_sc[buf, nl * C:(nl + 1) * C, sl]
            ek = e_sc[buf, (nl + 1) * C:(nl + 2) * C, sl]
            st = [st_sc[2 * p], st_sc[2 * p + 1]]
            st_pair = jnp.concatenate([s.astype(BF16) for s in st], axis=1)
            o = (_dot(A[p].astype(BF16), block_diag(vb[p]))
                 + _dot_nt(qb[p] * eb, block_diag(st_pair)))
            upd = _dot_tn(vb[p], kb[p] * ek)
            for j in range(2):
                h = 2 * p + j
                hs = slice(h * HG_DK, (h + 1) * HG_DK)
                js = slice(j * HG_DK, (j + 1) * HG_DK)
                st_sc[h] = st[j] * d_sc[buf, :, hs] + upd[js, js]
                oj = o[:, js]
                ms = jnp.mean(oj * oj, axis=-1, keepdims=True)
                y = oj * lax.rsqrt(ms + NORM_EPS) * gain * g_ref[rows, hs].astype(F32)
                o_ref[rows, hs] = y.astype(o_ref.dtype)

    prepare(0, 0)

    def step(c, carry):
        prepare(jnp.minimum(c + 1, chunks - 1), (c + 1) & 1)
        heads_of(c, c & 1)
        return carry

    lax.fori_loop(0, chunks, step, 0)


def _hgrn(q, lf, k, v, g, gain, B, S):
    T, W = q.shape
    C = HG_CHUNK
    TS = HG_STEP_TOKENS
    assert S % TS == 0 and TS % C == 0
    N = S // TS
    masks, nl = _hgrn_tables(C)
    row = lambda b, c: (b * N + c, 0)
    tok = lambda: pl.BlockSpec((TS, W), row)
    return pl.pallas_call(
        functools.partial(_hgrn_kernel, C=C, nl=nl, heads=HG_HEADS, chunks=TS // C),
        grid=(B, N),
        in_specs=[tok(), tok(), tok(), tok(), tok(),
                  pl.BlockSpec((1, HG_DV), lambda b, c: (0, 0)),
                  pl.BlockSpec(masks.shape, lambda b, c: (0, 0, 0))],
        out_specs=tok(),
        out_shape=jax.ShapeDtypeStruct((T, W), BF16),
        scratch_shapes=[pltpu.VMEM((2, (nl + 2) * C, W), BF16),
                        pltpu.VMEM((2, 1, W), F32),
                        pltpu.VMEM((HG_HEADS, HG_DV, HG_DK), F32)],
        compiler_params=_cparams(("parallel", "arbitrary")),
        name="hgrn2",
    )(q, lf, k, v, g, gain.reshape(1, HG_DV).astype(F32), jnp.asarray(masks, F32))


def _t5_bucket_np(dist):
    n = np.maximum(dist, 0)
    max_exact = REL_BUCKETS // 2
    nf = np.maximum(n, 1).astype(np.float32)
    large = max_exact + (np.log(nf / max_exact) / math.log(REL_MAX_DIST / max_exact)
                         * (REL_BUCKETS - max_exact)).astype(np.int32)
    large = np.minimum(large, REL_BUCKETS - 1)
    return np.where(n < max_exact, n, large).astype(np.int32)


def _bias_kernel(tab_ref, bucket_ref, o_ref):
    h = pl.program_id(0)
    bucket = bucket_ref[...]
    acc = jnp.zeros(bucket.shape, F32)
    for b in range(REL_BUCKETS):
        acc = jnp.where(bucket == b, tab_ref[b, h], acc)
    delta = (acc - tab_ref[REL_BUCKETS - 1, h]) * LOG2E
    blk = bucket.shape[0]
    r = lax.broadcasted_iota(jnp.int32, bucket.shape, 0)
    c = lax.broadcasted_iota(jnp.int32, bucket.shape, 1)
    o_ref[0] = jnp.where(c - blk <= r, delta, MASKED_LOGIT)


def _bias_tiles(rel_table):
    BLK = MOBA_BLOCK
    t = np.arange(BLK)[:, None]
    s = np.arange(2 * BLK)[None, :]
    bucket = _t5_bucket_np(t + BLK - s)
    H = rel_table.shape[1]
    return pl.pallas_call(
        _bias_kernel,
        grid=(H,),
        in_specs=[pl.BlockSpec(memory_space=pltpu.SMEM),
                  pl.BlockSpec(bucket.shape, lambda h: (0, 0))],
        out_specs=pl.BlockSpec((1, BLK, 2 * BLK), lambda h: (h, 0, 0)),
        out_shape=jax.ShapeDtypeStruct((H, BLK, 2 * BLK), F32),
        compiler_params=_cparams(("arbitrary",)),
        name="relbias",
    )(rel_table.astype(F32), jnp.asarray(bucket))


MOBA_HEADS_PER_STEP = 2
MOBA_PAIRS_PER_STEP = 4
MOBA_BOUND_SLACK = 1.0 + 2.0 ** -6
MOBA_SAFE_GAP = 100.0


def _moba_kernel(q_ref, k_ref, v_ref, dbias_ref, far_ref, qg_ref, kg_ref, o_ref,
                 qnear_sc, qfar_sc, kaug_sc, vaug_sc, near_sc, far_sc, mx_sc, mb_sc, acc_sc,
                 safe_sc, *, NB, HPS, PPS):
    BLK = MOBA_BLOCK
    W2 = 2 * BLK
    PAD = NB + 2
    SHIFT = NB + 3
    NS = NB // 2 - 1
    hg = pl.program_id(1)
    ip = pl.program_id(2)

    def prepare(hh):
        cols = slice(hh * AT_DH, (hh + 1) * AT_DH)
        kaug, vaug, qnear, qfar = kaug_sc.at[hh], vaug_sc.at[hh], qnear_sc.at[hh], qfar_sc.at[hh]
        far = far_ref[pl.ds(hg * HPS + hh, 1), :]
        far_hi = far.astype(BF16).astype(F32)
        far_lo = far - far_hi
        lane = lax.broadcasted_iota(jnp.int32, (1, LANES), 1)
        tail = jnp.where(lane == NB, far_hi, jnp.where(lane == NB + 1, far_lo, 0.0))
        tail = jnp.where((lane == SHIFT) | (lane == SHIFT + 1), 1.0, tail)
        kaug[0:BLK, 0:AT_DH] = jnp.zeros((BLK, AT_DH), BF16)
        kaug[0:BLK, AT_DH:] = jnp.broadcast_to(
            jnp.where(lane == PAD, 1.0, 0.0), (BLK, LANES)).astype(BF16)
        vaug[0:BLK, :] = jnp.zeros((BLK, AT_DH + LANES), BF16)
        kms = []
        ones8 = jnp.ones((8, BLK), BF16)
        for n in range(NB):
            kb = k_ref[n * BLK:(n + 1) * BLK, cols]
            kms.append(_dot(ones8, kb)[0:1, :] * (1.0 / BLK))
            kaug[(n + 1) * BLK:(n + 2) * BLK, 0:AT_DH] = kb
            kaug[(n + 1) * BLK:(n + 2) * BLK, AT_DH:] = jnp.broadcast_to(
                jnp.where(lane == n, 1.0, tail), (BLK, LANES)).astype(BF16)
        vaug[BLK:, 0:AT_DH] = v_ref[:, cols]
        vaug[BLK:, AT_DH:] = jnp.ones((NB * BLK, LANES), BF16)

        qk_bound = (AT_DH * jnp.max(jnp.abs(qg_ref[...]), keepdims=True)
                    * jnp.max(jnp.abs(kg_ref[...]), keepdims=True)
                    * (AT_DH ** -0.5 * LOG2E * MOBA_BOUND_SLACK))
        bias_max = jnp.max(dbias_ref[hh], keepdims=True)[0:1, 0:1] + far[:, 0:1]
        bias_self = dbias_ref[hh][0:1, BLK:BLK + 1] + far[:, 0:1]
        shift = qk_bound + bias_max
        gap = shift + qk_bound - bias_self
        shift_hi = shift.astype(BF16).astype(F32)

        kmean = jnp.concatenate(kms, axis=0)
        km_hi = kmean.astype(BF16)
        km_lo = (kmean - km_hi.astype(F32)).astype(BF16)
        blk = lax.broadcasted_iota(jnp.int32, (NB, BLK), 0)
        rowid = lax.broadcasted_iota(jnp.int32, (LANES - NB, BLK), 0) + NB
        rest = jnp.where(rowid < NB + 2, 1.0, jnp.where(rowid == PAD, MASKED_LOGIT, 0.0))
        rest = jnp.where(rowid == SHIFT, -shift_hi,
                         jnp.where(rowid == SHIFT + 1, shift_hi - shift, rest))
        lane_q = lax.broadcasted_iota(jnp.int32, (BLK, LANES), 1)
        for t in range(NB):
            qt = q_ref[t * BLK:(t + 1) * BLK, cols]
            gate = jnp.where(blk < t, _dot_nt(km_hi, qt) + _dot_nt(km_lo, qt), -jnp.inf)
            rank = jnp.zeros((NB, BLK), jnp.int32)
            for m in range(t):
                gm = gate[m:m + 1, :]
                rank = rank + ((gm > gate) | ((gm == gate) & (blk > m))).astype(jnp.int32)
            sel = ((blk < t) & (rank < MOBA_TOPK)) | (blk == t)
            near = jnp.concatenate([jnp.where(sel, 0.0, MASKED_LOGIT), rest], axis=0).T
            older = jnp.where((lane_q == t) | (lane_q == t - 1), MASKED_LOGIT, near)
            rows = slice(t * BLK, (t + 1) * BLK)
            qnear[rows, 0:AT_DH] = qt
            qfar[rows, 0:AT_DH] = qt
            qnear[rows, AT_DH:] = near.astype(BF16)
            qfar[rows, AT_DH:] = older.astype(BF16)
        return jnp.max(gap)

    @pl.when(ip == 0)
    def _():
        worst = prepare(0)
        for hh in range(1, HPS):
            worst = jnp.maximum(worst, prepare(hh))
        safe_sc[0] = (worst <= MOBA_SAFE_GAP).astype(jnp.int32)

    def lane_max(s):
        out = s[:, 0:LANES]
        for c in range(1, s.shape[1] // LANES):
            out = jnp.maximum(out, s[:, c * LANES:(c + 1) * LANES])
        return out

    def probs(s, m_b):
        return jnp.exp2(s - jnp.concatenate([m_b] * (W2 // LANES), axis=1)).astype(BF16)

    def plan(j):
        rows = [pl.multiple_of(b * BLK, BLK) for b in (j, NB - 1 - j)]
        n_first = jnp.maximum(j, 1) // 2
        visits = []
        for st in range(NS):
            second = st >= n_first
            jj = jnp.where(second, st - n_first, st)
            visits.append((second.astype(jnp.int32), jnp.where(second, rows[1], rows[0]),
                           pl.multiple_of(jj * W2 + BLK, BLK)))
        return rows, visits

    plans = [plan(ip * PPS + pp) for pp in range(PPS)]

    def one_pass():
        for pp, (rows, visits) in enumerate(plans):
            for hh in range(HPS):
                for slot in range(2):
                    s = (_dot_nt(qnear_sc[hh, pl.ds(rows[slot], BLK), :],
                                 kaug_sc[hh, pl.ds(rows[slot], W2), :]) + dbias_ref[hh])
                    acc_sc[pp, hh, slot] = _dot(jnp.exp2(s).astype(BF16),
                                                vaug_sc[hh, pl.ds(rows[slot], W2), :])
                for st, (slot, qrow, krow) in enumerate(visits):
                    s = _dot_nt(qfar_sc[hh, pl.ds(qrow, BLK), :],
                                kaug_sc[hh, pl.ds(krow, W2), :])
                    acc_sc[pp, hh, slot] += _dot(jnp.exp2(s).astype(BF16),
                                                 vaug_sc[hh, pl.ds(krow, W2), :])

    def two_pass():
        for pp, (rows, visits) in enumerate(plans):
            for hh in range(HPS):
                for slot in range(2):
                    near_sc[hh, slot] = (_dot_nt(qnear_sc[hh, pl.ds(rows[slot], BLK), :],
                                                 kaug_sc[hh, pl.ds(rows[slot], W2), :])
                                         + dbias_ref[hh])
                for st, (slot, qrow, krow) in enumerate(visits):
                    far_sc[hh, st] = _dot_nt(qfar_sc[hh, pl.ds(qrow, BLK), :],
                                             kaug_sc[hh, pl.ds(krow, W2), :])
            mx_sc[...] = jnp.full(mx_sc.shape, MASKED_LOGIT, F32)
            for hh in range(HPS):
                for st, (slot, qrow, krow) in enumerate(visits):
                    mx_sc[hh, slot * NS + st] = lane_max(far_sc[hh, st])
                for slot in range(2):
                    m = lane_max(near_sc[hh, slot])
                    for st in range(NS):
                        m = jnp.maximum(m, mx_sc[hh, slot * NS + st])
                    mb_sc[hh, slot] = jnp.broadcast_to(jnp.max(m, axis=-1, keepdims=True),
                                                       (BLK, LANES))
            for hh in range(HPS):
                for slot in range(2):
                    acc_sc[pp, hh, slot] = _dot(probs(near_sc[hh, slot], mb_sc[hh, slot]),
                                                vaug_sc[hh, pl.ds(rows[slot], W2), :])
                for st, (slot, qrow, krow) in enumerate(visits):
                    acc_sc[pp, hh, slot] += _dot(probs(far_sc[hh, st], mb_sc[hh, slot]),
                                                 vaug_sc[hh, pl.ds(krow, W2), :])

    lax.cond(safe_sc[0] != 0, one_pass, two_pass)

    for pp, (rows, visits) in enumerate(plans):
        for hh in range(HPS):
            for slot in range(2):
                acc = acc_sc[pp, hh, slot]
                o_ref[pl.ds(rows[slot], BLK), hh * AT_DH:(hh + 1) * AT_DH] = (
                    acc[:, 0:AT_DH] / acc[:, AT_DH:]).astype(o_ref.dtype)


def _moba(q, k, v, rel_table, qg, kg, B, S):
    T, W = q.shape
    H = AT_HEADS
    HPS = MOBA_HEADS_PER_STEP
    PPS = MOBA_PAIRS_PER_STEP
    NB = S // MOBA_BLOCK
    BLK = MOBA_BLOCK
    NS = max(NB // 2 - 1, 1)
    assert NB + 5 <= LANES and NB % (2 * PPS) == 0 and H % HPS == 0
    assert int(_t5_bucket_np(np.array([BLK + 1]))[0]) == REL_BUCKETS - 1
    dbias = _bias_tiles(rel_table)
    far = jnp.broadcast_to((rel_table[REL_BUCKETS - 1, :].astype(F32) * LOG2E)[:, None],
                           (H, LANES))
    heads = lambda: pl.BlockSpec((S, HPS * AT_DH), lambda b, g, i: (b, g))
    aug = AT_DH + LANES
    return pl.pallas_call(
        functools.partial(_moba_kernel, NB=NB, HPS=HPS, PPS=PPS),
        grid=(B, H // HPS, NB // (2 * PPS)),
        in_specs=[heads(), heads(), heads(),
                  pl.BlockSpec((HPS, BLK, 2 * BLK), lambda b, g, i: (g, 0, 0)),
                  pl.BlockSpec((H, LANES), lambda b, g, i: (0, 0)),
                  pl.BlockSpec((1, AT_DH), lambda b, g, i: (0, 0)),
                  pl.BlockSpec((1, AT_DH), lambda b, g, i: (0, 0))],
        out_specs=heads(),
        out_shape=jax.ShapeDtypeStruct((T, W), BF16),
        scratch_shapes=[pltpu.VMEM((HPS, S, aug), BF16),
                        pltpu.VMEM((HPS, S, aug), BF16),
                        pltpu.VMEM((HPS, S + BLK, aug), BF16),
                        pltpu.VMEM((HPS, S + BLK, aug), BF16),
                        pltpu.VMEM((HPS, 2, BLK, 2 * BLK), F32),
                        pltpu.VMEM((HPS, NS, BLK, 2 * BLK), F32),
                        pltpu.VMEM((HPS, 2 * NS, BLK, LANES), F32),
                        pltpu.VMEM((HPS, 2, BLK, LANES), F32),
                        pltpu.VMEM((PPS, HPS, 2, BLK, aug), F32),
                        pltpu.SMEM((1,), jnp.int32)],
        compiler_params=_cparams(("parallel", "parallel", "arbitrary")),
        name="moba",
    )(q, k, v, dbias, far, qg.reshape(1, AT_DH).astype(F32), kg.reshape(1, AT_DH).astype(F32))


def _merge_kernel(ohg_ref, oat_ref, gate_ref, x_ref, whg_ref, wat_ref, wout_ref, g2_ref,
                  x1_ref, h2_ref, *, D, sub):
    for r0 in range(0, x_ref.shape[0], sub):
        rows = slice(r0, r0 + sub)
        y_hg = _dot(ohg_ref[rows, :], whg_ref[...])
        y_at = _dot(oat_ref[rows, :], wat_ref[...])
        merged = (gate_ref[rows, 0:D].astype(F32) * y_hg
                  + gate_ref[rows, D:2 * D].astype(F32) * y_at)
        x1 = x_ref[rows, :] + _dot(merged.astype(BF16), wout_ref[...])
        x1_ref[rows, :] = x1
        ms = jnp.mean(x1 * x1, axis=-1, keepdims=True)
        h2_ref[rows, :] = (x1 * lax.rsqrt(ms + NORM_EPS) * g2_ref[...]).astype(h2_ref.dtype)


def _merge(ohg, oat, gates, x, whg, wat, wout, g2, tm=1024, sub=512):
    T, D = x.shape
    tok = lambda w: pl.BlockSpec((tm, w), lambda i: (i, 0))
    return pl.pallas_call(
        functools.partial(_merge_kernel, D=D, sub=sub),
        grid=(T // tm,),
        in_specs=[tok(ohg.shape[1]), tok(oat.shape[1]), tok(2 * D), tok(D),
                  _resident(whg), _resident(wat), _resident(wout),
                  pl.BlockSpec((1, D), lambda i: (0, 0))],
        out_specs=[tok(D), tok(D)],
        out_shape=[jax.ShapeDtypeStruct((T, D), F32), jax.ShapeDtypeStruct((T, D), BF16)],
        compiler_params=_cparams(("parallel",)),
        name="merge",
    )(ohg, oat, gates, x, whg, wat, wout, g2.reshape(1, D).astype(F32))


def _ffn_kernel(h_ref, x_ref, wg_ref, wu_ref, wd_ref, o_ref, *, tf):
    h = h_ref[...]
    acc = x_ref[...]
    for f0 in range(0, wg_ref.shape[1], tf):
        a = _dot(h, wg_ref[:, f0:f0 + tf])
        u = _dot(h, wu_ref[:, f0:f0 + tf])
        acc = acc + _dot((_silu(a) * u).astype(BF16), wd_ref[f0:f0 + tf, :])
    o_ref[...] = acc


def _ffn(h2, x1, wg, wu, wd, tm=512, tf=2816):
    T, D = x1.shape
    FF = wg.shape[1]
    assert FF % tf == 0
    tok = lambda: pl.BlockSpec((tm, D), lambda i: (i, 0))
    return pl.pallas_call(
        functools.partial(_ffn_kernel, tf=tf),
        grid=(T // tm,),
        in_specs=[tok(), tok(), _resident(wg), _resident(wu), _resident(wd)],
        out_specs=tok(),
        out_shape=jax.ShapeDtypeStruct((T, D), F32),
        compiler_params=_cparams(("parallel",)),
        name="ffn",
    )(h2, x1, wg, wu, wd)


def kernel(x, attn_norm_g, w_in, hg_lb_gamma, hg_out_norm_g, q_norm_g, k_norm_g, rel_bias_table,
           w_branch_hg, w_branch_attn, w_out, ffn_norm_g, w_ffn_gate, w_ffn_up, w_ffn_down):
    B, S, D = x.shape
    T = B * S
    depth = attn_norm_g.shape[0]
    assert depth == 1 and S % MOBA_BLOCK == 0 and S % HG_CHUNK == 0
    WH = HG_HEADS * HG_DK
    WV = HG_HEADS * HG_DV
    WA = AT_HEADS * AT_DH
    assert w_in.shape[2] == 2 * WH + 2 * WV + 3 * WA + 2 * D

    xt = x.reshape(T, D)
    for l in range(depth):
        hq, lf, hk, hi, hg, aq, ak, av, gates = _inproj(
            xt, attn_norm_g[l], w_in[l].astype(BF16), hg_lb_gamma, q_norm_g[l], k_norm_g[l],
            WH, WV, WA)
        o_hg = _hgrn(hq, lf, hk, hi, hg, hg_out_norm_g[l], B, S)
        o_at = _moba(aq, ak, av, rel_bias_table, q_norm_g[l], k_norm_g[l], B, S)
        x1, h2 = _merge(o_hg, o_at, gates, xt, w_branch_hg[l].astype(BF16),
                        w_branch_attn[l].astype(BF16), w_out[l].astype(BF16), ffn_norm_g[l])
        xt = _ffn(h2, x1, w_ffn_gate[l].astype(BF16), w_ffn_up[l].astype(BF16),
                  w_ffn_down[l].astype(BF16))
    return xt.reshape(B, S, D)
```

```python
import functools
import math

import numpy as np
import jax
import jax.numpy as jnp
from jax import lax
from jax.experimental import pallas as pl
from jax.experimental.pallas import tpu as pltpu

F32 = jnp.float32
BF16 = jnp.bfloat16

LANES = 128
NORM_EPS = 1e-6

HG_HEADS = 8
HG_DK = 128
HG_DV = 128
HG_CHUNK = 64
HG_STEP_TOKENS = 512
HG_HEAD_GROUP = 8
AT_HEADS = 8
AT_DH = 128
MOBA_BLOCK = 256
MOBA_TOPK = 3
REL_BUCKETS = 32
REL_MAX_DIST = 128
MASKED_LOGIT = -1e30
LOG2E = math.log2(math.e)

VMEM_LIMIT = 56 * 1024 * 1024


def _cparams(sem):
    return pltpu.CompilerParams(dimension_semantics=sem, vmem_limit_bytes=VMEM_LIMIT)


def _dot(a, b):
    return jnp.dot(a, b, preferred_element_type=F32)


def _dot_nt(a, b):
    return lax.dot_general(a, b, (((1,), (1,)), ((), ())), preferred_element_type=F32)


def _dot_tn(a, b):
    return lax.dot_general(a, b, (((0,), (0,)), ((), ())), preferred_element_type=F32)


def _resident(a):
    return pl.BlockSpec(a.shape, lambda *_: (0,) * a.ndim, pipeline_mode=pl.Buffered(1))


def _sigmoid(x):
    return 0.5 * jnp.tanh(0.5 * x) + 0.5


def _silu(x):
    h = 0.5 * x
    return h + h * jnp.tanh(h)


def _head_rmsnorm(acc, gain, scale):
    cols = []
    for c in range(acc.shape[1] // LANES):
        blk = acc[:, c * LANES:(c + 1) * LANES]
        ms = jnp.mean(blk * blk, axis=-1, keepdims=True)
        cols.append(blk * lax.rsqrt(ms + NORM_EPS) * (gain * scale))
    return jnp.concatenate(cols, axis=1)


def _inproj_kernel(x_ref, g_ref, w_ref, gamma_ref, qg_ref, kg_ref,
                   hq_ref, lf_ref, hk_ref, hi_ref, hg_ref, aq_ref, ak_ref, av_ref, gate_ref,
                   *, WH, WV, WA):
    x = x_ref[...]
    ms = jnp.mean(x * x, axis=-1, keepdims=True)
    h = (x * lax.rsqrt(ms + NORM_EPS) * g_ref[...]).astype(BF16)
    D = x.shape[1]
    starts = np.cumsum([0, WH, WH, WV, WV, WA, WA, WA])

    def segment(idx, width, off=0):
        c0 = int(starts[idx]) + off
        return _dot(h, w_ref[:, c0:c0 + width])

    for half in range(2):
        gate_ref[:, half * D:(half + 1) * D] = _sigmoid(
            segment(7, D, half * D)).astype(gate_ref.dtype)

    gamma = gamma_ref[...]
    eg = jnp.exp(gamma - jnp.max(gamma, axis=0, keepdims=True))
    lb = eg[0:1, :] / jnp.sum(eg, axis=0, keepdims=True)
    c0 = 0.5 + 0.5 * lb
    c1 = 0.5 - 0.5 * lb
    ct = c1 * jnp.tanh(0.5 * segment(1, WH))
    lf_ref[...] = jnp.log2(c0 + ct)
    hk_ref[...] = (c1 - ct).astype(hk_ref.dtype)

    hg_ref[...] = segment(3, WV).astype(hg_ref.dtype)
    aq_ref[...] = _head_rmsnorm(segment(4, WA), qg_ref[...],
                                AT_DH ** -0.5 * LOG2E).astype(aq_ref.dtype)
    ak_ref[...] = _head_rmsnorm(segment(5, WA), kg_ref[...], 1.0).astype(ak_ref.dtype)
    hq_ref[...] = segment(0, WH).astype(hq_ref.dtype)
    hi_ref[...] = segment(2, WV).astype(hi_ref.dtype)
    av_ref[...] = segment(6, WA).astype(av_ref.dtype)


def _inproj(x, g, w, gamma, qg, kg, WH, WV, WA, tm=512):
    T, D = x.shape
    tok = lambda width: pl.BlockSpec((tm, width), lambda i: (i, 0))
    small = lambda a: pl.BlockSpec(a.shape, lambda i: (0, 0))
    widths = [WH, WH, WH, WV, WV, WA, WA, WA, 2 * D]
    dtypes = [BF16, F32, BF16, BF16, BF16, BF16, BF16, BF16, BF16]
    g = g.reshape(1, D).astype(F32)
    gamma = gamma.astype(F32)
    qg = qg.reshape(1, AT_DH).astype(F32)
    kg = kg.reshape(1, AT_DH).astype(F32)
    return pl.pallas_call(
        functools.partial(_inproj_kernel, WH=WH, WV=WV, WA=WA),
        grid=(T // tm,),
        in_specs=[tok(D), small(g), _resident(w), small(gamma), small(qg), small(kg)],
        out_specs=[tok(wd) for wd in widths],
        out_shape=[jax.ShapeDtypeStruct((T, wd), dt) for wd, dt in zip(widths, dtypes)],
        compiler_params=_cparams(("parallel",)),
        name="inproj",
    )(x, g, w, gamma, qg, kg)


def _hgrn_tables(C):
    nl = int(math.log2(C))
    assert 1 << nl == C and nl >= 3
    masks = np.zeros((nl + 1, C, C), np.float32)
    masks[0] = np.eye(C)
    t = np.arange(C)
    for L in range(1, nl + 1):
        blk, half = 1 << L, 1 << (L - 1)
        base = (t // blk) * blk
        upper = (t - base) >= half
        same = base[:, None] == base[None, :]
        masks[L] = (same & upper[:, None] & (~upper)[None, :]).astype(np.float32)
    return masks, nl


def _hgrn_kernel(q_ref, lf_ref, k_ref, v_ref, g_ref, gain_ref, m_ref, o_ref,
                 e_sc, d_sc, st_sc, *, C, nl, heads, chunks):
    @pl.when(pl.program_id(1) == 0)
    def _():
        st_sc[...] = jnp.zeros_like(st_sc)

    gain = gain_ref[...]
    scan_shifts = [1 << j for j in range(nl)]
    row = lax.broadcasted_iota(jnp.int32, (C, lf_ref.shape[1]), 0)
    odd = (row & 1) == 1
    ph = row & 3

    def chunk_rows(c):
        return pl.ds(pl.multiple_of(c * C, C), C)

    def prepare(c, buf):
        lf = lf_ref[chunk_rows(c), :]
        b = lf
        for sh in scan_shifts:
            b = b + jnp.where(row >= sh, pltpu.roll(b, sh, axis=0), 0.0)
        e_sc[buf, 0:C, :] = jnp.exp2(jnp.where(odd, lf, 0.0)).astype(BF16)
        prev = pltpu.roll(lf, 1, axis=0)
        nxt = pltpu.roll(lf, C - 1, axis=0)
        x2 = jnp.where(ph == 0, nxt, jnp.where(ph == 1, 0.0, jnp.where(ph == 2, lf, lf + prev)))
        e_sc[buf, C:2 * C, :] = jnp.exp2(x2).astype(BF16)
        for L in range(3, nl + 1):
            blk, half = 1 << L, 1 << (L - 1)
            pieces = [b[base:base + blk, :] - b[base + half - 1:base + half, :]
                      for base in range(0, C, blk)]
            d = pieces[0] if len(pieces) == 1 else jnp.concatenate(pieces, axis=0)
            e_sc[buf, (L - 1) * C:L * C, :] = jnp.exp2(-jnp.abs(d)).astype(BF16)
        b_last = b[C - 1:C, :]
        e_sc[buf, nl * C:(nl + 1) * C, :] = (jnp.exp2(b) * HG_DK ** -0.5).astype(BF16)
        e_sc[buf, (nl + 1) * C:(nl + 2) * C, :] = jnp.exp2(b_last - b).astype(BF16)
        d_sc[buf] = jnp.exp2(b_last)

    def block_diag(x):
        n, d2 = x.shape
        z = jnp.zeros((n, d2 // 2), x.dtype)
        return jnp.concatenate([jnp.concatenate([x[:, :d2 // 2], z], axis=1),
                                jnp.concatenate([z, x[:, d2 // 2:]], axis=1)], axis=0)

    W2 = 2 * HG_DK
    pair_mask = [jnp.concatenate([m_ref[L], m_ref[L]], axis=1) for L in range(nl + 1)]

    def heads_of(c, buf):
        rows = chunk_rows(c)
        pairs = [slice(p * W2, (p + 1) * W2) for p in range(heads // 2)]
        qb = [q_ref[rows, sl] for sl in pairs]
        kb = [k_ref[rows, sl] for sl in pairs]
        vb = [v_ref[rows, sl] for sl in pairs]
        A = [pair_mask[0] * _dot_nt(q, block_diag(k)) for q, k in zip(qb, kb)]
        for L in range(1, nl + 1):
            for p, sl in enumerate(pairs):
                eL = e_sc[buf, (L - 1) * C:L * C, sl]
                A[p] = A[p] + pair_mask[L] * _dot_nt(qb[p] * eL, block_diag(kb[p] * eL))
        for p, sl in enumerate(pairs):
            eb = e_sc[buf, nl * C:(nl + 1) * C, sl]
            ek = e_sc[buf, (nl + 1) * C:(nl + 2) * C, sl]
            st = [st_sc[2 * p], st_sc[2 * p + 1]]
            st_pair = jnp.concatenate([s.astype(BF16) for s in st], axis=1)
            o = (_dot(A[p].astype(BF16), block_diag(vb[p]))
                 + _dot_nt(qb[p] * eb, block_diag(st_pair)))
            upd = _dot_tn(vb[p], kb[p] * ek)
            for j in range(2):
                h = 2 * p + j
                hs = slice(h * HG_DK, (h + 1) * HG_DK)
                js = slice(j * HG_DK, (j + 1) * HG_DK)
                st_sc[h] = st[j] * d_sc[buf, :, hs] + upd[js, js]
                oj = o[:, js]
                ms = jnp.mean(oj * oj, axis=-1, keepdims=True)
                y = oj * lax.rsqrt(ms + NORM_EPS) * gain * _silu(g_ref[rows, hs].astype(F32))
                o_ref[rows, hs] = y.astype(o_ref.dtype)

    prepare(0, 0)

    def step(c, carry):
        prepare(jnp.minimum(c + 1, chunks - 1), (c + 1) & 1)
        heads_of(c, c & 1)
        return carry

    lax.fori_loop(0, chunks, step, 0)


def _hgrn(q, lf, k, v, g, gain, B, S):
    T, W = q.shape
    C = HG_CHUNK
    TS = HG_STEP_TOKENS
    assert S % TS == 0 and TS % C == 0
    N = S // TS
    masks, nl = _hgrn_tables(C)
    row = lambda b, c: (b * N + c, 0)
    tok = lambda: pl.BlockSpec((TS, W), row)
    return pl.pallas_call(
        functools.partial(_hgrn_kernel, C=C, nl=nl, heads=HG_HEADS, chunks=TS // C),
        grid=(B, N),
        in_specs=[tok(), tok(), tok(), tok(), tok(),
                  pl.BlockSpec((1, HG_DV), lambda b, c: (0, 0)),
                  pl.BlockSpec(masks.shape, lambda b, c: (0, 0, 0))],
        out_specs=tok(),
        out_shape=jax.ShapeDtypeStruct((T, W), BF16),
        scratch_shapes=[pltpu.VMEM((2, (nl + 2) * C, W), BF16),
                        pltpu.VMEM((2, 1, W), F32),
                        pltpu.VMEM((HG_HEADS, HG_DV, HG_DK), F32)],
        compiler_params=_cparams(("parallel", "arbitrary")),
        name="hgrn2",
    )(q, lf, k, v, g, gain.reshape(1, HG_DV).astype(F32),
      jnp.asarray(masks * HG_DK ** -0.5, F32))


def _t5_bucket_np(dist):
    n = np.maximum(dist, 0)
    max_exact = REL_BUCKETS // 2
    nf = np.maximum(n, 1).astype(np.float32)
    large = max_exact + (np.log(nf / max_exact) / math.log(REL_MAX_DIST / max_exact)
                         * (REL_BUCKETS - max_exact)).astype(np.int32)
    large = np.minimum(large, REL_BUCKETS - 1)
    return np.where(n < max_exact, n, large).astype(np.int32)


def _bias_kernel(tab_ref, bucket_ref, o_ref):
    h = pl.program_id(0)
    bucket = bucket_ref[...]
    acc = jnp.zeros(bucket.shape, F32)
    for b in range(REL_BUCKETS):
        acc = jnp.where(bucket == b, tab_ref[b, h], acc)
    delta = (acc - tab_ref[REL_BUCKETS - 1, h]) * LOG2E
    blk = bucket.shape[0]
    r = lax.broadcasted_iota(jnp.int32, bucket.shape, 0)
    c = lax.broadcasted_iota(jnp.int32, bucket.shape, 1)
    o_ref[0] = jnp.where(c - blk <= r, delta, MASKED_LOGIT)


def _bias_tiles(rel_table):
    BLK = MOBA_BLOCK
    t = np.arange(BLK)[:, None]
    s = np.arange(2 * BLK)[None, :]
    bucket = _t5_bucket_np(t + BLK - s)
    H = rel_table.shape[1]
    return pl.pallas_call(
        _bias_kernel,
        grid=(H,),
        in_specs=[pl.BlockSpec(memory_space=pltpu.SMEM),
                  pl.BlockSpec(bucket.shape, lambda h: (0, 0))],
        out_specs=pl.BlockSpec((1, BLK, 2 * BLK), lambda h: (h, 0, 0)),
        out_shape=jax.ShapeDtypeStruct((H, BLK, 2 * BLK), F32),
        compiler_params=_cparams(("arbitrary",)),
        name="relbias",
    )(rel_table.astype(F32), jnp.asarray(bucket))


MOBA_HEADS_PER_STEP = 2
MOBA_PAIRS_PER_STEP = 2
MOBA_BOUND_SLACK = 1.0 + 2.0 ** -6
MOBA_SAFE_GAP = 100.0


def _moba_kernel(q_ref, k_ref, v_ref, dbias_ref, far_ref, qg_ref, kg_ref, o_ref,
                 qnear_sc, qfar_sc, kaug_sc, vaug_sc, near_sc, far_sc, mx_sc, mb_sc, acc_sc,
                 safe_sc, *, NB, HPS, PPS):
    BLK = MOBA_BLOCK
    W2 = 2 * BLK
    PAD = NB + 2
    SHIFT = NB + 3
    NS = NB // 2 - 1
    hg = pl.program_id(1)
    ip = pl.program_id(2)

    def prepare(hh):
        cols = slice(hh * AT_DH, (hh + 1) * AT_DH)
        kaug, vaug, qnear, qfar = kaug_sc.at[hh], vaug_sc.at[hh], qnear_sc.at[hh], qfar_sc.at[hh]
        far = far_ref[pl.ds(hg * HPS + hh, 1), :]
        far_hi = far.astype(BF16).astype(F32)
        far_lo = far - far_hi
        lane = lax.broadcasted_iota(jnp.int32, (1, LANES), 1)
        tail = jnp.where(lane == NB, far_hi, jnp.where(lane == NB + 1, far_lo, 0.0))
        tail = jnp.where((lane == SHIFT) | (lane == SHIFT + 1), 1.0, tail)
        kaug[0:BLK, 0:AT_DH] = jnp.zeros((BLK, AT_DH), BF16)
        kaug[0:BLK, AT_DH:] = jnp.broadcast_to(
            jnp.where(lane == PAD, 1.0, 0.0), (BLK, LANES)).astype(BF16)
        vaug[0:BLK, :] = jnp.zeros((BLK, AT_DH + LANES), BF16)
        kms = []
        ones8 = jnp.ones((8, BLK), BF16)
        for n in range(NB):
            kb = k_ref[n * BLK:(n + 1) * BLK, cols]
            kms.append(_dot(ones8, kb)[0:1, :] * (1.0 / BLK))
            kaug[(n + 1) * BLK:(n + 2) * BLK, 0:AT_DH] = kb
            kaug[(n + 1) * BLK:(n + 2) * BLK, AT_DH:] = jnp.broadcast_to(
                jnp.where(lane == n, 1.0, tail), (BLK, LANES)).astype(BF16)
        vaug[BLK:, 0:AT_DH] = v_ref[:, cols]
        vaug[BLK:, AT_DH:] = jnp.ones((NB * BLK, LANES), BF16)

        qk_bound = (AT_DH * jnp.max(jnp.abs(qg_ref[...]), keepdims=True)
                    * jnp.max(jnp.abs(kg_ref[...]), keepdims=True)
                    * (AT_DH ** -0.5 * LOG2E * MOBA_BOUND_SLACK))
        bias_max = jnp.max(dbias_ref[hh], keepdims=True)[0:1, 0:1] + far[:, 0:1]
        bias_self = dbias_ref[hh][0:1, BLK:BLK + 1] + far[:, 0:1]
        shift = qk_bound + bias_max
        gap = shift + qk_bound - bias_self
        shift_hi = shift.astype(BF16).astype(F32)

        kmean = jnp.concatenate(kms, axis=0)
        km_hi = kmean.astype(BF16)
        km_lo = (kmean - km_hi.astype(F32)).astype(BF16)
        blk = lax.broadcasted_iota(jnp.int32, (NB, BLK), 0)
        rowid = lax.broadcasted_iota(jnp.int32, (LANES - NB, BLK), 0) + NB
        rest = jnp.where(rowid < NB + 2, 1.0, jnp.where(rowid == PAD, MASKED_LOGIT, 0.0))
        rest = jnp.where(rowid == SHIFT, -shift_hi,
                         jnp.where(rowid == SHIFT + 1, shift_hi - shift, rest))
        lane_q = lax.broadcasted_iota(jnp.int32, (BLK, LANES), 1)
        for t in range(NB):
            qt = q_ref[t * BLK:(t + 1) * BLK, cols]
            gate = jnp.where(blk < t, _dot_nt(km_hi, qt) + _dot_nt(km_lo, qt), -jnp.inf)
            rank = jnp.zeros((NB, BLK), jnp.int32)
            for m in range(t):
                gm = gate[m:m + 1, :]
                rank = rank + ((gm > gate) | ((gm == gate) & (blk > m))).astype(jnp.int32)
            sel = ((blk < t) & (rank < MOBA_TOPK)) | (blk == t)
            near = jnp.concatenate([jnp.where(sel, 0.0, MASKED_LOGIT), rest], axis=0).T
            older = jnp.where((lane_q == t) | (lane_q == t - 1), MASKED_LOGIT, near)
            rows = slice(t * BLK, (t + 1) * BLK)
            qnear[rows, 0:AT_DH] = qt
            qfar[rows, 0:AT_DH] = qt
            qnear[rows, AT_DH:] = near.astype(BF16)
            qfar[rows, AT_DH:] = older.astype(BF16)
        return jnp.max(gap)

    @pl.when(ip == 0)
    def _():
        worst = prepare(0)
        for hh in range(1, HPS):
            worst = jnp.maximum(worst, prepare(hh))
        safe_sc[0] = (worst <= MOBA_SAFE_GAP).astype(jnp.int32)

    def lane_max(s):
        out = s[:, 0:LANES]
        for c in range(1, s.shape[1] // LANES):
            out = jnp.maximum(out, s[:, c * LANES:(c + 1) * LANES])
        return out

    def probs(s, m_b):
        return jnp.exp2(s - jnp.concatenate([m_b] * (W2 // LANES), axis=1)).astype(BF16)

    def plan(j):
        rows = [pl.multiple_of(b * BLK, BLK) for b in (j, NB - 1 - j)]
        n_first = jnp.maximum(j, 1) // 2
        visits = []
        for st in range(NS):
            second = st >= n_first
            jj = jnp.where(second, st - n_first, st)
            visits.append((second.astype(jnp.int32), jnp.where(second, rows[1], rows[0]),
                           pl.multiple_of(jj * W2 + BLK, BLK)))
        return rows, visits

    plans = [plan(ip * PPS + pp) for pp in range(PPS)]

    def one_pass():
        for pp, (rows, visits) in enumerate(plans):
            for hh in range(HPS):
                for slot in range(2):
                    s = (_dot_nt(qnear_sc[hh, pl.ds(rows[slot], BLK), :],
                                 kaug_sc[hh, pl.ds(rows[slot], W2), :]) + dbias_ref[hh])
                    acc_sc[pp, hh, slot] = _dot(jnp.exp2(s).astype(BF16),
                                                vaug_sc[hh, pl.ds(rows[slot], W2), :])
                for st, (slot, qrow, krow) in enumerate(visits):
                    s = _dot_nt(qfar_sc[hh, pl.ds(qrow, BLK), :],
                                kaug_sc[hh, pl.ds(krow, W2), :])
                    acc_sc[pp, hh, slot] += _dot(jnp.exp2(s).astype(BF16),
                                                 vaug_sc[hh, pl.ds(krow, W2), :])

    def two_pass():
        for pp, (rows, visits) in enumerate(plans):
            for hh in range(HPS):
                for slot in range(2):
                    near_sc[hh, slot] = (_dot_nt(qnear_sc[hh, pl.ds(rows[slot], BLK), :],
                                                 kaug_sc[hh, pl.ds(rows[slot], W2), :])
                                         + dbias_ref[hh])
                for st, (slot, qrow, krow) in enumerate(visits):
                    far_sc[hh, st] = _dot_nt(qfar_sc[hh, pl.ds(qrow, BLK), :],
                                             kaug_sc[hh, pl.ds(krow, W2), :])
            mx_sc[...] = jnp.full(mx_sc.shape, MASKED_LOGIT, F32)
            for hh in range(HPS):
                for st, (slot, qrow, krow) in enumerate(visits):
                    mx_sc[hh, slot * NS + st] = lane_max(far_sc[hh, st])
                for slot in range(2):
                    m = lane_max(near_sc[hh, slot])
                    for st in range(NS):
                        m = jnp.maximum(m, mx_sc[hh, slot * NS + st])
                    mb_sc[hh, slot] = jnp.broadcast_to(jnp.max(m, axis=-1, keepdims=True),
                                                       (BLK, LANES))
            for hh in range(HPS):
                for slot in range(2):
                    acc_sc[pp, hh, slot] = _dot(probs(near_sc[hh, slot], mb_sc[hh, slot]),
                                                vaug_sc[hh, pl.ds(rows[slot], W2), :])
                for st, (slot, qrow, krow) in enumerate(visits):
                    acc_sc[pp, hh, slot] += _dot(probs(far_sc[hh, st], mb_sc[hh, slot]),
                                                 vaug_sc[hh, pl.ds(krow, W2), :])

    lax.cond(safe_sc[0] != 0, one_pass, two_pass)

    for pp, (rows, visits) in enumerate(plans):
        for hh in range(HPS):
            for slot in range(2):
                acc = acc_sc[pp, hh, slot]
                o_ref[pl.ds(rows[slot], BLK), hh * AT_DH:(hh + 1) * AT_DH] = (
                    acc[:, 0:AT_DH] / acc[:, AT_DH:]).astype(o_ref.dtype)


def _moba(q, k, v, rel_table, qg, kg, B, S):
    T, W = q.shape
    H = AT_HEADS
    HPS = MOBA_HEADS_PER_STEP
    PPS = MOBA_PAIRS_PER_STEP
    NB = S // MOBA_BLOCK
    BLK = MOBA_BLOCK
    NS = max(NB // 2 - 1, 1)
    assert NB + 5 <= LANES and NB % (2 * PPS) == 0 and H % HPS == 0
    assert int(_t5_bucket_np(np.array([BLK + 1]))[0]) == REL_BUCKETS - 1
    dbias = _bias_tiles(rel_table)
    far = jnp.broadcast_to((rel_table[REL_BUCKETS - 1, :].astype(F32) * LOG2E)[:, None],
                           (H, LANES))
    heads = lambda: pl.BlockSpec((S, HPS * AT_DH), lambda b, g, i: (b, g))
    aug = AT_DH + LANES
    return pl.pallas_call(
        functools.partial(_moba_kernel, NB=NB, HPS=HPS, PPS=PPS),
        grid=(B, H // HPS, NB // (2 * PPS)),
        in_specs=[heads(), heads(), heads(),
                  pl.BlockSpec((HPS, BLK, 2 * BLK), lambda b, g, i: (g, 0, 0)),
                  pl.BlockSpec((H, LANES), lambda b, g, i: (0, 0)),
                  pl.BlockSpec((1, AT_DH), lambda b, g, i: (0, 0)),
                  pl.BlockSpec((1, AT_DH), lambda b, g, i: (0, 0))],
        out_specs=heads(),
        out_shape=jax.ShapeDtypeStruct((T, W), BF16),
        scratch_shapes=[pltpu.VMEM((HPS, S, aug), BF16),
                        pltpu.VMEM((HPS, S, aug), BF16),
                        pltpu.VMEM((HPS, S + BLK, aug), BF16),
                        pltpu.VMEM((HPS, S + BLK, aug), BF16),
                        pltpu.VMEM((HPS, 2, BLK, 2 * BLK), F32),
                        pltpu.VMEM((HPS, NS, BLK, 2 * BLK), F32),
                        pltpu.VMEM((HPS, 2 * NS, BLK, LANES), F32),
                        pltpu.VMEM((HPS, 2, BLK, LANES), F32),
                        pltpu.VMEM((PPS, HPS, 2, BLK, aug), F32),
                        pltpu.SMEM((1,), jnp.int32)],
        compiler_params=_cparams(("parallel", "parallel", "arbitrary")),
        name="moba",
    )(q, k, v, dbias, far, qg.reshape(1, AT_DH).astype(F32), kg.reshape(1, AT_DH).astype(F32))


def _merge_kernel(ohg_ref, oat_ref, gate_ref, x_ref, whg_ref, wat_ref, wout_ref, g2_ref,
                  x1_ref, h2_ref, *, D, sub):
    for r0 in range(0, x_ref.shape[0], sub):
        rows = slice(r0, r0 + sub)
        y_hg = _dot(ohg_ref[rows, :], whg_ref[...])
        y_at = _dot(oat_ref[rows, :], wat_ref[...])
        merged = (gate_ref[rows, 0:D].astype(F32) * y_hg
                  + gate_ref[rows, D:2 * D].astype(F32) * y_at)
        x1 = x_ref[rows, :] + _dot(merged.astype(BF16), wout_ref[...])
        x1_ref[rows, :] = x1
        ms = jnp.mean(x1 * x1, axis=-1, keepdims=True)
        h2_ref[rows, :] = (x1 * lax.rsqrt(ms + NORM_EPS) * g2_ref[...]).astype(h2_ref.dtype)


def _merge(ohg, oat, gates, x, whg, wat, wout, g2, tm=1024, sub=512):
    T, D = x.shape
    tok = lambda w: pl.BlockSpec((tm, w), lambda i: (i, 0))
    return pl.pallas_call(
        functools.partial(_merge_kernel, D=D, sub=sub),
        grid=(T // tm,),
        in_specs=[tok(ohg.shape[1]), tok(oat.shape[1]), tok(2 * D), tok(D),
                  _resident(whg), _resident(wat), _resident(wout),
                  pl.BlockSpec((1, D), lambda i: (0, 0))],
        out_specs=[tok(D), tok(D)],
        out_shape=[jax.ShapeDtypeStruct((T, D), F32), jax.ShapeDtypeStruct((T, D), BF16)],
        compiler_params=_cparams(("parallel",)),
        name="merge",
    )(ohg, oat, gates, x, whg, wat, wout, g2.reshape(1, D).astype(F32))


def _ffn_kernel(h_ref, x_ref, wg_ref, wu_ref, wd_ref, o_ref, *, tf):
    h = h_ref[...]
    acc = x_ref[...]
    for f0 in range(0, wg_ref.shape[1], tf):
        a = _dot(h, wg_ref[:, f0:f0 + tf])
        u = _dot(h, wu_ref[:, f0:f0 + tf])
        acc = acc + _dot((_silu(a) * u).astype(BF16), wd_ref[f0:f0 + tf, :])
    o_ref[...] = acc


def _ffn(h2, x1, wg, wu, wd, tm=512, tf=2816):
    T, D = x1.shape
    FF = wg.shape[1]
    assert FF % tf == 0
    tok = lambda: pl.BlockSpec((tm, D), lambda i: (i, 0))
    return pl.pallas_call(
        functools.partial(_ffn_kernel, tf=tf),
        grid=(T // tm,),
        in_specs=[tok(), tok(), _resident(wg), _resident(wu), _resident(wd)],
        out_specs=tok(),
        out_shape=jax.ShapeDtypeStruct((T, D), F32),
        compiler_params=_cparams(("parallel",)),
        name="ffn",
    )(h2, x1, wg, wu, wd)


def kernel(x, attn_norm_g, w_in, hg_lb_gamma, hg_out_norm_g, q_norm_g, k_norm_g, rel_bias_table,
           w_branch_hg, w_branch_attn, w_out, ffn_norm_g, w_ffn_gate, w_ffn_up, w_ffn_down):
    B, S, D = x.shape
    T = B * S
    depth = attn_norm_g.shape[0]
    assert depth == 1 and S % MOBA_BLOCK == 0 and S % HG_CHUNK == 0
    WH = HG_HEADS * HG_DK
    WV = HG_HEADS * HG_DV
    WA = AT_HEADS * AT_DH
    assert w_in.shape[2] == 2 * WH + 2 * WV + 3 * WA + 2 * D

    xt = x.reshape(T, D)
    for l in range(depth):
        hq, lf, hk, hi, hg, aq, ak, av, gates = _inproj(
            xt, attn_norm_g[l], w_in[l].astype(BF16), hg_lb_gamma, q_norm_g[l], k_norm_g[l],
            WH, WV, WA)
        o_hg = _hgrn(hq, lf, hk, hi, hg, hg_out_norm_g[l], B, S)
        o_at = _moba(aq, ak, av, rel_bias_table, q_norm_g[l], k_norm_g[l], B, S)
        x1, h2 = _merge(o_hg, o_at, gates, xt, w_branch_hg[l].astype(BF16),
                        w_branch_attn[l].astype(BF16), w_out[l].astype(BF16), ffn_norm_g[l])
        xt = _ffn(h2, x1, w_ffn_gate[l].astype(BF16), w_ffn_up[l].astype(BF16),
                  w_ffn_down[l].astype(BF16))
    return xt.reshape(B, S, D)
```

```python
import functools
import math

import numpy as np
import jax
import jax.numpy as jnp
from jax import lax
from jax.experimental import pallas as pl
from jax.experimental.pallas import tpu as pltpu

F32 = jnp.float32
BF16 = jnp.bfloat16

LANES = 128
NORM_EPS = 1e-6

HG_HEADS = 8
HG_DK = 128
HG_DV = 128
HG_CHUNK = 64
HG_STEP_TOKENS = 512
HG_HEAD_GROUP = 8
AT_HEADS = 8
AT_DH = 128
MOBA_BLOCK = 256
MOBA_TOPK = 3
REL_BUCKETS = 32
REL_MAX_DIST = 128
MASKED_LOGIT = -1e30
LOG2E = math.log2(math.e)

VMEM_LIMIT = 56 * 1024 * 1024


def _cparams(sem):
    return pltpu.CompilerParams(dimension_semantics=sem, vmem_limit_bytes=VMEM_LIMIT)


def _dot(a, b):
    return jnp.dot(a, b, preferred_element_type=F32)


def _dot_nt(a, b):
    return lax.dot_general(a, b, (((1,), (1,)), ((), ())), preferred_element_type=F32)


def _dot_tn(a, b):
    return lax.dot_general(a, b, (((0,), (0,)), ((), ())), preferred_element_type=F32)


def _resident(a):
    return pl.BlockSpec(a.shape, lambda *_: (0,) * a.ndim, pipeline_mode=pl.Buffered(1))


def _sigmoid(x):
    return 0.5 * jnp.tanh(0.5 * x) + 0.5


def _silu(x):
    h = 0.5 * x
    return h + h * jnp.tanh(h)


def _head_rmsnorm(acc, gain, scale):
    cols = []
    for c in range(acc.shape[1] // LANES):
        blk = acc[:, c * LANES:(c + 1) * LANES]
        ms = jnp.mean(blk * blk, axis=-1, keepdims=True)
        cols.append(blk * lax.rsqrt(ms + NORM_EPS) * (gain * scale))
    return jnp.concatenate(cols, axis=1)


def _inproj_kernel(x_ref, g_ref, w_ref, gamma_ref, qg_ref, kg_ref,
                   hq_ref, lf_ref, hk_ref, hi_ref, hg_ref, aq_ref, ak_ref, av_ref, gate_ref,
                   *, WH, WV, WA):
    x = x_ref[...]
    ms = jnp.mean(x * x, axis=-1, keepdims=True)
    h = (x * lax.rsqrt(ms + NORM_EPS) * g_ref[...]).astype(BF16)
    D = x.shape[1]
    starts = np.cumsum([0, WH, WH, WV, WV, WA, WA, WA])

    def segment(idx, width, off=0):
        c0 = int(starts[idx]) + off
        return _dot(h, w_ref[:, c0:c0 + width])

    for half in range(2):
        gate_ref[:, half * D:(half + 1) * D] = _sigmoid(
            segment(7, D, half * D)).astype(gate_ref.dtype)

    gamma = gamma_ref[...]
    eg = jnp.exp(gamma - jnp.max(gamma, axis=0, keepdims=True))
    lb = eg[0:1, :] / jnp.sum(eg, axis=0, keepdims=True)
    c0 = 0.5 + 0.5 * lb
    c1 = 0.5 - 0.5 * lb
    ct = c1 * jnp.tanh(0.5 * segment(1, WH))
    lf_ref[...] = jnp.log2(c0 + ct)
    hk_ref[...] = (c1 - ct).astype(hk_ref.dtype)

    hg_ref[...] = _silu(segment(3, WV)).astype(hg_ref.dtype)
    aq_ref[...] = _head_rmsnorm(segment(4, WA), qg_ref[...],
                                AT_DH ** -0.5 * LOG2E).astype(aq_ref.dtype)
    ak_ref[...] = _head_rmsnorm(segment(5, WA), kg_ref[...], 1.0).astype(ak_ref.dtype)
    hq_ref[...] = (segment(0, WH) * HG_DK ** -0.5).astype(hq_ref.dtype)
    hi_ref[...] = segment(2, WV).astype(hi_ref.dtype)
    av_ref[...] = segment(6, WA).astype(av_ref.dtype)


def _inproj(x, g, w, gamma, qg, kg, WH, WV, WA, tm=512):
    T, D = x.shape
    tok = lambda width: pl.BlockSpec((tm, width), lambda i: (i, 0))
    small = lambda a: pl.BlockSpec(a.shape, lambda i: (0, 0))
    widths = [WH, WH, WH, WV, WV, WA, WA, WA, 2 * D]
    dtypes = [BF16, F32, BF16, BF16, BF16, BF16, BF16, BF16, BF16]
    g = g.reshape(1, D).astype(F32)
    gamma = gamma.astype(F32)
    qg = qg.reshape(1, AT_DH).astype(F32)
    kg = kg.reshape(1, AT_DH).astype(F32)
    return pl.pallas_call(
        functools.partial(_inproj_kernel, WH=WH, WV=WV, WA=WA),
        grid=(T // tm,),
        in_specs=[tok(D), small(g), _resident(w), small(gamma), small(qg), small(kg)],
        out_specs=[tok(wd) for wd in widths],
        out_shape=[jax.ShapeDtypeStruct((T, wd), dt) for wd, dt in zip(widths, dtypes)],
        compiler_params=_cparams(("parallel",)),
        name="inproj",
    )(x, g, w, gamma, qg, kg)


def _hgrn_tables(C):
    nl = int(math.log2(C))
    assert 1 << nl == C and nl >= 3
    masks = np.zeros((nl + 1, C, C), np.float32)
    masks[0] = np.eye(C)
    t = np.arange(C)
    for L in range(1, nl + 1):
        blk, half = 1 << L, 1 << (L - 1)
        base = (t // blk) * blk
        upper = (t - base) >= half
        same = base[:, None] == base[None, :]
        masks[L] = (same & upper[:, None] & (~upper)[None, :]).astype(np.float32)
    return masks, nl


def _hgrn_kernel(q_ref, lf_ref, k_ref, v_ref, g_ref, gain_ref, m_ref, o_ref,
                 e_sc, d_sc, st_sc, *, C, nl, heads, chunks):
    @pl.when(pl.program_id(1) == 0)
    def _():
        st_sc[...] = jnp.zeros_like(st_sc)

    gain = gain_ref[...]
    scan_shifts = [1 << j for j in range(nl)]
    row = lax.broadcasted_iota(jnp.int32, (C, lf_ref.shape[1]), 0)
    odd = (row & 1) == 1
    ph = row & 3

    def chunk_rows(c):
        return pl.ds(pl.multiple_of(c * C, C), C)

    def prepare(c, buf):
        lf = lf_ref[chunk_rows(c), :]
        b = lf
        for sh in scan_shifts:
            b = b + jnp.where(row >= sh, pltpu.roll(b, sh, axis=0), 0.0)
        e_sc[buf, 0:C, :] = jnp.exp2(jnp.where(odd, lf, 0.0)).astype(BF16)
        prev = pltpu.roll(lf, 1, axis=0)
        nxt = pltpu.roll(lf, C - 1, axis=0)
        x2 = jnp.where(ph == 0, nxt, jnp.where(ph == 1, 0.0, jnp.where(ph == 2, lf, lf + prev)))
        e_sc[buf, C:2 * C, :] = jnp.exp2(x2).astype(BF16)
        for L in range(3, nl + 1):
            blk, half = 1 << L, 1 << (L - 1)
            pieces = [b[base:base + blk, :] - b[base + half - 1:base + half, :]
                      for base in range(0, C, blk)]
            d = pieces[0] if len(pieces) == 1 else jnp.concatenate(pieces, axis=0)
            e_sc[buf, (L - 1) * C:L * C, :] = jnp.exp2(-jnp.abs(d)).astype(BF16)
        b_last = b[C - 1:C, :]
        e_sc[buf, nl * C:(nl + 1) * C, :] = jnp.exp2(b).astype(BF16)
        e_sc[buf, (nl + 1) * C:(nl + 2) * C, :] = jnp.exp2(b_last - b).astype(BF16)
        d_sc[buf] = jnp.exp2(b_last)

    def block_diag(x):
        n, d2 = x.shape
        z = jnp.zeros((n, d2 // 2), x.dtype)
        return jnp.concatenate([jnp.concatenate([x[:, :d2 // 2], z], axis=1),
                                jnp.concatenate([z, x[:, d2 // 2:]], axis=1)], axis=0)

    W2 = 2 * HG_DK
    pair_mask = [jnp.concatenate([m_ref[L], m_ref[L]], axis=1) for L in range(nl + 1)]

    def heads_of(c, buf):
        rows = chunk_rows(c)
        pairs = [slice(p * W2, (p + 1) * W2) for p in range(heads // 2)]
        qb = [q_ref[rows, sl] for sl in pairs]
        kb = [k_ref[rows, sl] for sl in pairs]
        vb = [v_ref[rows, sl] for sl in pairs]
        A = [pair_mask[0] * _dot_nt(q, block_diag(k)) for q, k in zip(qb, kb)]
        for L in range(1, nl + 1):
            for p, sl in enumerate(pairs):
                eL = e_sc[buf, (L - 1) * C:L * C, sl]
                A[p] = A[p] + pair_mask[L] * _dot_nt(qb[p] * eL, block_diag(kb[p] * eL))
        for p, sl in enumerate(pairs):
            eb = e_sc[buf, nl * C:(nl + 1) * C, sl]
            ek = e_sc[buf, (nl + 1) * C:(nl + 2) * C, sl]
            st = [st_sc[2 * p], st_sc[2 * p + 1]]
            st_pair = jnp.concatenate([s.astype(BF16) for s in st], axis=1)
            o = (_dot(A[p].astype(BF16), block_diag(vb[p]))
                 + _dot_nt(qb[p] * eb, block_diag(st_pair)))
            upd = _dot_tn(vb[p], kb[p] * ek)
            for j in range(2):
                h = 2 * p + j
                hs = slice(h * HG_DK, (h + 1) * HG_DK)
                js = slice(j * HG_DK, (j + 1) * HG_DK)
                st_sc[h] = st[j] * d_sc[buf, :, hs] + upd[js, js]
                oj = o[:, js]
                ms = jnp.mean(oj * oj, axis=-1, keepdims=True)
                y = oj * lax.rsqrt(ms + NORM_EPS) * gain * g_ref[rows, hs].astype(F32)
                o_ref[rows, hs] = y.astype(o_ref.dtype)

    prepare(0, 0)

    def step(j, carry):
        c = 2 * j
        prepare(c + 1, 1)
        heads_of(c, 0)
        prepare(jnp.minimum(c + 2, chunks - 1), 0)
        heads_of(c + 1, 1)
        return carry

    assert chunks % 2 == 0
    lax.fori_loop(0, chunks // 2, step, 0)


def _hgrn(q, lf, k, v, g, gain, B, S):
    T, W = q.shape
    C = HG_CHUNK
    TS = HG_STEP_TOKENS
    assert S % TS == 0 and TS % C == 0
    N = S // TS
    masks, nl = _hgrn_tables(C)
    row = lambda b, c: (b * N + c, 0)
    tok = lambda: pl.BlockSpec((TS, W), row)
    return pl.pallas_call(
        functools.partial(_hgrn_kernel, C=C, nl=nl, heads=HG_HEADS, chunks=TS // C),
        grid=(B, N),
        in_specs=[tok(), tok(), tok(), tok(), tok(),
                  pl.BlockSpec((1, HG_DV), lambda b, c: (0, 0)),
                  pl.BlockSpec(masks.shape, lambda b, c: (0, 0, 0))],
        out_specs=tok(),
        out_shape=jax.ShapeDtypeStruct((T, W), BF16),
        scratch_shapes=[pltpu.VMEM((2, (nl + 2) * C, W), BF16),
                        pltpu.VMEM((2, 1, W), F32),
                        pltpu.VMEM((HG_HEADS, HG_DV, HG_DK), F32)],
        compiler_params=_cparams(("parallel", "arbitrary")),
        name="hgrn2",
    )(q, lf, k, v, g, gain.reshape(1, HG_DV).astype(F32), jnp.asarray(masks, F32))


def _t5_bucket_np(dist):
    n = np.maximum(dist, 0)
    max_exact = REL_BUCKETS // 2
    nf = np.maximum(n, 1).astype(np.float32)
    large = max_exact + (np.log(nf / max_exact) / math.log(REL_MAX_DIST / max_exact)
                         * (REL_BUCKETS - max_exact)).astype(np.int32)
    large = np.minimum(large, REL_BUCKETS - 1)
    return np.where(n < max_exact, n, large).astype(np.int32)


def _bias_kernel(tab_ref, bucket_ref, o_ref):
    h = pl.program_id(0)
    bucket = bucket_ref[...]
    acc = jnp.zeros(bucket.shape, F32)
    for b in range(REL_BUCKETS):
        acc = jnp.where(bucket == b, tab_ref[b, h], acc)
    delta = (acc - tab_ref[REL_BUCKETS - 1, h]) * LOG2E
    blk = bucket.shape[0]
    r = lax.broadcasted_iota(jnp.int32, bucket.shape, 0)
    c = lax.broadcasted_iota(jnp.int32, bucket.shape, 1)
    o_ref[0] = jnp.where(c - blk <= r, delta, MASKED_LOGIT)


def _bias_tiles(rel_table):
    BLK = MOBA_BLOCK
    t = np.arange(BLK)[:, None]
    s = np.arange(2 * BLK)[None, :]
    bucket = _t5_bucket_np(t + BLK - s)
    H = rel_table.shape[1]
    return pl.pallas_call(
        _bias_kernel,
        grid=(H,),
        in_specs=[pl.BlockSpec(memory_space=pltpu.SMEM),
                  pl.BlockSpec(bucket.shape, lambda h: (0, 0))],
        out_specs=pl.BlockSpec((1, BLK, 2 * BLK), lambda h: (h, 0, 0)),
        out_shape=jax.ShapeDtypeStruct((H, BLK, 2 * BLK), F32),
        compiler_params=_cparams(("arbitrary",)),
        name="relbias",
    )(rel_table.astype(F32), jnp.asarray(bucket))


MOBA_HEADS_PER_STEP = 2
MOBA_PAIRS_PER_STEP = 2
MOBA_BOUND_SLACK = 1.0 + 2.0 ** -6
MOBA_SAFE_GAP = 100.0


def _moba_kernel(q_ref, k_ref, v_ref, dbias_ref, far_ref, qg_ref, kg_ref, o_ref,
                 qnear_sc, qfar_sc, kaug_sc, vaug_sc, near_sc, far_sc, mx_sc, mb_sc, acc_sc,
                 safe_sc, *, NB, HPS, PPS):
    BLK = MOBA_BLOCK
    W2 = 2 * BLK
    PAD = NB + 2
    SHIFT = NB + 3
    NS = NB // 2 - 1
    hg = pl.program_id(1)
    ip = pl.program_id(2)

    def prepare(hh):
        cols = slice(hh * AT_DH, (hh + 1) * AT_DH)
        kaug, vaug, qnear, qfar = kaug_sc.at[hh], vaug_sc.at[hh], qnear_sc.at[hh], qfar_sc.at[hh]
        far = far_ref[pl.ds(hg * HPS + hh, 1), :]
        far_hi = far.astype(BF16).astype(F32)
        far_lo = far - far_hi
        lane = lax.broadcasted_iota(jnp.int32, (1, LANES), 1)
        tail = jnp.where(lane == NB, far_hi, jnp.where(lane == NB + 1, far_lo, 0.0))
        tail = jnp.where((lane == SHIFT) | (lane == SHIFT + 1), 1.0, tail)
        kaug[0:BLK, 0:AT_DH] = jnp.zeros((BLK, AT_DH), BF16)
        kaug[0:BLK, AT_DH:] = jnp.broadcast_to(
            jnp.where(lane == PAD, 1.0, 0.0), (BLK, LANES)).astype(BF16)
        vaug[0:BLK, :] = jnp.zeros((BLK, AT_DH + LANES), BF16)
        kms = []
        ones8 = jnp.ones((8, BLK), BF16)
        for n in range(NB):
            kb = k_ref[n * BLK:(n + 1) * BLK, cols]
            kms.append(_dot(ones8, kb)[0:1, :] * (1.0 / BLK))
            kaug[(n + 1) * BLK:(n + 2) * BLK, 0:AT_DH] = kb
            kaug[(n + 1) * BLK:(n + 2) * BLK, AT_DH:] = jnp.broadcast_to(
                jnp.where(lane == n, 1.0, tail), (BLK, LANES)).astype(BF16)
        vaug[BLK:, 0:AT_DH] = v_ref[:, cols]
        vaug[BLK:, AT_DH:] = jnp.ones((NB * BLK, LANES), BF16)

        qk_bound = (AT_DH * jnp.max(jnp.abs(qg_ref[...]), keepdims=True)
                    * jnp.max(jnp.abs(kg_ref[...]), keepdims=True)
                    * (AT_DH ** -0.5 * LOG2E * MOBA_BOUND_SLACK))
        bias_max = jnp.max(dbias_ref[hh], keepdims=True)[0:1, 0:1] + far[:, 0:1]
        bias_self = dbias_ref[hh][0:1, BLK:BLK + 1] + far[:, 0:1]
        shift = qk_bound + bias_max
        gap = shift + qk_bound - bias_self
        shift_hi = shift.astype(BF16).astype(F32)

        kmean = jnp.concatenate(kms, axis=0)
        km_hi = kmean.astype(BF16)
        km_lo = (kmean - km_hi.astype(F32)).astype(BF16)
        blk = lax.broadcasted_iota(jnp.int32, (NB, BLK), 0)
        rowid = lax.broadcasted_iota(jnp.int32, (LANES - NB, BLK), 0) + NB
        rest = jnp.where(rowid < NB + 2, 1.0, jnp.where(rowid == PAD, MASKED_LOGIT, 0.0))
        rest = jnp.where(rowid == SHIFT, -shift_hi,
                         jnp.where(rowid == SHIFT + 1, shift_hi - shift, rest))
        lane_q = lax.broadcasted_iota(jnp.int32, (BLK, LANES), 1)
        for t in range(NB):
            qt = q_ref[t * BLK:(t + 1) * BLK, cols]
            gate = jnp.where(blk < t, _dot_nt(km_hi, qt) + _dot_nt(km_lo, qt), -jnp.inf)
            rank = jnp.zeros((NB, BLK), jnp.int32)
            for m in range(t):
                gm = gate[m:m + 1, :]
                rank = rank + ((gm > gate) | ((gm == gate) & (blk > m))).astype(jnp.int32)
            sel = ((blk < t) & (rank < MOBA_TOPK)) | (blk == t)
            near = jnp.concatenate([jnp.where(sel, 0.0, MASKED_LOGIT), rest], axis=0).T
            older = jnp.where((lane_q == t) | (lane_q == t - 1), MASKED_LOGIT, near)
            rows = slice(t * BLK, (t + 1) * BLK)
            qnear[rows, 0:AT_DH] = qt
            qfar[rows, 0:AT_DH] = qt
            qnear[rows, AT_DH:] = near.astype(BF16)
            qfar[rows, AT_DH:] = older.astype(BF16)
        return jnp.max(gap)

    @pl.when(ip == 0)
    def _():
        worst = prepare(0)
        for hh in range(1, HPS):
            worst = jnp.maximum(worst, prepare(hh))
        safe_sc[0] = (worst <= MOBA_SAFE_GAP).astype(jnp.int32)

    def lane_max(s):
        out = s[:, 0:LANES]
        for c in range(1, s.shape[1] // LANES):
            out = jnp.maximum(out, s[:, c * LANES:(c + 1) * LANES])
        return out

    def probs(s, m_b):
        return jnp.exp2(s - jnp.concatenate([m_b] * (W2 // LANES), axis=1)).astype(BF16)

    def plan(j):
        rows = [pl.multiple_of(b * BLK, BLK) for b in (j, NB - 1 - j)]
        n_first = jnp.maximum(j, 1) // 2
        visits = []
        for st in range(NS):
            second = st >= n_first
            jj = jnp.where(second, st - n_first, st)
            visits.append((second.astype(jnp.int32), jnp.where(second, rows[1], rows[0]),
                           pl.multiple_of(jj * W2 + BLK, BLK)))
        return rows, visits

    plans = [plan(ip * PPS + pp) for pp in range(PPS)]

    def one_pass():
        for pp, (rows, visits) in enumerate(plans):
            for hh in range(HPS):
                for slot in range(2):
                    s = (_dot_nt(qnear_sc[hh, pl.ds(rows[slot], BLK), :],
                                 kaug_sc[hh, pl.ds(rows[slot], W2), :]) + dbias_ref[hh])
                    acc_sc[pp, hh, slot] = _dot(jnp.exp2(s).astype(BF16),
                                                vaug_sc[hh, pl.ds(rows[slot], W2), :])
                for st, (slot, qrow, krow) in enumerate(visits):
                    s = _dot_nt(qfar_sc[hh, pl.ds(qrow, BLK), :],
                                kaug_sc[hh, pl.ds(krow, W2), :])
                    acc_sc[pp, hh, slot] += _dot(jnp.exp2(s).astype(BF16),
                                                 vaug_sc[hh, pl.ds(krow, W2), :])

    def two_pass():
        for pp, (rows, visits) in enumerate(plans):
            for hh in range(HPS):
                for slot in range(2):
                    near_sc[hh, slot] = (_dot_nt(qnear_sc[hh, pl.ds(rows[slot], BLK), :],
                                                 kaug_sc[hh, pl.ds(rows[slot], W2), :])
                                         + dbias_ref[hh])
                for st, (slot, qrow, krow) in enumerate(visits):
                    far_sc[hh, st] = _dot_nt(qfar_sc[hh, pl.ds(qrow, BLK), :],
                                             kaug_sc[hh, pl.ds(krow, W2), :])
            mx_sc[...] = jnp.full(mx_sc.shape, MASKED_LOGIT, F32)
            for hh in range(HPS):
                for st, (slot, qrow, krow) in enumerate(visits):
                    mx_sc[hh, slot * NS + st] = lane_max(far_sc[hh, st])
                for slot in range(2):
                    m = lane_max(near_sc[hh, slot])
                    for st in range(NS):
                        m = jnp.maximum(m, mx_sc[hh, slot * NS + st])
                    mb_sc[hh, slot] = jnp.broadcast_to(jnp.max(m, axis=-1, keepdims=True),
                                                       (BLK, LANES))
            for hh in range(HPS):
                for slot in range(2):
                    acc_sc[pp, hh, slot] = _dot(probs(near_sc[hh, slot], mb_sc[hh, slot]),
                                                vaug_sc[hh, pl.ds(rows[slot], W2), :])
                for st, (slot, qrow, krow) in enumerate(visits):
                    acc_sc[pp, hh, slot] += _dot(probs(far_sc[hh, st], mb_sc[hh, slot]),
                                                 vaug_sc[hh, pl.ds(krow, W2), :])

    lax.cond(safe_sc[0] != 0, one_pass, two_pass)

    for pp, (rows, visits) in enumerate(plans):
        for hh in range(HPS):
            for slot in range(2):
                acc = acc_sc[pp, hh, slot]
                o_ref[pl.ds(rows[slot], BLK), hh * AT_DH:(hh + 1) * AT_DH] = (
                    acc[:, 0:AT_DH] / acc[:, AT_DH:]).astype(o_ref.dtype)


def _moba(q, k, v, rel_table, qg, kg, B, S):
    T, W = q.shape
    H = AT_HEADS
    HPS = MOBA_HEADS_PER_STEP
    PPS = MOBA_PAIRS_PER_STEP
    NB = S // MOBA_BLOCK
    BLK = MOBA_BLOCK
    NS = max(NB // 2 - 1, 1)
    assert NB + 5 <= LANES and NB % (2 * PPS) == 0 and H % HPS == 0
    assert int(_t5_bucket_np(np.array([BLK + 1]))[0]) == REL_BUCKETS - 1
    dbias = _bias_tiles(rel_table)
    far = jnp.broadcast_to((rel_table[REL_BUCKETS - 1, :].astype(F32) * LOG2E)[:, None],
                           (H, LANES))
    heads = lambda: pl.BlockSpec((S, HPS * AT_DH), lambda b, g, i: (b, g))
    aug = AT_DH + LANES
    return pl.pallas_call(
        functools.partial(_moba_kernel, NB=NB, HPS=HPS, PPS=PPS),
        grid=(B, H // HPS, NB // (2 * PPS)),
        in_specs=[heads(), heads(), heads(),
                  pl.BlockSpec((HPS, BLK, 2 * BLK), lambda b, g, i: (g, 0, 0)),
                  pl.BlockSpec((H, LANES), lambda b, g, i: (0, 0)),
                  pl.BlockSpec((1, AT_DH), lambda b, g, i: (0, 0)),
                  pl.BlockSpec((1, AT_DH), lambda b, g, i: (0, 0))],
        out_specs=heads(),
        out_shape=jax.ShapeDtypeStruct((T, W), BF16),
        scratch_shapes=[pltpu.VMEM((HPS, S, aug), BF16),
                        pltpu.VMEM((HPS, S, aug), BF16),
                        pltpu.VMEM((HPS, S + BLK, aug), BF16),
                        pltpu.VMEM((HPS, S + BLK, aug), BF16),
                        pltpu.VMEM((HPS, 2, BLK, 2 * BLK), F32),
                        pltpu.VMEM((HPS, NS, BLK, 2 * BLK), F32),
                        pltpu.VMEM((HPS, 2 * NS, BLK, LANES), F32),
                        pltpu.VMEM((HPS, 2, BLK, LANES), F32),
                        pltpu.VMEM((PPS, HPS, 2, BLK, aug), F32),
                        pltpu.SMEM((1,), jnp.int32)],
        compiler_params=_cparams(("parallel", "parallel", "arbitrary")),
        name="moba",
    )(q, k, v, dbias, far, qg.reshape(1, AT_DH).astype(F32), kg.reshape(1, AT_DH).astype(F32))


def _merge_kernel(ohg_ref, oat_ref, gate_ref, x_ref, whg_ref, wat_ref, wout_ref, g2_ref,
                  x1_ref, h2_ref, *, D, sub):
    for r0 in range(0, x_ref.shape[0], sub):
        rows = slice(r0, r0 + sub)
        y_hg = _dot(ohg_ref[rows, :], whg_ref[...])
        y_at = _dot(oat_ref[rows, :], wat_ref[...])
        merged = (gate_ref[rows, 0:D].astype(F32) * y_hg
                  + gate_ref[rows, D:2 * D].astype(F32) * y_at)
        x1 = x_ref[rows, :] + _dot(merged.astype(BF16), wout_ref[...])
        x1_ref[rows, :] = x1
        ms = jnp.mean(x1 * x1, axis=-1, keepdims=True)
        h2_ref[rows, :] = (x1 * lax.rsqrt(ms + NORM_EPS) * g2_ref[...]).astype(h2_ref.dtype)


def _merge(ohg, oat, gates, x, whg, wat, wout, g2, tm=1024, sub=512):
    T, D = x.shape
    tok = lambda w: pl.BlockSpec((tm, w), lambda i: (i, 0))
    return pl.pallas_call(
        functools.partial(_merge_kernel, D=D, sub=sub),
        grid=(T // tm,),
        in_specs=[tok(ohg.shape[1]), tok(oat.shape[1]), tok(2 * D), tok(D),
                  _resident(whg), _resident(wat), _resident(wout),
                  pl.BlockSpec((1, D), lambda i: (0, 0))],
        out_specs=[tok(D), tok(D)],
        out_shape=[jax.ShapeDtypeStruct((T, D), F32), jax.ShapeDtypeStruct((T, D), BF16)],
        compiler_params=_cparams(("parallel",)),
        name="merge",
    )(ohg, oat, gates, x, whg, wat, wout, g2.reshape(1, D).astype(F32))


def _ffn_kernel(h_ref, x_ref, wg_ref, wu_ref, wd_ref, o_ref, *, tf):
    h = h_ref[...]
    acc = x_ref[...]
    for f0 in range(0, wg_ref.shape[1], tf):
        a = _dot(h, wg_ref[:, f0:f0 + tf])
        u = _dot(h, wu_ref[:, f0:f0 + tf])
        acc = acc + _dot((_silu(a) * u).astype(BF16), wd_ref[f0:f0 + tf, :])
    o_ref[...] = acc


def _ffn(h2, x1, wg, wu, wd, tm=512, tf=2816):
    T, D = x1.shape
    FF = wg.shape[1]
    assert FF % tf == 0
    tok = lambda: pl.BlockSpec((tm, D), lambda i: (i, 0))
    return pl.pallas_call(
        functools.partial(_ffn_kernel, tf=tf),
        grid=(T // tm,),
        in_specs=[tok(), tok(), _resident(wg), _resident(wu), _resident(wd)],
        out_specs=tok(),
        out_shape=jax.ShapeDtypeStruct((T, D), F32),
        compiler_params=_cparams(("parallel",)),
        name="ffn",
    )(h2, x1, wg, wu, wd)


def kernel(x, attn_norm_g, w_in, hg_lb_gamma, hg_out_norm_g, q_norm_g, k_norm_g, rel_bias_table,
           w_branch_hg, w_branch_attn, w_out, ffn_norm_g, w_ffn_gate, w_ffn_up, w_ffn_down):
    B, S, D = x.shape
    T = B * S
    depth = attn_norm_g.shape[0]
    assert depth == 1 and S % MOBA_BLOCK == 0 and S % HG_CHUNK == 0
    WH = HG_HEADS * HG_DK
    WV = HG_HEADS * HG_DV
    WA = AT_HEADS * AT_DH
    assert w_in.shape[2] == 2 * WH + 2 * WV + 3 * WA + 2 * D

    xt = x.reshape(T, D)
    for l in range(depth):
        hq, lf, hk, hi, hg, aq, ak, av, gates = _inproj(
            xt, attn_norm_g[l], w_in[l].astype(BF16), hg_lb_gamma, q_norm_g[l], k_norm_g[l],
            WH, WV, WA)
        o_hg = _hgrn(hq, lf, hk, hi, hg, hg_out_norm_g[l], B, S)
        o_at = _moba(aq, ak, av, rel_bias_table, q_norm_g[l], k_norm_g[l], B, S)
        x1, h2 = _merge(o_hg, o_at, gates, xt, w_branch_hg[l].astype(BF16),
                        w_branch_attn[l].astype(BF16), w_out[l].astype(BF16), ffn_norm_g[l])
        xt = _ffn(h2, x1, w_ffn_gate[l].astype(BF16), w_ffn_up[l].astype(BF16),
                  w_ffn_down[l].astype(BF16))
    return xt.reshape(B, S, D)
```

```python
import functools
import math

import numpy as np
import jax
import jax.numpy as jnp
from jax import lax
from jax.experimental import pallas as pl
from jax.experimental.pallas import tpu as pltpu

F32 = jnp.float32
BF16 = jnp.bfloat16

LANES = 128
NORM_EPS = 1e-6

HG_HEADS = 8
HG_DK = 128
HG_DV = 128
HG_CHUNK = 64
HG_STEP_TOKENS = 512
HG_HEAD_GROUP = 8
AT_HEADS = 8
AT_DH = 128
MOBA_BLOCK = 256
MOBA_TOPK = 3
REL_BUCKETS = 32
REL_MAX_DIST = 128
MASKED_LOGIT = -1e30
LOG2E = math.log2(math.e)

VMEM_LIMIT = 56 * 1024 * 1024


def _cparams(sem):
    return pltpu.CompilerParams(dimension_semantics=sem, vmem_limit_bytes=VMEM_LIMIT)


def _dot(a, b):
    return jnp.dot(a, b, preferred_element_type=F32)


def _dot_nt(a, b):
    return lax.dot_general(a, b, (((1,), (1,)), ((), ())), preferred_element_type=F32)


def _dot_tn(a, b):
    return lax.dot_general(a, b, (((0,), (0,)), ((), ())), preferred_element_type=F32)


def _resident(a):
    return pl.BlockSpec(a.shape, lambda *_: (0,) * a.ndim, pipeline_mode=pl.Buffered(1))


def _sigmoid(x):
    return 0.5 * jnp.tanh(0.5 * x) + 0.5


def _silu(x):
    h = 0.5 * x
    return h + h * jnp.tanh(h)


def _head_rmsnorm(acc, gain, scale):
    cols = []
    for c in range(acc.shape[1] // LANES):
        blk = acc[:, c * LANES:(c + 1) * LANES]
        ms = jnp.mean(blk * blk, axis=-1, keepdims=True)
        cols.append(blk * lax.rsqrt(ms + NORM_EPS) * (gain * scale))
    return jnp.concatenate(cols, axis=1)


def _inproj_kernel(x_ref, g_ref, w_ref, gamma_ref, qg_ref, kg_ref,
                   hq_ref, lf_ref, hk_ref, hi_ref, hg_ref, aq_ref, ak_ref, av_ref, gate_ref,
                   *, WH, WV, WA):
    x = x_ref[...]
    ms = jnp.mean(x * x, axis=-1, keepdims=True)
    h = (x * lax.rsqrt(ms + NORM_EPS) * g_ref[...]).astype(BF16)
    D = x.shape[1]
    starts = np.cumsum([0, WH, WH, WV, WV, WA, WA, WA])

    def segment(idx, width, off=0):
        c0 = int(starts[idx]) + off
        return _dot(h, w_ref[:, c0:c0 + width])

    for half in range(2):
        gate_ref[:, half * D:(half + 1) * D] = _sigmoid(
            segment(7, D, half * D)).astype(gate_ref.dtype)

    gamma = gamma_ref[...]
    eg = jnp.exp(gamma - jnp.max(gamma, axis=0, keepdims=True))
    lb = eg[0:1, :] / jnp.sum(eg, axis=0, keepdims=True)
    c0 = 0.5 + 0.5 * lb
    c1 = 0.5 - 0.5 * lb
    ct = c1 * jnp.tanh(0.5 * segment(1, WH))
    lf_ref[...] = jnp.log2(c0 + ct)
    hk_ref[...] = (c1 - ct).astype(hk_ref.dtype)

    hg_ref[...] = _silu(segment(3, WV)).astype(hg_ref.dtype)
    aq_ref[...] = _head_rmsnorm(segment(4, WA), qg_ref[...],
                                AT_DH ** -0.5 * LOG2E).astype(aq_ref.dtype)
    ak_ref[...] = _head_rmsnorm(segment(5, WA), kg_ref[...], 1.0).astype(ak_ref.dtype)
    hq_ref[...] = (segment(0, WH) * HG_DK ** -0.5).astype(hq_ref.dtype)
    hi_ref[...] = segment(2, WV).astype(hi_ref.dtype)
    av_ref[...] = segment(6, WA).astype(av_ref.dtype)


def _inproj(x, g, w, gamma, qg, kg, WH, WV, WA, tm=512):
    T, D = x.shape
    tok = lambda width: pl.BlockSpec((tm, width), lambda i: (i, 0))
    small = lambda a: pl.BlockSpec(a.shape, lambda i: (0, 0))
    widths = [WH, WH, WH, WV, WV, WA, WA, WA, 2 * D]
    dtypes = [BF16, F32, BF16, BF16, BF16, BF16, BF16, BF16, BF16]
    g = g.reshape(1, D).astype(F32)
    gamma = gamma.astype(F32)
    qg = qg.reshape(1, AT_DH).astype(F32)
    kg = kg.reshape(1, AT_DH).astype(F32)
    return pl.pallas_call(
        functools.partial(_inproj_kernel, WH=WH, WV=WV, WA=WA),
        grid=(T // tm,),
        in_specs=[tok(D), small(g), _resident(w), small(gamma), small(qg), small(kg)],
        out_specs=[tok(wd) for wd in widths],
        out_shape=[jax.ShapeDtypeStruct((T, wd), dt) for wd, dt in zip(widths, dtypes)],
        compiler_params=_cparams(("parallel",)),
        name="inproj",
    )(x, g, w, gamma, qg, kg)


def _hgrn_tables(C):
    nl = int(math.log2(C))
    assert 1 << nl == C and nl >= 3
    masks = np.zeros((nl + 1, C, C), np.float32)
    masks[0] = np.eye(C)
    t = np.arange(C)
    for L in range(1, nl + 1):
        blk, half = 1 << L, 1 << (L - 1)
        base = (t // blk) * blk
        upper = (t - base) >= half
        same = base[:, None] == base[None, :]
        masks[L] = (same & upper[:, None] & (~upper)[None, :]).astype(np.float32)
    return masks, nl


def _hgrn_kernel(q_ref, lf_ref, k_ref, v_ref, g_ref, gain_ref, m_ref, o_ref,
                 e_sc, d_sc, st_sc, *, C, nl, heads, chunks):
    @pl.when(pl.program_id(1) == 0)
    def _():
        st_sc[...] = jnp.zeros_like(st_sc)

    gain = gain_ref[...]
    scan_shifts = [1 << j for j in range(nl)]
    row = lax.broadcasted_iota(jnp.int32, (C, lf_ref.shape[1]), 0)
    odd = (row & 1) == 1
    ph = row & 3

    def chunk_rows(c):
        return pl.ds(pl.multiple_of(c * C, C), C)

    def prepare(c, buf):
        lf = lf_ref[chunk_rows(c), :]
        b = lf
        for sh in scan_shifts:
            b = b + jnp.where(row >= sh, pltpu.roll(b, sh, axis=0), 0.0)
        e_sc[buf, 0:C, :] = jnp.exp2(jnp.where(odd, lf, 0.0)).astype(BF16)
        prev = pltpu.roll(lf, 1, axis=0)
        nxt = pltpu.roll(lf, C - 1, axis=0)
        x2 = jnp.where(ph == 0, nxt, jnp.where(ph == 1, 0.0, jnp.where(ph == 2, lf, lf + prev)))
        e_sc[buf, C:2 * C, :] = jnp.exp2(x2).astype(BF16)
        for L in range(3, nl + 1):
            blk, half = 1 << L, 1 << (L - 1)
            pieces = [b[base:base + blk, :] - b[base + half - 1:base + half, :]
                      for base in range(0, C, blk)]
            d = pieces[0] if len(pieces) == 1 else jnp.concatenate(pieces, axis=0)
            e_sc[buf, (L - 1) * C:L * C, :] = jnp.exp2(-jnp.abs(d)).astype(BF16)
        b_last = b[C - 1:C, :]
        e_sc[buf, nl * C:(nl + 1) * C, :] = jnp.exp2(b).astype(BF16)
        e_sc[buf, (nl + 1) * C:(nl + 2) * C, :] = jnp.exp2(b_last - b).astype(BF16)
        d_sc[buf] = jnp.exp2(b_last)

    def block_diag(x):
        n, d2 = x.shape
        z = jnp.zeros((n, d2 // 2), x.dtype)
        return jnp.concatenate([jnp.concatenate([x[:, :d2 // 2], z], axis=1),
                                jnp.concatenate([z, x[:, d2 // 2:]], axis=1)], axis=0)

    W2 = 2 * HG_DK
    pair_mask = [jnp.concatenate([m_ref[L], m_ref[L]], axis=1) for L in range(nl + 1)]

    def heads_of(c, buf):
        rows = chunk_rows(c)
        pairs = [slice(p * W2, (p + 1) * W2) for p in range(heads // 2)]
        qb = [q_ref[rows, sl] for sl in pairs]
        kb = [k_ref[rows, sl] for sl in pairs]
        vb = [v_ref[rows, sl] for sl in pairs]
        A = [pair_mask[0] * _dot_nt(q, block_diag(k)) for q, k in zip(qb, kb)]
        for L in range(1, nl + 1):
            for p, sl in enumerate(pairs):
                eL = e_sc[buf, (L - 1) * C:L * C, sl]
                A[p] = A[p] + pair_mask[L] * _dot_nt(qb[p] * eL, block_diag(kb[p] * eL))
        for p, sl in enumerate(pairs):
            eb = e_sc[buf, nl * C:(nl + 1) * C, sl]
            ek = e_sc[buf, (nl + 1) * C:(nl + 2) * C, sl]
            st = [st_sc[2 * p], st_sc[2 * p + 1]]
            st_pair = jnp.concatenate([s.astype(BF16) for s in st], axis=1)
            o = (_dot(A[p].astype(BF16), block_diag(vb[p]))
                 + _dot_nt(qb[p] * eb, block_diag(st_pair)))
            upd = _dot_tn(vb[p], kb[p] * ek)
            for j in range(2):
                h = 2 * p + j
                hs = slice(h * HG_DK, (h + 1) * HG_DK)
                js = slice(j * HG_DK, (j + 1) * HG_DK)
                st_sc[h] = st[j] * d_sc[buf, :, hs] + upd[js, js]
                oj = o[:, js]
                ms = jnp.mean(oj * oj, axis=-1, keepdims=True)
                y = oj * lax.rsqrt(ms + NORM_EPS) * gain * g_ref[rows, hs].astype(F32)
                o_ref[rows, hs] = y.astype(o_ref.dtype)

    prepare(0, 0)

    for c in range(chunks):
        if c + 1 < chunks:
            prepare(c + 1, (c + 1) % 2)
        heads_of(c, c % 2)


def _hgrn(q, lf, k, v, g, gain, B, S):
    T, W = q.shape
    C = HG_CHUNK
    TS = HG_STEP_TOKENS
    assert S % TS == 0 and TS % C == 0
    N = S // TS
    masks, nl = _hgrn_tables(C)
    row = lambda b, c: (b * N + c, 0)
    tok = lambda: pl.BlockSpec((TS, W), row)
    return pl.pallas_call(
        functools.partial(_hgrn_kernel, C=C, nl=nl, heads=HG_HEADS, chunks=TS // C),
        grid=(B, N),
        in_specs=[tok(), tok(), tok(), tok(), tok(),
                  pl.BlockSpec((1, HG_DV), lambda b, c: (0, 0)),
                  pl.BlockSpec(masks.shape, lambda b, c: (0, 0, 0))],
        out_specs=tok(),
        out_shape=jax.ShapeDtypeStruct((T, W), BF16),
        scratch_shapes=[pltpu.VMEM((2, (nl + 2) * C, W), BF16),
                        pltpu.VMEM((2, 1, W), F32),
                        pltpu.VMEM((HG_HEADS, HG_DV, HG_DK), F32)],
        compiler_params=_cparams(("parallel", "arbitrary")),
        name="hgrn2",
    )(q, lf, k, v, g, gain.reshape(1, HG_DV).astype(F32), jnp.asarray(masks, F32))


def _t5_bucket_np(dist):
    n = np.maximum(dist, 0)
    max_exact = REL_BUCKETS // 2
    nf = np.maximum(n, 1).astype(np.float32)
    large = max_exact + (np.log(nf / max_exact) / math.log(REL_MAX_DIST / max_exact)
                         * (REL_BUCKETS - max_exact)).astype(np.int32)
    large = np.minimum(large, REL_BUCKETS - 1)
    return np.where(n < max_exact, n, large).astype(np.int32)


def _bias_kernel(tab_ref, bucket_ref, o_ref):
    h = pl.program_id(0)
    bucket = bucket_ref[...]
    acc = jnp.zeros(bucket.shape, F32)
    for b in range(REL_BUCKETS):
        acc = jnp.where(bucket == b, tab_ref[b, h], acc)
    delta = (acc - tab_ref[REL_BUCKETS - 1, h]) * LOG2E
    blk = bucket.shape[0]
    r = lax.broadcasted_iota(jnp.int32, bucket.shape, 0)
    c = lax.broadcasted_iota(jnp.int32, bucket.shape, 1)
    o_ref[0] = jnp.where(c - blk <= r, delta, MASKED_LOGIT)


def _bias_tiles(rel_table):
    BLK = MOBA_BLOCK
    t = np.arange(BLK)[:, None]
    s = np.arange(2 * BLK)[None, :]
    bucket = _t5_bucket_np(t + BLK - s)
    H = rel_table.shape[1]
    return pl.pallas_call(
        _bias_kernel,
        grid=(H,),
        in_specs=[pl.BlockSpec(memory_space=pltpu.SMEM),
                  pl.BlockSpec(bucket.shape, lambda h: (0, 0))],
        out_specs=pl.BlockSpec((1, BLK, 2 * BLK), lambda h: (h, 0, 0)),
        out_shape=jax.ShapeDtypeStruct((H, BLK, 2 * BLK), F32),
        compiler_params=_cparams(("arbitrary",)),
        name="relbias",
    )(rel_table.astype(F32), jnp.asarray(bucket))


MOBA_HEADS_PER_STEP = 2
MOBA_PAIRS_PER_STEP = 2
MOBA_BOUND_SLACK = 1.0 + 2.0 ** -6
MOBA_SAFE_GAP = 100.0


def _moba_kernel(q_ref, k_ref, v_ref, dbias_ref, far_ref, qg_ref, kg_ref, o_ref,
                 qnear_sc, qfar_sc, kaug_sc, vaug_sc, near_sc, far_sc, mx_sc, mb_sc, acc_sc,
                 safe_sc, *, NB, HPS, PPS):
    BLK = MOBA_BLOCK
    W2 = 2 * BLK
    PAD = NB + 2
    SHIFT = NB + 3
    NS = NB // 2 - 1
    hg = pl.program_id(1)
    ip = pl.program_id(2)

    def prepare(hh):
        cols = slice(hh * AT_DH, (hh + 1) * AT_DH)
        kaug, vaug, qnear, qfar = kaug_sc.at[hh], vaug_sc.at[hh], qnear_sc.at[hh], qfar_sc.at[hh]
        far = far_ref[pl.ds(hg * HPS + hh, 1), :]
        far_hi = far.astype(BF16).astype(F32)
        far_lo = far - far_hi
        lane = lax.broadcasted_iota(jnp.int32, (1, LANES), 1)
        tail = jnp.where(lane == NB, far_hi, jnp.where(lane == NB + 1, far_lo, 0.0))
        tail = jnp.where((lane == SHIFT) | (lane == SHIFT + 1), 1.0, tail)
        kaug[0:BLK, 0:AT_DH] = jnp.zeros((BLK, AT_DH), BF16)
        kaug[0:BLK, AT_DH:] = jnp.broadcast_to(
            jnp.where(lane == PAD, 1.0, 0.0), (BLK, LANES)).astype(BF16)
        vaug[0:BLK, :] = jnp.zeros((BLK, AT_DH + LANES), BF16)
        kms = []
        ones8 = jnp.ones((8, BLK), BF16)
        for n in range(NB):
            kb = k_ref[n * BLK:(n + 1) * BLK, cols]
            kms.append(_dot(ones8, kb)[0:1, :] * (1.0 / BLK))
            kaug[(n + 1) * BLK:(n + 2) * BLK, 0:AT_DH] = kb
            kaug[(n + 1) * BLK:(n + 2) * BLK, AT_DH:] = jnp.broadcast_to(
                jnp.where(lane == n, 1.0, tail), (BLK, LANES)).astype(BF16)
        vaug[BLK:, 0:AT_DH] = v_ref[:, cols]
        vaug[BLK:, AT_DH:] = jnp.ones((NB * BLK, LANES), BF16)

        qk_bound = (AT_DH * jnp.max(jnp.abs(qg_ref[...]), keepdims=True)
                    * jnp.max(jnp.abs(kg_ref[...]), keepdims=True)
                    * (AT_DH ** -0.5 * LOG2E * MOBA_BOUND_SLACK))
        bias_max = jnp.max(dbias_ref[hh], keepdims=True)[0:1, 0:1] + far[:, 0:1]
        bias_self = dbias_ref[hh][0:1, BLK:BLK + 1] + far[:, 0:1]
        shift = qk_bound + bias_max
        gap = shift + qk_bound - bias_self
        shift_hi = shift.astype(BF16).astype(F32)

        kmean = jnp.concatenate(kms, axis=0)
        km_hi = kmean.astype(BF16)
        km_lo = (kmean - km_hi.astype(F32)).astype(BF16)
        blk = lax.broadcasted_iota(jnp.int32, (NB, BLK), 0)
        rowid = lax.broadcasted_iota(jnp.int32, (LANES - NB, BLK), 0) + NB
        rest = jnp.where(rowid < NB + 2, 1.0, jnp.where(rowid == PAD, MASKED_LOGIT, 0.0))
        rest = jnp.where(rowid == SHIFT, -shift_hi,
                         jnp.where(rowid == SHIFT + 1, shift_hi - shift, rest))
        lane_q = lax.broadcasted_iota(jnp.int32, (BLK, LANES), 1)
        for t in range(NB):
            qt = q_ref[t * BLK:(t + 1) * BLK, cols]
            gate = jnp.where(blk < t, _dot_nt(km_hi, qt) + _dot_nt(km_lo, qt), -jnp.inf)
            rank = jnp.zeros((NB, BLK), jnp.int32)
            for m in range(t):
                gm = gate[m:m + 1, :]
                rank = rank + ((gm > gate) | ((gm == gate) & (blk > m))).astype(jnp.int32)
            sel = ((blk < t) & (rank < MOBA_TOPK)) | (blk == t)
            near = jnp.concatenate([jnp.where(sel, 0.0, MASKED_LOGIT), rest], axis=0).T
            older = jnp.where((lane_q == t) | (lane_q == t - 1), MASKED_LOGIT, near)
            rows = slice(t * BLK, (t + 1) * BLK)
            qnear[rows, 0:AT_DH] = qt
            qfar[rows, 0:AT_DH] = qt
            qnear[rows, AT_DH:] = near.astype(BF16)
            qfar[rows, AT_DH:] = older.astype(BF16)
        return jnp.max(gap)

    @pl.when(ip == 0)
    def _():
        worst = prepare(0)
        for hh in range(1, HPS):
            worst = jnp.maximum(worst, prepare(hh))
        safe_sc[0] = (worst <= MOBA_SAFE_GAP).astype(jnp.int32)

    def lane_max(s):
        out = s[:, 0:LANES]
        for c in range(1, s.shape[1] // LANES):
            out = jnp.maximum(out, s[:, c * LANES:(c + 1) * LANES])
        return out

    def probs(s, m_b):
        return jnp.exp2(s - jnp.concatenate([m_b] * (W2 // LANES), axis=1)).astype(BF16)

    def plan(j):
        rows = [pl.multiple_of(b * BLK, BLK) for b in (j, NB - 1 - j)]
        n_first = jnp.maximum(j, 1) // 2
        visits = []
        for st in range(NS):
            second = st >= n_first
            jj = jnp.where(second, st - n_first, st)
            visits.append((second.astype(jnp.int32), jnp.where(second, rows[1], rows[0]),
                           pl.multiple_of(jj * W2 + BLK, BLK)))
        return rows, visits

    plans = [plan(ip * PPS + pp) for pp in range(PPS)]

    def one_pass():
        for pp, (rows, visits) in enumerate(plans):
            for hh in range(HPS):
                for slot in range(2):
                    s = (_dot_nt(qnear_sc[hh, pl.ds(rows[slot], BLK), :],
                                 kaug_sc[hh, pl.ds(rows[slot], W2), :]) + dbias_ref[hh])
                    acc_sc[pp, hh, slot] = _dot(jnp.exp2(s).astype(BF16),
                                                vaug_sc[hh, pl.ds(rows[slot], W2), :])
                for st, (slot, qrow, krow) in enumerate(visits):
                    s = _dot_nt(qfar_sc[hh, pl.ds(qrow, BLK), :],
                                kaug_sc[hh, pl.ds(krow, W2), :])
                    acc_sc[pp, hh, slot] += _dot(jnp.exp2(s).astype(BF16),
                                                 vaug_sc[hh, pl.ds(krow, W2), :])

    def two_pass():
        for pp, (rows, visits) in enumerate(plans):
            for hh in range(HPS):
                for slot in range(2):
                    near_sc[hh, slot] = (_dot_nt(qnear_sc[hh, pl.ds(rows[slot], BLK), :],
                                                 kaug_sc[hh, pl.ds(rows[slot], W2), :])
                                         + dbias_ref[hh])
                for st, (slot, qrow, krow) in enumerate(visits):
                    far_sc[hh, st] = _dot_nt(qfar_sc[hh, pl.ds(qrow, BLK), :],
                                             kaug_sc[hh, pl.ds(krow, W2), :])
            mx_sc[...] = jnp.full(mx_sc.shape, MASKED_LOGIT, F32)
            for hh in range(HPS):
                for st, (slot, qrow, krow) in enumerate(visits):
                    mx_sc[hh, slot * NS + st] = lane_max(far_sc[hh, st])
                for slot in range(2):
                    m = lane_max(near_sc[hh, slot])
                    for st in range(NS):
                        m = jnp.maximum(m, mx_sc[hh, slot * NS + st])
                    mb_sc[hh, slot] = jnp.broadcast_to(jnp.max(m, axis=-1, keepdims=True),
                                                       (BLK, LANES))
            for hh in range(HPS):
                for slot in range(2):
                    acc_sc[pp, hh, slot] = _dot(probs(near_sc[hh, slot], mb_sc[hh, slot]),
                                                vaug_sc[hh, pl.ds(rows[slot], W2), :])
                for st, (slot, qrow, krow) in enumerate(visits):
                    acc_sc[pp, hh, slot] += _dot(probs(far_sc[hh, st], mb_sc[hh, slot]),
                                                 vaug_sc[hh, pl.ds(krow, W2), :])

    lax.cond(safe_sc[0] != 0, one_pass, two_pass)

    for pp, (rows, visits) in enumerate(plans):
        for hh in range(HPS):
            for slot in range(2):
                acc = acc_sc[pp, hh, slot]
                o_ref[pl.ds(rows[slot], BLK), hh * AT_DH:(hh + 1) * AT_DH] = (
                    acc[:, 0:AT_DH] / acc[:, AT_DH:]).astype(o_ref.dtype)


def _moba(q, k, v, rel_table, qg, kg, B, S):
    T, W = q.shape
    H = AT_HEADS
    HPS = MOBA_HEADS_PER_STEP
    PPS = MOBA_PAIRS_PER_STEP
    NB = S // MOBA_BLOCK
    BLK = MOBA_BLOCK
    NS = max(NB // 2 - 1, 1)
    assert NB + 5 <= LANES and NB % (2 * PPS) == 0 and H % HPS == 0
    assert int(_t5_bucket_np(np.array([BLK + 1]))[0]) == REL_BUCKETS - 1
    dbias = _bias_tiles(rel_table)
    far = jnp.broadcast_to((rel_table[REL_BUCKETS - 1, :].astype(F32) * LOG2E)[:, None],
                           (H, LANES))
    heads = lambda: pl.BlockSpec((S, HPS * AT_DH), lambda b, g, i: (b, g))
    aug = AT_DH + LANES
    return pl.pallas_call(
        functools.partial(_moba_kernel, NB=NB, HPS=HPS, PPS=PPS),
        grid=(B, H // HPS, NB // (2 * PPS)),
        in_specs=[heads(), heads(), heads(),
                  pl.BlockSpec((HPS, BLK, 2 * BLK), lambda b, g, i: (g, 0, 0)),
                  pl.BlockSpec((H, LANES), lambda b, g, i: (0, 0)),
                  pl.BlockSpec((1, AT_DH), lambda b, g, i: (0, 0)),
                  pl.BlockSpec((1, AT_DH), lambda b, g, i: (0, 0))],
        out_specs=heads(),
        out_shape=jax.ShapeDtypeStruct((T, W), BF16),
        scratch_shapes=[pltpu.VMEM((HPS, S, aug), BF16),
                        pltpu.VMEM((HPS, S, aug), BF16),
                        pltpu.VMEM((HPS, S + BLK, aug), BF16),
                        pltpu.VMEM((HPS, S + BLK, aug), BF16),
                        pltpu.VMEM((HPS, 2, BLK, 2 * BLK), F32),
                        pltpu.VMEM((HPS, NS, BLK, 2 * BLK), F32),
                        pltpu.VMEM((HPS, 2 * NS, BLK, LANES), F32),
                        pltpu.VMEM((HPS, 2, BLK, LANES), F32),
                        pltpu.VMEM((PPS, HPS, 2, BLK, aug), F32),
                        pltpu.SMEM((1,), jnp.int32)],
        compiler_params=_cparams(("parallel", "parallel", "arbitrary")),
        name="moba",
    )(q, k, v, dbias, far, qg.reshape(1, AT_DH).astype(F32), kg.reshape(1, AT_DH).astype(F32))


def _merge_kernel(ohg_ref, oat_ref, gate_ref, x_ref, whg_ref, wat_ref, wout_ref, g2_ref,
                  x1_ref, h2_ref, *, D, sub):
    for r0 in range(0, x_ref.shape[0], sub):
        rows = slice(r0, r0 + sub)
        y_hg = _dot(ohg_ref[rows, :], whg_ref[...])
        y_at = _dot(oat_ref[rows, :], wat_ref[...])
        merged = (gate_ref[rows, 0:D].astype(F32) * y_hg
                  + gate_ref[rows, D:2 * D].astype(F32) * y_at)
        x1 = x_ref[rows, :] + _dot(merged.astype(BF16), wout_ref[...])
        x1_ref[rows, :] = x1
        ms = jnp.mean(x1 * x1, axis=-1, keepdims=True)
        h2_ref[rows, :] = (x1 * lax.rsqrt(ms + NORM_EPS) * g2_ref[...]).astype(h2_ref.dtype)


def _merge(ohg, oat, gates, x, whg, wat, wout, g2, tm=1024, sub=512):
    T, D = x.shape
    tok = lambda w: pl.BlockSpec((tm, w), lambda i: (i, 0))
    return pl.pallas_call(
        functools.partial(_merge_kernel, D=D, sub=sub),
        grid=(T // tm,),
        in_specs=[tok(ohg.shape[1]), tok(oat.shape[1]), tok(2 * D), tok(D),
                  _resident(whg), _resident(wat), _resident(wout),
                  pl.BlockSpec((1, D), lambda i: (0, 0))],
        out_specs=[tok(D), tok(D)],
        out_shape=[jax.ShapeDtypeStruct((T, D), F32), jax.ShapeDtypeStruct((T, D), BF16)],
        compiler_params=_cparams(("parallel",)),
        name="merge",
    )(ohg, oat, gates, x, whg, wat, wout, g2.reshape(1, D).astype(F32))


def _ffn_kernel(h_ref, x_ref, wg_ref, wu_ref, wd_ref, o_ref, *, tf):
    h = h_ref[...]
    acc = x_ref[...]
    for f0 in range(0, wg_ref.shape[1], tf):
        a = _dot(h, wg_ref[:, f0:f0 + tf])
        u = _dot(h, wu_ref[:, f0:f0 + tf])
        acc = acc + _dot((_silu(a) * u).astype(BF16), wd_ref[f0:f0 + tf, :])
    o_ref[...] = acc


def _ffn(h2, x1, wg, wu, wd, tm=512, tf=2816):
    T, D = x1.shape
    FF = wg.shape[1]
    assert FF % tf == 0
    tok = lambda: pl.BlockSpec((tm, D), lambda i: (i, 0))
    return pl.pallas_call(
        functools.partial(_ffn_kernel, tf=tf),
        grid=(T // tm,),
        in_specs=[tok(), tok(), _resident(wg), _resident(wu), _resident(wd)],
        out_specs=tok(),
        out_shape=jax.ShapeDtypeStruct((T, D), F32),
        compiler_params=_cparams(("parallel",)),
        name="ffn",
    )(h2, x1, wg, wu, wd)


def kernel(x, attn_norm_g, w_in, hg_lb_gamma, hg_out_norm_g, q_norm_g, k_norm_g, rel_bias_table,
           w_branch_hg, w_branch_attn, w_out, ffn_norm_g, w_ffn_gate, w_ffn_up, w_ffn_down):
    B, S, D = x.shape
    T = B * S
    depth = attn_norm_g.shape[0]
    assert depth == 1 and S % MOBA_BLOCK == 0 and S % HG_CHUNK == 0
    WH = HG_HEADS * HG_DK
    WV = HG_HEADS * HG_DV
    WA = AT_HEADS * AT_DH
    assert w_in.shape[2] == 2 * WH + 2 * WV + 3 * WA + 2 * D

    xt = x.reshape(T, D)
    for l in range(depth):
        hq, lf, hk, hi, hg, aq, ak, av, gates = _inproj(
            xt, attn_norm_g[l], w_in[l].astype(BF16), hg_lb_gamma, q_norm_g[l], k_norm_g[l],
            WH, WV, WA)
        o_hg = _hgrn(hq, lf, hk, hi, hg, hg_out_norm_g[l], B, S)
        o_at = _moba(aq, ak, av, rel_bias_table, q_norm_g[l], k_norm_g[l], B, S)
        x1, h2 = _merge(o_hg, o_at, gates, xt, w_branch_hg[l].astype(BF16),
                        w_branch_attn[l].astype(BF16), w_out[l].astype(BF16), ffn_norm_g[l])
        xt = _ffn(h2, x1, w_ffn_gate[l].astype(BF16), w_ffn_up[l].astype(BF16),
                  w_ffn_down[l].astype(BF16))
    return xt.reshape(B, S, D)
```

```python
import functools
import math

import numpy as np
import jax
import jax.numpy as jnp
from jax import lax
from jax.experimental import pallas as pl
from jax.experimental.pallas import tpu as pltpu

F32 = jnp.float32
BF16 = jnp.bfloat16

LANES = 128
NORM_EPS = 1e-6

HG_HEADS = 8
HG_DK = 128
HG_DV = 128
HG_CHUNK = 64
HG_STEP_TOKENS = 512
HG_HEAD_GROUP = 8
AT_HEADS = 8
AT_DH = 128
MOBA_BLOCK = 256
MOBA_TOPK = 3
REL_BUCKETS = 32
REL_MAX_DIST = 128
MASKED_LOGIT = -1e30
LOG2E = math.log2(math.e)

VMEM_LIMIT = 56 * 1024 * 1024


def _cparams(sem):
    return pltpu.CompilerParams(dimension_semantics=sem, vmem_limit_bytes=VMEM_LIMIT)


def _dot(a, b):
    return jnp.dot(a, b, preferred_element_type=F32)


def _dot_nt(a, b):
    return lax.dot_general(a, b, (((1,), (1,)), ((), ())), preferred_element_type=F32)


def _dot_tn(a, b):
    return lax.dot_general(a, b, (((0,), (0,)), ((), ())), preferred_element_type=F32)


def _resident(a):
    return pl.BlockSpec(a.shape, lambda *_: (0,) * a.ndim, pipeline_mode=pl.Buffered(1))


def _sigmoid(x):
    return 0.5 * jnp.tanh(0.5 * x) + 0.5


def _silu(x):
    h = 0.5 * x
    return h + h * jnp.tanh(h)


def _head_rmsnorm(acc, gain, scale):
    cols = []
    for c in range(acc.shape[1] // LANES):
        blk = acc[:, c * LANES:(c + 1) * LANES]
        ms = jnp.mean(blk * blk, axis=-1, keepdims=True)
        cols.append(blk * lax.rsqrt(ms + NORM_EPS) * (gain * scale))
    return jnp.concatenate(cols, axis=1)


def _inproj_kernel(x_ref, g_ref, w_ref, gamma_ref, qg_ref, kg_ref,
                   hq_ref, lf_ref, hk_ref, hi_ref, hg_ref, aq_ref, ak_ref, av_ref, gate_ref,
                   *, WH, WV, WA):
    x = x_ref[...]
    ms = jnp.mean(x * x, axis=-1, keepdims=True)
    h = (x * lax.rsqrt(ms + NORM_EPS) * g_ref[...]).astype(BF16)
    D = x.shape[1]
    starts = np.cumsum([0, WH, WH, WV, WV, WA, WA, WA])

    def segment(idx, width, off=0):
        c0 = int(starts[idx]) + off
        return _dot(h, w_ref[:, c0:c0 + width])

    for half in range(2):
        gate_ref[:, half * D:(half + 1) * D] = _sigmoid(
            segment(7, D, half * D)).astype(gate_ref.dtype)

    gamma = gamma_ref[...]
    eg = jnp.exp(gamma - jnp.max(gamma, axis=0, keepdims=True))
    lb = eg[0:1, :] / jnp.sum(eg, axis=0, keepdims=True)
    c0 = 0.5 + 0.5 * lb
    c1 = 0.5 - 0.5 * lb
    ct = c1 * jnp.tanh(0.5 * segment(1, WH))
    lf_ref[...] = jnp.log2(c0 + ct)
    hk_ref[...] = (c1 - ct).astype(hk_ref.dtype)

    hg_ref[...] = _silu(segment(3, WV)).astype(hg_ref.dtype)
    aq_ref[...] = _head_rmsnorm(segment(4, WA), qg_ref[...],
                                AT_DH ** -0.5 * LOG2E).astype(aq_ref.dtype)
    ak_ref[...] = _head_rmsnorm(segment(5, WA), kg_ref[...], 1.0).astype(ak_ref.dtype)
    hq_ref[...] = (segment(0, WH) * HG_DK ** -0.5).astype(hq_ref.dtype)
    hi_ref[...] = segment(2, WV).astype(hi_ref.dtype)
    av_ref[...] = segment(6, WA).astype(av_ref.dtype)


def _inproj(x, g, w, gamma, qg, kg, WH, WV, WA, tm=512):
    T, D = x.shape
    tok = lambda width: pl.BlockSpec((tm, width), lambda i: (i, 0))
    small = lambda a: pl.BlockSpec(a.shape, lambda i: (0, 0))
    widths = [WH, WH, WH, WV, WV, WA, WA, WA, 2 * D]
    dtypes = [BF16, F32, BF16, BF16, BF16, BF16, BF16, BF16, BF16]
    g = g.reshape(1, D).astype(F32)
    gamma = gamma.astype(F32)
    qg = qg.reshape(1, AT_DH).astype(F32)
    kg = kg.reshape(1, AT_DH).astype(F32)
    return pl.pallas_call(
        functools.partial(_inproj_kernel, WH=WH, WV=WV, WA=WA),
        grid=(T // tm,),
        in_specs=[tok(D), small(g), _resident(w), small(gamma), small(qg), small(kg)],
        out_specs=[tok(wd) for wd in widths],
        out_shape=[jax.ShapeDtypeStruct((T, wd), dt) for wd, dt in zip(widths, dtypes)],
        compiler_params=_cparams(("parallel",)),
        name="inproj",
    )(x, g, w, gamma, qg, kg)


def _hgrn_tables(C):
    nl = int(math.log2(C))
    assert 1 << nl == C and nl >= 3
    masks = np.zeros((nl + 1, C, C), np.float32)
    masks[0] = np.eye(C)
    t = np.arange(C)
    for L in range(1, nl + 1):
        blk, half = 1 << L, 1 << (L - 1)
        base = (t // blk) * blk
        upper = (t - base) >= half
        same = base[:, None] == base[None, :]
        masks[L] = (same & upper[:, None] & (~upper)[None, :]).astype(np.float32)
    return masks, nl


def _hgrn_kernel(q_ref, lf_ref, k_ref, v_ref, g_ref, gain_ref, m_ref, o_ref,
                 e_sc, d_sc, st_sc, *, C, nl, heads, chunks):
    @pl.when(pl.program_id(1) == 0)
    def _():
        st_sc[...] = jnp.zeros_like(st_sc)

    gain = gain_ref[...]
    scan_shifts = [1 << j for j in range(nl)]
    row = lax.broadcasted_iota(jnp.int32, (C, lf_ref.shape[1]), 0)
    odd = (row & 1) == 1
    ph = row & 3

    def chunk_rows(c):
        return pl.ds(pl.multiple_of(c * C, C), C)

    def prepare(c, buf):
        lf = lf_ref[chunk_rows(c), :]
        b = lf
        for sh in scan_shifts:
            b = b + jnp.where(row >= sh, pltpu.roll(b, sh, axis=0), 0.0)
        e_sc[buf, 0:C, :] = jnp.exp2(jnp.where(odd, lf, 0.0)).astype(BF16)
        prev = pltpu.roll(lf, 1, axis=0)
        nxt = pltpu.roll(lf, C - 1, axis=0)
        x2 = jnp.where(ph == 0, nxt, jnp.where(ph == 1, 0.0, jnp.where(ph == 2, lf, lf + prev)))
        e_sc[buf, C:2 * C, :] = jnp.exp2(x2).astype(BF16)
        for L in range(3, nl + 1):
            blk, half = 1 << L, 1 << (L - 1)
            pieces = [b[base:base + blk, :] - b[base + half - 1:base + half, :]
                      for base in range(0, C, blk)]
            d = pieces[0] if len(pieces) == 1 else jnp.concatenate(pieces, axis=0)
            e_sc[buf, (L - 1) * C:L * C, :] = jnp.exp2(-jnp.abs(d)).astype(BF16)
        b_last = b[C - 1:C, :]
        e_sc[buf, nl * C:(nl + 1) * C, :] = jnp.exp2(b).astype(BF16)
        e_sc[buf, (nl + 1) * C:(nl + 2) * C, :] = jnp.exp2(b_last - b).astype(BF16)
        d_sc[buf] = jnp.exp2(b_last)

    def block_diag(x):
        n, d2 = x.shape
        z = jnp.zeros((n, d2 // 2), x.dtype)
        return jnp.concatenate([jnp.concatenate([x[:, :d2 // 2], z], axis=1),
                                jnp.concatenate([z, x[:, d2 // 2:]], axis=1)], axis=0)

    W2 = 2 * HG_DK
    pair_mask = [jnp.concatenate([m_ref[L], m_ref[L]], axis=1) for L in range(nl + 1)]

    def heads_of(c, buf):
        rows = chunk_rows(c)
        pairs = [slice(p * W2, (p + 1) * W2) for p in range(heads // 2)]
        qb = [q_ref[rows, sl] for sl in pairs]
        kb = [k_ref[rows, sl] for sl in pairs]
        vb = [v_ref[rows, sl] for sl in pairs]
        A = [pair_mask[0] * _dot_nt(q, block_diag(k)) for q, k in zip(qb, kb)]
        for L in range(1, nl + 1):
            for p, sl in enumerate(pairs):
                eL = e_sc[buf, (L - 1) * C:L * C, sl]
                A[p] = A[p] + pair_mask[L] * _dot_nt(qb[p] * eL, block_diag(kb[p] * eL))
        for p, sl in enumerate(pairs):
            eb = e_sc[buf, nl * C:(nl + 1) * C, sl]
            ek = e_sc[buf, (nl + 1) * C:(nl + 2) * C, sl]
            st = [st_sc[2 * p], st_sc[2 * p + 1]]
            st_pair = jnp.concatenate([s.astype(BF16) for s in st], axis=1)
            o = (_dot(A[p].astype(BF16), block_diag(vb[p]))
                 + _dot_nt(qb[p] * eb, block_diag(st_pair)))
            upd = _dot_tn(vb[p], kb[p] * ek)
            for j in range(2):
                h = 2 * p + j
                hs = slice(h * HG_DK, (h + 1) * HG_DK)
                js = slice(j * HG_DK, (j + 1) * HG_DK)
                st_sc[h] = st[j] * d_sc[buf, :, hs] + upd[js, js]
                oj = o[:, js]
                ms = jnp.mean(oj * oj, axis=-1, keepdims=True)
                y = oj * lax.rsqrt(ms + NORM_EPS) * gain * g_ref[rows, hs].astype(F32)
                o_ref[rows, hs] = y.astype(o_ref.dtype)

    prepare(0, 0)

    for c in range(chunks):
        if c + 1 < chunks:
            prepare(c + 1, (c + 1) % 2)
        heads_of(c, c % 2)


def _hgrn(q, lf, k, v, g, gain, B, S):
    T, W = q.shape
    C = HG_CHUNK
    TS = HG_STEP_TOKENS
    assert S % TS == 0 and TS % C == 0
    N = S // TS
    masks, nl = _hgrn_tables(C)
    row = lambda b, c: (b * N + c, 0)
    tok = lambda: pl.BlockSpec((TS, W), row)
    return pl.pallas_call(
        functools.partial(_hgrn_kernel, C=C, nl=nl, heads=HG_HEADS, chunks=TS // C),
        grid=(B, N),
        in_specs=[tok(), tok(), tok(), tok(), tok(),
                  pl.BlockSpec((1, HG_DV), lambda b, c: (0, 0)),
                  pl.BlockSpec(masks.shape, lambda b, c: (0, 0, 0))],
        out_specs=tok(),
        out_shape=jax.ShapeDtypeStruct((T, W), BF16),
        scratch_shapes=[pltpu.VMEM((2, (nl + 2) * C, W), BF16),
                        pltpu.VMEM((2, 1, W), F32),
                        pltpu.VMEM((HG_HEADS, HG_DV, HG_DK), F32)],
        compiler_params=_cparams(("parallel", "arbitrary")),
        name="hgrn2",
    )(q, lf, k, v, g, gain.reshape(1, HG_DV).astype(F32), jnp.asarray(masks, F32))


def _t5_bucket_np(dist):
    n = np.maximum(dist, 0)
    max_exact = REL_BUCKETS // 2
    nf = np.maximum(n, 1).astype(np.float32)
    large = max_exact + (np.log(nf / max_exact) / math.log(REL_MAX_DIST / max_exact)
                         * (REL_BUCKETS - max_exact)).astype(np.int32)
    large = np.minimum(large, REL_BUCKETS - 1)
    return np.where(n < max_exact, n, large).astype(np.int32)


def _bias_kernel(tab_ref, bucket_ref, o_ref):
    h = pl.program_id(0)
    bucket = bucket_ref[...]
    acc = jnp.zeros(bucket.shape, F32)
    for b in range(REL_BUCKETS):
        acc = jnp.where(bucket == b, tab_ref[b, h], acc)
    delta = (acc - tab_ref[REL_BUCKETS - 1, h]) * LOG2E
    blk = bucket.shape[0]
    r = lax.broadcasted_iota(jnp.int32, bucket.shape, 0)
    c = lax.broadcasted_iota(jnp.int32, bucket.shape, 1)
    o_ref[0] = jnp.where(c - blk <= r, delta, MASKED_LOGIT)


def _bias_tiles(rel_table):
    BLK = MOBA_BLOCK
    t = np.arange(BLK)[:, None]
    s = np.arange(2 * BLK)[None, :]
    bucket = _t5_bucket_np(t + BLK - s)
    H = rel_table.shape[1]
    return pl.pallas_call(
        _bias_kernel,
        grid=(H,),
        in_specs=[pl.BlockSpec(memory_space=pltpu.SMEM),
                  pl.BlockSpec(bucket.shape, lambda h: (0, 0))],
        out_specs=pl.BlockSpec((1, BLK, 2 * BLK), lambda h: (h, 0, 0)),
        out_shape=jax.ShapeDtypeStruct((H, BLK, 2 * BLK), F32),
        compiler_params=_cparams(("arbitrary",)),
        name="relbias",
    )(rel_table.astype(F32), jnp.asarray(bucket))


MOBA_HEADS_PER_STEP = 1
MOBA_BOUND_SLACK = 1.0 + 2.0 ** -6
MOBA_SAFE_GAP = 100.0


def _moba_kernel(q_ref, k_ref, v_ref, dbias_ref, far_ref, qg_ref, kg_ref, o_ref,
                 qnear_sc, qfar_sc, kaug_sc, vaug_sc, near_sc, far_sc, mx_sc, mb_sc, acc_sc,
                 *, NB, HPS):
    BLK = MOBA_BLOCK
    W2 = 2 * BLK
    PAD = NB + 2
    SHIFT = NB + 3
    NS = NB // 2 - 1
    hg = pl.program_id(1)

    def prepare(hh):
        cols = slice(hh * AT_DH, (hh + 1) * AT_DH)
        kaug, vaug, qnear, qfar = kaug_sc.at[hh], vaug_sc.at[hh], qnear_sc.at[hh], qfar_sc.at[hh]
        far = far_ref[pl.ds(hg * HPS + hh, 1), :]
        far_hi = far.astype(BF16).astype(F32)
        far_lo = far - far_hi
        lane = lax.broadcasted_iota(jnp.int32, (1, LANES), 1)
        tail = jnp.where(lane == NB, far_hi, jnp.where(lane == NB + 1, far_lo, 0.0))
        tail = jnp.where((lane == SHIFT) | (lane == SHIFT + 1), 1.0, tail)
        kaug[0:BLK, 0:AT_DH] = jnp.zeros((BLK, AT_DH), BF16)
        kaug[0:BLK, AT_DH:] = jnp.broadcast_to(
            jnp.where(lane == PAD, 1.0, 0.0), (BLK, LANES)).astype(BF16)
        vaug[0:BLK, :] = jnp.zeros((BLK, AT_DH + LANES), BF16)
        kms = []
        ones8 = jnp.ones((8, BLK), BF16)
        for n in range(NB):
            kb = k_ref[n * BLK:(n + 1) * BLK, cols]
            kms.append(_dot(ones8, kb)[0:1, :] * (1.0 / BLK))
            kaug[(n + 1) * BLK:(n + 2) * BLK, 0:AT_DH] = kb
            kaug[(n + 1) * BLK:(n + 2) * BLK, AT_DH:] = jnp.broadcast_to(
                jnp.where(lane == n, 1.0, tail), (BLK, LANES)).astype(BF16)
        vaug[BLK:, 0:AT_DH] = v_ref[:, cols]
        vaug[BLK:, AT_DH:] = jnp.ones((NB * BLK, LANES), BF16)

        qk_bound = (AT_DH * jnp.max(jnp.abs(qg_ref[...]), keepdims=True)
                    * jnp.max(jnp.abs(kg_ref[...]), keepdims=True)
                    * (AT_DH ** -0.5 * LOG2E * MOBA_BOUND_SLACK))
        bias_max = jnp.max(dbias_ref[hh], keepdims=True)[0:1, 0:1] + far[:, 0:1]
        bias_self = dbias_ref[hh][0:1, BLK:BLK + 1] + far[:, 0:1]
        shift = qk_bound + bias_max
        gap = shift + qk_bound - bias_self
        shift_hi = shift.astype(BF16).astype(F32)

        kmean = jnp.concatenate(kms, axis=0)
        km_hi = kmean.astype(BF16)
        km_lo = (kmean - km_hi.astype(F32)).astype(BF16)
        blk = lax.broadcasted_iota(jnp.int32, (NB, BLK), 0)
        rowid = lax.broadcasted_iota(jnp.int32, (LANES - NB, BLK), 0) + NB
        rest = jnp.where(rowid < NB + 2, 1.0, jnp.where(rowid == PAD, MASKED_LOGIT, 0.0))
        rest = jnp.where(rowid == SHIFT, -shift_hi,
                         jnp.where(rowid == SHIFT + 1, shift_hi - shift, rest))
        lane_q = lax.broadcasted_iota(jnp.int32, (BLK, LANES), 1)
        for t in range(NB):
            qt = q_ref[t * BLK:(t + 1) * BLK, cols]
            gate = jnp.where(blk < t, _dot_nt(km_hi, qt) + _dot_nt(km_lo, qt), -jnp.inf)
            rank = jnp.zeros((NB, BLK), jnp.int32)
            for m in range(t):
                gm = gate[m:m + 1, :]
                rank = rank + ((gm > gate) | ((gm == gate) & (blk > m))).astype(jnp.int32)
            sel = ((blk < t) & (rank < MOBA_TOPK)) | (blk == t)
            near = jnp.concatenate([jnp.where(sel, 0.0, MASKED_LOGIT), rest], axis=0).T
            older = jnp.where((lane_q == t) | (lane_q == t - 1), MASKED_LOGIT, near)
            rows = slice(t * BLK, (t + 1) * BLK)
            qnear[rows, 0:AT_DH] = qt
            qfar[rows, 0:AT_DH] = qt
            qnear[rows, AT_DH:] = near.astype(BF16)
            qfar[rows, AT_DH:] = older.astype(BF16)
        return jnp.max(gap)

    worst = prepare(0)
    for hh in range(1, HPS):
        worst = jnp.maximum(worst, prepare(hh))

    def lane_max(s):
        out = s[:, 0:LANES]
        for c in range(1, s.shape[1] // LANES):
            out = jnp.maximum(out, s[:, c * LANES:(c + 1) * LANES])
        return out

    def probs(s, m_b):
        return jnp.exp2(s - jnp.concatenate([m_b] * (W2 // LANES), axis=1)).astype(BF16)

    def one_pass():
        for hh in range(HPS):
            for i in range(NB):
                row = i * BLK
                s = (_dot_nt(qnear_sc[hh, row:row + BLK, :], kaug_sc[hh, row:row + W2, :])
                     + dbias_ref[hh])
                acc = _dot(jnp.exp2(s).astype(BF16), vaug_sc[hh, row:row + W2, :])
                older = (max(i, 1) // 2) * W2
                if older:
                    s = _dot_nt(qfar_sc[hh, row:row + BLK, :], kaug_sc[hh, BLK:BLK + older, :])
                    acc = acc + _dot(jnp.exp2(s).astype(BF16), vaug_sc[hh, BLK:BLK + older, :])
                o_ref[row:row + BLK, hh * AT_DH:(hh + 1) * AT_DH] = (
                    acc[:, 0:AT_DH] / acc[:, AT_DH:]).astype(o_ref.dtype)

    def two_pass():
        def pair(j, carry):
            rows = [pl.multiple_of(b * BLK, BLK) for b in (j, NB - 1 - j)]
            n_first = jnp.maximum(j, 1) // 2
            visits = []
            for st in range(NS):
                second = st >= n_first
                jj = jnp.where(second, st - n_first, st)
                visits.append((second.astype(jnp.int32), jnp.where(second, rows[1], rows[0]),
                               pl.multiple_of(jj * W2 + BLK, BLK)))
            for hh in range(HPS):
                for slot in range(2):
                    near_sc[hh, slot] = (_dot_nt(qnear_sc[hh, pl.ds(rows[slot], BLK), :],
                                                 kaug_sc[hh, pl.ds(rows[slot], W2), :])
                                         + dbias_ref[hh])
                for st, (slot, qrow, krow) in enumerate(visits):
                    far_sc[hh, st] = _dot_nt(qfar_sc[hh, pl.ds(qrow, BLK), :],
                                             kaug_sc[hh, pl.ds(krow, W2), :])
            mx_sc[...] = jnp.full(mx_sc.shape, MASKED_LOGIT, F32)
            for hh in range(HPS):
                for st, (slot, qrow, krow) in enumerate(visits):
                    mx_sc[hh, slot * NS + st] = lane_max(far_sc[hh, st])
                for slot in range(2):
                    m = lane_max(near_sc[hh, slot])
                    for st in range(NS):
                        m = jnp.maximum(m, mx_sc[hh, slot * NS + st])
                    mb_sc[hh, slot] = jnp.broadcast_to(jnp.max(m, axis=-1, keepdims=True),
                                                       (BLK, LANES))
            for hh in range(HPS):
                for slot in range(2):
                    acc_sc[hh, slot] = _dot(probs(near_sc[hh, slot], mb_sc[hh, slot]),
                                            vaug_sc[hh, pl.ds(rows[slot], W2), :])
                for st, (slot, qrow, krow) in enumerate(visits):
                    acc_sc[hh, slot] += _dot(probs(far_sc[hh, st], mb_sc[hh, slot]),
                                             vaug_sc[hh, pl.ds(krow, W2), :])
                for slot in range(2):
                    acc = acc_sc[hh, slot]
                    o_ref[pl.ds(rows[slot], BLK), hh * AT_DH:(hh + 1) * AT_DH] = (
                        acc[:, 0:AT_DH] / acc[:, AT_DH:]).astype(o_ref.dtype)
            return carry

        lax.fori_loop(0, NB // 2, pair, 0)

    lax.cond(worst <= MOBA_SAFE_GAP, one_pass, two_pass)


def _moba(q, k, v, rel_table, qg, kg, B, S):
    T, W = q.shape
    H = AT_HEADS
    HPS = MOBA_HEADS_PER_STEP
    NB = S // MOBA_BLOCK
    BLK = MOBA_BLOCK
    NS = max(NB // 2 - 1, 1)
    assert NB + 5 <= LANES and NB % 2 == 0 and H % HPS == 0
    assert int(_t5_bucket_np(np.array([BLK + 1]))[0]) == REL_BUCKETS - 1
    dbias = _bias_tiles(rel_table)
    far = jnp.broadcast_to((rel_table[REL_BUCKETS - 1, :].astype(F32) * LOG2E)[:, None],
                           (H, LANES))
    heads = lambda: pl.BlockSpec((S, HPS * AT_DH), lambda b, g: (b, g))
    aug = AT_DH + LANES
    return pl.pallas_call(
        functools.partial(_moba_kernel, NB=NB, HPS=HPS),
        grid=(B, H // HPS),
        in_specs=[heads(), heads(), heads(),
                  pl.BlockSpec((HPS, BLK, 2 * BLK), lambda b, g: (g, 0, 0)),
                  pl.BlockSpec((H, LANES), lambda b, g: (0, 0)),
                  pl.BlockSpec((1, AT_DH), lambda b, g: (0, 0)),
                  pl.BlockSpec((1, AT_DH), lambda b, g: (0, 0))],
        out_specs=heads(),
        out_shape=jax.ShapeDtypeStruct((T, W), BF16),
        scratch_shapes=[pltpu.VMEM((HPS, S, aug), BF16),
                        pltpu.VMEM((HPS, S, aug), BF16),
                        pltpu.VMEM((HPS, S + BLK, aug), BF16),
                        pltpu.VMEM((HPS, S + BLK, aug), BF16),
                        pltpu.VMEM((HPS, 2, BLK, 2 * BLK), F32),
                        pltpu.VMEM((HPS, NS, BLK, 2 * BLK), F32),
                        pltpu.VMEM((HPS, 2 * NS, BLK, LANES), F32),
                        pltpu.VMEM((HPS, 2, BLK, LANES), F32),
                        pltpu.VMEM((HPS, 2, BLK, aug), F32)],
        compiler_params=_cparams(("parallel", "parallel")),
        name="moba",
    )(q, k, v, dbias, far, qg.reshape(1, AT_DH).astype(F32), kg.reshape(1, AT_DH).astype(F32))


def _merge_kernel(ohg_ref, oat_ref, gate_ref, x_ref, whg_ref, wat_ref, wout_ref, g2_ref,
                  x1_ref, h2_ref, *, D, sub):
    for r0 in range(0, x_ref.shape[0], sub):
        rows = slice(r0, r0 + sub)
        y_hg = _dot(ohg_ref[rows, :], whg_ref[...])
        y_at = _dot(oat_ref[rows, :], wat_ref[...])
        merged = (gate_ref[rows, 0:D].astype(F32) * y_hg
                  + gate_ref[rows, D:2 * D].astype(F32) * y_at)
        x1 = x_ref[rows, :] + _dot(merged.astype(BF16), wout_ref[...])
        x1_ref[rows, :] = x1
        ms = jnp.mean(x1 * x1, axis=-1, keepdims=True)
        h2_ref[rows, :] = (x1 * lax.rsqrt(ms + NORM_EPS) * g2_ref[...]).astype(h2_ref.dtype)


def _merge(ohg, oat, gates, x, whg, wat, wout, g2, tm=1024, sub=512):
    T, D = x.shape
    tok = lambda w: pl.BlockSpec((tm, w), lambda i: (i, 0))
    return pl.pallas_call(
        functools.partial(_merge_kernel, D=D, sub=sub),
        grid=(T // tm,),
        in_specs=[tok(ohg.shape[1]), tok(oat.shape[1]), tok(2 * D), tok(D),
                  _resident(whg), _resident(wat), _resident(wout),
                  pl.BlockSpec((1, D), lambda i: (0, 0))],
        out_specs=[tok(D), tok(D)],
        out_shape=[jax.ShapeDtypeStruct((T, D), F32), jax.ShapeDtypeStruct((T, D), BF16)],
        compiler_params=_cparams(("parallel",)),
        name="merge",
    )(ohg, oat, gates, x, whg, wat, wout, g2.reshape(1, D).astype(F32))


def _ffn_kernel(h_ref, x_ref, wg_ref, wu_ref, wd_ref, o_ref, *, tf):
    h = h_ref[...]
    acc = x_ref[...]
    for f0 in range(0, wg_ref.shape[1], tf):
        a = _dot(h, wg_ref[:, f0:f0 + tf])
        u = _dot(h, wu_ref[:, f0:f0 + tf])
        acc = acc + _dot((_silu(a) * u).astype(BF16), wd_ref[f0:f0 + tf, :])
    o_ref[...] = acc


def _ffn(h2, x1, wg, wu, wd, tm=512, tf=2816):
    T, D = x1.shape
    FF = wg.shape[1]
    assert FF % tf == 0
    tok = lambda: pl.BlockSpec((tm, D), lambda i: (i, 0))
    return pl.pallas_call(
        functools.partial(_ffn_kernel, tf=tf),
        grid=(T // tm,),
        in_specs=[tok(), tok(), _resident(wg), _resident(wu), _resident(wd)],
        out_specs=tok(),
        out_shape=jax.ShapeDtypeStruct((T, D), F32),
        compiler_params=_cparams(("parallel",)),
        name="ffn",
    )(h2, x1, wg, wu, wd)


def kernel(x, attn_norm_g, w_in, hg_lb_gamma, hg_out_norm_g, q_norm_g, k_norm_g, rel_bias_table,
           w_branch_hg, w_branch_attn, w_out, ffn_norm_g, w_ffn_gate, w_ffn_up, w_ffn_down):
    B, S, D = x.shape
    T = B * S
    depth = attn_norm_g.shape[0]
    assert depth == 1 and S % MOBA_BLOCK == 0 and S % HG_CHUNK == 0
    WH = HG_HEADS * HG_DK
    WV = HG_HEADS * HG_DV
    WA = AT_HEADS * AT_DH
    assert w_in.shape[2] == 2 * WH + 2 * WV + 3 * WA + 2 * D

    xt = x.reshape(T, D)
    for l in range(depth):
        hq, lf, hk, hi, hg, aq, ak, av, gates = _inproj(
            xt, attn_norm_g[l], w_in[l].astype(BF16), hg_lb_gamma, q_norm_g[l], k_norm_g[l],
            WH, WV, WA)
        o_hg = _hgrn(hq, lf, hk, hi, hg, hg_out_norm_g[l], B, S)
        o_at = _moba(aq, ak, av, rel_bias_table, q_norm_g[l], k_norm_g[l], B, S)
        x1, h2 = _merge(o_hg, o_at, gates, xt, w_branch_hg[l].astype(BF16),
                        w_branch_attn[l].astype(BF16), w_out[l].astype(BF16), ffn_norm_g[l])
        xt = _ffn(h2, x1, w_ffn_gate[l].astype(BF16), w_ffn_up[l].astype(BF16),
                  w_ffn_down[l].astype(BF16))
    return xt.reshape(B, S, D)
```

```python
import functools
import math

import numpy as np
import jax
import jax.numpy as jnp
from jax import lax
from jax.experimental import pallas as pl
from jax.experimental.pallas import tpu as pltpu

F32 = jnp.float32
BF16 = jnp.bfloat16

LANES = 128
NORM_EPS = 1e-6

HG_HEADS = 8
HG_DK = 128
HG_DV = 128
HG_CHUNK = 64
HG_STEP_TOKENS = 512
HG_HEAD_GROUP = 8
AT_HEADS = 8
AT_DH = 128
MOBA_BLOCK = 256
MOBA_TOPK = 3
REL_BUCKETS = 32
REL_MAX_DIST = 128
MASKED_LOGIT = -1e30
LOG2E = math.log2(math.e)

VMEM_LIMIT = 56 * 1024 * 1024


def _cparams(sem):
    return pltpu.CompilerParams(dimension_semantics=sem, vmem_limit_bytes=VMEM_LIMIT)


def _dot(a, b):
    return jnp.dot(a, b, preferred_element_type=F32)


def _dot_nt(a, b):
    return lax.dot_general(a, b, (((1,), (1,)), ((), ())), preferred_element_type=F32)


def _dot_tn(a, b):
    return lax.dot_general(a, b, (((0,), (0,)), ((), ())), preferred_element_type=F32)


def _resident(a):
    return pl.BlockSpec(a.shape, lambda *_: (0,) * a.ndim, pipeline_mode=pl.Buffered(1))


def _sigmoid(x):
    return 0.5 * jnp.tanh(0.5 * x) + 0.5


def _silu(x):
    h = 0.5 * x
    return h + h * jnp.tanh(h)


def _head_rmsnorm(acc, gain, scale):
    cols = []
    for c in range(acc.shape[1] // LANES):
        blk = acc[:, c * LANES:(c + 1) * LANES]
        ms = jnp.mean(blk * blk, axis=-1, keepdims=True)
        cols.append(blk * lax.rsqrt(ms + NORM_EPS) * (gain * scale))
    return jnp.concatenate(cols, axis=1)


def _inproj_kernel(x_ref, g_ref, w_ref, gamma_ref, qg_ref, kg_ref,
                   hq_ref, lf_ref, hk_ref, hi_ref, hg_ref, aq_ref, ak_ref, av_ref, gate_ref,
                   *, WH, WV, WA):
    x = x_ref[...]
    ms = jnp.mean(x * x, axis=-1, keepdims=True)
    h = (x * lax.rsqrt(ms + NORM_EPS) * g_ref[...]).astype(BF16)
    D = x.shape[1]
    starts = np.cumsum([0, WH, WH, WV, WV, WA, WA, WA])

    def segment(idx, width, off=0):
        c0 = int(starts[idx]) + off
        return _dot(h, w_ref[:, c0:c0 + width])

    for half in range(2):
        gate_ref[:, half * D:(half + 1) * D] = _sigmoid(
            segment(7, D, half * D)).astype(gate_ref.dtype)

    gamma = gamma_ref[...]
    eg = jnp.exp(gamma - jnp.max(gamma, axis=0, keepdims=True))
    lb = eg[0:1, :] / jnp.sum(eg, axis=0, keepdims=True)
    c0 = 0.5 + 0.5 * lb
    c1 = 0.5 - 0.5 * lb
    ct = c1 * jnp.tanh(0.5 * segment(1, WH))
    lf_ref[...] = jnp.log2(c0 + ct)
    hk_ref[...] = (c1 - ct).astype(hk_ref.dtype)

    hg_ref[...] = _silu(segment(3, WV)).astype(hg_ref.dtype)
    aq_ref[...] = _head_rmsnorm(segment(4, WA), qg_ref[...],
                                AT_DH ** -0.5 * LOG2E).astype(aq_ref.dtype)
    ak_ref[...] = _head_rmsnorm(segment(5, WA), kg_ref[...], 1.0).astype(ak_ref.dtype)
    hq_ref[...] = (segment(0, WH) * HG_DK ** -0.5).astype(hq_ref.dtype)
    hi_ref[...] = segment(2, WV).astype(hi_ref.dtype)
    av_ref[...] = segment(6, WA).astype(av_ref.dtype)


def _inproj(x, g, w, gamma, qg, kg, WH, WV, WA, tm=512):
    T, D = x.shape
    tok = lambda width: pl.BlockSpec((tm, width), lambda i: (i, 0))
    small = lambda a: pl.BlockSpec(a.shape, lambda i: (0, 0))
    widths = [WH, WH, WH, WV, WV, WA, WA, WA, 2 * D]
    dtypes = [BF16, F32, BF16, BF16, BF16, BF16, BF16, BF16, BF16]
    g = g.reshape(1, D).astype(F32)
    gamma = gamma.astype(F32)
    qg = qg.reshape(1, AT_DH).astype(F32)
    kg = kg.reshape(1, AT_DH).astype(F32)
    return pl.pallas_call(
        functools.partial(_inproj_kernel, WH=WH, WV=WV, WA=WA),
        grid=(T // tm,),
        in_specs=[tok(D), small(g), _resident(w), small(gamma), small(qg), small(kg)],
        out_specs=[tok(wd) for wd in widths],
        out_shape=[jax.ShapeDtypeStruct((T, wd), dt) for wd, dt in zip(widths, dtypes)],
        compiler_params=_cparams(("parallel",)),
        name="inproj",
    )(x, g, w, gamma, qg, kg)


def _hgrn_tables(C):
    nl = int(math.log2(C))
    assert 1 << nl == C and nl >= 3
    masks = np.zeros((nl + 1, C, C), np.float32)
    masks[0] = np.eye(C)
    t = np.arange(C)
    for L in range(1, nl + 1):
        blk, half = 1 << L, 1 << (L - 1)
        base = (t // blk) * blk
        upper = (t - base) >= half
        same = base[:, None] == base[None, :]
        masks[L] = (same & upper[:, None] & (~upper)[None, :]).astype(np.float32)
    return masks, nl


def _hgrn_kernel(q_ref, lf_ref, k_ref, v_ref, g_ref, gain_ref, m_ref, o_ref,
                 e_sc, d_sc, st_sc, *, C, nl, heads, chunks):
    @pl.when(pl.program_id(1) == 0)
    def _():
        st_sc[...] = jnp.zeros_like(st_sc)

    gain = gain_ref[...]
    scan_shifts = [1 << j for j in range(nl)]
    row = lax.broadcasted_iota(jnp.int32, (C, lf_ref.shape[1]), 0)
    odd = (row & 1) == 1
    ph = row & 3

    def chunk_rows(c):
        return pl.ds(pl.multiple_of(c * C, C), C)

    def prepare(c, buf):
        lf = lf_ref[chunk_rows(c), :]
        b = lf
        for sh in scan_shifts:
            b = b + jnp.where(row >= sh, pltpu.roll(b, sh, axis=0), 0.0)
        e_sc[buf, 0:C, :] = jnp.exp2(jnp.where(odd, lf, 0.0)).astype(BF16)
        prev = pltpu.roll(lf, 1, axis=0)
        nxt = pltpu.roll(lf, C - 1, axis=0)
        x2 = jnp.where(ph == 0, nxt, jnp.where(ph == 1, 0.0, jnp.where(ph == 2, lf, lf + prev)))
        e_sc[buf, C:2 * C, :] = jnp.exp2(x2).astype(BF16)
        for L in range(3, nl + 1):
            blk, half = 1 << L, 1 << (L - 1)
            pieces = [b[base:base + blk, :] - b[base + half - 1:base + half, :]
                      for base in range(0, C, blk)]
            d = pieces[0] if len(pieces) == 1 else jnp.concatenate(pieces, axis=0)
            e_sc[buf, (L - 1) * C:L * C, :] = jnp.exp2(-jnp.abs(d)).astype(BF16)
        b_last = b[C - 1:C, :]
        e_sc[buf, nl * C:(nl + 1) * C, :] = jnp.exp2(b).astype(BF16)
        e_sc[buf, (nl + 1) * C:(nl + 2) * C, :] = jnp.exp2(b_last - b).astype(BF16)
        d_sc[buf] = jnp.exp2(b_last)

    def block_diag(x):
        n, d2 = x.shape
        z = jnp.zeros((n, d2 // 2), x.dtype)
        return jnp.concatenate([jnp.concatenate([x[:, :d2 // 2], z], axis=1),
                                jnp.concatenate([z, x[:, d2 // 2:]], axis=1)], axis=0)

    W2 = 2 * HG_DK
    pair_mask = [jnp.concatenate([m_ref[L], m_ref[L]], axis=1) for L in range(nl + 1)]

    def heads_of(c, buf):
        rows = chunk_rows(c)
        pairs = [slice(p * W2, (p + 1) * W2) for p in range(heads // 2)]
        qb = [q_ref[rows, sl] for sl in pairs]
        kb = [k_ref[rows, sl] for sl in pairs]
        vb = [v_ref[rows, sl] for sl in pairs]
        A = [pair_mask[0] * _dot_nt(q, block_diag(k)) for q, k in zip(qb, kb)]
        for L in range(1, nl + 1):
            for p, sl in enumerate(pairs):
                eL = e_sc[buf, (L - 1) * C:L * C, sl]
                A[p] = A[p] + pair_mask[L] * _dot_nt(qb[p] * eL, block_diag(kb[p] * eL))
        for p, sl in enumerate(pairs):
            eb = e_sc[buf, nl * C:(nl + 1) * C, sl]
            ek = e_sc[buf, (nl + 1) * C:(nl + 2) * C, sl]
            st = [st_sc[2 * p], st_sc[2 * p + 1]]
            st_pair = jnp.concatenate([s.astype(BF16) for s in st], axis=1)
            o = (_dot(A[p].astype(BF16), block_diag(vb[p]))
                 + _dot_nt(qb[p] * eb, block_diag(st_pair)))
            upd = _dot_tn(vb[p], kb[p] * ek)
            for j in range(2):
                h = 2 * p + j
                hs = slice(h * HG_DK, (h + 1) * HG_DK)
                js = slice(j * HG_DK, (j + 1) * HG_DK)
                st_sc[h] = st[j] * d_sc[buf, :, hs] + upd[js, js]
                oj = o[:, js]
                ms = jnp.mean(oj * oj, axis=-1, keepdims=True)
                y = oj * lax.rsqrt(ms + NORM_EPS) * gain * g_ref[rows, hs].astype(F32)
                o_ref[rows, hs] = y.astype(o_ref.dtype)

    prepare(0, 0)

    for c in range(chunks):
        if c + 1 < chunks:
            prepare(c + 1, (c + 1) % 2)
        heads_of(c, c % 2)


def _hgrn(q, lf, k, v, g, gain, B, S):
    T, W = q.shape
    C = HG_CHUNK
    TS = HG_STEP_TOKENS
    assert S % TS == 0 and TS % C == 0
    N = S // TS
    masks, nl = _hgrn_tables(C)
    row = lambda b, c: (b * N + c, 0)
    tok = lambda: pl.BlockSpec((TS, W), row)
    return pl.pallas_call(
        functools.partial(_hgrn_kernel, C=C, nl=nl, heads=HG_HEADS, chunks=TS // C),
        grid=(B, N),
        in_specs=[tok(), tok(), tok(), tok(), tok(),
                  pl.BlockSpec((1, HG_DV), lambda b, c: (0, 0)),
                  pl.BlockSpec(masks.shape, lambda b, c: (0, 0, 0))],
        out_specs=tok(),
        out_shape=jax.ShapeDtypeStruct((T, W), BF16),
        scratch_shapes=[pltpu.VMEM((2, (nl + 2) * C, W), BF16),
                        pltpu.VMEM((2, 1, W), F32),
                        pltpu.VMEM((HG_HEADS, HG_DV, HG_DK), F32)],
        compiler_params=_cparams(("parallel", "arbitrary")),
        name="hgrn2",
    )(q, lf, k, v, g, gain.reshape(1, HG_DV).astype(F32), jnp.asarray(masks, F32))


def _t5_bucket_np(dist):
    n = np.maximum(dist, 0)
    max_exact = REL_BUCKETS // 2
    nf = np.maximum(n, 1).astype(np.float32)
    large = max_exact + (np.log(nf / max_exact) / math.log(REL_MAX_DIST / max_exact)
                         * (REL_BUCKETS - max_exact)).astype(np.int32)
    large = np.minimum(large, REL_BUCKETS - 1)
    return np.where(n < max_exact, n, large).astype(np.int32)


def _bias_kernel(tab_ref, bucket_ref, o_ref):
    h = pl.program_id(0)
    bucket = bucket_ref[...]
    acc = jnp.zeros(bucket.shape, F32)
    for b in range(REL_BUCKETS):
        acc = jnp.where(bucket == b, tab_ref[b, h], acc)
    delta = (acc - tab_ref[REL_BUCKETS - 1, h]) * LOG2E
    blk = bucket.shape[0]
    r = lax.broadcasted_iota(jnp.int32, bucket.shape, 0)
    c = lax.broadcasted_iota(jnp.int32, bucket.shape, 1)
    o_ref[0] = jnp.where(c - blk <= r, delta, MASKED_LOGIT)


def _bias_tiles(rel_table):
    BLK = MOBA_BLOCK
    t = np.arange(BLK)[:, None]
    s = np.arange(2 * BLK)[None, :]
    bucket = _t5_bucket_np(t + BLK - s)
    H = rel_table.shape[1]
    return pl.pallas_call(
        _bias_kernel,
        grid=(H,),
        in_specs=[pl.BlockSpec(memory_space=pltpu.SMEM),
                  pl.BlockSpec(bucket.shape, lambda h: (0, 0))],
        out_specs=pl.BlockSpec((1, BLK, 2 * BLK), lambda h: (h, 0, 0)),
        out_shape=jax.ShapeDtypeStruct((H, BLK, 2 * BLK), F32),
        compiler_params=_cparams(("arbitrary",)),
        name="relbias",
    )(rel_table.astype(F32), jnp.asarray(bucket))


MOBA_HEADS_PER_STEP = 2
MOBA_BOUND_SLACK = 1.0 + 2.0 ** -6
MOBA_SAFE_GAP = 100.0


def _moba_kernel(q_ref, k_ref, v_ref, dbias_ref, far_ref, qg_ref, kg_ref, o_ref,
                 qnear_sc, qfar_sc, kaug_sc, vaug_sc, near_sc, far_sc, mx_sc, mb_sc, acc_sc,
                 *, NB, HPS):
    BLK = MOBA_BLOCK
    W2 = 2 * BLK
    PAD = NB + 2
    SHIFT = NB + 3
    NS = NB // 2 - 1
    hg = pl.program_id(1)

    def prepare(hh):
        cols = slice(hh * AT_DH, (hh + 1) * AT_DH)
        kaug, vaug, qnear, qfar = kaug_sc.at[hh], vaug_sc.at[hh], qnear_sc.at[hh], qfar_sc.at[hh]
        far = far_ref[pl.ds(hg * HPS + hh, 1), :]
        far_hi = far.astype(BF16).astype(F32)
        far_lo = far - far_hi
        lane = lax.broadcasted_iota(jnp.int32, (1, LANES), 1)
        tail = jnp.where(lane == NB, far_hi, jnp.where(lane == NB + 1, far_lo, 0.0))
        tail = jnp.where((lane == SHIFT) | (lane == SHIFT + 1), 1.0, tail)
        kaug[0:BLK, 0:AT_DH] = jnp.zeros((BLK, AT_DH), BF16)
        kaug[0:BLK, AT_DH:] = jnp.broadcast_to(
            jnp.where(lane == PAD, 1.0, 0.0), (BLK, LANES)).astype(BF16)
        vaug[0:BLK, :] = jnp.zeros((BLK, AT_DH + LANES), BF16)
        kms = []
        ones8 = jnp.ones((8, BLK), BF16)
        for n in range(NB):
            kb = k_ref[n * BLK:(n + 1) * BLK, cols]
            kms.append(_dot(ones8, kb)[0:1, :] * (1.0 / BLK))
            kaug[(n + 1) * BLK:(n + 2) * BLK, 0:AT_DH] = kb
            kaug[(n + 1) * BLK:(n + 2) * BLK, AT_DH:] = jnp.broadcast_to(
                jnp.where(lane == n, 1.0, tail), (BLK, LANES)).astype(BF16)
        vaug[BLK:, 0:AT_DH] = v_ref[:, cols]
        vaug[BLK:, AT_DH:] = jnp.ones((NB * BLK, LANES), BF16)

        qk_bound = (AT_DH * jnp.max(jnp.abs(qg_ref[...]), keepdims=True)
                    * jnp.max(jnp.abs(kg_ref[...]), keepdims=True)
                    * (AT_DH ** -0.5 * LOG2E * MOBA_BOUND_SLACK))
        bias_max = jnp.max(dbias_ref[hh], keepdims=True)[0:1, 0:1] + far[:, 0:1]
        bias_self = dbias_ref[hh][0:1, BLK:BLK + 1] + far[:, 0:1]
        shift = qk_bound + bias_max
        gap = shift + qk_bound - bias_self
        shift_hi = shift.astype(BF16).astype(F32)

        kmean = jnp.concatenate(kms, axis=0)
        km_hi = kmean.astype(BF16)
        km_lo = (kmean - km_hi.astype(F32)).astype(BF16)
        blk = lax.broadcasted_iota(jnp.int32, (NB, BLK), 0)
        rowid = lax.broadcasted_iota(jnp.int32, (LANES - NB, BLK), 0) + NB
        rest = jnp.where(rowid < NB + 2, 1.0, jnp.where(rowid == PAD, MASKED_LOGIT, 0.0))
        rest = jnp.where(rowid == SHIFT, -shift_hi,
                         jnp.where(rowid == SHIFT + 1, shift_hi - shift, rest))
        lane_q = lax.broadcasted_iota(jnp.int32, (BLK, LANES), 1)
        for t in range(NB):
            qt = q_ref[t * BLK:(t + 1) * BLK, cols]
            gate = jnp.where(blk < t, _dot_nt(km_hi, qt) + _dot_nt(km_lo, qt), -jnp.inf)
            rank = jnp.zeros((NB, BLK), jnp.int32)
            for m in range(t):
                gm = gate[m:m + 1, :]
                rank = rank + ((gm > gate) | ((gm == gate) & (blk > m))).astype(jnp.int32)
            sel = ((blk < t) & (rank < MOBA_TOPK)) | (blk == t)
            near = jnp.concatenate([jnp.where(sel, 0.0, MASKED_LOGIT), rest], axis=0).T
            older = jnp.where((lane_q == t) | (lane_q == t - 1), MASKED_LOGIT, near)
            rows = slice(t * BLK, (t + 1) * BLK)
            qnear[rows, 0:AT_DH] = qt
            qfar[rows, 0:AT_DH] = qt
            qnear[rows, AT_DH:] = near.astype(BF16)
            qfar[rows, AT_DH:] = older.astype(BF16)
        return jnp.max(gap)

    worst = prepare(0)
    for hh in range(1, HPS):
        worst = jnp.maximum(worst, prepare(hh))

    def lane_max(s):
        out = s[:, 0:LANES]
        for c in range(1, s.shape[1] // LANES):
            out = jnp.maximum(out, s[:, c * LANES:(c + 1) * LANES])
        return out

    def probs(s, m_b):
        return jnp.exp2(s - jnp.concatenate([m_b] * (W2 // LANES), axis=1)).astype(BF16)

    def one_pass():
        for i in range(NB):
            for hh in range(HPS):
                row = i * BLK
                s = (_dot_nt(qnear_sc[hh, row:row + BLK, :], kaug_sc[hh, row:row + W2, :])
                     + dbias_ref[hh])
                acc = _dot(jnp.exp2(s).astype(BF16), vaug_sc[hh, row:row + W2, :])
                older = (max(i, 1) // 2) * W2
                if older:
                    s = _dot_nt(qfar_sc[hh, row:row + BLK, :], kaug_sc[hh, BLK:BLK + older, :])
                    acc = acc + _dot(jnp.exp2(s).astype(BF16), vaug_sc[hh, BLK:BLK + older, :])
                o_ref[row:row + BLK, hh * AT_DH:(hh + 1) * AT_DH] = (
                    acc[:, 0:AT_DH] / acc[:, AT_DH:]).astype(o_ref.dtype)

    def two_pass():
        def pair(j, carry):
            rows = [pl.multiple_of(b * BLK, BLK) for b in (j, NB - 1 - j)]
            n_first = jnp.maximum(j, 1) // 2
            visits = []
            for st in range(NS):
                second = st >= n_first
                jj = jnp.where(second, st - n_first, st)
                visits.append((second.astype(jnp.int32), jnp.where(second, rows[1], rows[0]),
                               pl.multiple_of(jj * W2 + BLK, BLK)))
            for hh in range(HPS):
                for slot in range(2):
                    near_sc[hh, slot] = (_dot_nt(qnear_sc[hh, pl.ds(rows[slot], BLK), :],
                                                 kaug_sc[hh, pl.ds(rows[slot], W2), :])
                                         + dbias_ref[hh])
                for st, (slot, qrow, krow) in enumerate(visits):
                    far_sc[hh, st] = _dot_nt(qfar_sc[hh, pl.ds(qrow, BLK), :],
                                             kaug_sc[hh, pl.ds(krow, W2), :])
            mx_sc[...] = jnp.full(mx_sc.shape, MASKED_LOGIT, F32)
            for hh in range(HPS):
                for st, (slot, qrow, krow) in enumerate(visits):
                    mx_sc[hh, slot * NS + st] = lane_max(far_sc[hh, st])
                for slot in range(2):
                    m = lane_max(near_sc[hh, slot])
                    for st in range(NS):
                        m = jnp.maximum(m, mx_sc[hh, slot * NS + st])
                    mb_sc[hh, slot] = jnp.broadcast_to(jnp.max(m, axis=-1, keepdims=True),
                                                       (BLK, LANES))
            for hh in range(HPS):
                for slot in range(2):
                    acc_sc[hh, slot] = _dot(probs(near_sc[hh, slot], mb_sc[hh, slot]),
                                            vaug_sc[hh, pl.ds(rows[slot], W2), :])
                for st, (slot, qrow, krow) in enumerate(visits):
                    acc_sc[hh, slot] += _dot(probs(far_sc[hh, st], mb_sc[hh, slot]),
                                             vaug_sc[hh, pl.ds(krow, W2), :])
                for slot in range(2):
                    acc = acc_sc[hh, slot]
                    o_ref[pl.ds(rows[slot], BLK), hh * AT_DH:(hh + 1) * AT_DH] = (
                        acc[:, 0:AT_DH] / acc[:, AT_DH:]).astype(o_ref.dtype)
            return carry

        lax.fori_loop(0, NB // 2, pair, 0)

    lax.cond(worst <= MOBA_SAFE_GAP, one_pass, two_pass)


def _moba(q, k, v, rel_table, qg, kg, B, S):
    T, W = q.shape
    H = AT_HEADS
    HPS = MOBA_HEADS_PER_STEP
    NB = S // MOBA_BLOCK
    BLK = MOBA_BLOCK
    NS = max(NB // 2 - 1, 1)
    assert NB + 5 <= LANES and NB % 2 == 0 and H % HPS == 0
    assert int(_t5_bucket_np(np.array([BLK + 1]))[0]) == REL_BUCKETS - 1
    dbias = _bias_tiles(rel_table)
    far = jnp.broadcast_to((rel_table[REL_BUCKETS - 1, :].astype(F32) * LOG2E)[:, None],
                           (H, LANES))
    heads = lambda: pl.BlockSpec((S, HPS * AT_DH), lambda b, g: (b, g))
    aug = AT_DH + LANES
    return pl.pallas_call(
        functools.partial(_moba_kernel, NB=NB, HPS=HPS),
        grid=(B, H // HPS),
        in_specs=[heads(), heads(), heads(),
                  pl.BlockSpec((HPS, BLK, 2 * BLK), lambda b, g: (g, 0, 0)),
                  pl.BlockSpec((H, LANES), lambda b, g: (0, 0)),
                  pl.BlockSpec((1, AT_DH), lambda b, g: (0, 0)),
                  pl.BlockSpec((1, AT_DH), lambda b, g: (0, 0))],
        out_specs=heads(),
        out_shape=jax.ShapeDtypeStruct((T, W), BF16),
        scratch_shapes=[pltpu.VMEM((HPS, S, aug), BF16),
                        pltpu.VMEM((HPS, S, aug), BF16),
                        pltpu.VMEM((HPS, S + BLK, aug), BF16),
                        pltpu.VMEM((HPS, S + BLK, aug), BF16),
                        pltpu.VMEM((HPS, 2, BLK, 2 * BLK), F32),
                        pltpu.VMEM((HPS, NS, BLK, 2 * BLK), F32),
                        pltpu.VMEM((HPS, 2 * NS, BLK, LANES), F32),
                        pltpu.VMEM((HPS, 2, BLK, LANES), F32),
                        pltpu.VMEM((HPS, 2, BLK, aug), F32)],
        compiler_params=_cparams(("parallel", "parallel")),
        name="moba",
    )(q, k, v, dbias, far, qg.reshape(1, AT_DH).astype(F32), kg.reshape(1, AT_DH).astype(F32))


def _merge_kernel(ohg_ref, oat_ref, gate_ref, x_ref, whg_ref, wat_ref, wout_ref, g2_ref,
                  x1_ref, h2_ref, *, D, sub):
    for r0 in range(0, x_ref.shape[0], sub):
        rows = slice(r0, r0 + sub)
        y_hg = _dot(ohg_ref[rows, :], whg_ref[...])
        y_at = _dot(oat_ref[rows, :], wat_ref[...])
        merged = (gate_ref[rows, 0:D].astype(F32) * y_hg
                  + gate_ref[rows, D:2 * D].astype(F32) * y_at)
        x1 = x_ref[rows, :] + _dot(merged.astype(BF16), wout_ref[...])
        x1_ref[rows, :] = x1
        ms = jnp.mean(x1 * x1, axis=-1, keepdims=True)
        h2_ref[rows, :] = (x1 * lax.rsqrt(ms + NORM_EPS) * g2_ref[...]).astype(h2_ref.dtype)


def _merge(ohg, oat, gates, x, whg, wat, wout, g2, tm=1024, sub=512):
    T, D = x.shape
    tok = lambda w: pl.BlockSpec((tm, w), lambda i: (i, 0))
    return pl.pallas_call(
        functools.partial(_merge_kernel, D=D, sub=sub),
        grid=(T // tm,),
        in_specs=[tok(ohg.shape[1]), tok(oat.shape[1]), tok(2 * D), tok(D),
                  _resident(whg), _resident(wat), _resident(wout),
                  pl.BlockSpec((1, D), lambda i: (0, 0))],
        out_specs=[tok(D), tok(D)],
        out_shape=[jax.ShapeDtypeStruct((T, D), F32), jax.ShapeDtypeStruct((T, D), BF16)],
        compiler_params=_cparams(("parallel",)),
        name="merge",
    )(ohg, oat, gates, x, whg, wat, wout, g2.reshape(1, D).astype(F32))


def _ffn_kernel(h_ref, x_ref, wg_ref, wu_ref, wd_ref, o_ref, *, tf):
    h = h_ref[...]
    acc = x_ref[...]
    for f0 in range(0, wg_ref.shape[1], tf):
        a = _dot(h, wg_ref[:, f0:f0 + tf])
        u = _dot(h, wu_ref[:, f0:f0 + tf])
        acc = acc + _dot((_silu(a) * u).astype(BF16), wd_ref[f0:f0 + tf, :])
    o_ref[...] = acc


def _ffn(h2, x1, wg, wu, wd, tm=512, tf=2816):
    T, D = x1.shape
    FF = wg.shape[1]
    assert FF % tf == 0
    tok = lambda: pl.BlockSpec((tm, D), lambda i: (i, 0))
    return pl.pallas_call(
        functools.partial(_ffn_kernel, tf=tf),
        grid=(T // tm,),
        in_specs=[tok(), tok(), _resident(wg), _resident(wu), _resident(wd)],
        out_specs=tok(),
        out_shape=jax.ShapeDtypeStruct((T, D), F32),
        compiler_params=_cparams(("parallel",)),
        name="ffn",
    )(h2, x1, wg, wu, wd)


def kernel(x, attn_norm_g, w_in, hg_lb_gamma, hg_out_norm_g, q_norm_g, k_norm_g, rel_bias_table,
           w_branch_hg, w_branch_attn, w_out, ffn_norm_g, w_ffn_gate, w_ffn_up, w_ffn_down):
    B, S, D = x.shape
    T = B * S
    depth = attn_norm_g.shape[0]
    assert depth == 1 and S % MOBA_BLOCK == 0 and S % HG_CHUNK == 0
    WH = HG_HEADS * HG_DK
    WV = HG_HEADS * HG_DV
    WA = AT_HEADS * AT_DH
    assert w_in.shape[2] == 2 * WH + 2 * WV + 3 * WA + 2 * D

    xt = x.reshape(T, D)
    for l in range(depth):
        hq, lf, hk, hi, hg, aq, ak, av, gates = _inproj(
            xt, attn_norm_g[l], w_in[l].astype(BF16), hg_lb_gamma, q_norm_g[l], k_norm_g[l],
            WH, WV, WA)
        o_hg = _hgrn(hq, lf, hk, hi, hg, hg_out_norm_g[l], B, S)
        o_at = _moba(aq, ak, av, rel_bias_table, q_norm_g[l], k_norm_g[l], B, S)
        x1, h2 = _merge(o_hg, o_at, gates, xt, w_branch_hg[l].astype(BF16),
                        w_branch_attn[l].astype(BF16), w_out[l].astype(BF16), ffn_norm_g[l])
        xt = _ffn(h2, x1, w_ffn_gate[l].astype(BF16), w_ffn_up[l].astype(BF16),
                  w_ffn_down[l].astype(BF16))
    return xt.reshape(B, S, D)
```

```python
import functools
import math

import numpy as np
import jax
import jax.numpy as jnp
from jax import lax
from jax.experimental import pallas as pl
from jax.experimental.pallas import tpu as pltpu

F32 = jnp.float32
BF16 = jnp.bfloat16

LANES = 128
NORM_EPS = 1e-6

HG_HEADS = 8
HG_DK = 128
HG_DV = 128
HG_CHUNK = 64
HG_STEP_TOKENS = 512
AT_HEADS = 8
AT_DH = 128
MOBA_BLOCK = 256
MOBA_TOPK = 3
REL_BUCKETS = 32
REL_MAX_DIST = 128
MASKED_LOGIT = -1e30
LOG2E = math.log2(math.e)

VMEM_LIMIT = 56 * 1024 * 1024


def _cparams(sem):
    return pltpu.CompilerParams(dimension_semantics=sem, vmem_limit_bytes=VMEM_LIMIT)


def _dot(a, b):
    return jnp.dot(a, b, preferred_element_type=F32)


def _dot_nt(a, b):
    return lax.dot_general(a, b, (((1,), (1,)), ((), ())), preferred_element_type=F32)


def _dot_tn(a, b):
    return lax.dot_general(a, b, (((0,), (0,)), ((), ())), preferred_element_type=F32)


def _resident(a):
    return pl.BlockSpec(a.shape, lambda *_: (0,) * a.ndim, pipeline_mode=pl.Buffered(1))


def _sigmoid(x):
    return 0.5 * jnp.tanh(0.5 * x) + 0.5


def _silu(x):
    h = 0.5 * x
    return h + h * jnp.tanh(h)


def _head_rmsnorm(acc, gain, scale):
    cols = []
    for c in range(acc.shape[1] // LANES):
        blk = acc[:, c * LANES:(c + 1) * LANES]
        ms = jnp.mean(blk * blk, axis=-1, keepdims=True)
        cols.append(blk * lax.rsqrt(ms + NORM_EPS) * (gain * scale))
    return jnp.concatenate(cols, axis=1)


def _inproj_kernel(x_ref, g_ref, w_ref, gamma_ref, qg_ref, kg_ref,
                   hq_ref, lf_ref, hk_ref, hi_ref, hg_ref, aq_ref, ak_ref, av_ref, gate_ref,
                   *, WH, WV, WA):
    x = x_ref[...]
    ms = jnp.mean(x * x, axis=-1, keepdims=True)
    h = (x * lax.rsqrt(ms + NORM_EPS) * g_ref[...]).astype(BF16)
    D = x.shape[1]
    starts = np.cumsum([0, WH, WH, WV, WV, WA, WA, WA])

    def segment(idx, width, off=0):
        c0 = int(starts[idx]) + off
        return _dot(h, w_ref[:, c0:c0 + width])

    for half in range(2):
        gate_ref[:, half * D:(half + 1) * D] = _sigmoid(
            segment(7, D, half * D)).astype(gate_ref.dtype)

    gamma = gamma_ref[...]
    eg = jnp.exp(gamma - jnp.max(gamma, axis=0, keepdims=True))
    lb = eg[0:1, :] / jnp.sum(eg, axis=0, keepdims=True)
    c0 = 0.5 + 0.5 * lb
    c1 = 0.5 - 0.5 * lb
    ct = c1 * jnp.tanh(0.5 * segment(1, WH))
    lf_ref[...] = jnp.log2(c0 + ct)
    hk_ref[...] = (c1 - ct).astype(hk_ref.dtype)

    hg_ref[...] = _silu(segment(3, WV)).astype(hg_ref.dtype)
    aq_ref[...] = _head_rmsnorm(segment(4, WA), qg_ref[...],
                                AT_DH ** -0.5 * LOG2E).astype(aq_ref.dtype)
    ak_ref[...] = _head_rmsnorm(segment(5, WA), kg_ref[...], 1.0).astype(ak_ref.dtype)
    hq_ref[...] = (segment(0, WH) * HG_DK ** -0.5).astype(hq_ref.dtype)
    hi_ref[...] = segment(2, WV).astype(hi_ref.dtype)
    av_ref[...] = segment(6, WA).astype(av_ref.dtype)


def _inproj(x, g, w, gamma, qg, kg, WH, WV, WA, tm=512):
    T, D = x.shape
    tok = lambda width: pl.BlockSpec((tm, width), lambda i: (i, 0))
    small = lambda a: pl.BlockSpec(a.shape, lambda i: (0, 0))
    widths = [WH, WH, WH, WV, WV, WA, WA, WA, 2 * D]
    dtypes = [BF16, F32, BF16, BF16, BF16, BF16, BF16, BF16, BF16]
    g = g.reshape(1, D).astype(F32)
    gamma = gamma.astype(F32)
    qg = qg.reshape(1, AT_DH).astype(F32)
    kg = kg.reshape(1, AT_DH).astype(F32)
    return pl.pallas_call(
        functools.partial(_inproj_kernel, WH=WH, WV=WV, WA=WA),
        grid=(T // tm,),
        in_specs=[tok(D), small(g), _resident(w), small(gamma), small(qg), small(kg)],
        out_specs=[tok(wd) for wd in widths],
        out_shape=[jax.ShapeDtypeStruct((T, wd), dt) for wd, dt in zip(widths, dtypes)],
        compiler_params=_cparams(("parallel",)),
        name="inproj",
    )(x, g, w, gamma, qg, kg)


def _hgrn_tables(C):
    nl = int(math.log2(C))
    assert 1 << nl == C and nl >= 3
    masks = np.zeros((nl + 1, C, C), np.float32)
    masks[0] = np.eye(C)
    t = np.arange(C)
    for L in range(1, nl + 1):
        blk, half = 1 << L, 1 << (L - 1)
        base = (t // blk) * blk
        upper = (t - base) >= half
        same = base[:, None] == base[None, :]
        masks[L] = (same & upper[:, None] & (~upper)[None, :]).astype(np.float32)
    return masks, nl


def _hgrn_kernel(q_ref, lf_ref, k_ref, v_ref, g_ref, gain_ref, m_ref, o_ref,
                 e_sc, d_sc, st_sc, *, C, nl, heads, chunks):
    @pl.when(pl.program_id(1) == 0)
    def _():
        st_sc[...] = jnp.zeros_like(st_sc)

    gain = gain_ref[...]
    scan_shifts = [1 << j for j in range(nl)]
    row = lax.broadcasted_iota(jnp.int32, (C, lf_ref.shape[1]), 0)
    odd = (row & 1) == 1
    ph = row & 3

    def chunk_rows(c):
        return pl.ds(pl.multiple_of(c * C, C), C)

    def prepare(c, buf):
        lf = lf_ref[chunk_rows(c), :]
        b = lf
        for sh in scan_shifts:
            b = b + jnp.where(row >= sh, pltpu.roll(b, sh, axis=0), 0.0)
        e_sc[buf, 0:C, :] = jnp.exp2(jnp.where(odd, lf, 0.0)).astype(BF16)
        prev = pltpu.roll(lf, 1, axis=0)
        nxt = pltpu.roll(lf, C - 1, axis=0)
        x2 = jnp.where(ph == 0, nxt, jnp.where(ph == 1, 0.0, jnp.where(ph == 2, lf, lf + prev)))
        e_sc[buf, C:2 * C, :] = jnp.exp2(x2).astype(BF16)
        for L in range(3, nl + 1):
            blk, half = 1 << L, 1 << (L - 1)
            pieces = [b[base:base + blk, :] - b[base + half - 1:base + half, :]
                      for base in range(0, C, blk)]
            d = pieces[0] if len(pieces) == 1 else jnp.concatenate(pieces, axis=0)
            e_sc[buf, (L - 1) * C:L * C, :] = jnp.exp2(-jnp.abs(d)).astype(BF16)
        b_last = b[C - 1:C, :]
        e_sc[buf, nl * C:(nl + 1) * C, :] = jnp.exp2(b).astype(BF16)
        e_sc[buf, (nl + 1) * C:(nl + 2) * C, :] = jnp.exp2(b_last - b).astype(BF16)
        d_sc[buf] = jnp.exp2(b_last)

    def block_diag(x):
        n, d2 = x.shape
        z = jnp.zeros((n, d2 // 2), x.dtype)
        return jnp.concatenate([jnp.concatenate([x[:, :d2 // 2], z], axis=1),
                                jnp.concatenate([z, x[:, d2 // 2:]], axis=1)], axis=0)

    W2 = 2 * HG_DK
    pair_mask = [jnp.concatenate([m_ref[L], m_ref[L]], axis=1) for L in range(nl + 1)]

    def heads_of(c, buf):
        rows = chunk_rows(c)
        pairs = [slice(p * W2, (p + 1) * W2) for p in range(heads // 2)]
        qb = [q_ref[rows, sl] for sl in pairs]
        kb = [k_ref[rows, sl] for sl in pairs]
        vb = [v_ref[rows, sl] for sl in pairs]
        A = [pair_mask[0] * _dot_nt(q, block_diag(k)) for q, k in zip(qb, kb)]
        for L in range(1, nl + 1):
            for p, sl in enumerate(pairs):
                eL = e_sc[buf, (L - 1) * C:L * C, sl]
                A[p] = A[p] + pair_mask[L] * _dot_nt(qb[p] * eL, block_diag(kb[p] * eL))
        for p, sl in enumerate(pairs):
            eb = e_sc[buf, nl * C:(nl + 1) * C, sl]
            ek = e_sc[buf, (nl + 1) * C:(nl + 2) * C, sl]
            st = [st_sc[2 * p], st_sc[2 * p + 1]]
            st_pair = jnp.concatenate([s.astype(BF16) for s in st], axis=1)
            o = (_dot(A[p].astype(BF16), block_diag(vb[p]))
                 + _dot_nt(qb[p] * eb, block_diag(st_pair)))
            upd = _dot_tn(vb[p], kb[p] * ek)
            for j in range(2):
                h = 2 * p + j
                hs = slice(h * HG_DK, (h + 1) * HG_DK)
                js = slice(j * HG_DK, (j + 1) * HG_DK)
                st_sc[h] = st[j] * d_sc[buf, :, hs] + upd[js, js]
                oj = o[:, js]
                ms = jnp.mean(oj * oj, axis=-1, keepdims=True)
                y = oj * lax.rsqrt(ms + NORM_EPS) * gain * g_ref[rows, hs].astype(F32)
                o_ref[rows, hs] = y.astype(o_ref.dtype)

    prepare(0, 0)

    for c in range(chunks):
        if c + 1 < chunks:
            prepare(c + 1, (c + 1) % 2)
        heads_of(c, c % 2)


def _hgrn(q, lf, k, v, g, gain, B, S):
    T, W = q.shape
    C = HG_CHUNK
    TS = HG_STEP_TOKENS
    assert S % TS == 0 and TS % C == 0
    N = S // TS
    masks, nl = _hgrn_tables(C)
    row = lambda b, c: (b * N + c, 0)
    tok = lambda: pl.BlockSpec((TS, W), row)
    return pl.pallas_call(
        functools.partial(_hgrn_kernel, C=C, nl=nl, heads=HG_HEADS, chunks=TS // C),
        grid=(B, N),
        in_specs=[tok(), tok(), tok(), tok(), tok(),
                  pl.BlockSpec((1, HG_DV), lambda b, c: (0, 0)),
                  pl.BlockSpec(masks.shape, lambda b, c: (0, 0, 0))],
        out_specs=tok(),
        out_shape=jax.ShapeDtypeStruct((T, W), BF16),
        scratch_shapes=[pltpu.VMEM((2, (nl + 2) * C, W), BF16),
                        pltpu.VMEM((2, 1, W), F32),
                        pltpu.VMEM((HG_HEADS, HG_DV, HG_DK), F32)],
        compiler_params=_cparams(("parallel", "arbitrary")),
        name="hgrn2",
    )(q, lf, k, v, g, gain.reshape(1, HG_DV).astype(F32), jnp.asarray(masks, F32))


def _t5_bucket_np(dist):
    n = np.maximum(dist, 0)
    max_exact = REL_BUCKETS // 2
    nf = np.maximum(n, 1).astype(np.float32)
    large = max_exact + (np.log(nf / max_exact) / math.log(REL_MAX_DIST / max_exact)
                         * (REL_BUCKETS - max_exact)).astype(np.int32)
    large = np.minimum(large, REL_BUCKETS - 1)
    return np.where(n < max_exact, n, large).astype(np.int32)


def _bias_kernel(tab_ref, bucket_ref, o_ref):
    h = pl.program_id(0)
    bucket = bucket_ref[...]
    acc = jnp.zeros(bucket.shape, F32)
    for b in range(REL_BUCKETS):
        acc = jnp.where(bucket == b, tab_ref[b, h], acc)
    delta = (acc - tab_ref[REL_BUCKETS - 1, h]) * LOG2E
    blk = bucket.shape[0]
    r = lax.broadcasted_iota(jnp.int32, bucket.shape, 0)
    c = lax.broadcasted_iota(jnp.int32, bucket.shape, 1)
    o_ref[0] = jnp.where(c - blk <= r, delta, MASKED_LOGIT)


def _bias_tiles(rel_table):
    BLK = MOBA_BLOCK
    t = np.arange(BLK)[:, None]
    s = np.arange(2 * BLK)[None, :]
    bucket = _t5_bucket_np(t + BLK - s)
    H = rel_table.shape[1]
    return pl.pallas_call(
        _bias_kernel,
        grid=(H,),
        in_specs=[pl.BlockSpec(memory_space=pltpu.SMEM),
                  pl.BlockSpec(bucket.shape, lambda h: (0, 0))],
        out_specs=pl.BlockSpec((1, BLK, 2 * BLK), lambda h: (h, 0, 0)),
        out_shape=jax.ShapeDtypeStruct((H, BLK, 2 * BLK), F32),
        compiler_params=_cparams(("arbitrary",)),
        name="relbias",
    )(rel_table.astype(F32), jnp.asarray(bucket))


MOBA_HEADS_PER_STEP = 1
MOBA_BOUND_SLACK = 1.0 + 2.0 ** -6
MOBA_SAFE_GAP = 100.0


def _moba_kernel(q_ref, k_ref, v_ref, dbias_ref, far_ref, qg_ref, kg_ref, o_ref,
                 qnear_sc, qfar_sc, kaug_sc, vaug_sc, near_sc, far_sc, mx_sc, mb_sc, acc_sc,
                 *, NB, HPS):
    BLK = MOBA_BLOCK
    W2 = 2 * BLK
    PAD = NB + 2
    SHIFT = NB + 3
    NS = NB // 2 - 1
    hg = pl.program_id(1)

    def prepare(hh):
        cols = slice(hh * AT_DH, (hh + 1) * AT_DH)
        kaug, vaug, qnear, qfar = kaug_sc.at[hh], vaug_sc.at[hh], qnear_sc.at[hh], qfar_sc.at[hh]
        far = far_ref[pl.ds(hg * HPS + hh, 1), :]
        far_hi = far.astype(BF16).astype(F32)
        far_lo = far - far_hi
        lane = lax.broadcasted_iota(jnp.int32, (1, LANES), 1)
        tail = jnp.where(lane == NB, far_hi, jnp.where(lane == NB + 1, far_lo, 0.0))
        tail = jnp.where((lane == SHIFT) | (lane == SHIFT + 1), 1.0, tail)
        kaug[0:BLK, 0:AT_DH] = jnp.zeros((BLK, AT_DH), BF16)
        kaug[0:BLK, AT_DH:] = jnp.broadcast_to(
            jnp.where(lane == PAD, 1.0, 0.0), (BLK, LANES)).astype(BF16)
        vaug[0:BLK, :] = jnp.zeros((BLK, AT_DH + LANES), BF16)
        kms = []
        ones8 = jnp.ones((8, BLK), BF16)
        for n in range(NB):
            kb = k_ref[n * BLK:(n + 1) * BLK, cols]
            kms.append(_dot(ones8, kb)[0:1, :] * (1.0 / BLK))
            kaug[(n + 1) * BLK:(n + 2) * BLK, 0:AT_DH] = kb
            kaug[(n + 1) * BLK:(n + 2) * BLK, AT_DH:] = jnp.broadcast_to(
                jnp.where(lane == n, 1.0, tail), (BLK, LANES)).astype(BF16)
        vaug[BLK:, 0:AT_DH] = v_ref[:, cols]
        vaug[BLK:, AT_DH:] = jnp.ones((NB * BLK, LANES), BF16)

        qk_bound = (AT_DH * jnp.max(jnp.abs(qg_ref[...]), keepdims=True)
                    * jnp.max(jnp.abs(kg_ref[...]), keepdims=True)
                    * (AT_DH ** -0.5 * LOG2E * MOBA_BOUND_SLACK))
        bias_max = jnp.max(dbias_ref[hh], keepdims=True)[0:1, 0:1] + far[:, 0:1]
        bias_self = dbias_ref[hh][0:1, BLK:BLK + 1] + far[:, 0:1]
        shift = qk_bound + bias_max
        gap = shift + qk_bound - bias_self
        shift_hi = shift.astype(BF16).astype(F32)

        kmean = jnp.concatenate(kms, axis=0)
        km_hi = kmean.astype(BF16)
        km_lo = (kmean - km_hi.astype(F32)).astype(BF16)
        blk = lax.broadcasted_iota(jnp.int32, (NB, BLK), 0)
        rowid = lax.broadcasted_iota(jnp.int32, (LANES - NB, BLK), 0) + NB
        rest = jnp.where(rowid < NB + 2, 1.0, jnp.where(rowid == PAD, MASKED_LOGIT, 0.0))
        rest = jnp.where(rowid == SHIFT, -shift_hi,
                         jnp.where(rowid == SHIFT + 1, shift_hi - shift, rest))
        lane_q = lax.broadcasted_iota(jnp.int32, (BLK, LANES), 1)
        for t in range(NB):
            qt = q_ref[t * BLK:(t + 1) * BLK, cols]
            gate = jnp.where(blk < t, _dot_nt(km_hi, qt) + _dot_nt(km_lo, qt), -jnp.inf)
            rank = jnp.zeros((NB, BLK), jnp.int32)
            for m in range(t):
                gm = gate[m:m + 1, :]
                rank = rank + ((gm > gate) | ((gm == gate) & (blk > m))).astype(jnp.int32)
            sel = ((blk < t) & (rank < MOBA_TOPK)) | (blk == t)
            near = jnp.concatenate([jnp.where(sel, 0.0, MASKED_LOGIT), rest], axis=0).T
            older = jnp.where((lane_q == t) | (lane_q == t - 1), MASKED_LOGIT, near)
            rows = slice(t * BLK, (t + 1) * BLK)
            qnear[rows, 0:AT_DH] = qt
            qfar[rows, 0:AT_DH] = qt
            qnear[rows, AT_DH:] = near.astype(BF16)
            qfar[rows, AT_DH:] = older.astype(BF16)
        return jnp.max(gap)

    worst = prepare(0)
    for hh in range(1, HPS):
        worst = jnp.maximum(worst, prepare(hh))

    def lane_max(s):
        out = s[:, 0:LANES]
        for c in range(1, s.shape[1] // LANES):
            out = jnp.maximum(out, s[:, c * LANES:(c + 1) * LANES])
        return out

    def probs(s, m_b):
        return jnp.exp2(s - jnp.concatenate([m_b] * (W2 // LANES), axis=1)).astype(BF16)

    def one_pass():
        for hh in range(HPS):
            for i in range(NB):
                row = i * BLK
                s = (_dot_nt(qnear_sc[hh, row:row + BLK, :], kaug_sc[hh, row:row + W2, :])
                     + dbias_ref[hh])
                acc = _dot(jnp.exp2(s).astype(BF16), vaug_sc[hh, row:row + W2, :])
                older = (max(i, 1) // 2) * W2
                if older:
                    s = _dot_nt(qfar_sc[hh, row:row + BLK, :], kaug_sc[hh, BLK:BLK + older, :])
                    acc = acc + _dot(jnp.exp2(s).astype(BF16), vaug_sc[hh, BLK:BLK + older, :])
                o_ref[row:row + BLK, hh * AT_DH:(hh + 1) * AT_DH] = (
                    acc[:, 0:AT_DH] / acc[:, AT_DH:]).astype(o_ref.dtype)

    def two_pass():
        def pair(j, carry):
            rows = [pl.multiple_of(b * BLK, BLK) for b in (j, NB - 1 - j)]
            n_first = jnp.maximum(j, 1) // 2
            visits = []
            for st in range(NS):
                second = st >= n_first
                jj = jnp.where(second, st - n_first, st)
                visits.append((second.astype(jnp.int32), jnp.where(second, rows[1], rows[0]),
                               pl.multiple_of(jj * W2 + BLK, BLK)))
            for hh in range(HPS):
                for slot in range(2):
                    near_sc[hh, slot] = (_dot_nt(qnear_sc[hh, pl.ds(rows[slot], BLK), :],
                                                 kaug_sc[hh, pl.ds(rows[slot], W2), :])
                                         + dbias_ref[hh])
                for st, (slot, qrow, krow) in enumerate(visits):
                    far_sc[hh, st] = _dot_nt(qfar_sc[hh, pl.ds(qrow, BLK), :],
                                             kaug_sc[hh, pl.ds(krow, W2), :])
            mx_sc[...] = jnp.full(mx_sc.shape, MASKED_LOGIT, F32)
            for hh in range(HPS):
                for st, (slot, qrow, krow) in enumerate(visits):
                    mx_sc[hh, slot * NS + st] = lane_max(far_sc[hh, st])
                for slot in range(2):
                    m = lane_max(near_sc[hh, slot])
                    for st in range(NS):
                        m = jnp.maximum(m, mx_sc[hh, slot * NS + st])
                    mb_sc[hh, slot] = jnp.broadcast_to(jnp.max(m, axis=-1, keepdims=True),
                                                       (BLK, LANES))
            for hh in range(HPS):
                for slot in range(2):
                    acc_sc[hh, slot] = _dot(probs(near_sc[hh, slot], mb_sc[hh, slot]),
                                            vaug_sc[hh, pl.ds(rows[slot], W2), :])
                for st, (slot, qrow, krow) in enumerate(visits):
                    acc_sc[hh, slot] += _dot(probs(far_sc[hh, st], mb_sc[hh, slot]),
                                             vaug_sc[hh, pl.ds(krow, W2), :])
                for slot in range(2):
                    acc = acc_sc[hh, slot]
                    o_ref[pl.ds(rows[slot], BLK), hh * AT_DH:(hh + 1) * AT_DH] = (
                        acc[:, 0:AT_DH] / acc[:, AT_DH:]).astype(o_ref.dtype)
            return carry

        lax.fori_loop(0, NB // 2, pair, 0)

    lax.cond(worst <= MOBA_SAFE_GAP, one_pass, two_pass)


def _moba(q, k, v, rel_table, qg, kg, B, S):
    T, W = q.shape
    H = AT_HEADS
    HPS = MOBA_HEADS_PER_STEP
    NB = S // MOBA_BLOCK
    BLK = MOBA_BLOCK
    NS = max(NB // 2 - 1, 1)
    assert NB + 5 <= LANES and NB % 2 == 0 and H % HPS == 0
    assert int(_t5_bucket_np(np.array([BLK + 1]))[0]) == REL_BUCKETS - 1
    dbias = _bias_tiles(rel_table)
    far = jnp.broadcast_to((rel_table[REL_BUCKETS - 1, :].astype(F32) * LOG2E)[:, None],
                           (H, LANES))
    heads = lambda: pl.BlockSpec((S, HPS * AT_DH), lambda b, g: (b, g))
    aug = AT_DH + LANES
    return pl.pallas_call(
        functools.partial(_moba_kernel, NB=NB, HPS=HPS),
        grid=(B, H // HPS),
        in_specs=[heads(), heads(), heads(),
                  pl.BlockSpec((HPS, BLK, 2 * BLK), lambda b, g: (g, 0, 0)),
                  pl.BlockSpec((H, LANES), lambda b, g: (0, 0)),
                  pl.BlockSpec((1, AT_DH), lambda b, g: (0, 0)),
                  pl.BlockSpec((1, AT_DH), lambda b, g: (0, 0))],
        out_specs=heads(),
        out_shape=jax.ShapeDtypeStruct((T, W), BF16),
        scratch_shapes=[pltpu.VMEM((HPS, S, aug), BF16),
                        pltpu.VMEM((HPS, S, aug), BF16),
                        pltpu.VMEM((HPS, S + BLK, aug), BF16),
                        pltpu.VMEM((HPS, S + BLK, aug), BF16),
                        pltpu.VMEM((HPS, 2, BLK, 2 * BLK), F32),
                        pltpu.VMEM((HPS, NS, BLK, 2 * BLK), F32),
                        pltpu.VMEM((HPS, 2 * NS, BLK, LANES), F32),
                        pltpu.VMEM((HPS, 2, BLK, LANES), F32),
                        pltpu.VMEM((HPS, 2, BLK, aug), F32)],
        compiler_params=_cparams(("parallel", "parallel")),
        name="moba",
    )(q, k, v, dbias, far, qg.reshape(1, AT_DH).astype(F32), kg.reshape(1, AT_DH).astype(F32))


def _merge_kernel(ohg_ref, oat_ref, gate_ref, x_ref, whg_ref, wat_ref, wout_ref, g2_ref,
                  x1_ref, h2_ref, *, D, sub):
    for r0 in range(0, x_ref.shape[0], sub):
        rows = slice(r0, r0 + sub)
        y_hg = _dot(ohg_ref[rows, :], whg_ref[...])
        y_at = _dot(oat_ref[rows, :], wat_ref[...])
        merged = (gate_ref[rows, 0:D].astype(F32) * y_hg
                  + gate_ref[rows, D:2 * D].astype(F32) * y_at)
        x1 = x_ref[rows, :] + _dot(merged.astype(BF16), wout_ref[...])
        x1_ref[rows, :] = x1
        ms = jnp.mean(x1 * x1, axis=-1, keepdims=True)
        h2_ref[rows, :] = (x1 * lax.rsqrt(ms + NORM_EPS) * g2_ref[...]).astype(h2_ref.dtype)


def _merge(ohg, oat, gates, x, whg, wat, wout, g2, tm=1024, sub=512):
    T, D = x.shape
    tok = lambda w: pl.BlockSpec((tm, w), lambda i: (i, 0))
    return pl.pallas_call(
        functools.partial(_merge_kernel, D=D, sub=sub),
        grid=(T // tm,),
        in_specs=[tok(ohg.shape[1]), tok(oat.shape[1]), tok(2 * D), tok(D),
                  _resident(whg), _resident(wat), _resident(wout),
                  pl.BlockSpec((1, D), lambda i: (0, 0))],
        out_specs=[tok(D), tok(D)],
        out_shape=[jax.ShapeDtypeStruct((T, D), F32), jax.ShapeDtypeStruct((T, D), BF16)],
        compiler_params=_cparams(("parallel",)),
        name="merge",
    )(ohg, oat, gates, x, whg, wat, wout, g2.reshape(1, D).astype(F32))


def _ffn_kernel(h_ref, x_ref, wg_ref, wu_ref, wd_ref, o_ref, *, tf):
    h = h_ref[...]
    acc = x_ref[...]
    for f0 in range(0, wg_ref.shape[1], tf):
        a = _dot(h, wg_ref[:, f0:f0 + tf])
        u = _dot(h, wu_ref[:, f0:f0 + tf])
        acc = acc + _dot((_silu(a) * u).astype(BF16), wd_ref[f0:f0 + tf, :])
    o_ref[...] = acc


def _ffn(h2, x1, wg, wu, wd, tm=512, tf=2816):
    T, D = x1.shape
    FF = wg.shape[1]
    assert FF % tf == 0
    tok = lambda: pl.BlockSpec((tm, D), lambda i: (i, 0))
    return pl.pallas_call(
        functools.partial(_ffn_kernel, tf=tf),
        grid=(T // tm,),
        in_specs=[tok(), tok(), _resident(wg), _resident(wu), _resident(wd)],
        out_specs=tok(),
        out_shape=jax.ShapeDtypeStruct((T, D), F32),
        compiler_params=_cparams(("parallel",)),
        name="ffn",
    )(h2, x1, wg, wu, wd)


def kernel(x, attn_norm_g, w_in, hg_lb_gamma, hg_out_norm_g, q_norm_g, k_norm_g, rel_bias_table,
           w_branch_hg, w_branch_attn, w_out, ffn_norm_g, w_ffn_gate, w_ffn_up, w_ffn_down):
    B, S, D = x.shape
    T = B * S
    depth = attn_norm_g.shape[0]
    assert depth == 1 and S % MOBA_BLOCK == 0 and S % HG_CHUNK == 0
    WH = HG_HEADS * HG_DK
    WV = HG_HEADS * HG_DV
    WA = AT_HEADS * AT_DH
    assert w_in.shape[2] == 2 * WH + 2 * WV + 3 * WA + 2 * D

    xt = x.reshape(T, D)
    for l in range(depth):
        hq, lf, hk, hi, hg, aq, ak, av, gates = _inproj(
            xt, attn_norm_g[l], w_in[l].astype(BF16), hg_lb_gamma, q_norm_g[l], k_norm_g[l],
            WH, WV, WA)
        o_hg = _hgrn(hq, lf, hk, hi, hg, hg_out_norm_g[l], B, S)
        o_at = _moba(aq, ak, av, rel_bias_table, q_norm_g[l], k_norm_g[l], B, S)
        x1, h2 = _merge(o_hg, o_at, gates, xt, w_branch_hg[l].astype(BF16),
                        w_branch_attn[l].astype(BF16), w_out[l].astype(BF16), ffn_norm_g[l])
        xt = _ffn(h2, x1, w_ffn_gate[l].astype(BF16), w_ffn_up[l].astype(BF16),
                  w_ffn_down[l].astype(BF16))
    return xt.reshape(B, S, D)
```
